```python
import math
import jax, jax.numpy as jnp
from jax import lax
import numpy as np

D_MODEL = 2048
BATCH = 4
SEQ = 4096
DEPTH = 2

MIX_WIDTH = D_MODEL
HEAD_DIM = 128
ROPE_DIM = HEAD_DIM // 4
ROPE_THETA = 500000.0
EPS = 1e-6

MOBA_HEADS = 4
MOBA_WIDTH = MOBA_HEADS * HEAD_DIM
MOBA_BLOCK = 256
MOBA_TOPK = 3
MOBA_Q_CHUNK = 32

NSA_HEADS = 4
NSA_WIDTH = NSA_HEADS * HEAD_DIM
NSA_KV_WIDTH = HEAD_DIM
NSA_CMP_LEN = 32
NSA_CMP_STRIDE = 16
NSA_SEL_BLOCK = 64
NSA_SEL_TOPN = 16
NSA_WINDOW = 512
NSA_Q_CHUNK = 64

S5_WIDTH = 1024
S5_GROUP = 16
S5_GROUPS = S5_WIDTH // S5_GROUP
S5_STATE = 64

IN_SPLITS = (MOBA_WIDTH, MOBA_WIDTH, MOBA_WIDTH, MOBA_WIDTH,
             NSA_WIDTH,
             NSA_KV_WIDTH, NSA_KV_WIDTH, NSA_KV_WIDTH, NSA_KV_WIDTH,
             NSA_KV_WIDTH, NSA_KV_WIDTH,
             NSA_HEADS * 3, NSA_WIDTH,
             S5_WIDTH, S5_WIDTH)
IN_WIDTH = 4 * MOBA_WIDTH + 2 * NSA_WIDTH + 6 * NSA_KV_WIDTH + NSA_HEADS * 3 + 2 * S5_WIDTH

kernel_name = "hybrid_moba_nsa_s5_parallel_heads"


def rms_norm(x, w):
    xf = x.astype(jnp.float32)
    y = xf * lax.rsqrt(jnp.mean(xf * xf, axis=-1, keepdims=True) + EPS)
    return (y * w.astype(jnp.float32)).astype(x.dtype)


def rope_tables(pos):
    inv = ROPE_THETA ** (-jnp.arange(0, ROPE_DIM, 2, dtype=jnp.float32) / ROPE_DIM)
    ang = pos.astype(jnp.float32)[:, None] * inv[None, :]
    return jnp.cos(ang), jnp.sin(ang)


def apply_rope(x, cos, sin):
    half = ROPE_DIM // 2
    x1, x2, xp = x[..., :half], x[..., half:ROPE_DIM], x[..., ROPE_DIM:]
    c, s = cos.astype(x.dtype), sin.astype(x.dtype)
    return jnp.concatenate([x1 * c - x2 * s, x2 * c + x1 * s, xp], axis=-1)


def moba_attention(q, k, v):
    b, h, s, dh = q.shape
    nb = -(-s // MOBA_BLOCK)
    pad = nb * MOBA_BLOCK - s
    kp = jnp.pad(k, ((0, 0), (0, 0), (0, pad), (0, 0)))
    vp = jnp.pad(v, ((0, 0), (0, 0), (0, pad), (0, 0)))
    kb = kp.reshape(b, h, nb, MOBA_BLOCK, dh)
    vb = vp.reshape(b, h, nb, MOBA_BLOCK, dh)
    k_mean = jnp.mean(kb.astype(jnp.float32), axis=3)
    topk = min(MOBA_TOPK, nb)
    n_sel = topk * MOBA_BLOCK
    scale = dh ** -0.5
    bi = jnp.arange(b)[:, None, None, None]
    hi = jnp.arange(h)[None, :, None, None]
    blk = jnp.arange(nb)

    def chunk(c):
        start = c * MOBA_Q_CHUNK
        t = start + jnp.arange(MOBA_Q_CHUNK)
        own = start // MOBA_BLOCK
        qc = lax.dynamic_slice_in_dim(q, start, MOBA_Q_CHUNK, axis=2)
        gate = jnp.einsum('bhqd,bhnd->bhqn', qc.astype(jnp.float32), k_mean)
        gate = jnp.where(blk < own, gate, -jnp.inf)
        _, idx = lax.top_k(gate, topk)
        keep = idx < own
        k_sel = kb[bi, hi, idx]
        v_sel = vb[bi, hi, idx]
        s_sel = jnp.einsum('bhqd,bhqnkd->bhqnk', qc, k_sel).astype(jnp.float32) * scale
        s_sel = jnp.where(keep[..., None], s_sel, -jnp.inf).reshape(b, h, MOBA_Q_CHUNK, n_sel)
        k_own = lax.dynamic_slice_in_dim(kp, own * MOBA_BLOCK, MOBA_BLOCK, axis=2)
        v_own = lax.dynamic_slice_in_dim(vp, own * MOBA_BLOCK, MOBA_BLOCK, axis=2)
        s_own = jnp.einsum('bhqd,bhkd->bhqk', qc, k_own).astype(jnp.float32) * scale
        kpos = own * MOBA_BLOCK + jnp.arange(MOBA_BLOCK)
        s_own = jnp.where(kpos[None, :] <= t[:, None], s_own, -jnp.inf)
        p = jax.nn.softmax(jnp.concatenate([s_sel, s_own], axis=-1), axis=-1).astype(v.dtype)
        p_sel = p[..., :n_sel].reshape(b, h, MOBA_Q_CHUNK, topk, MOBA_BLOCK)
        return (jnp.einsum('bhqnk,bhqnkd->bhqd', p_sel, v_sel)
                + jnp.einsum('bhqk,bhkd->bhqd', p[..., n_sel:], v_own))

    out = lax.map(chunk, jnp.arange(s // MOBA_Q_CHUNK))
    return out.transpose(1, 2, 0, 3, 4).reshape(b, h, s, dh)


def nsa_attention(q, kc_tok, vc_tok, ks, vs, kw, vw, gates, kc_norm, pe_k, pe_v,
                  ck_w1, ck_w2, cv_w1, cv_w2):
    b, h, s, dh = q.shape
    scale = dh ** -0.5
    t_all = jnp.arange(s)
    n_cmp = (s - NSA_CMP_LEN) // NSA_CMP_STRIDE + 1
    tok = np.arange(n_cmp)[:, None] * NSA_CMP_STRIDE + np.arange(NSA_CMP_LEN)[None, :]
    end = tok[:, -1]

    def compress(x_tok, pe, w1, w2):
        blocks = (x_tok[:, tok] + pe).reshape(b, n_cmp, NSA_CMP_LEN * dh)
        return jax.nn.gelu(blocks @ w1) @ w2

    k_cmp = compress(kc_tok, pe_k, ck_w1, ck_w2)
    v_cmp = compress(vc_tok, pe_v, cv_w1, cv_w2)
    cos_c, sin_c = rope_tables(jnp.asarray(end, jnp.float32))
    k_cmp = apply_rope(rms_norm(k_cmp, kc_norm), cos_c, sin_c)
    valid = jnp.asarray(end)[None, :] <= t_all[:, None]
    s_cmp = jnp.einsum('bhsd,bnd->bhsn', q, k_cmp).astype(jnp.float32) * scale
    p_cmp = jax.nn.softmax(jnp.where(valid, s_cmp, jnp.float32(-1e30)), axis=-1)
    p_cmp = jnp.where(valid, p_cmp, 0.0)
    o_cmp = jnp.einsum('bhsn,bnd->bhsd', p_cmp.astype(v_cmp.dtype), v_cmp)
    n_sel = s // NSA_SEL_BLOCK
    ci = np.arange(n_cmp)[:, None] * NSA_CMP_STRIDE
    sj = np.arange(n_sel)[None, :] * NSA_SEL_BLOCK
    overlap = ((ci < sj + NSA_SEL_BLOCK) & (ci + NSA_CMP_LEN > sj)).astype(np.float32)
    imp = jnp.einsum('bhsn,nj->bsj', p_cmp, jnp.asarray(overlap))
    n_top = min(NSA_SEL_TOPN, n_sel)
    ks_b = ks.reshape(b, n_sel, NSA_SEL_BLOCK, dh)
    vs_b = vs.reshape(b, n_sel, NSA_SEL_BLOCK, dh)
    kw_pad = jnp.pad(kw, ((0, 0), (NSA_WINDOW, 0), (0, 0)))
    vw_pad = jnp.pad(vw, ((0, 0), (NSA_WINDOW, 0), (0, 0)))
    bi = jnp.arange(b)[:, None, None]
    sel_ids = jnp.arange(n_sel)[None, :]

    def chunk(c):
        start = c * NSA_Q_CHUNK
        t = start + jnp.arange(NSA_Q_CHUNK)
        qc = lax.dynamic_slice_in_dim(q, start, NSA_Q_CHUNK, axis=2)
        cur = (t // NSA_SEL_BLOCK)[:, None]
        score = lax.dynamic_slice_in_dim(imp, start, NSA_Q_CHUNK, axis=1)
        score = jnp.where(sel_ids <= cur, score, -jnp.inf)
        forced = (sel_ids == 0) | (sel_ids == cur) | (sel_ids == cur - 1)
        score = jnp.where(forced, jnp.inf, score)
        _, idx = lax.top_k(score, n_top)
        kpos = idx[..., None] * NSA_SEL_BLOCK + jnp.arange(NSA_SEL_BLOCK)
        smask = kpos <= t[None, :, None, None]
        k_g = ks_b[bi, idx]
        v_g = vs_b[bi, idx]
        s_sel = jnp.einsum('bhqd,bqnkd->bhqnk', qc, k_g).astype(jnp.float32) * scale
        s_sel = jnp.where(smask[:, None], s_sel, -jnp.inf).reshape(b, h, NSA_Q_CHUNK, n_top * NSA_SEL_BLOCK)
        p_sel = jax.nn.softmax(s_sel, axis=-1).astype(vs.dtype).reshape(b, h, NSA_Q_CHUNK, n_top, NSA_SEL_BLOCK)
        o_sel = jnp.einsum('bhqnk,bqnkd->bhqd', p_sel, v_g)
        k_win = lax.dynamic_slice_in_dim(kw_pad, start, NSA_WINDOW + NSA_Q_CHUNK, axis=1)
        v_win = lax.dynamic_slice_in_dim(vw_pad, start, NSA_WINDOW + NSA_Q_CHUNK, axis=1)
        wpos = start - NSA_WINDOW + jnp.arange(NSA_WINDOW + NSA_Q_CHUNK)
        wmask = ((wpos[None, :] <= t[:, None]) & (wpos[None, :] > t[:, None] - NSA_WINDOW)
                 & (wpos[None, :] >= 0))
        s_win = jnp.einsum('bhqd,bkd->bhqk', qc, k_win).astype(jnp.float32) * scale
        p_win = jax.nn.softmax(jnp.where(wmask, s_win, -jnp.inf), axis=-1).astype(vw.dtype)
        o_win = jnp.einsum('bhqk,bkd->bhqd', p_win, v_win)
        return o_sel, o_win

    o_sel, o_win = lax.map(chunk, jnp.arange(s // NSA_Q_CHUNK))
    o_sel = o_sel.transpose(1, 2, 0, 3, 4).reshape(b, h, s, dh)
    o_win = o_win.transpose(1, 2, 0, 3, 4).reshape(b, h, s, dh)
    g = gates.astype(q.dtype)
    return g[..., 0:1] * o_cmp + g[..., 1:2] * o_sel + g[..., 2:3] * o_win


def _complex_affine_combine(e1, e2):
    a1r, a1i, b1r, b1i = e1
    a2r, a2i, b2r, b2i = e2
    return (a2r * a1r - a2i * a1i,
            a2r * a1i + a2i * a1r,
            a2r * b1r - a2i * b1i + b2r,
            a2r * b1i + a2i * b1r + b2i)


def s5_ssm(u, a_re, a_im, b_re, b_im, c_re, c_im, d, log_dt):
    bsz, s, _ = u.shape
    uf = u.astype(jnp.float32)
    ug = uf.reshape(bsz, s, S5_GROUPS, S5_GROUP)
    dt = jnp.exp(log_dt.astype(jnp.float32))[:, None]
    ar, ai = a_re.astype(jnp.float32), a_im.astype(jnp.float32)
    mag = jnp.exp(dt * ar)
    ang = dt * ai
    abar_r, abar_i = mag * jnp.cos(ang), mag * jnp.sin(ang)
    nr, ni = abar_r - 1.0, abar_i
    den = ar * ar + ai * ai
    fr = (nr * ar + ni * ai) / den
    fi = (ni * ar - nr * ai) / den
    br, bim = b_re.astype(jnp.float32), b_im.astype(jnp.float32)
    bbar_r = fr[..., None] * br - fi[..., None] * bim
    bbar_i = fr[..., None] * bim + fi[..., None] * br
    bu_r = jnp.einsum('bsgc,gpc->sbgp', ug, bbar_r)
    bu_i = jnp.einsum('bsgc,gpc->sbgp', ug, bbar_i)
    a_r = jnp.broadcast_to(abar_r, (s, 1, S5_GROUPS, S5_STATE))
    a_i = jnp.broadcast_to(abar_i, (s, 1, S5_GROUPS, S5_STATE))
    _, _, x_r, x_i = lax.associative_scan(_complex_affine_combine, (a_r, a_i, bu_r, bu_i), axis=0)
    y = (jnp.einsum('sbgp,gcp->bsgc', x_r, c_re.astype(jnp.float32))
         - jnp.einsum('sbgp,gcp->bsgc', x_i, c_im.astype(jnp.float32)))
    y = y.reshape(bsz, s, S5_WIDTH) + d.astype(jnp.float32) * uf
    return y.astype(u.dtype)


def hybrid_layer(x, norm_w, w_in, w_out, moba_q_norm, moba_k_norm, nsa_q_norm, nsa_kc_norm,
                 nsa_ks_norm, nsa_kw_norm, nsa_pe_k, nsa_pe_v, nsa_cmp_k_w1, nsa_cmp_k_w2,
                 nsa_cmp_v_w1, nsa_cmp_v_w2, s5_a_re, s5_a_im, s5_b_re, s5_b_im, s5_c_re, s5_c_im,
                 s5_d, s5_log_dt, s5_glu_w):
    b, s, _ = x.shape
    hdn = rms_norm(x, norm_w)
    proj = hdn @ w_in
    offsets = np.cumsum(IN_SPLITS)[:-1].tolist()
    (mq, mk, mv, mz, nq, nkc, nvc, nks, nvs, nkw, nvw, ng, nz, su, sz) = jnp.split(proj, offsets, axis=-1)
    cos, sin = rope_tables(jnp.arange(s, dtype=jnp.float32))

    def heads(t, n):
        return t.reshape(b, s, n, HEAD_DIM).transpose(0, 2, 1, 3)

    mq = apply_rope(rms_norm(heads(mq, MOBA_HEADS), moba_q_norm), cos, sin)
    mk = apply_rope(rms_norm(heads(mk, MOBA_HEADS), moba_k_norm), cos, sin)
    o_moba = moba_attention(mq, mk, heads(mv, MOBA_HEADS))
    o_moba = o_moba.transpose(0, 2, 1, 3).reshape(b, s, MOBA_WIDTH)
    nq = apply_rope(rms_norm(heads(nq, NSA_HEADS), nsa_q_norm), cos, sin)
    nks = apply_rope(rms_norm(nks, nsa_ks_norm), cos, sin)
    nkw = apply_rope(rms_norm(nkw, nsa_kw_norm), cos, sin)
    gates = jax.nn.sigmoid(ng).reshape(b, s, NSA_HEADS, 3).transpose(0, 2, 1, 3)
    o_nsa = nsa_attention(nq, nkc, nvc, nks, nvs, nkw, nvw, gates, nsa_kc_norm, nsa_pe_k, nsa_pe_v,
                          nsa_cmp_k_w1, nsa_cmp_k_w2, nsa_cmp_v_w1, nsa_cmp_v_w2)
    o_nsa = o_nsa.transpose(0, 2, 1, 3).reshape(b, s, NSA_WIDTH)
    y5 = jax.nn.gelu(s5_ssm(su, s5_a_re, s5_a_im, s5_b_re, s5_b_im, s5_c_re, s5_c_im, s5_d, s5_log_dt))
    o_s5 = y5 * jax.nn.sigmoid(y5 @ s5_glu_w)
    mixed = jnp.concatenate([o_moba * jax.nn.silu(mz), o_nsa * jax.nn.silu(nz), o_s5 * jax.nn.silu(sz)], axis=-1)
    return x + mixed @ w_out


def setup_inputs(seed: int = 0) -> dict:
    key = jax.random.key(seed)
    k = jax.random.split(key, 32)
    f32 = jnp.float32
    L = DEPTH

    def nrm(kk, shape, scale):
        return scale * jax.random.normal(kk, shape, f32)

    n_idx = jnp.arange(S5_STATE, dtype=f32)[None, None, :]
    return {
        "x": jax.random.normal(k[0], (BATCH, SEQ, D_MODEL), f32),
        "norm_w": 1.0 + nrm(k[1], (L, D_MODEL), 0.02),
        "w_in": nrm(k[2], (L, D_MODEL, IN_WIDTH), D_MODEL ** -0.5),
        "w_out": nrm(k[3], (L, MIX_WIDTH, D_MODEL), MIX_WIDTH ** -0.5),
        "moba_q_norm": 1.0 + nrm(k[4], (L, HEAD_DIM), 0.02),
        "moba_k_norm": 1.0 + nrm(k[5], (L, HEAD_DIM), 0.02),
        "nsa_q_norm": 1.0 + nrm(k[6], (L, HEAD_DIM), 0.02),
        "nsa_kc_norm": 1.0 + nrm(k[7], (L, HEAD_DIM), 0.02),
        "nsa_ks_norm": 1.0 + nrm(k[8], (L, HEAD_DIM), 0.02),
        "nsa_kw_norm": 1.0 + nrm(k[9], (L, HEAD_DIM), 0.02),
        "nsa_pe_k": nrm(k[10], (L, NSA_CMP_LEN, HEAD_DIM), 0.1),
        "nsa_pe_v": nrm(k[11], (L, NSA_CMP_LEN, HEAD_DIM), 0.1),
        "nsa_cmp_k_w1": nrm(k[12], (L, NSA_CMP_LEN * HEAD_DIM, HEAD_DIM), (NSA_CMP_LEN * HEAD_DIM) ** -0.5),
        "nsa_cmp_k_w2": nrm(k[13], (L, HEAD_DIM, HEAD_DIM), HEAD_DIM ** -0.5),
        "nsa_cmp_v_w1": nrm(k[14], (L, NSA_CMP_LEN * HEAD_DIM, HEAD_DIM), (NSA_CMP_LEN * HEAD_DIM) ** -0.5),
        "nsa_cmp_v_w2": nrm(k[15], (L, HEAD_DIM, HEAD_DIM), HEAD_DIM ** -0.5),
        "s5_a_re": -0.5 * jnp.exp(nrm(k[16], (L, S5_GROUPS, S5_STATE), 0.01)),
        "s5_a_im": math.pi * n_idx + nrm(k[17], (L, S5_GROUPS, S5_STATE), 0.01),
        "s5_b_re": nrm(k[18], (L, S5_GROUPS, S5_STATE, S5_GROUP), (2 * S5_GROUP) ** -0.5),
        "s5_b_im": nrm(k[19], (L, S5_GROUPS, S5_STATE, S5_GROUP), (2 * S5_GROUP) ** -0.5),
        "s5_c_re": nrm(k[20], (L, S5_GROUPS, S5_GROUP, S5_STATE), S5_STATE ** -0.5),
        "s5_c_im": nrm(k[21], (L, S5_GROUPS, S5_GROUP, S5_STATE), S5_STATE ** -0.5),
        "s5_d": nrm(k[22], (L, S5_WIDTH), 1.0),
        "s5_log_dt": jax.random.uniform(k[23], (L, S5_GROUPS), f32, math.log(0.001), math.log(0.1)),
        "s5_glu_w": nrm(k[24], (L, S5_WIDTH, S5_WIDTH), S5_WIDTH ** -0.5),
    }


def reference(x, norm_w, w_in, w_out, moba_q_norm, moba_k_norm, nsa_q_norm, nsa_kc_norm,
              nsa_ks_norm, nsa_kw_norm, nsa_pe_k, nsa_pe_v, nsa_cmp_k_w1, nsa_cmp_k_w2,
              nsa_cmp_v_w1, nsa_cmp_v_w2, s5_a_re, s5_a_im, s5_b_re, s5_b_im, s5_c_re, s5_c_im,
              s5_d, s5_log_dt, s5_glu_w):
    for l in range(DEPTH):
        x = hybrid_layer(x, norm_w[l], w_in[l], w_out[l], moba_q_norm[l], moba_k_norm[l],
                         nsa_q_norm[l], nsa_kc_norm[l], nsa_ks_norm[l], nsa_kw_norm[l],
                         nsa_pe_k[l], nsa_pe_v[l], nsa_cmp_k_w1[l], nsa_cmp_k_w2[l],
                         nsa_cmp_v_w1[l], nsa_cmp_v_w2[l], s5_a_re[l], s5_a_im[l], s5_b_re[l],
                         s5_b_im[l], s5_c_re[l], s5_c_im[l], s5_d[l], s5_log_dt[l], s5_glu_w[l])
    return x
```

```python
import functools
import math

import numpy as np
import jax
import jax.numpy as jnp
from jax import lax
from jax.experimental import pallas as pl
from jax.experimental.pallas import tpu as pltpu

F32 = jnp.float32
BF16 = jnp.bfloat16
HIGHEST = lax.Precision.HIGHEST

LANE = 128
HEAD_DIM = 128
ROPE_DIM = HEAD_DIM // 4
ROPE_HALF = ROPE_DIM // 2
ROPE_THETA = 500000.0
EPS = 1e-6
SCALE = HEAD_DIM ** -0.5
NEG = -1e30

MOBA_HEADS = 4
MOBA_BLOCK = 256
MOBA_TOPK = 3

NSA_HEADS = 4
NSA_CMP_LEN = 32
NSA_CMP_STRIDE = 16
NSA_SEL_BLOCK = 64
NSA_SEL_TOPN = 16
NSA_WINDOW = 512
NSA_TQ = 256
NSA_SEL_PER_TILE = NSA_TQ // NSA_SEL_BLOCK

S5_WIDTH = 1024
S5_GROUP = 16
S5_GROUPS = S5_WIDTH // S5_GROUP
S5_STATE = 64
S5_CHUNK = 16
S5_MBLK = S5_WIDTH // LANE
S5_GPB = LANE // S5_GROUP
S5_HALF = S5_GPB * S5_STATE

CB_MQ, CB_MK, CB_MV, CB_MZ, CB_NQ, CB_NZ, CB_SU, CB_SZ = 0, 4, 8, 12, 16, 20, 24, 32
CB_NKS, CB_NVS, CB_NKW, CB_NVW, CB_NKC, CB_NVC, CB_NG = 40, 41, 42, 43, 44, 45, 46
IN_BLOCKS = 48

VMEM_LIMIT = 56 * 1024 * 1024


def _cparams(sem):
    return pltpu.CompilerParams(dimension_semantics=sem, vmem_limit_bytes=VMEM_LIMIT)


def _iota(shape, dim):
    return lax.broadcasted_iota(jnp.int32, shape, dim)


def _head_norm(x, w):
    return x * lax.rsqrt(jnp.mean(x * x, axis=-1, keepdims=True) + EPS) * w


def _rope(x, c, s1, s2):
    return x * c + pltpu.roll(x, LANE - ROPE_HALF, 1) * s1 + pltpu.roll(x, ROPE_HALF, 1) * s2


def _dot_nt(a, b):
    return lax.dot_general(a, b, (((1,), (1,)), ((), ())), preferred_element_type=F32)


def _rope_tables(pos):
    inv = ROPE_THETA ** (-jnp.arange(0, ROPE_DIM, 2, dtype=F32) / ROPE_DIM)
    ang = pos.astype(F32)[:, None] * inv[None, :]
    cos, sin = jnp.cos(ang), jnp.sin(ang)
    n = pos.shape[0]
    c = jnp.concatenate([cos, cos, jnp.ones((n, LANE - ROPE_DIM), F32)], axis=1)
    s1 = jnp.concatenate([-sin, jnp.zeros((n, LANE - ROPE_HALF), F32)], axis=1)
    s2 = jnp.concatenate([jnp.zeros((n, ROPE_HALF), F32), sin, jnp.zeros((n, LANE - ROPE_DIM), F32)], axis=1)
    return c, s1, s2


def _inproj_kernel(x_ref, nw_ref, w_ref, o_ref, h_ref):
    @pl.when(pl.program_id(1) == 0)
    def _():
        x = x_ref[...]
        ms = jnp.mean(x * x, axis=-1, keepdims=True)
        h_ref[...] = (x * lax.rsqrt(ms + EPS) * nw_ref[...]).astype(BF16)

    res = jnp.dot(h_ref[...], w_ref[...], preferred_element_type=F32)
    for c in range(o_ref.shape[0]):
        o_ref[c] = res[:, c * LANE:(c + 1) * LANE]


def _inproj(x2, norm_w, w_perm):
    rows, d = x2.shape
    tm, tn = 1024, 1024
    nb = tn // LANE
    return pl.pallas_call(
        _inproj_kernel,
        grid=(rows // tm, IN_BLOCKS // nb),
        in_specs=[pl.BlockSpec((tm, d), lambda i, j: (i, 0)),
                  pl.BlockSpec((1, d), lambda i, j: (0, 0)),
                  pl.BlockSpec((d, tn), lambda i, j: (0, j))],
        out_specs=pl.BlockSpec((nb, tm, LANE), lambda i, j: (j, i, 0)),
        out_shape=jax.ShapeDtypeStruct((IN_BLOCKS, rows, LANE), F32),
        scratch_shapes=[pltpu.VMEM((tm, d), BF16)],
        compiler_params=_cparams(("parallel", "arbitrary")),
        name="inproj",
    )(x2, norm_w.reshape(1, d), w_perm)


def _kprep_kernel(mk_ref, mv_ref, nk_ref, mkw_ref, ksw_ref, kww_ref, c_ref, s1_ref, s2_ref,
                  mk_o, mv_o, nk_o, km_o):
    c, s1, s2 = c_ref[...], s1_ref[...], s2_ref[...]
    means = []
    for h in range(MOBA_HEADS):
        k = _rope(_head_norm(mk_ref[h], mkw_ref[...]), c, s1, s2)
        mk_o[h] = k.astype(BF16)
        mv_o[h] = mv_ref[h].astype(BF16)
        means.append(jnp.mean(k, axis=0, keepdims=True))
    km_o[0] = jnp.concatenate(means, axis=0)
    nk_o[0] = _rope(_head_norm(nk_ref[0], ksw_ref[...]), c, s1, s2).astype(BF16)
    nk_o[1] = nk_ref[1].astype(BF16)
    nk_o[2] = _rope(_head_norm(nk_ref[2], kww_ref[...]), c, s1, s2).astype(BF16)
    nk_o[3] = nk_ref[3].astype(BF16)


def _kprep(proj3, seq, mk_w, ks_w, kw_w, tabs):
    rows = proj3.shape[1]
    t = MOBA_BLOCK
    nt = seq // t
    blk4 = lambda cb: pl.BlockSpec((4, t, LANE), lambda i: (cb // 4, i, 0))
    wspec = pl.BlockSpec((1, LANE), lambda i: (0, 0))
    tspec = pl.BlockSpec((t, LANE), lambda i: (i % nt, 0))
    out4 = pl.BlockSpec((4, t, LANE), lambda i: (0, i, 0))
    return pl.pallas_call(
        _kprep_kernel,
        grid=(rows // t,),
        in_specs=[blk4(CB_MK), blk4(CB_MV), blk4(CB_NKS), wspec, wspec, wspec, tspec, tspec, tspec],
        out_specs=[out4, out4, out4, pl.BlockSpec((1, MOBA_HEADS, LANE), lambda i: (i, 0, 0))],
        out_shape=[jax.ShapeDtypeStruct((4, rows, LANE), BF16),
                   jax.ShapeDtypeStruct((4, rows, LANE), BF16),
                   jax.ShapeDtypeStruct((4, rows, LANE), BF16),
                   jax.ShapeDtypeStruct((rows // t, MOBA_HEADS, LANE), F32)],
        compiler_params=_cparams(("parallel",)),
        name="kprep",
    )(proj3, proj3, proj3, mk_w.reshape(1, LANE), ks_w.reshape(1, LANE), kw_w.reshape(1, LANE), *tabs)


def _flash_step(qb, k, v, allowed, m, l, acc):
    s = jnp.where(allowed, _dot_nt(qb, k) * SCALE, NEG)
    m_new = jnp.maximum(m, jnp.max(s, axis=1, keepdims=True))
    alpha = jnp.exp(m - m_new)
    p = jnp.exp(s - m_new)
    l_new = alpha * l + jnp.sum(p, axis=1, keepdims=True)
    acc_new = alpha * acc + jnp.dot(p.astype(BF16), v, preferred_element_type=F32)
    return m_new, l_new, acc_new


def _moba_kernel(q_ref, z_ref, qw_ref, c_ref, s1_ref, s2_ref, km_ref, k_ref, v_ref, o_ref):
    i = pl.program_id(2)
    t = MOBA_BLOCK
    nb = km_ref.shape[1]
    qf = _rope(_head_norm(q_ref[0], qw_ref[...]), c_ref[...], s1_ref[...], s2_ref[...])
    qb = qf.astype(BF16)

    blk = _iota((t, nb), 1)
    gate = lax.dot_general(qf, km_ref[0], (((1,), (1,)), ((), ())), precision=HIGHEST,
                           preferred_element_type=F32)
    gate = jnp.where(blk < i, gate, -jnp.inf)
    rank = jnp.zeros((t, nb), F32)
    for mm in range(nb):
        cm = gate[:, mm:mm + 1]
        beats = (cm > gate) | ((cm == gate) & (blk > mm))
        rank = rank + jnp.where(beats, 1.0, 0.0)
    sel = jnp.where((rank < MOBA_TOPK) & (blk < i), 1.0, 0.0)

    start = pl.multiple_of(i * t, t)
    causal = _iota((t, t), 1) <= _iota((t, t), 0)
    m0 = jnp.full((t, 1), NEG, F32)
    l0 = jnp.zeros((t, 1), F32)
    a0 = jnp.zeros((t, HEAD_DIM), F32)
    carry = _flash_step(qb, k_ref[0, pl.ds(start, t), :], v_ref[0, pl.ds(start, t), :], causal, m0, l0, a0)

    def body(j, carry):
        selj = jnp.sum(jnp.where(blk == j, sel, 0.0), axis=1, keepdims=True)
        off = pl.multiple_of(j * t, t)
        return _flash_step(qb, k_ref[0, pl.ds(off, t), :], v_ref[0, pl.ds(off, t), :], selj > 0.5, *carry)

    m, l, acc = lax.fori_loop(0, i, body, carry)
    z = z_ref[0]
    o_ref[0] = (acc / l * (z * jax.nn.sigmoid(z))).astype(BF16)


def _moba(proj3, bsz, seq, q_w, tabs, km_t, mk_n, mv_b):
    rows = proj3.shape[1]
    t = MOBA_BLOCK
    nb = seq // t
    qspec = lambda cb: pl.BlockSpec((1, t, LANE), lambda b, h, i: (cb + h, b * nb + i, 0))
    tspec = pl.BlockSpec((t, LANE), lambda b, h, i: (i, 0))
    kvspec = pl.BlockSpec((1, seq, LANE), lambda b, h, i: (h, b, 0))
    return pl.pallas_call(
        _moba_kernel,
        grid=(bsz, MOBA_HEADS, nb),
        in_specs=[qspec(CB_MQ), qspec(CB_MZ), pl.BlockSpec((1, LANE), lambda b, h, i: (0, 0)),
                  tspec, tspec, tspec,
                  pl.BlockSpec((1, nb, LANE), lambda b, h, i: (h, b, 0)), kvspec, kvspec],
        out_specs=pl.BlockSpec((1, t, LANE), lambda b, h, i: (h, b * nb + i, 0)),
        out_shape=jax.ShapeDtypeStruct((MOBA_HEADS, rows, LANE), BF16),
        compiler_params=_cparams(("parallel", "parallel", "parallel")),
        name="moba",
    )(proj3, proj3, q_w.reshape(1, LANE), *tabs, km_t, mk_n, mv_b)


def _cmp_kernel(hk_ref, hv_ref, pek_ref, pev_ref, w1k_ref, w2k_ref, w1v_ref, w2v_ref, nw_ref,
                c_ref, s1_ref, s2_ref, ko_ref, vo_ref):
    half = NSA_CMP_STRIDE * HEAD_DIM

    def compress(h, pe, w1_ref, w2_ref):
        ha = (h + pe[:, :half]).astype(BF16)
        hb = (h + pe[:, half:]).astype(BF16)
        a = jnp.dot(ha, w1_ref[:half, :], preferred_element_type=F32)
        b = jnp.dot(hb, w1_ref[half:, :], preferred_element_type=F32)
        pre = a + pltpu.roll(b, b.shape[0] - 1, 0)
        return jnp.dot(jax.nn.gelu(pre).astype(BF16), w2_ref[...], preferred_element_type=F32)

    kc = compress(hk_ref[0, 0], pek_ref[...], w1k_ref, w2k_ref)
    vc = compress(hv_ref[0, 0], pev_ref[...], w1v_ref, w2v_ref)
    ko_ref[0] = _rope(_head_norm(kc, nw_ref[...]), c_ref[...], s1_ref[...], s2_ref[...]).astype(BF16)
    vo_ref[0] = vc.astype(BF16)


def _compress(proj3, bsz, seq, pe_k, pe_v, w1k, w2k, w1v, w2v, kc_w, ctabs):
    nh = seq // NSA_CMP_STRIDE
    wide = NSA_CMP_STRIDE * HEAD_DIM
    p4 = proj3.reshape(IN_BLOCKS, bsz, nh, wide)
    hspec = lambda cb: pl.BlockSpec((1, 1, nh, wide), lambda b: (cb, b, 0, 0))
    full = lambda shape: pl.BlockSpec(shape, lambda b: tuple(0 for _ in shape))
    ospec = pl.BlockSpec((1, nh, LANE), lambda b: (b, 0, 0))
    return pl.pallas_call(
        _cmp_kernel,
        grid=(bsz,),
        in_specs=[hspec(CB_NKC), hspec(CB_NVC), full((1, 2 * wide)), full((1, 2 * wide)),
                  full((2 * wide, LANE)), full((LANE, LANE)), full((2 * wide, LANE)), full((LANE, LANE)),
                  full((1, LANE)), full((nh, LANE)), full((nh, LANE)), full((nh, LANE))],
        out_specs=[ospec, ospec],
        out_shape=[jax.ShapeDtypeStruct((bsz, nh, LANE), BF16)] * 2,
        compiler_params=_cparams(("parallel",)),
        name="nsa_compress",
    )(p4, p4, pe_k.reshape(1, 2 * wide), pe_v.reshape(1, 2 * wide), w1k.astype(BF16), w2k.astype(BF16),
      w1v.astype(BF16), w2v.astype(BF16), kc_w.reshape(1, LANE), *ctabs)


def _nsa_kernel(q_ref, z_ref, g_ref, qw_ref, c_ref, s1_ref, s2_ref, kc_ref, vc_ref, ov_ref,
                ks_ref, vs_ref, kw_ref, vw_ref, o_ref, qb_s, m_s, l_s, acc_s, ocmp_s, osel_s):
    i = pl.program_id(1)
    t = NSA_TQ
    ncmp = kc_ref.shape[1]
    nsel = ov_ref.shape[1]
    trow = i * t + _iota((t, 1), 0)

    c, s1, s2 = c_ref[...], s1_ref[...], s2_ref[...]
    for h in range(NSA_HEADS):
        qb_s[h] = _rope(_head_norm(q_ref[h], qw_ref[...]), c, s1, s2).astype(BF16)

    valid = (_iota((t, ncmp), 1) * NSA_CMP_STRIDE + (NSA_CMP_LEN - 1)) <= trow
    kc, vc = kc_ref[0], vc_ref[0]
    psum = jnp.zeros((t, ncmp), F32)
    for h in range(NSA_HEADS):
        s = jnp.where(valid, _dot_nt(qb_s[h], kc) * SCALE, NEG)
        e = jnp.exp(s - jnp.max(s, axis=1, keepdims=True))
        p = jnp.where(valid, e / jnp.sum(e, axis=1, keepdims=True), 0.0)
        ocmp_s[h] = jnp.dot(p.astype(BF16), vc, preferred_element_type=F32)
        psum = psum + p

    imp = jnp.dot(psum, ov_ref[...], precision=HIGHEST, preferred_element_type=F32)
    jj = _iota((t, nsel), 1)
    cur = trow >> int(math.log2(NSA_SEL_BLOCK))
    score = jnp.where(jj <= cur, imp, -jnp.inf)
    score = jnp.where((jj == 0) | (jj == cur) | (jj == cur - 1), jnp.inf, score)
    rank = jnp.zeros((t, nsel), F32)
    for mm in range(nsel):
        cm = score[:, mm:mm + 1]
        beats = (cm > score) | ((cm == score) & (jj > mm))
        rank = rank + jnp.where(beats, 1.0, 0.0)
    selb = jnp.where(rank < NSA_SEL_TOPN, 1.0, 0.0).astype(BF16)

    def reset():
        m_s[...] = jnp.full(m_s.shape, NEG, F32)
        l_s[...] = jnp.zeros(l_s.shape, F32)
        acc_s[...] = jnp.zeros(acc_s.shape, F32)

    def attend(k, v, allowed):
        for h in range(NSA_HEADS):
            m, l, acc = _flash_step(qb_s[h], k, v, allowed, m_s[h], l_s[h], acc_s[h])
            m_s[h] = m
            l_s[h] = l
            acc_s[h] = acc

    reset()

    def sel_body(step, carry):
        kt = i - step
        off = pl.multiple_of(kt * t, t)
        expand = jnp.where(_iota((nsel, t), 0) == kt * NSA_SEL_PER_TILE
                           + (_iota((nsel, t), 1) >> int(math.log2(NSA_SEL_BLOCK))), 1.0, 0.0).astype(BF16)
        picked = jnp.dot(selb, expand, preferred_element_type=F32)
        kpos = kt * t + _iota((t, t), 1)
        attend(ks_ref[pl.ds(off, t), :], vs_ref[pl.ds(off, t), :], (picked > 0.5) & (kpos <= trow))
        return carry

    lax.fori_loop(0, i + 1, sel_body, 0)
    for h in range(NSA_HEADS):
        osel_s[h] = acc_s[h] / l_s[h]

    reset()

    def win_body(step, carry):
        kt = i - step
        off = pl.multiple_of(kt * t, t)
        kpos = kt * t + _iota((t, t), 1)
        attend(kw_ref[pl.ds(off, t), :], vw_ref[pl.ds(off, t), :],
               (kpos <= trow) & (kpos > trow - NSA_WINDOW))
        return carry

    lax.fori_loop(0, jnp.minimum(i, NSA_WINDOW // t) + 1, win_body, 0)

    g = jax.nn.sigmoid(g_ref[0])
    for h in range(NSA_HEADS):
        o = (g[:, 3 * h:3 * h + 1] * ocmp_s[h] + g[:, 3 * h + 1:3 * h + 2] * osel_s[h]
             + g[:, 3 * h + 2:3 * h + 3] * (acc_s[h] / l_s[h]))
        z = z_ref[h]
        o_ref[h] = (o * (z * jax.nn.sigmoid(z))).astype(BF16)


def _nsa(proj3, bsz, seq, q_w, tabs, k_cmp, v_cmp, overlap, nk):
    rows = proj3.shape[1]
    t = NSA_TQ
    nt = seq // t
    ncmp = seq // NSA_CMP_STRIDE
    nsel = seq // NSA_SEL_BLOCK
    blk4 = lambda cb: pl.BlockSpec((4, t, LANE), lambda b, i: (cb // 4, b * nt + i, 0))
    tspec = pl.BlockSpec((t, LANE), lambda b, i: (i, 0))
    cspec = pl.BlockSpec((1, ncmp, LANE), lambda b, i: (b, 0, 0))
    kvspec = lambda which: pl.BlockSpec((None, seq, LANE), lambda b, i: (which, b, 0))
    return pl.pallas_call(
        _nsa_kernel,
        grid=(bsz, nt),
        in_specs=[blk4(CB_NQ), blk4(CB_NZ),
                  pl.BlockSpec((1, t, LANE), lambda b, i: (CB_NG, b * nt + i, 0)),
                  pl.BlockSpec((1, LANE), lambda b, i: (0, 0)), tspec, tspec, tspec, cspec, cspec,
                  pl.BlockSpec((ncmp, nsel), lambda b, i: (0, 0)),
                  kvspec(0), kvspec(1), kvspec(2), kvspec(3)],
        out_specs=pl.BlockSpec((4, t, LANE), lambda b, i: (0, b * nt + i, 0)),
        out_shape=jax.ShapeDtypeStruct((NSA_HEADS, rows, LANE), BF16),
        scratch_shapes=[pltpu.VMEM((NSA_HEADS, t, LANE), BF16),
                        pltpu.VMEM((NSA_HEADS, t, 1), F32),
                        pltpu.VMEM((NSA_HEADS, t, 1), F32),
                        pltpu.VMEM((NSA_HEADS, t, LANE), F32),
                        pltpu.VMEM((NSA_HEADS, t, LANE), F32),
                        pltpu.VMEM((NSA_HEADS, t, LANE), F32)],
        compiler_params=_cparams(("parallel", "parallel")),
        name="nsa",
    )(proj3, proj3, proj3, q_w.reshape(1, LANE), *tabs, k_cmp, v_cmp, overlap, nk, nk, nk, nk)


def _s5_weights(a_re, a_im, b_re, b_im, c_re, c_im, log_dt):
    t = S5_CHUNK
    dt = jnp.exp(log_dt.astype(F32))[:, None]
    ar, ai = a_re.astype(F32), a_im.astype(F32)
    ang = dt * ai
    mag = jnp.exp(dt * ar)
    abar_r, abar_i = mag * jnp.cos(ang), mag * jnp.sin(ang)
    nr, ni = abar_r - 1.0, abar_i
    den = ar * ar + ai * ai
    fr = (nr * ar + ni * ai) / den
    fi = (ni * ar - nr * ai) / den
    br, bi = b_re.astype(F32), b_im.astype(F32)
    bbar_r = fr[..., None] * br - fi[..., None] * bi
    bbar_i = fr[..., None] * bi + fi[..., None] * br

    def powers(tau):
        tau = jnp.asarray(tau, F32)[:, None, None]
        pmag = jnp.exp(tau * (dt * ar)[None])
        return pmag * jnp.cos(tau * ang[None]), pmag * jnp.sin(tau * ang[None])

    pw_r, pw_i = powers(np.arange(t + 1))
    cr, ci = c_re.astype(F32), c_im.astype(F32)
    cp_r = cr[None] * pw_r[:, :, None, :] - ci[None] * pw_i[:, :, None, :]
    cp_i = cr[None] * pw_i[:, :, None, :] + ci[None] * pw_r[:, :, None, :]
    klag = (jnp.einsum('tgcp,gpd->tgdc', cp_r[:t], bbar_r, precision=HIGHEST)
            - jnp.einsum('tgcp,gpd->tgdc', cp_i[:t], bbar_i, precision=HIGHEST))
    eye = jnp.eye(S5_GPB, dtype=F32)
    s_idx = np.arange(t)[:, None]
    t_idx = np.arange(t)[None, :]
    lag = np.clip(t_idx - s_idx, 0, t - 1)
    toe = klag[lag] * jnp.asarray((t_idx >= s_idx), F32)[:, :, None, None, None]
    toe = toe.reshape(t, t, S5_MBLK, S5_GPB, S5_GROUP, S5_GROUP)
    m_toe = jnp.einsum('stmgdc,gh->msgdthc', toe, eye).reshape(S5_MBLK, t * LANE, t * LANE)
    rev_r, rev_i = powers(t - 1 - np.arange(t))
    bp_r = rev_r[..., None] * bbar_r[None] - rev_i[..., None] * bbar_i[None]
    bp_i = rev_r[..., None] * bbar_i[None] + rev_i[..., None] * bbar_r[None]
    bp = jnp.stack([bp_r, bp_i]).reshape(2, t, S5_MBLK, S5_GPB, S5_STATE, S5_GROUP)
    w_b = jnp.einsum('rsmgpd,gh->msgdrhp', bp, eye).reshape(S5_MBLK, t * LANE, 2 * S5_HALF)
    cpo = jnp.stack([cp_r[1:], -cp_i[1:]]).reshape(2, t, S5_MBLK, S5_GPB, S5_GROUP, S5_STATE)
    w_c = jnp.einsum('rtmgcp,gh->mrgpthc', cpo, eye).reshape(S5_MBLK, 2 * S5_HALF, t * LANE)
    a_t = jnp.concatenate([pw_r[t].reshape(S5_MBLK, 1, S5_HALF), pw_i[t].reshape(S5_MBLK, 1, S5_HALF)], axis=2)
    w_in = jnp.concatenate([m_toe, w_b], axis=2).astype(BF16)
    return w_in, w_c.astype(BF16), a_t


def _s5_in_kernel(u_ref, w_ref, d_ref, y_ref, b_ref):
    u = u_ref[0]
    res = jnp.dot(u.astype(BF16), w_ref[0], preferred_element_type=F32)
    wide = y_ref.shape[2]
    y_ref[0] = res[:, :wide] + d_ref[0] * u
    b_ref[0] = res[:, wide:]


def _s5_in(proj3, w_in, d_t):
    rows = proj3.shape[1] // S5_CHUNK
    wide = S5_CHUNK * LANE
    p3 = proj3.reshape(IN_BLOCKS, rows, wide)
    tr = min(rows, 512)
    return pl.pallas_call(
        _s5_in_kernel,
        grid=(S5_MBLK, rows // tr),
        in_specs=[pl.BlockSpec((1, tr, wide), lambda m, r: (CB_SU + m, r, 0)),
                  pl.BlockSpec((1, wide, wide + 2 * S5_HALF), lambda m, r: (m, 0, 0)),
                  pl.BlockSpec((1, 1, wide), lambda m, r: (m, 0, 0))],
        out_specs=[pl.BlockSpec((1, tr, wide), lambda m, r: (m, r, 0)),
                   pl.BlockSpec((1, tr, 2 * S5_HALF), lambda m, r: (m, r, 0))],
        out_shape=[jax.ShapeDtypeStruct((S5_MBLK, rows, wide), F32),
                   jax.ShapeDtypeStruct((S5_MBLK, rows, 2 * S5_HALF), F32)],
        compiler_params=_cparams(("parallel", "parallel")),
        name="s5_in",
    )(p3, w_in, d_t)


def _s5_scan_kernel(b_ref, a_ref, o_ref, *, bsz, nk):
    ar = a_ref[0, :, :S5_HALF]
    ai = a_ref[0, :, S5_HALF:]

    def body(k, carry):
        out = []
        for b in range(bsz):
            sr, si = carry[2 * b], carry[2 * b + 1]
            row = b * nk + k
            o_ref[0, pl.ds(row, 1), :] = jnp.concatenate([sr, si], axis=1)
            x = b_ref[0, pl.ds(row, 1), :]
            out.append(ar * sr - ai * si + x[:, :S5_HALF])
            out.append(ar * si + ai * sr + x[:, S5_HALF:])
        return tuple(out)

    zero = jnp.zeros((1, S5_HALF), F32)
    lax.fori_loop(0, nk, body, tuple(zero for _ in range(2 * bsz)))


def _s5_scan(bst, a_t, bsz):
    _, rows, wide = bst.shape
    spec = pl.BlockSpec((1, rows, wide), lambda m: (m, 0, 0))
    return pl.pallas_call(
        functools.partial(_s5_scan_kernel, bsz=bsz, nk=rows // bsz),
        grid=(S5_MBLK,),
        in_specs=[spec, pl.BlockSpec((1, 1, wide), lambda m: (m, 0, 0))],
        out_specs=spec,
        out_shape=jax.ShapeDtypeStruct(bst.shape, F32),
        compiler_params=_cparams(("parallel",)),
        name="s5_scan",
    )(bst, a_t)


def _s5_out_kernel(y_ref, s_ref, w_ref, o_ref):
    y = y_ref[0] + jnp.dot(s_ref[0].astype(BF16), w_ref[0], preferred_element_type=F32)
    o_ref[0] = jax.nn.gelu(y)


def _s5_out(y_intra, s_prev, w_c):
    _, rows, wide = y_intra.shape
    tr = min(rows, 512)
    return pl.pallas_call(
        _s5_out_kernel,
        grid=(S5_MBLK, rows // tr),
        in_specs=[pl.BlockSpec((1, tr, wide), lambda m, r: (m, r, 0)),
                  pl.BlockSpec((1, tr, 2 * S5_HALF), lambda m, r: (m, r, 0)),
                  pl.BlockSpec((1, 2 * S5_HALF, wide), lambda m, r: (m, 0, 0))],
        out_specs=pl.BlockSpec((1, tr, wide), lambda m, r: (m, r, 0)),
        out_shape=jax.ShapeDtypeStruct(y_intra.shape, F32),
        compiler_params=_cparams(("parallel", "parallel")),
        name="s5_out",
    )(y_intra, s_prev, w_c)


def _glu_kernel(y_ref, z_ref, w_ref, o_ref):
    nb = y_ref.shape[0]
    y = jnp.concatenate([y_ref[c] for c in range(nb)], axis=1)
    z = jnp.concatenate([z_ref[c] for c in range(nb)], axis=1)
    gate = jax.nn.sigmoid(jnp.dot(y.astype(BF16), w_ref[...], preferred_element_type=F32))
    o = (y * gate * (z * jax.nn.sigmoid(z))).astype(BF16)
    for c in range(nb):
        o_ref[c] = o[:, c * LANE:(c + 1) * LANE]


def _glu(y5, proj3, glu_w):
    nb, rows, _ = y5.shape
    tm = 512
    spec = lambda blk: pl.BlockSpec((nb, tm, LANE), lambda i: (blk, i, 0))
    return pl.pallas_call(
        _glu_kernel,
        grid=(rows // tm,),
        in_specs=[spec(0), spec(CB_SZ // nb), pl.BlockSpec((S5_WIDTH, S5_WIDTH), lambda i: (0, 0))],
        out_specs=spec(0),
        out_shape=jax.ShapeDtypeStruct((nb, rows, LANE), BF16),
        compiler_params=_cparams(("parallel",)),
        name="s5_glu",
    )(y5, proj3, glu_w.astype(BF16))


def _outproj_kernel(a_ref, b_ref, c_ref, w_ref, x_ref, o_ref):
    parts = ([a_ref[h] for h in range(a_ref.shape[0])] + [b_ref[h] for h in range(b_ref.shape[0])]
             + [c_ref[h] for h in range(c_ref.shape[0])])
    mixed = jnp.concatenate(parts, axis=1)
    o_ref[...] = x_ref[...] + jnp.dot(mixed, w_ref[...], preferred_element_type=F32)


def _outproj(m_moba, m_nsa, m_s5, w_out, x2):
    rows, d = x2.shape
    tm = 512
    lspec = lambda n: pl.BlockSpec((n, tm, LANE), lambda i: (0, i, 0))
    return pl.pallas_call(
        _outproj_kernel,
        grid=(rows // tm,),
        in_specs=[lspec(m_moba.shape[0]), lspec(m_nsa.shape[0]), lspec(m_s5.shape[0]),
                  pl.BlockSpec(w_out.shape, lambda i: (0, 0)),
                  pl.BlockSpec((tm, d), lambda i: (i, 0))],
        out_specs=pl.BlockSpec((tm, d), lambda i: (i, 0)),
        out_shape=jax.ShapeDtypeStruct((rows, d), F32),
        compiler_params=_cparams(("parallel",)),
        name="outproj",
    )(m_moba, m_nsa, m_s5, w_out.astype(BF16), x2)


def _permute_w_in(w_in):
    d = w_in.shape[0]
    seg = lambda a, n: w_in[:, a:a + n]
    mw, nw, kvw = MOBA_HEADS * HEAD_DIM, NSA_HEADS * HEAD_DIM, HEAD_DIM
    o_nq = 4 * mw
    o_kv = o_nq + nw
    o_ng = o_kv + 6 * kvw
    o_nz = o_ng + 3 * NSA_HEADS
    o_su = o_nz + nw
    o_sz = o_su + S5_WIDTH
    cols = [seg(0, 4 * mw), seg(o_nq, nw), seg(o_nz, nw), seg(o_su, S5_WIDTH), seg(o_sz, S5_WIDTH),
            seg(o_kv + 2 * kvw, 4 * kvw), seg(o_kv, 2 * kvw), seg(o_ng, 3 * NSA_HEADS),
            jnp.zeros((d, 2 * LANE - 3 * NSA_HEADS), w_in.dtype)]
    return jnp.concatenate(cols, axis=1).astype(BF16)


def _layer(x2, bsz, seq, tabs, ctabs, overlap, norm_w, w_in, w_out, moba_q_norm, moba_k_norm, nsa_q_norm,
           nsa_kc_norm, nsa_ks_norm, nsa_kw_norm, nsa_pe_k, nsa_pe_v, nsa_cmp_k_w1, nsa_cmp_k_w2,
           nsa_cmp_v_w1, nsa_cmp_v_w2, s5_a_re, s5_a_im, s5_b_re, s5_b_im, s5_c_re, s5_c_im, s5_d,
           s5_log_dt, s5_glu_w):
    proj3 = _inproj(x2, norm_w, _permute_w_in(w_in))
    mk_n, mv_b, nk, kmean = _kprep(proj3, seq, moba_k_norm, nsa_ks_norm, nsa_kw_norm, tabs)
    m_moba = _moba(proj3, bsz, seq, moba_q_norm, tabs, kmean.transpose(1, 0, 2), mk_n, mv_b)
    k_cmp, v_cmp = _compress(proj3, bsz, seq, nsa_pe_k, nsa_pe_v, nsa_cmp_k_w1, nsa_cmp_k_w2,
                             nsa_cmp_v_w1, nsa_cmp_v_w2, nsa_kc_norm, ctabs)
    m_nsa = _nsa(proj3, bsz, seq, nsa_q_norm, tabs, k_cmp, v_cmp, overlap, nk)
    w_s5_in, w_s5_out, a_t = _s5_weights(s5_a_re, s5_a_im, s5_b_re, s5_b_im, s5_c_re, s5_c_im, s5_log_dt)
    d_t = jnp.tile(s5_d.astype(F32).reshape(S5_MBLK, 1, LANE), (1, 1, S5_CHUNK))
    y_intra, bst = _s5_in(proj3, w_s5_in, d_t)
    y5 = _s5_out(y_intra, _s5_scan(bst, a_t, bsz), w_s5_out)
    m_s5 = _glu(y5.reshape(S5_MBLK, x2.shape[0], LANE), proj3, s5_glu_w)
    return _outproj(m_moba, m_nsa, m_s5, w_out, x2)


def kernel(x, norm_w, w_in, w_out, moba_q_norm, moba_k_norm, nsa_q_norm, nsa_kc_norm, nsa_ks_norm, nsa_kw_norm, nsa_pe_k, nsa_pe_v, nsa_cmp_k_w1, nsa_cmp_k_w2, nsa_cmp_v_w1, nsa_cmp_v_w2, s5_a_re, s5_a_im, s5_b_re, s5_b_im, s5_c_re, s5_c_im, s5_d, s5_log_dt, s5_glu_w):
    bsz, seq, d = x.shape
    tabs = _rope_tables(jnp.arange(seq, dtype=F32))
    ncmp = seq // NSA_CMP_STRIDE
    ctabs = _rope_tables(jnp.arange(ncmp, dtype=F32) * NSA_CMP_STRIDE + (NSA_CMP_LEN - 1))
    nsel = seq // NSA_SEL_BLOCK
    ci = np.arange(ncmp)[:, None] * NSA_CMP_STRIDE
    sj = np.arange(nsel)[None, :] * NSA_SEL_BLOCK
    overlap = jnp.asarray(((ci < sj + NSA_SEL_BLOCK) & (ci + NSA_CMP_LEN > sj)).astype(np.float32))
    params = (norm_w, w_in, w_out, moba_q_norm, moba_k_norm, nsa_q_norm, nsa_kc_norm, nsa_ks_norm,
              nsa_kw_norm, nsa_pe_k, nsa_pe_v, nsa_cmp_k_w1, nsa_cmp_k_w2, nsa_cmp_v_w1, nsa_cmp_v_w2,
              s5_a_re, s5_a_im, s5_b_re, s5_b_im, s5_c_re, s5_c_im, s5_d, s5_log_dt, s5_glu_w)
    x2 = x.reshape(bsz * seq, d)
    for layer in range(norm_w.shape[0]):
        x2 = _layer(x2, bsz, seq, tabs, ctabs, overlap, *[p[layer] for p in params])
    return x2.reshape(bsz, seq, d)
```

```python
import functools
import math

import numpy as np
import jax
import jax.numpy as jnp
from jax import lax
from jax.experimental import pallas as pl
from jax.experimental.pallas import tpu as pltpu

F32 = jnp.float32
BF16 = jnp.bfloat16
HIGHEST = lax.Precision.HIGHEST

LANE = 128
HEAD_DIM = 128
ROPE_DIM = HEAD_DIM // 4
ROPE_HALF = ROPE_DIM // 2
ROPE_THETA = 500000.0
EPS = 1e-6
SCALE = HEAD_DIM ** -0.5
NEG = -1e30

MOBA_HEADS = 4
MOBA_BLOCK = 256
MOBA_TOPK = 3

NSA_HEADS = 4
NSA_CMP_LEN = 32
NSA_CMP_STRIDE = 16
NSA_SEL_BLOCK = 64
NSA_SEL_TOPN = 16
NSA_WINDOW = 512
NSA_TQ = 256
NSA_SEL_PER_TILE = NSA_TQ // NSA_SEL_BLOCK
assert NSA_WINDOW == 2 * NSA_TQ

S5_WIDTH = 1024
S5_GROUP = 16
S5_GROUPS = S5_WIDTH // S5_GROUP
S5_STATE = 64
S5_CHUNK = 16
S5_MBLK = S5_WIDTH // LANE
S5_GPB = LANE // S5_GROUP
S5_HALF = S5_GPB * S5_STATE

CB_MQ, CB_MK, CB_MV, CB_MZ, CB_NQ, CB_NZ, CB_SU, CB_SZ = 0, 4, 8, 12, 16, 20, 24, 32
CB_NKS, CB_NVS, CB_NKW, CB_NVW, CB_NKC, CB_NVC, CB_NG = 40, 41, 42, 43, 44, 45, 46
IN_BLOCKS = 48

VMEM_LIMIT = 56 * 1024 * 1024


def _cparams(sem):
    return pltpu.CompilerParams(dimension_semantics=sem, vmem_limit_bytes=VMEM_LIMIT)


def _iota(shape, dim):
    return lax.broadcasted_iota(jnp.int32, shape, dim)


def _head_norm(x, w):
    return x * lax.rsqrt(jnp.mean(x * x, axis=-1, keepdims=True) + EPS) * w


def _rope(x, c, s1, s2):
    return x * c + pltpu.roll(x, LANE - ROPE_HALF, 1) * s1 + pltpu.roll(x, ROPE_HALF, 1) * s2


def _dot_nt(a, b):
    return lax.dot_general(a, b, (((1,), (1,)), ((), ())), preferred_element_type=F32)


def _rope_tables(pos):
    inv = ROPE_THETA ** (-jnp.arange(0, ROPE_DIM, 2, dtype=F32) / ROPE_DIM)
    ang = pos.astype(F32)[:, None] * inv[None, :]
    cos, sin = jnp.cos(ang), jnp.sin(ang)
    n = pos.shape[0]
    c = jnp.concatenate([cos, cos, jnp.ones((n, LANE - ROPE_DIM), F32)], axis=1)
    s1 = jnp.concatenate([-sin, jnp.zeros((n, LANE - ROPE_HALF), F32)], axis=1)
    s2 = jnp.concatenate([jnp.zeros((n, ROPE_HALF), F32), sin, jnp.zeros((n, LANE - ROPE_DIM), F32)], axis=1)
    return c, s1, s2


def _inproj_kernel(x_ref, nw_ref, w_ref, o_ref, h_ref):
    @pl.when(pl.program_id(1) == 0)
    def _():
        x = x_ref[...]
        ms = jnp.mean(x * x, axis=-1, keepdims=True)
        h_ref[...] = (x * lax.rsqrt(ms + EPS) * nw_ref[...]).astype(BF16)

    res = jnp.dot(h_ref[...], w_ref[...], preferred_element_type=F32)
    for c in range(o_ref.shape[0]):
        o_ref[c] = res[:, c * LANE:(c + 1) * LANE]


def _inproj(x2, norm_w, w_perm):
    rows, d = x2.shape
    tm, tn = 1024, 1024
    nb = tn // LANE
    return pl.pallas_call(
        _inproj_kernel,
        grid=(rows // tm, IN_BLOCKS // nb),
        in_specs=[pl.BlockSpec((tm, d), lambda i, j: (i, 0)),
                  pl.BlockSpec((1, d), lambda i, j: (0, 0)),
                  pl.BlockSpec((d, tn), lambda i, j: (0, j))],
        out_specs=pl.BlockSpec((nb, tm, LANE), lambda i, j: (j, i, 0)),
        out_shape=jax.ShapeDtypeStruct((IN_BLOCKS, rows, LANE), F32),
        scratch_shapes=[pltpu.VMEM((tm, d), BF16)],
        compiler_params=_cparams(("parallel", "arbitrary")),
        name="inproj",
    )(x2, norm_w.reshape(1, d), w_perm)


def _kprep_kernel(mk_ref, mv_ref, nk_ref, mkw_ref, ksw_ref, kww_ref, c_ref, s1_ref, s2_ref,
                  mk_o, mv_o, nk_o, km_o):
    c, s1, s2 = c_ref[...], s1_ref[...], s2_ref[...]
    means = []
    for h in range(MOBA_HEADS):
        k = _rope(_head_norm(mk_ref[h], mkw_ref[...]), c, s1, s2)
        mk_o[h] = k.astype(BF16)
        mv_o[h] = mv_ref[h].astype(BF16)
        means.append(jnp.mean(k, axis=0, keepdims=True))
    km_o[0] = jnp.concatenate(means, axis=0)
    nk_o[0] = _rope(_head_norm(nk_ref[0], ksw_ref[...]), c, s1, s2).astype(BF16)
    nk_o[1] = nk_ref[1].astype(BF16)
    nk_o[2] = _rope(_head_norm(nk_ref[2], kww_ref[...]), c, s1, s2).astype(BF16)
    nk_o[3] = nk_ref[3].astype(BF16)


def _kprep(proj3, seq, mk_w, ks_w, kw_w, tabs):
    rows = proj3.shape[1]
    t = MOBA_BLOCK
    nt = seq // t
    blk4 = lambda cb: pl.BlockSpec((4, t, LANE), lambda i: (cb // 4, i, 0))
    wspec = pl.BlockSpec((1, LANE), lambda i: (0, 0))
    tspec = pl.BlockSpec((t, LANE), lambda i: (i % nt, 0))
    out4 = pl.BlockSpec((4, t, LANE), lambda i: (0, i, 0))
    return pl.pallas_call(
        _kprep_kernel,
        grid=(rows // t,),
        in_specs=[blk4(CB_MK), blk4(CB_MV), blk4(CB_NKS), wspec, wspec, wspec, tspec, tspec, tspec],
        out_specs=[out4, out4, out4, pl.BlockSpec((1, MOBA_HEADS, LANE), lambda i: (i, 0, 0))],
        out_shape=[jax.ShapeDtypeStruct((4, rows, LANE), BF16),
                   jax.ShapeDtypeStruct((4, rows, LANE), BF16),
                   jax.ShapeDtypeStruct((4, rows, LANE), BF16),
                   jax.ShapeDtypeStruct((rows // t, MOBA_HEADS, LANE), F32)],
        compiler_params=_cparams(("parallel",)),
        name="kprep",
    )(proj3, proj3, proj3, mk_w.reshape(1, LANE), ks_w.reshape(1, LANE), kw_w.reshape(1, LANE), *tabs)


def _lanes(x, n):
    return x if n == LANE else jnp.concatenate([x] * (n // LANE), axis=1)


def _softmax_part(s, m):
    m_new = jnp.maximum(m, jnp.max(s, axis=1, keepdims=True))
    return m_new, jnp.exp(s - _lanes(m_new, s.shape[1])), jnp.exp(m - m_new)


def _flash_step(s, v, m, l, acc):
    m_new, p, alpha = _softmax_part(s, m)
    l_new = alpha * l + jnp.sum(p, axis=1, keepdims=True)
    acc_new = alpha * acc + jnp.dot(p.astype(BF16), v, preferred_element_type=F32)
    return m_new, l_new, acc_new


def _pad_rows(x, rows):
    return x if x.shape[0] == rows else jnp.concatenate(
        [x, jnp.zeros((rows - x.shape[0], x.shape[1]), x.dtype)], axis=0)


def _topk_mask_t(score_t, row_idx, k):
    n = score_t.shape[0]
    rank = jnp.zeros(score_t.shape, F32)
    for mm in range(n):
        cm = score_t[mm:mm + 1, :]
        beats = (cm > score_t) | ((cm == score_t) & (row_idx > mm))
        rank = rank + jnp.where(beats, 1.0, 0.0)
    return rank < k


def _moba_kernel(q_ref, z_ref, qw_ref, c_ref, s1_ref, s2_ref, km_ref, k_ref, v_ref, o_ref):
    i = pl.program_id(2)
    t = MOBA_BLOCK
    nb = km_ref.shape[1]
    qf = _rope(_head_norm(q_ref[0], qw_ref[...]), c_ref[...], s1_ref[...], s2_ref[...])
    qb = (qf * SCALE).astype(BF16)

    gate_t = lax.dot_general(_pad_rows(km_ref[0], LANE), qf, (((1,), (1,)), ((), ())), precision=HIGHEST,
                             preferred_element_type=F32)[:nb]
    blk_t = _iota((nb, t), 0)
    past = blk_t < i
    top = _topk_mask_t(jnp.where(past, gate_t, -jnp.inf), blk_t, MOBA_TOPK)
    sel_t = jnp.where(top & past, 1.0, 0.0)
    selb = _pad_rows(sel_t, LANE).T.astype(BF16)

    def picked(j):
        onehot = jnp.where(_iota((LANE, LANE), 0) == j, 1.0, 0.0).astype(BF16)
        return jnp.dot(selb, onehot, preferred_element_type=F32)

    start = pl.multiple_of(i * t, t)
    s = jnp.where(_iota((t, t), 1) <= _iota((t, t), 0), _dot_nt(qb, k_ref[0, pl.ds(start, t), :]), NEG)
    m = jnp.max(s, axis=1, keepdims=True) + jnp.zeros((t, LANE), F32)
    p = jnp.exp(s - _lanes(m, t))
    l = jnp.sum(p, axis=1, keepdims=True) + jnp.zeros((t, LANE), F32)
    acc = jnp.dot(p.astype(BF16), v_ref[0, pl.ds(start, t), :], preferred_element_type=F32)

    def body(pair, carry):
        m, l, acc = carry
        off = pl.multiple_of(pair * (2 * t), 2 * t)
        m_new, p, alpha = _softmax_part(_dot_nt(qb, k_ref[0, pl.ds(off, 2 * t), :]), m)
        ea, eb = picked(2 * pair), picked(2 * pair + 1)
        pa, pb = p[:, :t], p[:, t:]
        l_new = alpha * l + ea * jnp.sum(pa, axis=1, keepdims=True) + eb * jnp.sum(pb, axis=1, keepdims=True)
        acc_new = (alpha * acc
                   + ea * jnp.dot(pa.astype(BF16), v_ref[0, pl.ds(off, t), :], preferred_element_type=F32)
                   + eb * jnp.dot(pb.astype(BF16), v_ref[0, pl.ds(pl.multiple_of(off + t, t), t), :],
                                  preferred_element_type=F32))
        return m_new, l_new, acc_new

    m, l, acc = lax.fori_loop(0, (i + 1) // 2, body, (m, l, acc))
    z = z_ref[0]
    o_ref[0] = (acc / l * (z * jax.nn.sigmoid(z))).astype(BF16)


def _moba(proj3, bsz, seq, q_w, tabs, km_t, mk_n, mv_b):
    rows = proj3.shape[1]
    t = MOBA_BLOCK
    nb = seq // t
    qspec = lambda cb: pl.BlockSpec((1, t, LANE), lambda b, h, i: (cb + h, b * nb + i, 0))
    tspec = pl.BlockSpec((t, LANE), lambda b, h, i: (i, 0))
    kvspec = pl.BlockSpec((1, seq, LANE), lambda b, h, i: (h, b, 0))
    return pl.pallas_call(
        _moba_kernel,
        grid=(bsz, MOBA_HEADS, nb),
        in_specs=[qspec(CB_MQ), qspec(CB_MZ), pl.BlockSpec((1, LANE), lambda b, h, i: (0, 0)),
                  tspec, tspec, tspec,
                  pl.BlockSpec((1, nb, LANE), lambda b, h, i: (h, b, 0)), kvspec, kvspec],
        out_specs=pl.BlockSpec((1, t, LANE), lambda b, h, i: (h, b * nb + i, 0)),
        out_shape=jax.ShapeDtypeStruct((MOBA_HEADS, rows, LANE), BF16),
        compiler_params=_cparams(("parallel", "parallel", "parallel")),
        name="moba",
    )(proj3, proj3, q_w.reshape(1, LANE), *tabs, km_t, mk_n, mv_b)


def _cmp_kernel(hk_ref, hv_ref, pek_ref, pev_ref, w1k_ref, w2k_ref, w1v_ref, w2v_ref, nw_ref,
                c_ref, s1_ref, s2_ref, ko_ref, vo_ref):
    half = NSA_CMP_STRIDE * HEAD_DIM

    def compress(h, pe, w1_ref, w2_ref):
        ha = (h + pe[:, :half]).astype(BF16)
        hb = (h + pe[:, half:]).astype(BF16)
        a = jnp.dot(ha, w1_ref[:half, :], preferred_element_type=F32)
        b = jnp.dot(hb, w1_ref[half:, :], preferred_element_type=F32)
        pre = a + pltpu.roll(b, b.shape[0] - 1, 0)
        return jnp.dot(jax.nn.gelu(pre).astype(BF16), w2_ref[...], preferred_element_type=F32)

    kc = compress(hk_ref[0, 0], pek_ref[...], w1k_ref, w2k_ref)
    vc = compress(hv_ref[0, 0], pev_ref[...], w1v_ref, w2v_ref)
    ko_ref[0] = _rope(_head_norm(kc, nw_ref[...]), c_ref[...], s1_ref[...], s2_ref[...]).astype(BF16)
    vo_ref[0] = vc.astype(BF16)


def _compress(proj3, bsz, seq, pe_k, pe_v, w1k, w2k, w1v, w2v, kc_w, ctabs):
    nh = seq // NSA_CMP_STRIDE
    wide = NSA_CMP_STRIDE * HEAD_DIM
    p4 = proj3.reshape(IN_BLOCKS, bsz, nh, wide)
    hspec = lambda cb: pl.BlockSpec((1, 1, nh, wide), lambda b: (cb, b, 0, 0))
    full = lambda shape: pl.BlockSpec(shape, lambda b: tuple(0 for _ in shape))
    ospec = pl.BlockSpec((1, nh, LANE), lambda b: (b, 0, 0))
    return pl.pallas_call(
        _cmp_kernel,
        grid=(bsz,),
        in_specs=[hspec(CB_NKC), hspec(CB_NVC), full((1, 2 * wide)), full((1, 2 * wide)),
                  full((2 * wide, LANE)), full((LANE, LANE)), full((2 * wide, LANE)), full((LANE, LANE)),
                  full((1, LANE)), full((nh, LANE)), full((nh, LANE)), full((nh, LANE))],
        out_specs=[ospec, ospec],
        out_shape=[jax.ShapeDtypeStruct((bsz, nh, LANE), BF16)] * 2,
        compiler_params=_cparams(("parallel",)),
        name="nsa_compress",
    )(p4, p4, pe_k.reshape(1, 2 * wide), pe_v.reshape(1, 2 * wide), w1k.astype(BF16), w2k.astype(BF16),
      w1v.astype(BF16), w2v.astype(BF16), kc_w.reshape(1, LANE), *ctabs)


def _nsa_kernel(q_ref, z_ref, g_ref, qw_ref, c_ref, s1_ref, s2_ref, kc_ref, vc_ref, ov_ref,
                ks_ref, vs_ref, kw_ref, vw_ref, o_ref, qb_s, m_s, l_s, acc_s, ocmp_s, osel_s):
    i = pl.program_id(1)
    t = NSA_TQ
    ncmp = kc_ref.shape[1]
    nsel = ks_ref.shape[0] // NSA_SEL_BLOCK
    sel_shift = int(math.log2(NSA_SEL_BLOCK))
    trow = i * t + _iota((t, 1), 0)
    row = _iota((t, t), 0)
    col = _iota((t, t), 1)
    causal = col <= row

    c, s1, s2 = c_ref[...], s1_ref[...], s2_ref[...]
    for h in range(NSA_HEADS):
        qb_s[h] = (_rope(_head_norm(q_ref[h], qw_ref[...]), c, s1, s2) * SCALE).astype(BF16)

    valid = (_iota((t, ncmp), 1) * NSA_CMP_STRIDE + (NSA_CMP_LEN - 1)) <= trow
    kc, vc = kc_ref[0], vc_ref[0]
    psum = jnp.zeros((t, ncmp), F32)
    for h in range(NSA_HEADS):
        s = jnp.where(valid, _dot_nt(qb_s[h], kc), NEG)
        e = jnp.exp(s - jnp.max(s, axis=1, keepdims=True))
        p = jnp.where(valid, e / jnp.sum(e, axis=1, keepdims=True), 0.0)
        ocmp_s[h] = jnp.dot(p.astype(BF16), vc, preferred_element_type=F32)
        psum = psum + p

    imp_t = jnp.dot(psum, ov_ref[...], precision=HIGHEST, preferred_element_type=F32).T[:nsel]
    jj = _iota((nsel, t), 0)
    cur = (i * t + _iota((1, t), 1)) >> sel_shift
    score = jnp.where(jj <= cur, imp_t, -jnp.inf)
    score = jnp.where((jj == 0) | (jj == cur) | (jj == cur - 1), jnp.inf, score)
    top = _topk_mask_t(score, jj, NSA_SEL_TOPN)
    selbias = _pad_rows(jnp.where(top, 0.0, NEG), LANE).T.astype(BF16)

    def block_bias(kt):
        onehot = jnp.where(_iota((LANE, t), 0) == kt * NSA_SEL_PER_TILE + (_iota((LANE, t), 1) >> sel_shift),
                           1.0, 0.0).astype(BF16)
        return jnp.dot(selbias, onehot, preferred_element_type=F32)

    own = pl.multiple_of(i * t, t)
    bias = block_bias(i)
    k, v = ks_ref[pl.ds(own, t), :], vs_ref[pl.ds(own, t), :]
    for h in range(NSA_HEADS):
        s = jnp.where(causal, _dot_nt(qb_s[h], k) + bias, NEG)
        m = jnp.max(s, axis=1, keepdims=True) + jnp.zeros((t, LANE), F32)
        p = jnp.exp(s - _lanes(m, t))
        m_s[h] = m
        l_s[h] = jnp.sum(p, axis=1, keepdims=True) + jnp.zeros((t, LANE), F32)
        acc_s[h] = jnp.dot(p.astype(BF16), v, preferred_element_type=F32)

    def sel_body(kt, carry):
        off = pl.multiple_of(kt * t, t)
        bias = block_bias(kt)
        k, v = ks_ref[pl.ds(off, t), :], vs_ref[pl.ds(off, t), :]
        for h in range(NSA_HEADS):
            m, l, acc = _flash_step(_dot_nt(qb_s[h], k) + bias, v, m_s[h], l_s[h], acc_s[h])
            m_s[h] = m
            l_s[h] = l
            acc_s[h] = acc
        return carry

    lax.fori_loop(0, i, sel_body, 0)
    for h in range(NSA_HEADS):
        osel_s[h] = acc_s[h] / l_s[h]

    far = pl.multiple_of(jnp.maximum(i - 2, 0) * t, t)
    mid = pl.multiple_of(jnp.maximum(i - 1, 0) * t, t)
    k3 = jnp.concatenate([kw_ref[pl.ds(far, t), :], kw_ref[pl.ds(mid, t), :], kw_ref[pl.ds(own, t), :]], axis=0)
    v3 = jnp.concatenate([vw_ref[pl.ds(far, t), :], vw_ref[pl.ds(mid, t), :], vw_ref[pl.ds(own, t), :]], axis=0)
    allowed = jnp.concatenate([(col > row) & (i >= 2), jnp.full((t, t), True) & (i >= 1), causal], axis=1)
    g = jax.nn.sigmoid(g_ref[0])
    for h in range(NSA_HEADS):
        s = jnp.where(allowed, _dot_nt(qb_s[h], k3), NEG)
        p = jnp.exp(s - jnp.max(s, axis=1, keepdims=True))
        o_win = jnp.dot(p.astype(BF16), v3, preferred_element_type=F32) / jnp.sum(p, axis=1, keepdims=True)
        o = (g[:, 3 * h:3 * h + 1] * ocmp_s[h] + g[:, 3 * h + 1:3 * h + 2] * osel_s[h]
             + g[:, 3 * h + 2:3 * h + 3] * o_win)
        z = z_ref[h]
        o_ref[h] = (o * (z * jax.nn.sigmoid(z))).astype(BF16)


def _nsa(proj3, bsz, seq, q_w, tabs, k_cmp, v_cmp, overlap, nk):
    rows = proj3.shape[1]
    t = NSA_TQ
    nt = seq // t
    ncmp = seq // NSA_CMP_STRIDE
    nsel = seq // NSA_SEL_BLOCK
    blk4 = lambda cb: pl.BlockSpec((4, t, LANE), lambda b, i: (cb // 4, b * nt + i, 0))
    tspec = pl.BlockSpec((t, LANE), lambda b, i: (i, 0))
    cspec = pl.BlockSpec((1, ncmp, LANE), lambda b, i: (b, 0, 0))
    kvspec = lambda which: pl.BlockSpec((None, seq, LANE), lambda b, i: (which, b, 0))
    return pl.pallas_call(
        _nsa_kernel,
        grid=(bsz, nt),
        in_specs=[blk4(CB_NQ), blk4(CB_NZ),
                  pl.BlockSpec((1, t, LANE), lambda b, i: (CB_NG, b * nt + i, 0)),
                  pl.BlockSpec((1, LANE), lambda b, i: (0, 0)), tspec, tspec, tspec, cspec, cspec,
                  pl.BlockSpec((ncmp, LANE), lambda b, i: (0, 0)),
                  kvspec(0), kvspec(1), kvspec(2), kvspec(3)],
        out_specs=pl.BlockSpec((4, t, LANE), lambda b, i: (0, b * nt + i, 0)),
        out_shape=jax.ShapeDtypeStruct((NSA_HEADS, rows, LANE), BF16),
        scratch_shapes=[pltpu.VMEM((NSA_HEADS, t, LANE), BF16),
                        pltpu.VMEM((NSA_HEADS, t, LANE), F32),
                        pltpu.VMEM((NSA_HEADS, t, LANE), F32),
                        pltpu.VMEM((NSA_HEADS, t, LANE), F32),
                        pltpu.VMEM((NSA_HEADS, t, LANE), F32),
                        pltpu.VMEM((NSA_HEADS, t, LANE), F32)],
        compiler_params=_cparams(("parallel", "parallel")),
        name="nsa",
    )(proj3, proj3, proj3, q_w.reshape(1, LANE), *tabs, k_cmp, v_cmp, overlap, nk, nk, nk, nk)


def _s5_weights(a_re, a_im, b_re, b_im, c_re, c_im, log_dt):
    t = S5_CHUNK
    dt = jnp.exp(log_dt.astype(F32))[:, None]
    ar, ai = a_re.astype(F32), a_im.astype(F32)
    ang = dt * ai
    mag = jnp.exp(dt * ar)
    abar_r, abar_i = mag * jnp.cos(ang), mag * jnp.sin(ang)
    nr, ni = abar_r - 1.0, abar_i
    den = ar * ar + ai * ai
    fr = (nr * ar + ni * ai) / den
    fi = (ni * ar - nr * ai) / den
    bt_r, bt_i = b_re.astype(F32).transpose(0, 2, 1), b_im.astype(F32).transpose(0, 2, 1)
    bbar_r = fr[:, None, :] * bt_r - fi[:, None, :] * bt_i
    bbar_i = fr[:, None, :] * bt_i + fi[:, None, :] * bt_r

    def powers(tau):
        tau = jnp.asarray(tau, F32)[:, None, None]
        pmag = jnp.exp(tau * (dt * ar)[None])
        return pmag * jnp.cos(tau * ang[None]), pmag * jnp.sin(tau * ang[None])

    pw_r, pw_i = powers(np.arange(t + 1))
    cr, ci = c_re.astype(F32), c_im.astype(F32)
    cp_r = cr[None] * pw_r[:, :, None, :] - ci[None] * pw_i[:, :, None, :]
    cp_i = cr[None] * pw_i[:, :, None, :] + ci[None] * pw_r[:, :, None, :]
    klag = (jnp.einsum('tgcp,gdp->tgdc', cp_r[:t], bbar_r, precision=HIGHEST)
            - jnp.einsum('tgcp,gdp->tgdc', cp_i[:t], bbar_i, precision=HIGHEST))
    kc = (klag.reshape(t, S5_MBLK, S5_GPB, S5_GROUP, S5_GROUP).transpose(1, 0, 2, 3, 4)
          .reshape(S5_MBLK, t * LANE, S5_GROUP))
    rev_r, rev_i = powers(t - 1 - np.arange(t))
    bp_r = rev_r[:, :, None, :] * bbar_r[None] - rev_i[:, :, None, :] * bbar_i[None]
    bp_i = rev_r[:, :, None, :] * bbar_i[None] + rev_i[:, :, None, :] * bbar_r[None]
    bpc = (jnp.stack([bp_r, bp_i], axis=3).reshape(t, S5_MBLK, LANE, 2 * S5_STATE).transpose(1, 0, 2, 3)
           .reshape(S5_MBLK, t * LANE, 2 * S5_STATE))
    cpo = jnp.stack([cp_r[1:], -cp_i[1:]]).reshape(2, t, S5_MBLK, S5_GPB, S5_GROUP, S5_STATE)
    cpc = cpo.transpose(2, 0, 3, 5, 1, 4).reshape(S5_MBLK, 2 * S5_HALF, t * S5_GROUP)
    a_t = jnp.concatenate([pw_r[t].reshape(S5_MBLK, 1, S5_HALF), pw_i[t].reshape(S5_MBLK, 1, S5_HALF)], axis=2)
    w_in, w_c = _s5_wgen(kc, bpc, cpc)
    return w_in, w_c, a_t


def _s5_wgen_kernel(kc_ref, bp_ref, cp_ref, win_ref, wc_ref):
    t = S5_CHUNK
    wide = t * LANE
    gs, ps = int(math.log2(S5_GROUP)), int(math.log2(S5_STATE))

    def expand(x, copy_mask, group_mask):
        ex = jnp.where(copy_mask, 1.0, 0.0).astype(BF16)
        return jnp.where(group_mask, jnp.dot(x.astype(BF16), ex, preferred_element_type=F32), 0.0).astype(BF16)

    r, c = _iota((S5_GROUP, LANE), 0), _iota((S5_GROUP, LANE), 1)
    copy_k = r == (c & (S5_GROUP - 1))
    r, c = _iota((LANE, LANE), 0), _iota((LANE, LANE), 1)
    same_k = (r >> gs) == (c >> gs)
    zero = jnp.zeros((LANE, LANE), BF16)
    for lag in range(t):
        bd = expand(kc_ref[0, lag * LANE:(lag + 1) * LANE, :], copy_k, same_k)
        for s in range(t - lag):
            win_ref[0, s * LANE:(s + 1) * LANE, (s + lag) * LANE:(s + lag + 1) * LANE] = bd
    for s in range(1, t):
        for tt in range(s):
            win_ref[0, s * LANE:(s + 1) * LANE, tt * LANE:(tt + 1) * LANE] = zero
    r, c = _iota((LANE, 2 * S5_HALF), 0), _iota((LANE, 2 * S5_HALF), 1)
    copy_b = ((r >> ps) == (c >> (ps + 3))) & ((r & (S5_STATE - 1)) == (c & (S5_STATE - 1)))
    same_b = (r >> gs) == ((c >> ps) & (S5_GPB - 1))
    for s in range(t):
        win_ref[0, s * LANE:(s + 1) * LANE, wide:] = expand(bp_ref[0, s * LANE:(s + 1) * LANE, :],
                                                            copy_b, same_b)
    rows = 2 * S5_STATE * 2
    r, c = _iota((t * S5_GROUP, wide), 0), _iota((t * S5_GROUP, wide), 1)
    copy_c = ((r >> gs) == (c >> (gs + 3))) & ((r & (S5_GROUP - 1)) == (c & (S5_GROUP - 1)))
    for ch in range(2 * S5_HALF // rows):
        r, c = ch * rows + _iota((rows, wide), 0), _iota((rows, wide), 1)
        same_c = ((r >> ps) & (S5_GPB - 1)) == ((c >> gs) & (S5_GPB - 1))
        wc_ref[0, ch * rows:(ch + 1) * rows, :] = expand(cp_ref[0, ch * rows:(ch + 1) * rows, :],
                                                         copy_c, same_c)


def _s5_wgen(kc, bpc, cpc):
    t = S5_CHUNK
    wide = t * LANE
    spec = lambda a: pl.BlockSpec((1,) + a.shape[1:], lambda m: (m, 0, 0))
    return pl.pallas_call(
        _s5_wgen_kernel,
        grid=(S5_MBLK,),
        in_specs=[spec(kc), spec(bpc), spec(cpc)],
        out_specs=[pl.BlockSpec((1, wide, wide + 2 * S5_HALF), lambda m: (m, 0, 0)),
                   pl.BlockSpec((1, 2 * S5_HALF, wide), lambda m: (m, 0, 0))],
        out_shape=[jax.ShapeDtypeStruct((S5_MBLK, wide, wide + 2 * S5_HALF), BF16),
                   jax.ShapeDtypeStruct((S5_MBLK, 2 * S5_HALF, wide), BF16)],
        compiler_params=_cparams(("parallel",)),
        name="s5_wgen",
    )(kc, bpc, cpc)


def _s5_in_kernel(u_ref, w_ref, d_ref, y_ref, b_ref):
    u = u_ref[0]
    res = jnp.dot(u.astype(BF16), w_ref[0], preferred_element_type=F32)
    wide = y_ref.shape[2]
    y_ref[0] = res[:, :wide] + d_ref[0] * u
    b_ref[0] = res[:, wide:]


def _s5_in(proj3, w_in, d_t):
    rows = proj3.shape[1] // S5_CHUNK
    wide = S5_CHUNK * LANE
    p3 = proj3.reshape(IN_BLOCKS, rows, wide)
    tr = min(rows, 512)
    return pl.pallas_call(
        _s5_in_kernel,
        grid=(S5_MBLK, rows // tr),
        in_specs=[pl.BlockSpec((1, tr, wide), lambda m, r: (CB_SU + m, r, 0)),
                  pl.BlockSpec((1, wide, wide + 2 * S5_HALF), lambda m, r: (m, 0, 0)),
                  pl.BlockSpec((1, 1, wide), lambda m, r: (m, 0, 0))],
        out_specs=[pl.BlockSpec((1, tr, wide), lambda m, r: (m, r, 0)),
                   pl.BlockSpec((1, tr, 2 * S5_HALF), lambda m, r: (m, r, 0))],
        out_shape=[jax.ShapeDtypeStruct((S5_MBLK, rows, wide), F32),
                   jax.ShapeDtypeStruct((S5_MBLK, rows, 2 * S5_HALF), F32)],
        compiler_params=_cparams(("parallel", "parallel")),
        name="s5_in",
    )(p3, w_in, d_t)


def _s5_scan_kernel(b_ref, a_ref, o_ref, *, bsz, nk):
    ar = a_ref[0, :, :S5_HALF]
    ai = a_ref[0, :, S5_HALF:]

    def body(k, carry):
        out = []
        for b in range(bsz):
            sr, si = carry[2 * b], carry[2 * b + 1]
            row = b * nk + k
            o_ref[0, pl.ds(row, 1), :] = jnp.concatenate([sr, si], axis=1)
            x = b_ref[0, pl.ds(row, 1), :]
            out.append(ar * sr - ai * si + x[:, :S5_HALF])
            out.append(ar * si + ai * sr + x[:, S5_HALF:])
        return tuple(out)

    zero = jnp.zeros((1, S5_HALF), F32)
    lax.fori_loop(0, nk, body, tuple(zero for _ in range(2 * bsz)))


def _s5_scan(bst, a_t, bsz):
    _, rows, wide = bst.shape
    spec = pl.BlockSpec((1, rows, wide), lambda m: (m, 0, 0))
    return pl.pallas_call(
        functools.partial(_s5_scan_kernel, bsz=bsz, nk=rows // bsz),
        grid=(S5_MBLK,),
        in_specs=[spec, pl.BlockSpec((1, 1, wide), lambda m: (m, 0, 0))],
        out_specs=spec,
        out_shape=jax.ShapeDtypeStruct(bst.shape, F32),
        compiler_params=_cparams(("parallel",)),
        name="s5_scan",
    )(bst, a_t)


def _s5_out_kernel(y_ref, s_ref, w_ref, o_ref):
    y = y_ref[0] + jnp.dot(s_ref[0].astype(BF16), w_ref[0], preferred_element_type=F32)
    o_ref[0] = jax.nn.gelu(y)


def _s5_out(y_intra, s_prev, w_c):
    _, rows, wide = y_intra.shape
    tr = min(rows, 512)
    return pl.pallas_call(
        _s5_out_kernel,
        grid=(S5_MBLK, rows // tr),
        in_specs=[pl.BlockSpec((1, tr, wide), lambda m, r: (m, r, 0)),
                  pl.BlockSpec((1, tr, 2 * S5_HALF), lambda m, r: (m, r, 0)),
                  pl.BlockSpec((1, 2 * S5_HALF, wide), lambda m, r: (m, 0, 0))],
        out_specs=pl.BlockSpec((1, tr, wide), lambda m, r: (m, r, 0)),
        out_shape=jax.ShapeDtypeStruct(y_intra.shape, F32),
        compiler_params=_cparams(("parallel", "parallel")),
        name="s5_out",
    )(y_intra, s_prev, w_c)


def _glu_kernel(y_ref, z_ref, w_ref, o_ref):
    nb = y_ref.shape[0]
    y = jnp.concatenate([y_ref[c] for c in range(nb)], axis=1)
    z = jnp.concatenate([z_ref[c] for c in range(nb)], axis=1)
    gate = jax.nn.sigmoid(jnp.dot(y.astype(BF16), w_ref[...], preferred_element_type=F32))
    o = (y * gate * (z * jax.nn.sigmoid(z))).astype(BF16)
    for c in range(nb):
        o_ref[c] = o[:, c * LANE:(c + 1) * LANE]


def _glu(y5, proj3, glu_w):
    nb, rows, _ = y5.shape
    tm = 512
    spec = lambda blk: pl.BlockSpec((nb, tm, LANE), lambda i: (blk, i, 0))
    return pl.pallas_call(
        _glu_kernel,
        grid=(rows // tm,),
        in_specs=[spec(0), spec(CB_SZ // nb), pl.BlockSpec((S5_WIDTH, S5_WIDTH), lambda i: (0, 0))],
        out_specs=spec(0),
        out_shape=jax.ShapeDtypeStruct((nb, rows, LANE), BF16),
        compiler_params=_cparams(("parallel",)),
        name="s5_glu",
    )(y5, proj3, glu_w.astype(BF16))


def _outproj_kernel(a_ref, b_ref, c_ref, w_ref, x_ref, o_ref):
    parts = ([a_ref[h] for h in range(a_ref.shape[0])] + [b_ref[h] for h in range(b_ref.shape[0])]
             + [c_ref[h] for h in range(c_ref.shape[0])])
    mixed = jnp.concatenate(parts, axis=1)
    o_ref[...] = x_ref[...] + jnp.dot(mixed, w_ref[...], preferred_element_type=F32)


def _outproj(m_moba, m_nsa, m_s5, w_out, x2):
    rows, d = x2.shape
    tm = 512
    lspec = lambda n: pl.BlockSpec((n, tm, LANE), lambda i: (0, i, 0))
    return pl.pallas_call(
        _outproj_kernel,
        grid=(rows // tm,),
        in_specs=[lspec(m_moba.shape[0]), lspec(m_nsa.shape[0]), lspec(m_s5.shape[0]),
                  pl.BlockSpec(w_out.shape, lambda i: (0, 0)),
                  pl.BlockSpec((tm, d), lambda i: (i, 0))],
        out_specs=pl.BlockSpec((tm, d), lambda i: (i, 0)),
        out_shape=jax.ShapeDtypeStruct((rows, d), F32),
        compiler_params=_cparams(("parallel",)),
        name="outproj",
    )(m_moba, m_nsa, m_s5, w_out.astype(BF16), x2)


def _permute_w_in(w_in):
    d = w_in.shape[0]
    seg = lambda a, n: w_in[:, a:a + n]
    mw, nw, kvw = MOBA_HEADS * HEAD_DIM, NSA_HEADS * HEAD_DIM, HEAD_DIM
    o_nq = 4 * mw
    o_kv = o_nq + nw
    o_ng = o_kv + 6 * kvw
    o_nz = o_ng + 3 * NSA_HEADS
    o_su = o_nz + nw
    o_sz = o_su + S5_WIDTH
    cols = [seg(0, 4 * mw), seg(o_nq, nw), seg(o_nz, nw), seg(o_su, S5_WIDTH), seg(o_sz, S5_WIDTH),
            seg(o_kv + 2 * kvw, 4 * kvw), seg(o_kv, 2 * kvw), seg(o_ng, 3 * NSA_HEADS),
            jnp.zeros((d, 2 * LANE - 3 * NSA_HEADS), w_in.dtype)]
    return jnp.concatenate(cols, axis=1).astype(BF16)


def _layer(x2, bsz, seq, tabs, ctabs, overlap, norm_w, w_in, w_out, moba_q_norm, moba_k_norm, nsa_q_norm,
           nsa_kc_norm, nsa_ks_norm, nsa_kw_norm, nsa_pe_k, nsa_pe_v, nsa_cmp_k_w1, nsa_cmp_k_w2,
           nsa_cmp_v_w1, nsa_cmp_v_w2, s5_a_re, s5_a_im, s5_b_re, s5_b_im, s5_c_re, s5_c_im, s5_d,
           s5_log_dt, s5_glu_w):
    proj3 = _inproj(x2, norm_w, _permute_w_in(w_in))
    mk_n, mv_b, nk, kmean = _kprep(proj3, seq, moba_k_norm, nsa_ks_norm, nsa_kw_norm, tabs)
    m_moba = _moba(proj3, bsz, seq, moba_q_norm, tabs, kmean.transpose(1, 0, 2), mk_n, mv_b)
    k_cmp, v_cmp = _compress(proj3, bsz, seq, nsa_pe_k, nsa_pe_v, nsa_cmp_k_w1, nsa_cmp_k_w2,
                             nsa_cmp_v_w1, nsa_cmp_v_w2, nsa_kc_norm, ctabs)
    m_nsa = _nsa(proj3, bsz, seq, nsa_q_norm, tabs, k_cmp, v_cmp, overlap, nk)
    w_s5_in, w_s5_out, a_t = _s5_weights(s5_a_re, s5_a_im, s5_b_re, s5_b_im, s5_c_re, s5_c_im, s5_log_dt)
    d_t = jnp.tile(s5_d.astype(F32).reshape(S5_MBLK, 1, LANE), (1, 1, S5_CHUNK))
    y_intra, bst = _s5_in(proj3, w_s5_in, d_t)
    y5 = _s5_out(y_intra, _s5_scan(bst, a_t, bsz), w_s5_out)
    m_s5 = _glu(y5.reshape(S5_MBLK, x2.shape[0], LANE), proj3, s5_glu_w)
    return _outproj(m_moba, m_nsa, m_s5, w_out, x2)


def kernel(x, norm_w, w_in, w_out, moba_q_norm, moba_k_norm, nsa_q_norm, nsa_kc_norm, nsa_ks_norm, nsa_kw_norm, nsa_pe_k, nsa_pe_v, nsa_cmp_k_w1, nsa_cmp_k_w2, nsa_cmp_v_w1, nsa_cmp_v_w2, s5_a_re, s5_a_im, s5_b_re, s5_b_im, s5_c_re, s5_c_im, s5_d, s5_log_dt, s5_glu_w):
    bsz, seq, d = x.shape
    tabs = _rope_tables(jnp.arange(seq, dtype=F32))
    ncmp = seq // NSA_CMP_STRIDE
    ctabs = _rope_tables(jnp.arange(ncmp, dtype=F32) * NSA_CMP_STRIDE + (NSA_CMP_LEN - 1))
    nsel = seq // NSA_SEL_BLOCK
    ci = np.arange(ncmp)[:, None] * NSA_CMP_STRIDE
    sj = np.arange(nsel)[None, :] * NSA_SEL_BLOCK
    overlap = np.zeros((ncmp, LANE), np.float32)
    overlap[:, :nsel] = (ci < sj + NSA_SEL_BLOCK) & (ci + NSA_CMP_LEN > sj)
    overlap = jnp.asarray(overlap)
    params = (norm_w, w_in, w_out, moba_q_norm, moba_k_norm, nsa_q_norm, nsa_kc_norm, nsa_ks_norm,
              nsa_kw_norm, nsa_pe_k, nsa_pe_v, nsa_cmp_k_w1, nsa_cmp_k_w2, nsa_cmp_v_w1, nsa_cmp_v_w2,
              s5_a_re, s5_a_im, s5_b_re, s5_b_im, s5_c_re, s5_c_im, s5_d, s5_log_dt, s5_glu_w)
    x2 = x.reshape(bsz * seq, d)
    for layer in range(norm_w.shape[0]):
        x2 = _layer(x2, bsz, seq, tabs, ctabs, overlap, *[p[layer] for p in params])
    return x2.reshape(bsz, seq, d)
```

```python
import functools
import math

import numpy as np
import jax
import jax.numpy as jnp
from jax import lax
from jax.experimental import pallas as pl
from jax.experimental.pallas import tpu as pltpu

F32 = jnp.float32
BF16 = jnp.bfloat16
HIGHEST = lax.Precision.HIGHEST

LANE = 128
HEAD_DIM = 128
ROPE_DIM = HEAD_DIM // 4
ROPE_HALF = ROPE_DIM // 2
ROPE_THETA = 500000.0
EPS = 1e-6
SCALE = HEAD_DIM ** -0.5
NEG = -1e30

MOBA_HEADS = 4
MOBA_BLOCK = 256
MOBA_TOPK = 3

NSA_HEADS = 4
NSA_CMP_LEN = 32
NSA_CMP_STRIDE = 16
NSA_SEL_BLOCK = 64
NSA_SEL_TOPN = 16
NSA_WINDOW = 512
NSA_TQ = 256
NSA_SEL_PER_TILE = NSA_TQ // NSA_SEL_BLOCK
assert NSA_WINDOW == 2 * NSA_TQ

S5_WIDTH = 1024
S5_GROUP = 16
S5_GROUPS = S5_WIDTH // S5_GROUP
S5_STATE = 64
S5_CHUNK = 16
S5_MBLK = S5_WIDTH // LANE
S5_GPB = LANE // S5_GROUP
S5_HALF = S5_GPB * S5_STATE

CB_MQ, CB_MK, CB_MV, CB_MZ, CB_NQ, CB_NZ, CB_SU, CB_SZ = 0, 4, 8, 12, 16, 20, 24, 32
CB_NKS, CB_NVS, CB_NKW, CB_NVW, CB_NKC, CB_NVC, CB_NG = 40, 41, 42, 43, 44, 45, 46
IN_BLOCKS = 48

VMEM_LIMIT = 56 * 1024 * 1024


def _cparams(sem):
    return pltpu.CompilerParams(dimension_semantics=sem, vmem_limit_bytes=VMEM_LIMIT)


def _iota(shape, dim):
    return lax.broadcasted_iota(jnp.int32, shape, dim)


def _head_norm(x, w):
    return x * lax.rsqrt(jnp.mean(x * x, axis=-1, keepdims=True) + EPS) * w


def _rope(x, c, s1, s2):
    return x * c + pltpu.roll(x, LANE - ROPE_HALF, 1) * s1 + pltpu.roll(x, ROPE_HALF, 1) * s2


def _dot_nt(a, b):
    return lax.dot_general(a, b, (((1,), (1,)), ((), ())), preferred_element_type=F32)


def _rope_tables(pos):
    inv = ROPE_THETA ** (-jnp.arange(0, ROPE_DIM, 2, dtype=F32) / ROPE_DIM)
    ang = pos.astype(F32)[:, None] * inv[None, :]
    cos, sin = jnp.cos(ang), jnp.sin(ang)
    n = pos.shape[0]
    c = jnp.concatenate([cos, cos, jnp.ones((n, LANE - ROPE_DIM), F32)], axis=1)
    s1 = jnp.concatenate([-sin, jnp.zeros((n, LANE - ROPE_HALF), F32)], axis=1)
    s2 = jnp.concatenate([jnp.zeros((n, ROPE_HALF), F32), sin, jnp.zeros((n, LANE - ROPE_DIM), F32)], axis=1)
    return c, s1, s2


def _inproj_kernel(x_ref, nw_ref, w_ref, o_ref, h_ref):
    @pl.when(pl.program_id(1) == 0)
    def _():
        x = x_ref[...]
        ms = jnp.mean(x * x, axis=-1, keepdims=True)
        h_ref[...] = (x * lax.rsqrt(ms + EPS) * nw_ref[...]).astype(BF16)

    res = jnp.dot(h_ref[...], w_ref[...], preferred_element_type=F32)
    for c in range(o_ref.shape[0]):
        o_ref[c] = res[:, c * LANE:(c + 1) * LANE]


def _inproj(x2, norm_w, w_perm):
    rows, d = x2.shape
    tm, tn = 1024, 1024
    nb = tn // LANE
    return pl.pallas_call(
        _inproj_kernel,
        grid=(rows // tm, IN_BLOCKS // nb),
        in_specs=[pl.BlockSpec((tm, d), lambda i, j: (i, 0)),
                  pl.BlockSpec((1, d), lambda i, j: (0, 0)),
                  pl.BlockSpec((d, tn), lambda i, j: (0, j))],
        out_specs=pl.BlockSpec((nb, tm, LANE), lambda i, j: (j, i, 0)),
        out_shape=jax.ShapeDtypeStruct((IN_BLOCKS, rows, LANE), F32),
        scratch_shapes=[pltpu.VMEM((tm, d), BF16)],
        compiler_params=_cparams(("parallel", "arbitrary")),
        name="inproj",
    )(x2, norm_w.reshape(1, d), w_perm)


def _kprep_kernel(mk_ref, mv_ref, nk_ref, mkw_ref, ksw_ref, kww_ref, c_ref, s1_ref, s2_ref,
                  mk_o, mv_o, nk_o, km_o):
    c, s1, s2 = c_ref[...], s1_ref[...], s2_ref[...]
    means = []
    for h in range(MOBA_HEADS):
        k = _rope(_head_norm(mk_ref[h], mkw_ref[...]), c, s1, s2)
        mk_o[h] = k.astype(BF16)
        mv_o[h] = mv_ref[h].astype(BF16)
        means.append(jnp.mean(k, axis=0, keepdims=True))
    km_o[0] = jnp.concatenate(means, axis=0)
    nk_o[0] = _rope(_head_norm(nk_ref[0], ksw_ref[...]), c, s1, s2).astype(BF16)
    nk_o[1] = nk_ref[1].astype(BF16)
    nk_o[2] = _rope(_head_norm(nk_ref[2], kww_ref[...]), c, s1, s2).astype(BF16)
    nk_o[3] = nk_ref[3].astype(BF16)


def _kprep(proj3, seq, mk_w, ks_w, kw_w, tabs):
    rows = proj3.shape[1]
    t = MOBA_BLOCK
    nt = seq // t
    blk4 = lambda cb: pl.BlockSpec((4, t, LANE), lambda i: (cb // 4, i, 0))
    wspec = pl.BlockSpec((1, LANE), lambda i: (0, 0))
    tspec = pl.BlockSpec((t, LANE), lambda i: (i % nt, 0))
    out4 = pl.BlockSpec((4, t, LANE), lambda i: (0, i, 0))
    return pl.pallas_call(
        _kprep_kernel,
        grid=(rows // t,),
        in_specs=[blk4(CB_MK), blk4(CB_MV), blk4(CB_NKS), wspec, wspec, wspec, tspec, tspec, tspec],
        out_specs=[out4, out4, out4, pl.BlockSpec((1, MOBA_HEADS, LANE), lambda i: (i, 0, 0))],
        out_shape=[jax.ShapeDtypeStruct((4, rows, LANE), BF16),
                   jax.ShapeDtypeStruct((4, rows, LANE), BF16),
                   jax.ShapeDtypeStruct((4, rows, LANE), BF16),
                   jax.ShapeDtypeStruct((rows // t, MOBA_HEADS, LANE), F32)],
        compiler_params=_cparams(("parallel",)),
        name="kprep",
    )(proj3, proj3, proj3, mk_w.reshape(1, LANE), ks_w.reshape(1, LANE), kw_w.reshape(1, LANE), *tabs)


def _lanes(x, n):
    return x if n == LANE else jnp.concatenate([x] * (n // LANE), axis=1)


def _softmax_part(s, m):
    m_new = jnp.maximum(m, jnp.max(s, axis=1, keepdims=True))
    return m_new, jnp.exp(s - _lanes(m_new, s.shape[1])), jnp.exp(m - m_new)


def _flash_step(s, v, m, l, acc):
    m_new, p, alpha = _softmax_part(s, m)
    l_new = alpha * l + jnp.sum(p, axis=1, keepdims=True)
    acc_new = alpha * acc + jnp.dot(p.astype(BF16), v, preferred_element_type=F32)
    return m_new, l_new, acc_new


def _pad_rows(x, rows):
    return x if x.shape[0] == rows else jnp.concatenate(
        [x, jnp.zeros((rows - x.shape[0], x.shape[1]), x.dtype)], axis=0)


def _topk_mask_t(score_t, row_idx, k):
    n = score_t.shape[0]
    rank = jnp.zeros(score_t.shape, F32)
    for mm in range(n):
        cm = score_t[mm:mm + 1, :]
        beats = (cm > score_t) | ((cm == score_t) & (row_idx > mm))
        rank = rank + jnp.where(beats, 1.0, 0.0)
    return rank < k


def _moba_kernel(q_ref, z_ref, qw_ref, c_ref, s1_ref, s2_ref, km_ref, k_ref, v_ref, o_ref):
    i = pl.program_id(2)
    t = MOBA_BLOCK
    nb = km_ref.shape[1]
    qf = _rope(_head_norm(q_ref[0], qw_ref[...]), c_ref[...], s1_ref[...], s2_ref[...])
    qb = (qf * SCALE).astype(BF16)

    gate_t = lax.dot_general(_pad_rows(km_ref[0], LANE), qf, (((1,), (1,)), ((), ())), precision=HIGHEST,
                             preferred_element_type=F32)[:nb]
    blk_t = _iota((nb, t), 0)
    past = blk_t < i
    top = _topk_mask_t(jnp.where(past, gate_t, -jnp.inf), blk_t, MOBA_TOPK)
    sel_t = jnp.where(top & past, 1.0, 0.0)
    selb = _pad_rows(sel_t, LANE).T.astype(BF16)

    def picked(j):
        onehot = jnp.where(_iota((LANE, LANE), 0) == j, 1.0, 0.0).astype(BF16)
        return jnp.dot(selb, onehot, preferred_element_type=F32)

    start = pl.multiple_of(i * t, t)
    s = jnp.where(_iota((t, t), 1) <= _iota((t, t), 0), _dot_nt(qb, k_ref[0, pl.ds(start, t), :]), NEG)
    m = jnp.max(s, axis=1, keepdims=True) + jnp.zeros((t, LANE), F32)
    p = jnp.exp(s - _lanes(m, t))
    l = jnp.sum(p, axis=1, keepdims=True) + jnp.zeros((t, LANE), F32)
    acc = jnp.dot(p.astype(BF16), v_ref[0, pl.ds(start, t), :], preferred_element_type=F32)

    def body(pair, carry):
        m, l, acc = carry
        off = pl.multiple_of(pair * (2 * t), 2 * t)
        m_new, p, alpha = _softmax_part(_dot_nt(qb, k_ref[0, pl.ds(off, 2 * t), :]), m)
        ea, eb = picked(2 * pair), picked(2 * pair + 1)
        pa, pb = p[:, :t], p[:, t:]
        l_new = alpha * l + ea * jnp.sum(pa, axis=1, keepdims=True) + eb * jnp.sum(pb, axis=1, keepdims=True)
        acc_new = (alpha * acc
                   + ea * jnp.dot(pa.astype(BF16), v_ref[0, pl.ds(off, t), :], preferred_element_type=F32)
                   + eb * jnp.dot(pb.astype(BF16), v_ref[0, pl.ds(pl.multiple_of(off + t, t), t), :],
                                  preferred_element_type=F32))
        return m_new, l_new, acc_new

    m, l, acc = lax.fori_loop(0, (i + 1) // 2, body, (m, l, acc))
    z = z_ref[0]
    o_ref[0] = (acc / l * (z * jax.nn.sigmoid(z))).astype(BF16)


def _moba(proj3, bsz, seq, q_w, tabs, km_t, mk_n, mv_b):
    rows = proj3.shape[1]
    t = MOBA_BLOCK
    nb = seq // t
    qspec = lambda cb: pl.BlockSpec((1, t, LANE), lambda b, h, i: (cb + h, b * nb + i, 0))
    tspec = pl.BlockSpec((t, LANE), lambda b, h, i: (i, 0))
    kvspec = pl.BlockSpec((1, seq, LANE), lambda b, h, i: (h, b, 0))
    return pl.pallas_call(
        _moba_kernel,
        grid=(bsz, MOBA_HEADS, nb),
        in_specs=[qspec(CB_MQ), qspec(CB_MZ), pl.BlockSpec((1, LANE), lambda b, h, i: (0, 0)),
                  tspec, tspec, tspec,
                  pl.BlockSpec((1, nb, LANE), lambda b, h, i: (h, b, 0)), kvspec, kvspec],
        out_specs=pl.BlockSpec((1, t, LANE), lambda b, h, i: (h, b * nb + i, 0)),
        out_shape=jax.ShapeDtypeStruct((MOBA_HEADS, rows, LANE), BF16),
        compiler_params=_cparams(("parallel", "parallel", "parallel")),
        name="moba",
    )(proj3, proj3, q_w.reshape(1, LANE), *tabs, km_t, mk_n, mv_b)


def _cmp_kernel(hk_ref, hv_ref, pek_ref, pev_ref, w1k_ref, w2k_ref, w1v_ref, w2v_ref, nw_ref,
                c_ref, s1_ref, s2_ref, ko_ref, vo_ref):
    st = NSA_CMP_STRIDE
    nh = ko_ref.shape[1]

    def compress(x_ref, pe_ref, w1_ref, w2_ref):
        a = jnp.zeros((nh, LANE), F32)
        b = jnp.zeros((nh, LANE), F32)
        for l in range(st):
            x = x_ref[0, pl.ds(l, nh, stride=st), :]
            a = a + jnp.dot((x + pe_ref[l:l + 1, :]).astype(BF16), w1_ref[l * LANE:(l + 1) * LANE, :],
                            preferred_element_type=F32)
            b = b + jnp.dot((x + pe_ref[st + l:st + l + 1, :]).astype(BF16),
                            w1_ref[(st + l) * LANE:(st + l + 1) * LANE, :], preferred_element_type=F32)
        pre = a + pltpu.roll(b, nh - 1, 0)
        return jnp.dot(jax.nn.gelu(pre).astype(BF16), w2_ref[...], preferred_element_type=F32)

    kc = compress(hk_ref, pek_ref, w1k_ref, w2k_ref)
    vc = compress(hv_ref, pev_ref, w1v_ref, w2v_ref)
    ko_ref[0] = _rope(_head_norm(kc, nw_ref[...]), c_ref[...], s1_ref[...], s2_ref[...]).astype(BF16)
    vo_ref[0] = vc.astype(BF16)


def _compress(proj3, bsz, seq, pe_k, pe_v, w1k, w2k, w1v, w2v, kc_w, ctabs):
    nh = seq // NSA_CMP_STRIDE
    wide = NSA_CMP_LEN * HEAD_DIM
    hspec = lambda cb: pl.BlockSpec((1, seq, LANE), lambda b: (cb, b, 0))
    full = lambda shape: pl.BlockSpec(shape, lambda b: tuple(0 for _ in shape))
    ospec = pl.BlockSpec((1, nh, LANE), lambda b: (b, 0, 0))
    return pl.pallas_call(
        _cmp_kernel,
        grid=(bsz,),
        in_specs=[hspec(CB_NKC), hspec(CB_NVC), full((NSA_CMP_LEN, LANE)), full((NSA_CMP_LEN, LANE)),
                  full((wide, LANE)), full((LANE, LANE)), full((wide, LANE)), full((LANE, LANE)),
                  full((1, LANE)), full((nh, LANE)), full((nh, LANE)), full((nh, LANE))],
        out_specs=[ospec, ospec],
        out_shape=[jax.ShapeDtypeStruct((bsz, nh, LANE), BF16)] * 2,
        compiler_params=_cparams(("parallel",)),
        name="nsa_compress",
    )(proj3, proj3, pe_k, pe_v, w1k.astype(BF16), w2k.astype(BF16),
      w1v.astype(BF16), w2v.astype(BF16), kc_w.reshape(1, LANE), *ctabs)


def _nsa_kernel(q_ref, z_ref, g_ref, qw_ref, c_ref, s1_ref, s2_ref, kc_ref, vc_ref, ov_ref,
                ks_ref, vs_ref, kw_ref, vw_ref, o_ref, qb_s, m_s, l_s, acc_s, ocmp_s, osel_s):
    i = pl.program_id(1)
    t = NSA_TQ
    ncmp = kc_ref.shape[1]
    nsel = ks_ref.shape[0] // NSA_SEL_BLOCK
    sel_shift = int(math.log2(NSA_SEL_BLOCK))
    trow = i * t + _iota((t, 1), 0)
    row = _iota((t, t), 0)
    col = _iota((t, t), 1)
    causal = col <= row

    c, s1, s2 = c_ref[...], s1_ref[...], s2_ref[...]
    for h in range(NSA_HEADS):
        qb_s[h] = (_rope(_head_norm(q_ref[h], qw_ref[...]), c, s1, s2) * SCALE).astype(BF16)

    valid = (_iota((t, ncmp), 1) * NSA_CMP_STRIDE + (NSA_CMP_LEN - 1)) <= trow
    kc, vc = kc_ref[0], vc_ref[0]
    psum = jnp.zeros((t, ncmp), F32)
    for h in range(NSA_HEADS):
        s = jnp.where(valid, _dot_nt(qb_s[h], kc), NEG)
        e = jnp.exp(s - jnp.max(s, axis=1, keepdims=True))
        p = jnp.where(valid, e / jnp.sum(e, axis=1, keepdims=True), 0.0)
        ocmp_s[h] = jnp.dot(p.astype(BF16), vc, preferred_element_type=F32)
        psum = psum + p

    imp_t = jnp.dot(psum, ov_ref[...], precision=HIGHEST, preferred_element_type=F32).T[:nsel]
    jj = _iota((nsel, t), 0)
    cur = (i * t + _iota((1, t), 1)) >> sel_shift
    score = jnp.where(jj <= cur, imp_t, -jnp.inf)
    score = jnp.where((jj == 0) | (jj == cur) | (jj == cur - 1), jnp.inf, score)
    top = _topk_mask_t(score, jj, NSA_SEL_TOPN)
    selbias = _pad_rows(jnp.where(top, 0.0, NEG), LANE).T.astype(BF16)

    def block_bias(kt):
        onehot = jnp.where(_iota((LANE, t), 0) == kt * NSA_SEL_PER_TILE + (_iota((LANE, t), 1) >> sel_shift),
                           1.0, 0.0).astype(BF16)
        return jnp.dot(selbias, onehot, preferred_element_type=F32)

    own = pl.multiple_of(i * t, t)
    bias = block_bias(i)
    k, v = ks_ref[pl.ds(own, t), :], vs_ref[pl.ds(own, t), :]
    for h in range(NSA_HEADS):
        s = jnp.where(causal, _dot_nt(qb_s[h], k) + bias, NEG)
        m = jnp.max(s, axis=1, keepdims=True) + jnp.zeros((t, LANE), F32)
        p = jnp.exp(s - _lanes(m, t))
        m_s[h] = m
        l_s[h] = jnp.sum(p, axis=1, keepdims=True) + jnp.zeros((t, LANE), F32)
        acc_s[h] = jnp.dot(p.astype(BF16), v, preferred_element_type=F32)

    def sel_body(kt, carry):
        off = pl.multiple_of(kt * t, t)
        bias = block_bias(kt)
        k, v = ks_ref[pl.ds(off, t), :], vs_ref[pl.ds(off, t), :]
        for h in range(NSA_HEADS):
            m, l, acc = _flash_step(_dot_nt(qb_s[h], k) + bias, v, m_s[h], l_s[h], acc_s[h])
            m_s[h] = m
            l_s[h] = l
            acc_s[h] = acc
        return carry

    lax.fori_loop(0, i, sel_body, 0)
    for h in range(NSA_HEADS):
        osel_s[h] = acc_s[h] / l_s[h]

    far = pl.multiple_of(jnp.maximum(i - 2, 0) * t, t)
    mid = pl.multiple_of(jnp.maximum(i - 1, 0) * t, t)
    k3 = jnp.concatenate([kw_ref[pl.ds(far, t), :], kw_ref[pl.ds(mid, t), :], kw_ref[pl.ds(own, t), :]], axis=0)
    v3 = jnp.concatenate([vw_ref[pl.ds(far, t), :], vw_ref[pl.ds(mid, t), :], vw_ref[pl.ds(own, t), :]], axis=0)
    allowed = jnp.concatenate([(col > row) & (i >= 2), jnp.full((t, t), True) & (i >= 1), causal], axis=1)
    g = jax.nn.sigmoid(g_ref[0])
    for h in range(NSA_HEADS):
        s = jnp.where(allowed, _dot_nt(qb_s[h], k3), NEG)
        p = jnp.exp(s - jnp.max(s, axis=1, keepdims=True))
        o_win = jnp.dot(p.astype(BF16), v3, preferred_element_type=F32) / jnp.sum(p, axis=1, keepdims=True)
        o = (g[:, 3 * h:3 * h + 1] * ocmp_s[h] + g[:, 3 * h + 1:3 * h + 2] * osel_s[h]
             + g[:, 3 * h + 2:3 * h + 3] * o_win)
        z = z_ref[h]
        o_ref[h] = (o * (z * jax.nn.sigmoid(z))).astype(BF16)


def _nsa(proj3, bsz, seq, q_w, tabs, k_cmp, v_cmp, overlap, nk):
    rows = proj3.shape[1]
    t = NSA_TQ
    nt = seq // t
    ncmp = seq // NSA_CMP_STRIDE
    nsel = seq // NSA_SEL_BLOCK
    blk4 = lambda cb: pl.BlockSpec((4, t, LANE), lambda b, i: (cb // 4, b * nt + i, 0))
    tspec = pl.BlockSpec((t, LANE), lambda b, i: (i, 0))
    cspec = pl.BlockSpec((1, ncmp, LANE), lambda b, i: (b, 0, 0))
    kvspec = lambda which: pl.BlockSpec((None, seq, LANE), lambda b, i: (which, b, 0))
    return pl.pallas_call(
        _nsa_kernel,
        grid=(bsz, nt),
        in_specs=[blk4(CB_NQ), blk4(CB_NZ),
                  pl.BlockSpec((1, t, LANE), lambda b, i: (CB_NG, b * nt + i, 0)),
                  pl.BlockSpec((1, LANE), lambda b, i: (0, 0)), tspec, tspec, tspec, cspec, cspec,
                  pl.BlockSpec((ncmp, LANE), lambda b, i: (0, 0)),
                  kvspec(0), kvspec(1), kvspec(2), kvspec(3)],
        out_specs=pl.BlockSpec((4, t, LANE), lambda b, i: (0, b * nt + i, 0)),
        out_shape=jax.ShapeDtypeStruct((NSA_HEADS, rows, LANE), BF16),
        scratch_shapes=[pltpu.VMEM((NSA_HEADS, t, LANE), BF16),
                        pltpu.VMEM((NSA_HEADS, t, LANE), F32),
                        pltpu.VMEM((NSA_HEADS, t, LANE), F32),
                        pltpu.VMEM((NSA_HEADS, t, LANE), F32),
                        pltpu.VMEM((NSA_HEADS, t, LANE), F32),
                        pltpu.VMEM((NSA_HEADS, t, LANE), F32)],
        compiler_params=_cparams(("parallel", "parallel")),
        name="nsa",
    )(proj3, proj3, proj3, q_w.reshape(1, LANE), *tabs, k_cmp, v_cmp, overlap, nk, nk, nk, nk)


def _s5_weights(a_re, a_im, b_re, b_im, c_re, c_im, log_dt):
    t = S5_CHUNK
    dt = jnp.exp(log_dt.astype(F32))[:, None]
    ar, ai = a_re.astype(F32), a_im.astype(F32)
    ang = dt * ai
    mag = jnp.exp(dt * ar)
    abar_r, abar_i = mag * jnp.cos(ang), mag * jnp.sin(ang)
    nr, ni = abar_r - 1.0, abar_i
    den = ar * ar + ai * ai
    fr = (nr * ar + ni * ai) / den
    fi = (ni * ar - nr * ai) / den
    bt_r, bt_i = b_re.astype(F32).transpose(0, 2, 1), b_im.astype(F32).transpose(0, 2, 1)
    bbar_r = fr[:, None, :] * bt_r - fi[:, None, :] * bt_i
    bbar_i = fr[:, None, :] * bt_i + fi[:, None, :] * bt_r

    def powers(tau):
        tau = jnp.asarray(tau, F32)[:, None, None]
        pmag = jnp.exp(tau * (dt * ar)[None])
        return pmag * jnp.cos(tau * ang[None]), pmag * jnp.sin(tau * ang[None])

    pw_r, pw_i = powers(np.arange(t + 1))
    cr, ci = c_re.astype(F32), c_im.astype(F32)
    cp_r = cr[None] * pw_r[:, :, None, :] - ci[None] * pw_i[:, :, None, :]
    cp_i = cr[None] * pw_i[:, :, None, :] + ci[None] * pw_r[:, :, None, :]
    klag = jnp.sum(cp_r[:t, :, None, :, :] * bbar_r[None, :, :, None, :]
                   - cp_i[:t, :, None, :, :] * bbar_i[None, :, :, None, :], axis=-1)
    kc = (klag.reshape(t, S5_MBLK, S5_GPB, S5_GROUP, S5_GROUP).transpose(1, 0, 2, 3, 4)
          .reshape(S5_MBLK, t * LANE, S5_GROUP))
    rev_r, rev_i = powers(t - 1 - np.arange(t))
    bp_r = rev_r[:, :, None, :] * bbar_r[None] - rev_i[:, :, None, :] * bbar_i[None]
    bp_i = rev_r[:, :, None, :] * bbar_i[None] + rev_i[:, :, None, :] * bbar_r[None]
    bpc = (jnp.stack([bp_r, bp_i], axis=3).reshape(t, S5_MBLK, LANE, 2 * S5_STATE).transpose(1, 0, 2, 3)
           .reshape(S5_MBLK, t * LANE, 2 * S5_STATE))
    cpo = jnp.stack([cp_r[1:], -cp_i[1:]]).reshape(2, t, S5_MBLK, S5_GPB, S5_GROUP, S5_STATE)
    cpc = cpo.transpose(2, 0, 3, 5, 1, 4).reshape(S5_MBLK, 2 * S5_HALF, t * S5_GROUP)
    a_t = jnp.concatenate([pw_r[t].reshape(S5_MBLK, 1, S5_HALF), pw_i[t].reshape(S5_MBLK, 1, S5_HALF)], axis=2)
    w_in, w_c = _s5_wgen(kc, bpc, cpc)
    return w_in, w_c, a_t


def _s5_wgen_kernel(kc_ref, bp_ref, cp_ref, win_ref, wc_ref):
    t = S5_CHUNK
    wide = t * LANE
    gs, ps = int(math.log2(S5_GROUP)), int(math.log2(S5_STATE))

    def expand(x, copy_mask, group_mask):
        ex = jnp.where(copy_mask, 1.0, 0.0).astype(BF16)
        return jnp.where(group_mask, jnp.dot(x.astype(BF16), ex, preferred_element_type=F32), 0.0).astype(BF16)

    r, c = _iota((S5_GROUP, LANE), 0), _iota((S5_GROUP, LANE), 1)
    copy_k = r == (c & (S5_GROUP - 1))
    r, c = _iota((LANE, LANE), 0), _iota((LANE, LANE), 1)
    same_k = (r >> gs) == (c >> gs)
    zero = jnp.zeros((LANE, LANE), BF16)
    for lag in range(t):
        bd = expand(kc_ref[0, lag * LANE:(lag + 1) * LANE, :], copy_k, same_k)
        for s in range(t - lag):
            win_ref[0, s * LANE:(s + 1) * LANE, (s + lag) * LANE:(s + lag + 1) * LANE] = bd
    for s in range(1, t):
        for tt in range(s):
            win_ref[0, s * LANE:(s + 1) * LANE, tt * LANE:(tt + 1) * LANE] = zero
    r, c = _iota((LANE, 2 * S5_HALF), 0), _iota((LANE, 2 * S5_HALF), 1)
    copy_b = ((r >> ps) == (c >> (ps + 3))) & ((r & (S5_STATE - 1)) == (c & (S5_STATE - 1)))
    same_b = (r >> gs) == ((c >> ps) & (S5_GPB - 1))
    for s in range(t):
        win_ref[0, s * LANE:(s + 1) * LANE, wide:] = expand(bp_ref[0, s * LANE:(s + 1) * LANE, :],
                                                            copy_b, same_b)
    rows = 2 * S5_STATE * 2
    r, c = _iota((t * S5_GROUP, wide), 0), _iota((t * S5_GROUP, wide), 1)
    copy_c = ((r >> gs) == (c >> (gs + 3))) & ((r & (S5_GROUP - 1)) == (c & (S5_GROUP - 1)))
    for ch in range(2 * S5_HALF // rows):
        r, c = ch * rows + _iota((rows, wide), 0), _iota((rows, wide), 1)
        same_c = ((r >> ps) & (S5_GPB - 1)) == ((c >> gs) & (S5_GPB - 1))
        wc_ref[0, ch * rows:(ch + 1) * rows, :] = expand(cp_ref[0, ch * rows:(ch + 1) * rows, :],
                                                         copy_c, same_c)


def _s5_wgen(kc, bpc, cpc):
    t = S5_CHUNK
    wide = t * LANE
    spec = lambda a: pl.BlockSpec((1,) + a.shape[1:], lambda m: (m, 0, 0))
    return pl.pallas_call(
        _s5_wgen_kernel,
        grid=(S5_MBLK,),
        in_specs=[spec(kc), spec(bpc), spec(cpc)],
        out_specs=[pl.BlockSpec((1, wide, wide + 2 * S5_HALF), lambda m: (m, 0, 0)),
                   pl.BlockSpec((1, 2 * S5_HALF, wide), lambda m: (m, 0, 0))],
        out_shape=[jax.ShapeDtypeStruct((S5_MBLK, wide, wide + 2 * S5_HALF), BF16),
                   jax.ShapeDtypeStruct((S5_MBLK, 2 * S5_HALF, wide), BF16)],
        compiler_params=_cparams(("parallel",)),
        name="s5_wgen",
    )(kc, bpc, cpc)


def _s5_in_kernel(u_ref, w_ref, d_ref, y_ref, b_ref):
    tr = y_ref.shape[1]
    u = jnp.concatenate([u_ref[0, pl.ds(s, tr, stride=S5_CHUNK), :] for s in range(S5_CHUNK)], axis=1)
    res = jnp.dot(u.astype(BF16), w_ref[0], preferred_element_type=F32)
    wide = y_ref.shape[2]
    y_ref[0] = res[:, :wide] + d_ref[0] * u
    b_ref[0] = res[:, wide:]


def _s5_in(proj3, w_in, d_t):
    rows = proj3.shape[1] // S5_CHUNK
    wide = S5_CHUNK * LANE
    tr = min(rows, 512)
    return pl.pallas_call(
        _s5_in_kernel,
        grid=(S5_MBLK, rows // tr),
        in_specs=[pl.BlockSpec((1, tr * S5_CHUNK, LANE), lambda m, r: (CB_SU + m, r, 0)),
                  pl.BlockSpec((1, wide, wide + 2 * S5_HALF), lambda m, r: (m, 0, 0)),
                  pl.BlockSpec((1, 1, wide), lambda m, r: (m, 0, 0))],
        out_specs=[pl.BlockSpec((1, tr, wide), lambda m, r: (m, r, 0)),
                   pl.BlockSpec((1, tr, 2 * S5_HALF), lambda m, r: (m, r, 0))],
        out_shape=[jax.ShapeDtypeStruct((S5_MBLK, rows, wide), F32),
                   jax.ShapeDtypeStruct((S5_MBLK, rows, 2 * S5_HALF), F32)],
        compiler_params=_cparams(("parallel", "parallel")),
        name="s5_in",
    )(proj3, w_in, d_t)


def _s5_scan_kernel(b_ref, a_ref, o_ref, *, bsz, nk):
    ar = a_ref[0, :, :S5_HALF]
    ai = a_ref[0, :, S5_HALF:]

    def body(k, carry):
        out = []
        for b in range(bsz):
            sr, si = carry[2 * b], carry[2 * b + 1]
            row = b * nk + k
            o_ref[0, pl.ds(row, 1), :] = jnp.concatenate([sr, si], axis=1)
            x = b_ref[0, pl.ds(row, 1), :]
            out.append(ar * sr - ai * si + x[:, :S5_HALF])
            out.append(ar * si + ai * sr + x[:, S5_HALF:])
        return tuple(out)

    zero = jnp.zeros((1, S5_HALF), F32)
    lax.fori_loop(0, nk, body, tuple(zero for _ in range(2 * bsz)))


def _s5_scan(bst, a_t, bsz):
    _, rows, wide = bst.shape
    spec = pl.BlockSpec((1, rows, wide), lambda m: (m, 0, 0))
    return pl.pallas_call(
        functools.partial(_s5_scan_kernel, bsz=bsz, nk=rows // bsz),
        grid=(S5_MBLK,),
        in_specs=[spec, pl.BlockSpec((1, 1, wide), lambda m: (m, 0, 0))],
        out_specs=spec,
        out_shape=jax.ShapeDtypeStruct(bst.shape, F32),
        compiler_params=_cparams(("parallel",)),
        name="s5_scan",
    )(bst, a_t)


def _s5_out_kernel(y_ref, s_ref, w_ref, o_ref):
    tr = y_ref.shape[1]
    y = jax.nn.gelu(y_ref[0] + jnp.dot(s_ref[0].astype(BF16), w_ref[0], preferred_element_type=F32))
    for s in range(S5_CHUNK):
        o_ref[0, pl.ds(s, tr, stride=S5_CHUNK), :] = y[:, s * LANE:(s + 1) * LANE]


def _s5_out(y_intra, s_prev, w_c):
    _, rows, wide = y_intra.shape
    tr = min(rows, 512)
    return pl.pallas_call(
        _s5_out_kernel,
        grid=(S5_MBLK, rows // tr),
        in_specs=[pl.BlockSpec((1, tr, wide), lambda m, r: (m, r, 0)),
                  pl.BlockSpec((1, tr, 2 * S5_HALF), lambda m, r: (m, r, 0)),
                  pl.BlockSpec((1, 2 * S5_HALF, wide), lambda m, r: (m, 0, 0))],
        out_specs=pl.BlockSpec((1, tr * S5_CHUNK, LANE), lambda m, r: (m, r, 0)),
        out_shape=jax.ShapeDtypeStruct((S5_MBLK, rows * S5_CHUNK, LANE), F32),
        compiler_params=_cparams(("parallel", "parallel")),
        name="s5_out",
    )(y_intra, s_prev, w_c)


def _glu_kernel(y_ref, z_ref, w_ref, o_ref):
    nb = y_ref.shape[0]
    y = jnp.concatenate([y_ref[c] for c in range(nb)], axis=1)
    z = jnp.concatenate([z_ref[c] for c in range(nb)], axis=1)
    gate = jax.nn.sigmoid(jnp.dot(y.astype(BF16), w_ref[...], preferred_element_type=F32))
    o = (y * gate * (z * jax.nn.sigmoid(z))).astype(BF16)
    for c in range(nb):
        o_ref[c] = o[:, c * LANE:(c + 1) * LANE]


def _glu(y5, proj3, glu_w):
    nb, rows, _ = y5.shape
    tm = 512
    spec = lambda blk: pl.BlockSpec((nb, tm, LANE), lambda i: (blk, i, 0))
    return pl.pallas_call(
        _glu_kernel,
        grid=(rows // tm,),
        in_specs=[spec(0), spec(CB_SZ // nb), pl.BlockSpec((S5_WIDTH, S5_WIDTH), lambda i: (0, 0))],
        out_specs=spec(0),
        out_shape=jax.ShapeDtypeStruct((nb, rows, LANE), BF16),
        compiler_params=_cparams(("parallel",)),
        name="s5_glu",
    )(y5, proj3, glu_w.astype(BF16))


def _outproj_kernel(a_ref, b_ref, c_ref, w_ref, x_ref, o_ref):
    parts = ([a_ref[h] for h in range(a_ref.shape[0])] + [b_ref[h] for h in range(b_ref.shape[0])]
             + [c_ref[h] for h in range(c_ref.shape[0])])
    mixed = jnp.concatenate(parts, axis=1)
    o_ref[...] = x_ref[...] + jnp.dot(mixed, w_ref[...], preferred_element_type=F32)


def _outproj(m_moba, m_nsa, m_s5, w_out, x2):
    rows, d = x2.shape
    tm = 512
    lspec = lambda n: pl.BlockSpec((n, tm, LANE), lambda i: (0, i, 0))
    return pl.pallas_call(
        _outproj_kernel,
        grid=(rows // tm,),
        in_specs=[lspec(m_moba.shape[0]), lspec(m_nsa.shape[0]), lspec(m_s5.shape[0]),
                  pl.BlockSpec(w_out.shape, lambda i: (0, 0)),
                  pl.BlockSpec((tm, d), lambda i: (i, 0))],
        out_specs=pl.BlockSpec((tm, d), lambda i: (i, 0)),
        out_shape=jax.ShapeDtypeStruct((rows, d), F32),
        compiler_params=_cparams(("parallel",)),
        name="outproj",
    )(m_moba, m_nsa, m_s5, w_out.astype(BF16), x2)


def _wperm_kernel(w_ref, o_ref):
    mw, nw, kvw, ng = MOBA_HEADS * HEAD_DIM, NSA_HEADS * HEAD_DIM, HEAD_DIM, 3 * NSA_HEADS
    o_kv = 4 * mw + nw
    o_ng = o_kv + 6 * kvw
    o_nz = o_ng + ng
    run = nw + 2 * S5_WIDTH

    def put(dst_blk, src, width):
        o_ref[:, dst_blk * LANE:dst_blk * LANE + width] = w_ref[:, src:src + width].astype(BF16)

    put(CB_MQ, 0, o_kv)
    put(CB_NZ, o_nz, run)
    put(CB_NKS, o_kv + 2 * kvw, 4 * kvw)
    put(CB_NKC, o_kv, 2 * kvw)
    tail = w_ref[:, o_ng:o_ng + 2 * LANE]
    o_ref[:, CB_NG * LANE:] = jnp.where(_iota(tail.shape, 1) < ng, tail, 0.0).astype(BF16)


def _permute_w_in(w_in_all, layer):
    _, d, width = w_in_all.shape
    tr = 256
    return pl.pallas_call(
        _wperm_kernel,
        grid=(d // tr,),
        in_specs=[pl.BlockSpec((None, tr, width), lambda i: (layer, i, 0))],
        out_specs=pl.BlockSpec((tr, IN_BLOCKS * LANE), lambda i: (i, 0)),
        out_shape=jax.ShapeDtypeStruct((d, IN_BLOCKS * LANE), BF16),
        compiler_params=_cparams(("parallel",)),
        name="w_in_permute",
    )(w_in_all)


def _layer(x2, bsz, seq, tabs, ctabs, overlap, w_in, norm_w, w_out, moba_q_norm, moba_k_norm, nsa_q_norm,
           nsa_kc_norm, nsa_ks_norm, nsa_kw_norm, nsa_pe_k, nsa_pe_v, nsa_cmp_k_w1, nsa_cmp_k_w2,
           nsa_cmp_v_w1, nsa_cmp_v_w2, s5_a_re, s5_a_im, s5_b_re, s5_b_im, s5_c_re, s5_c_im, s5_d,
           s5_log_dt, s5_glu_w):
    proj3 = _inproj(x2, norm_w, w_in)
    mk_n, mv_b, nk, kmean = _kprep(proj3, seq, moba_k_norm, nsa_ks_norm, nsa_kw_norm, tabs)
    m_moba = _moba(proj3, bsz, seq, moba_q_norm, tabs, kmean.transpose(1, 0, 2), mk_n, mv_b)
    k_cmp, v_cmp = _compress(proj3, bsz, seq, nsa_pe_k, nsa_pe_v, nsa_cmp_k_w1, nsa_cmp_k_w2,
                             nsa_cmp_v_w1, nsa_cmp_v_w2, nsa_kc_norm, ctabs)
    m_nsa = _nsa(proj3, bsz, seq, nsa_q_norm, tabs, k_cmp, v_cmp, overlap, nk)
    w_s5_in, w_s5_out, a_t = _s5_weights(s5_a_re, s5_a_im, s5_b_re, s5_b_im, s5_c_re, s5_c_im, s5_log_dt)
    d_t = jnp.tile(s5_d.astype(F32).reshape(S5_MBLK, 1, LANE), (1, 1, S5_CHUNK))
    y_intra, bst = _s5_in(proj3, w_s5_in, d_t)
    y5 = _s5_out(y_intra, _s5_scan(bst, a_t, bsz), w_s5_out)
    m_s5 = _glu(y5, proj3, s5_glu_w)
    return _outproj(m_moba, m_nsa, m_s5, w_out, x2)


def kernel(x, norm_w, w_in, w_out, moba_q_norm, moba_k_norm, nsa_q_norm, nsa_kc_norm, nsa_ks_norm, nsa_kw_norm, nsa_pe_k, nsa_pe_v, nsa_cmp_k_w1, nsa_cmp_k_w2, nsa_cmp_v_w1, nsa_cmp_v_w2, s5_a_re, s5_a_im, s5_b_re, s5_b_im, s5_c_re, s5_c_im, s5_d, s5_log_dt, s5_glu_w):
    bsz, seq, d = x.shape
    tabs = _rope_tables(jnp.arange(seq, dtype=F32))
    ncmp = seq // NSA_CMP_STRIDE
    ctabs = _rope_tables(jnp.arange(ncmp, dtype=F32) * NSA_CMP_STRIDE + (NSA_CMP_LEN - 1))
    nsel = seq // NSA_SEL_BLOCK
    ci = np.arange(ncmp)[:, None] * NSA_CMP_STRIDE
    sj = np.arange(nsel)[None, :] * NSA_SEL_BLOCK
    overlap = np.zeros((ncmp, LANE), np.float32)
    overlap[:, :nsel] = (ci < sj + NSA_SEL_BLOCK) & (ci + NSA_CMP_LEN > sj)
    overlap = jnp.asarray(overlap)
    params = (norm_w, w_out, moba_q_norm, moba_k_norm, nsa_q_norm, nsa_kc_norm, nsa_ks_norm,
              nsa_kw_norm, nsa_pe_k, nsa_pe_v, nsa_cmp_k_w1, nsa_cmp_k_w2, nsa_cmp_v_w1, nsa_cmp_v_w2,
              s5_a_re, s5_a_im, s5_b_re, s5_b_im, s5_c_re, s5_c_im, s5_d, s5_log_dt, s5_glu_w)
    x2 = x.reshape(bsz * seq, d)
    for layer in range(norm_w.shape[0]):
        x2 = _layer(x2, bsz, seq, tabs, ctabs, overlap, _permute_w_in(w_in, layer), *[p[layer] for p in params])
    return x2.reshape(bsz, seq, d)
```

```python
import functools
import math

import numpy as np
import jax
import jax.numpy as jnp
from jax import lax
from jax.experimental import pallas as pl
from jax.experimental.pallas import tpu as pltpu

F32 = jnp.float32
BF16 = jnp.bfloat16
HIGHEST = lax.Precision.HIGHEST

LANE = 128
HEAD_DIM = 128
ROPE_DIM = HEAD_DIM // 4
ROPE_HALF = ROPE_DIM // 2
ROPE_THETA = 500000.0
EPS = 1e-6
SCALE = HEAD_DIM ** -0.5
NEG = -1e30

MOBA_HEADS = 4
MOBA_BLOCK = 256
MOBA_TOPK = 3

NSA_HEADS = 4
NSA_CMP_LEN = 32
NSA_CMP_STRIDE = 16
NSA_SEL_BLOCK = 64
NSA_SEL_TOPN = 16
NSA_WINDOW = 512
NSA_TQ = 256
NSA_SEL_PER_TILE = NSA_TQ // NSA_SEL_BLOCK
assert NSA_WINDOW == 2 * NSA_TQ

S5_WIDTH = 1024
S5_GROUP = 16
S5_GROUPS = S5_WIDTH // S5_GROUP
S5_STATE = 64
S5_CHUNK = 16
S5_MBLK = S5_WIDTH // LANE
S5_GPB = LANE // S5_GROUP
S5_HALF = S5_GPB * S5_STATE

CB_MQ, CB_MK, CB_MV, CB_MZ, CB_NQ, CB_NZ, CB_SU, CB_SZ = 0, 4, 8, 12, 16, 20, 24, 32
CB_NKS, CB_NVS, CB_NKW, CB_NVW, CB_NKC, CB_NVC, CB_NG = 40, 41, 42, 43, 44, 45, 46
IN_BLOCKS = 48

VMEM_LIMIT = 56 * 1024 * 1024


def _cparams(sem):
    return pltpu.CompilerParams(dimension_semantics=sem, vmem_limit_bytes=VMEM_LIMIT)


def _iota(shape, dim):
    return lax.broadcasted_iota(jnp.int32, shape, dim)


def _head_norm(x, w):
    return x * lax.rsqrt(jnp.mean(x * x, axis=-1, keepdims=True) + EPS) * w


def _rope(x, c, s1, s2):
    return x * c + pltpu.roll(x, LANE - ROPE_HALF, 1) * s1 + pltpu.roll(x, ROPE_HALF, 1) * s2


def _dot_nt(a, b):
    return lax.dot_general(a, b, (((1,), (1,)), ((), ())), preferred_element_type=F32)


def _rope_tables(pos):
    inv = ROPE_THETA ** (-jnp.arange(0, ROPE_DIM, 2, dtype=F32) / ROPE_DIM)
    ang = pos.astype(F32)[:, None] * inv[None, :]
    cos, sin = jnp.cos(ang), jnp.sin(ang)
    n = pos.shape[0]
    c = jnp.concatenate([cos, cos, jnp.ones((n, LANE - ROPE_DIM), F32)], axis=1)
    s1 = jnp.concatenate([-sin, jnp.zeros((n, LANE - ROPE_HALF), F32)], axis=1)
    s2 = jnp.concatenate([jnp.zeros((n, ROPE_HALF), F32), sin, jnp.zeros((n, LANE - ROPE_DIM), F32)], axis=1)
    return c, s1, s2


def _inproj_kernel(x_ref, nw_ref, w_ref, o_ref, h_ref):
    @pl.when(pl.program_id(1) == 0)
    def _():
        x = x_ref[...]
        ms = jnp.mean(x * x, axis=-1, keepdims=True)
        h_ref[...] = (x * lax.rsqrt(ms + EPS) * nw_ref[...]).astype(BF16)

    res = jnp.dot(h_ref[...], w_ref[...], preferred_element_type=F32)
    for c in range(o_ref.shape[0]):
        o_ref[c] = res[:, c * LANE:(c + 1) * LANE]


def _inproj(x2, norm_w, w_perm):
    rows, d = x2.shape
    tm, tn = 1024, 1024
    nb = tn // LANE
    return pl.pallas_call(
        _inproj_kernel,
        grid=(rows // tm, IN_BLOCKS // nb),
        in_specs=[pl.BlockSpec((tm, d), lambda i, j: (i, 0)),
                  pl.BlockSpec((1, d), lambda i, j: (0, 0)),
                  pl.BlockSpec((d, tn), lambda i, j: (0, j))],
        out_specs=pl.BlockSpec((nb, tm, LANE), lambda i, j: (j, i, 0)),
        out_shape=jax.ShapeDtypeStruct((IN_BLOCKS, rows, LANE), F32),
        scratch_shapes=[pltpu.VMEM((tm, d), BF16)],
        compiler_params=_cparams(("parallel", "arbitrary")),
        name="inproj",
    )(x2, norm_w.reshape(1, d), w_perm)


def _kprep_kernel(mk_ref, mv_ref, nk_ref, mkw_ref, ksw_ref, kww_ref, c_ref, s1_ref, s2_ref,
                  mk_o, mv_o, nk_o, km_o):
    c, s1, s2 = c_ref[...], s1_ref[...], s2_ref[...]
    means = []
    for h in range(MOBA_HEADS):
        k = _rope(_head_norm(mk_ref[h], mkw_ref[...]), c, s1, s2)
        mk_o[h] = k.astype(BF16)
        mv_o[h] = mv_ref[h].T.astype(BF16)
        means.append(jnp.mean(k, axis=0, keepdims=True))
    km_o[0] = jnp.concatenate(means, axis=0)
    nk_o[0] = _rope(_head_norm(nk_ref[0], ksw_ref[...]), c, s1, s2).astype(BF16)
    nk_o[1] = nk_ref[1].astype(BF16)
    nk_o[2] = _rope(_head_norm(nk_ref[2], kww_ref[...]), c, s1, s2).astype(BF16)
    nk_o[3] = nk_ref[3].astype(BF16)


def _kprep(proj3, seq, mk_w, ks_w, kw_w, tabs):
    rows = proj3.shape[1]
    t = MOBA_BLOCK
    nt = seq // t
    blk4 = lambda cb: pl.BlockSpec((4, t, LANE), lambda i: (cb // 4, i, 0))
    wspec = pl.BlockSpec((1, LANE), lambda i: (0, 0))
    tspec = pl.BlockSpec((t, LANE), lambda i: (i % nt, 0))
    out4 = pl.BlockSpec((4, t, LANE), lambda i: (0, i, 0))
    return pl.pallas_call(
        _kprep_kernel,
        grid=(rows // t,),
        in_specs=[blk4(CB_MK), blk4(CB_MV), blk4(CB_NKS), wspec, wspec, wspec, tspec, tspec, tspec],
        out_specs=[out4, pl.BlockSpec((4, None, LANE, t), lambda i: (0, i // nt, 0, i % nt)), out4,
                   pl.BlockSpec((1, MOBA_HEADS, LANE), lambda i: (i, 0, 0))],
        out_shape=[jax.ShapeDtypeStruct((4, rows, LANE), BF16),
                   jax.ShapeDtypeStruct((4, rows // seq, LANE, seq), BF16),
                   jax.ShapeDtypeStruct((4, rows, LANE), BF16),
                   jax.ShapeDtypeStruct((rows // t, MOBA_HEADS, LANE), F32)],
        compiler_params=_cparams(("parallel",)),
        name="kprep",
    )(proj3, proj3, proj3, mk_w.reshape(1, LANE), ks_w.reshape(1, LANE), kw_w.reshape(1, LANE), *tabs)


def _lanes(x, n):
    return x if n == LANE else jnp.concatenate([x] * (n // LANE), axis=1)


def _softmax_part(s, m):
    m_new = jnp.maximum(m, jnp.max(s, axis=1, keepdims=True))
    return m_new, jnp.exp(s - _lanes(m_new, s.shape[1])), jnp.exp(m - m_new)


def _flash_step(s, v, m, l, acc):
    m_new, p, alpha = _softmax_part(s, m)
    l_new = alpha * l + jnp.sum(p, axis=1, keepdims=True)
    acc_new = alpha * acc + jnp.dot(p.astype(BF16), v, preferred_element_type=F32)
    return m_new, l_new, acc_new


def _pad_rows(x, rows):
    return x if x.shape[0] == rows else jnp.concatenate(
        [x, jnp.zeros((rows - x.shape[0], x.shape[1]), x.dtype)], axis=0)


def _topk_mask_t(score_t, row_idx, k):
    n = score_t.shape[0]
    rank = jnp.zeros(score_t.shape, F32)
    for mm in range(n):
        cm = score_t[mm:mm + 1, :]
        beats = (cm > score_t) | ((cm == score_t) & (row_idx > mm))
        rank = rank + jnp.where(beats, 1.0, 0.0)
    return rank < k


def _moba_kernel(q_ref, z_ref, qw_ref, c_ref, s1_ref, s2_ref, km_ref, k_ref, vt_ref, o_ref,
                 qt_s, sel_s, s_s, p_s, acc_s):
    i = pl.program_id(1)
    t = MOBA_BLOCK
    nb = km_ref.shape[1]
    nh = q_ref.shape[0]
    npairs = (i + 1) // 2
    blk_t = _iota((nb, t), 0)
    past = blk_t < i
    causal = _iota((t, t), 0) <= _iota((t, t), 1)
    start = pl.multiple_of(i * t, t)

    def scores(h, pair):
        off = pl.multiple_of(pair * (2 * t), 2 * t)
        return jnp.dot(k_ref[h, pl.ds(off, 2 * t), :], qt_s[h], preferred_element_type=F32)

    def weighted_values(h, pair):
        off = pl.multiple_of(pair * (2 * t), 2 * t)
        off_b = pl.multiple_of(off + t, t)
        ea, eb = sel_s[h, pl.ds(2 * pair, 1), :], sel_s[h, pl.ds(2 * pair + 1, 1), :]
        return (ea * jnp.dot(vt_ref[h, :, pl.ds(off, t)], p_s[h, :t, :], preferred_element_type=F32)
                + eb * jnp.dot(vt_ref[h, :, pl.ds(off_b, t)], p_s[h, t:, :], preferred_element_type=F32))

    state = []
    for h in range(nh):
        qf = _rope(_head_norm(q_ref[h], qw_ref[...]), c_ref[...], s1_ref[...], s2_ref[...])
        qt = (qf * SCALE).T.astype(BF16)
        qt_s[h] = qt
        gate_t = lax.dot_general(_pad_rows(km_ref[h], LANE), qf, (((1,), (1,)), ((), ())), precision=HIGHEST,
                                 preferred_element_type=F32)[:nb]
        top = _topk_mask_t(jnp.where(past, gate_t, -jnp.inf), blk_t, MOBA_TOPK)
        sel_s[h] = jnp.where(top & past, 1.0, 0.0)
        s = jnp.where(causal, jnp.dot(k_ref[h, pl.ds(start, t), :], qt, preferred_element_type=F32), NEG)
        m = jnp.max(s, axis=0, keepdims=True)
        p = jnp.exp(s - m)
        acc_s[h] = jnp.dot(vt_ref[h, :, pl.ds(start, t)], p.astype(BF16), preferred_element_type=F32)
        state += [m, jnp.sum(p, axis=0, keepdims=True)]
        s_s[0, h] = scores(h, 0)
        p_s[h] = jnp.zeros((2 * t, t), BF16)

    def body(pair, carry):
        slot = pair % 2
        nxt = jnp.minimum(pair + 1, npairs - 1)
        prev = jnp.maximum(pair - 1, 0)
        out = []
        for h in range(nh):
            m, l = carry[2 * h:2 * h + 2]
            s_s[1 - slot, h] = scores(h, nxt)
            acc = acc_s[h] + weighted_values(h, prev)
            s = s_s[slot, h]
            m_new = jnp.maximum(m, jnp.max(s, axis=0, keepdims=True))
            p = jnp.exp(s - m_new)
            alpha = jnp.exp(m - m_new)
            ea, eb = sel_s[h, pl.ds(2 * pair, 1), :], sel_s[h, pl.ds(2 * pair + 1, 1), :]
            l_new = (alpha * l + ea * jnp.sum(p[:t], axis=0, keepdims=True)
                     + eb * jnp.sum(p[t:], axis=0, keepdims=True))
            acc_s[h] = alpha * acc
            p_s[h] = p.astype(BF16)
            out += [m_new, l_new]
        return tuple(out)

    state = lax.fori_loop(0, npairs, body, tuple(state))
    last = jnp.maximum(npairs - 1, 0)
    for h in range(nh):
        l = state[2 * h + 1]
        acc = acc_s[h] + weighted_values(h, last)
        z = z_ref[h]
        o_ref[h] = ((acc * (1.0 / l)).T * (z * jax.nn.sigmoid(z))).astype(BF16)


def _moba(proj3, bsz, seq, q_w, tabs, km_t, mk_n, mv_t):
    rows = proj3.shape[1]
    t = MOBA_BLOCK
    nb = seq // t
    nh = MOBA_HEADS
    qspec = lambda cb: pl.BlockSpec((nh, t, LANE), lambda b, i: (cb // nh, b * nb + i, 0))
    tspec = pl.BlockSpec((t, LANE), lambda b, i: (i, 0))
    return pl.pallas_call(
        _moba_kernel,
        grid=(bsz, nb),
        in_specs=[qspec(CB_MQ), qspec(CB_MZ), pl.BlockSpec((1, LANE), lambda b, i: (0, 0)),
                  tspec, tspec, tspec,
                  pl.BlockSpec((nh, nb, LANE), lambda b, i: (0, b, 0)),
                  pl.BlockSpec((nh, seq, LANE), lambda b, i: (0, b, 0)),
                  pl.BlockSpec((nh, None, LANE, seq), lambda b, i: (0, b, 0, 0))],
        out_specs=pl.BlockSpec((nh, t, LANE), lambda b, i: (0, b * nb + i, 0)),
        out_shape=jax.ShapeDtypeStruct((nh, rows, LANE), BF16),
        scratch_shapes=[pltpu.VMEM((nh, LANE, t), BF16),
                        pltpu.VMEM((nh, nb, t), F32),
                        pltpu.VMEM((2, nh, 2 * t, t), F32),
                        pltpu.VMEM((nh, 2 * t, t), BF16),
                        pltpu.VMEM((nh, LANE, t), F32)],
        compiler_params=_cparams(("parallel", "parallel")),
        name="moba",
    )(proj3, proj3, q_w.reshape(1, LANE), *tabs, km_t, mk_n, mv_t)


def _cmp_kernel(hk_ref, hv_ref, pek_ref, pev_ref, w1k_ref, w2k_ref, w1v_ref, w2v_ref, nw_ref,
                c_ref, s1_ref, s2_ref, ko_ref, vo_ref):
    st = NSA_CMP_STRIDE
    nh = ko_ref.shape[1]

    def compress(x_ref, pe_ref, w1_ref, w2_ref):
        a = jnp.zeros((nh, LANE), F32)
        b = jnp.zeros((nh, LANE), F32)
        for l in range(st):
            x = x_ref[0, pl.ds(l, nh, stride=st), :]
            a = a + jnp.dot((x + pe_ref[l:l + 1, :]).astype(BF16), w1_ref[l * LANE:(l + 1) * LANE, :],
                            preferred_element_type=F32)
            b = b + jnp.dot((x + pe_ref[st + l:st + l + 1, :]).astype(BF16),
                            w1_ref[(st + l) * LANE:(st + l + 1) * LANE, :], preferred_element_type=F32)
        pre = a + pltpu.roll(b, nh - 1, 0)
        return jnp.dot(jax.nn.gelu(pre).astype(BF16), w2_ref[...], preferred_element_type=F32)

    kc = compress(hk_ref, pek_ref, w1k_ref, w2k_ref)
    vc = compress(hv_ref, pev_ref, w1v_ref, w2v_ref)
    ko_ref[0] = _rope(_head_norm(kc, nw_ref[...]), c_ref[...], s1_ref[...], s2_ref[...]).astype(BF16)
    vo_ref[0] = vc.astype(BF16)


def _compress(proj3, bsz, seq, pe_k, pe_v, w1k, w2k, w1v, w2v, kc_w, ctabs):
    nh = seq // NSA_CMP_STRIDE
    wide = NSA_CMP_LEN * HEAD_DIM
    hspec = lambda cb: pl.BlockSpec((1, seq, LANE), lambda b: (cb, b, 0))
    full = lambda shape: pl.BlockSpec(shape, lambda b: tuple(0 for _ in shape))
    ospec = pl.BlockSpec((1, nh, LANE), lambda b: (b, 0, 0))
    return pl.pallas_call(
        _cmp_kernel,
        grid=(bsz,),
        in_specs=[hspec(CB_NKC), hspec(CB_NVC), full((NSA_CMP_LEN, LANE)), full((NSA_CMP_LEN, LANE)),
                  full((wide, LANE)), full((LANE, LANE)), full((wide, LANE)), full((LANE, LANE)),
                  full((1, LANE)), full((nh, LANE)), full((nh, LANE)), full((nh, LANE))],
        out_specs=[ospec, ospec],
        out_shape=[jax.ShapeDtypeStruct((bsz, nh, LANE), BF16)] * 2,
        compiler_params=_cparams(("parallel",)),
        name="nsa_compress",
    )(proj3, proj3, pe_k, pe_v, w1k.astype(BF16), w2k.astype(BF16),
      w1v.astype(BF16), w2v.astype(BF16), kc_w.reshape(1, LANE), *ctabs)


def _nsa_kernel(q_ref, z_ref, g_ref, qw_ref, c_ref, s1_ref, s2_ref, kc_ref, vc_ref, ov_ref,
                ks_ref, vs_ref, kw_ref, vw_ref, o_ref, qb_s, m_s, l_s, acc_s, ocmp_s, osel_s):
    i = pl.program_id(1)
    t = NSA_TQ
    ncmp = kc_ref.shape[1]
    nsel = ks_ref.shape[0] // NSA_SEL_BLOCK
    sel_shift = int(math.log2(NSA_SEL_BLOCK))
    trow = i * t + _iota((t, 1), 0)
    row = _iota((t, t), 0)
    col = _iota((t, t), 1)
    causal = col <= row

    c, s1, s2 = c_ref[...], s1_ref[...], s2_ref[...]
    for h in range(NSA_HEADS):
        qb_s[h] = (_rope(_head_norm(q_ref[h], qw_ref[...]), c, s1, s2) * SCALE).astype(BF16)

    valid = (_iota((t, ncmp), 1) * NSA_CMP_STRIDE + (NSA_CMP_LEN - 1)) <= trow
    kc, vc = kc_ref[0], vc_ref[0]
    psum = jnp.zeros((t, ncmp), F32)
    for h in range(NSA_HEADS):
        s = jnp.where(valid, _dot_nt(qb_s[h], kc), NEG)
        mx = jnp.max(s, axis=1, keepdims=True)
        e = jnp.exp(s - mx)
        p = e * jnp.where(mx > 0.5 * NEG, 1.0 / jnp.sum(e, axis=1, keepdims=True), 0.0)
        ocmp_s[h] = jnp.dot(p.astype(BF16), vc, preferred_element_type=F32)
        psum = psum + p

    imp_t = jnp.dot(psum, ov_ref[...], precision=HIGHEST, preferred_element_type=F32).T[:nsel]
    jj = _iota((nsel, t), 0)
    cur = (i * t + _iota((1, t), 1)) >> sel_shift
    score = jnp.where(jj <= cur, imp_t, -jnp.inf)
    score = jnp.where((jj == 0) | (jj == cur) | (jj == cur - 1), jnp.inf, score)
    top = _topk_mask_t(score, jj, NSA_SEL_TOPN)
    selbias = _pad_rows(jnp.where(top, 0.0, NEG), LANE).T.astype(BF16)

    def block_bias(kt):
        onehot = jnp.where(_iota((LANE, t), 0) == kt * NSA_SEL_PER_TILE + (_iota((LANE, t), 1) >> sel_shift),
                           1.0, 0.0).astype(BF16)
        return jnp.dot(selbias, onehot, preferred_element_type=F32)

    own = pl.multiple_of(i * t, t)
    bias = block_bias(i)
    k, v = ks_ref[pl.ds(own, t), :], vs_ref[pl.ds(own, t), :]
    for h in range(NSA_HEADS):
        s = jnp.where(causal, _dot_nt(qb_s[h], k) + bias, NEG)
        m = jnp.max(s, axis=1, keepdims=True) + jnp.zeros((t, LANE), F32)
        p = jnp.exp(s - _lanes(m, t))
        m_s[h] = m
        l_s[h] = jnp.sum(p, axis=1, keepdims=True) + jnp.zeros((t, LANE), F32)
        acc_s[h] = jnp.dot(p.astype(BF16), v, preferred_element_type=F32)

    def sel_body(kt, carry):
        off = pl.multiple_of(kt * t, t)
        bias = block_bias(kt)
        k, v = ks_ref[pl.ds(off, t), :], vs_ref[pl.ds(off, t), :]
        for h in range(NSA_HEADS):
            m, l, acc = _flash_step(_dot_nt(qb_s[h], k) + bias, v, m_s[h], l_s[h], acc_s[h])
            m_s[h] = m
            l_s[h] = l
            acc_s[h] = acc
        return carry

    lax.fori_loop(0, i, sel_body, 0)
    for h in range(NSA_HEADS):
        osel_s[h] = acc_s[h] * (1.0 / l_s[h])

    far = pl.multiple_of(jnp.maximum(i - 2, 0) * t, t)
    mid = pl.multiple_of(jnp.maximum(i - 1, 0) * t, t)
    k3 = jnp.concatenate([kw_ref[pl.ds(far, t), :], kw_ref[pl.ds(mid, t), :], kw_ref[pl.ds(own, t), :]], axis=0)
    v3 = jnp.concatenate([vw_ref[pl.ds(far, t), :], vw_ref[pl.ds(mid, t), :], vw_ref[pl.ds(own, t), :]], axis=0)
    allowed = jnp.concatenate([(col > row) & (i >= 2), jnp.full((t, t), True) & (i >= 1), causal], axis=1)
    g = jax.nn.sigmoid(g_ref[0])
    for h in range(NSA_HEADS):
        s = jnp.where(allowed, _dot_nt(qb_s[h], k3), NEG)
        p = jnp.exp(s - jnp.max(s, axis=1, keepdims=True))
        o_win = jnp.dot(p.astype(BF16), v3, preferred_element_type=F32) * (1.0 / jnp.sum(p, axis=1, keepdims=True))
        o = (g[:, 3 * h:3 * h + 1] * ocmp_s[h] + g[:, 3 * h + 1:3 * h + 2] * osel_s[h]
             + g[:, 3 * h + 2:3 * h + 3] * o_win)
        z = z_ref[h]
        o_ref[h] = (o * (z * jax.nn.sigmoid(z))).astype(BF16)


def _nsa(proj3, bsz, seq, q_w, tabs, k_cmp, v_cmp, overlap, nk):
    rows = proj3.shape[1]
    t = NSA_TQ
    nt = seq // t
    ncmp = seq // NSA_CMP_STRIDE
    nsel = seq // NSA_SEL_BLOCK
    blk4 = lambda cb: pl.BlockSpec((4, t, LANE), lambda b, i: (cb // 4, b * nt + i, 0))
    tspec = pl.BlockSpec((t, LANE), lambda b, i: (i, 0))
    cspec = pl.BlockSpec((1, ncmp, LANE), lambda b, i: (b, 0, 0))
    kvspec = lambda which: pl.BlockSpec((None, seq, LANE), lambda b, i: (which, b, 0))
    return pl.pallas_call(
        _nsa_kernel,
        grid=(bsz, nt),
        in_specs=[blk4(CB_NQ), blk4(CB_NZ),
                  pl.BlockSpec((1, t, LANE), lambda b, i: (CB_NG, b * nt + i, 0)),
                  pl.BlockSpec((1, LANE), lambda b, i: (0, 0)), tspec, tspec, tspec, cspec, cspec,
                  pl.BlockSpec((ncmp, LANE), lambda b, i: (0, 0)),
                  kvspec(0), kvspec(1), kvspec(2), kvspec(3)],
        out_specs=pl.BlockSpec((4, t, LANE), lambda b, i: (0, b * nt + i, 0)),
        out_shape=jax.ShapeDtypeStruct((NSA_HEADS, rows, LANE), BF16),
        scratch_shapes=[pltpu.VMEM((NSA_HEADS, t, LANE), BF16),
                        pltpu.VMEM((NSA_HEADS, t, LANE), F32),
                        pltpu.VMEM((NSA_HEADS, t, LANE), F32),
                        pltpu.VMEM((NSA_HEADS, t, LANE), F32),
                        pltpu.VMEM((NSA_HEADS, t, LANE), F32),
                        pltpu.VMEM((NSA_HEADS, t, LANE), F32)],
        compiler_params=_cparams(("parallel", "parallel")),
        name="nsa",
    )(proj3, proj3, proj3, q_w.reshape(1, LANE), *tabs, k_cmp, v_cmp, overlap, nk, nk, nk, nk)


def _s5_weights(a_re, a_im, b_re, b_im, c_re, c_im, log_dt):
    t = S5_CHUNK
    dt = jnp.exp(log_dt.astype(F32))[:, None]
    ar, ai = a_re.astype(F32), a_im.astype(F32)
    ang = dt * ai
    mag = jnp.exp(dt * ar)
    abar_r, abar_i = mag * jnp.cos(ang), mag * jnp.sin(ang)
    nr, ni = abar_r - 1.0, abar_i
    den = ar * ar + ai * ai
    fr = (nr * ar + ni * ai) / den
    fi = (ni * ar - nr * ai) / den
    bt_r, bt_i = b_re.astype(F32).transpose(0, 2, 1), b_im.astype(F32).transpose(0, 2, 1)
    bbar_r = fr[:, None, :] * bt_r - fi[:, None, :] * bt_i
    bbar_i = fr[:, None, :] * bt_i + fi[:, None, :] * bt_r

    def powers(tau):
        tau = jnp.asarray(tau, F32)[:, None, None]
        pmag = jnp.exp(tau * (dt * ar)[None])
        return pmag * jnp.cos(tau * ang[None]), pmag * jnp.sin(tau * ang[None])

    pw_r, pw_i = powers(np.arange(t + 1))
    cr, ci = c_re.astype(F32), c_im.astype(F32)
    cp_r = cr[None] * pw_r[:, :, None, :] - ci[None] * pw_i[:, :, None, :]
    cp_i = cr[None] * pw_i[:, :, None, :] + ci[None] * pw_r[:, :, None, :]
    klag = jnp.sum(cp_r[:t, :, None, :, :] * bbar_r[None, :, :, None, :]
                   - cp_i[:t, :, None, :, :] * bbar_i[None, :, :, None, :], axis=-1)
    kc = (klag.reshape(t, S5_MBLK, S5_GPB, S5_GROUP, S5_GROUP).transpose(1, 0, 2, 3, 4)
          .reshape(S5_MBLK, t * LANE, S5_GROUP))
    rev_r, rev_i = powers(t - 1 - np.arange(t))
    bp_r = rev_r[:, :, None, :] * bbar_r[None] - rev_i[:, :, None, :] * bbar_i[None]
    bp_i = rev_r[:, :, None, :] * bbar_i[None] + rev_i[:, :, None, :] * bbar_r[None]
    bpc = (jnp.stack([bp_r, bp_i], axis=3).reshape(t, S5_MBLK, LANE, 2 * S5_STATE).transpose(1, 0, 2, 3)
           .reshape(S5_MBLK, t * LANE, 2 * S5_STATE))
    cpo = jnp.stack([cp_r[1:], -cp_i[1:]]).reshape(2, t, S5_MBLK, S5_GPB, S5_GROUP, S5_STATE)
    cpc = cpo.transpose(2, 0, 3, 5, 1, 4).reshape(S5_MBLK, 2 * S5_HALF, t * S5_GROUP)
    a_t = jnp.concatenate([pw_r[t].reshape(S5_MBLK, 1, S5_HALF), pw_i[t].reshape(S5_MBLK, 1, S5_HALF)], axis=2)
    w_in, w_c = _s5_wgen(kc, bpc, cpc)
    return w_in, w_c, a_t


def _s5_wgen_kernel(kc_ref, bp_ref, cp_ref, win_ref, wc_ref):
    t = S5_CHUNK
    wide = t * LANE
    gs, ps = int(math.log2(S5_GROUP)), int(math.log2(S5_STATE))

    def expand(x, copy_mask, group_mask):
        ex = jnp.where(copy_mask, 1.0, 0.0).astype(BF16)
        return jnp.where(group_mask, jnp.dot(x.astype(BF16), ex, preferred_element_type=F32), 0.0).astype(BF16)

    r, c = _iota((S5_GROUP, LANE), 0), _iota((S5_GROUP, LANE), 1)
    copy_k = r == (c & (S5_GROUP - 1))
    r, c = _iota((LANE, LANE), 0), _iota((LANE, LANE), 1)
    same_k = (r >> gs) == (c >> gs)
    zero = jnp.zeros((LANE, LANE), BF16)
    for lag in range(t):
        bd = expand(kc_ref[0, lag * LANE:(lag + 1) * LANE, :], copy_k, same_k)
        for s in range(t - lag):
            win_ref[0, s * LANE:(s + 1) * LANE, (s + lag) * LANE:(s + lag + 1) * LANE] = bd
    for s in range(1, t):
        for tt in range(s):
            win_ref[0, s * LANE:(s + 1) * LANE, tt * LANE:(tt + 1) * LANE] = zero
    r, c = _iota((LANE, 2 * S5_HALF), 0), _iota((LANE, 2 * S5_HALF), 1)
    copy_b = ((r >> ps) == (c >> (ps + 3))) & ((r & (S5_STATE - 1)) == (c & (S5_STATE - 1)))
    same_b = (r >> gs) == ((c >> ps) & (S5_GPB - 1))
    for s in range(t):
        win_ref[0, s * LANE:(s + 1) * LANE, wide:] = expand(bp_ref[0, s * LANE:(s + 1) * LANE, :],
                                                            copy_b, same_b)
    rows = 2 * S5_STATE * 2
    r, c = _iota((t * S5_GROUP, wide), 0), _iota((t * S5_GROUP, wide), 1)
    copy_c = ((r >> gs) == (c >> (gs + 3))) & ((r & (S5_GROUP - 1)) == (c & (S5_GROUP - 1)))
    for ch in range(2 * S5_HALF // rows):
        r, c = ch * rows + _iota((rows, wide), 0), _iota((rows, wide), 1)
        same_c = ((r >> ps) & (S5_GPB - 1)) == ((c >> gs) & (S5_GPB - 1))
        wc_ref[0, ch * rows:(ch + 1) * rows, :] = expand(cp_ref[0, ch * rows:(ch + 1) * rows, :],
                                                         copy_c, same_c)


def _s5_wgen(kc, bpc, cpc):
    t = S5_CHUNK
    wide = t * LANE
    spec = lambda a: pl.BlockSpec((1,) + a.shape[1:], lambda m: (m, 0, 0))
    return pl.pallas_call(
        _s5_wgen_kernel,
        grid=(S5_MBLK,),
        in_specs=[spec(kc), spec(bpc), spec(cpc)],
        out_specs=[pl.BlockSpec((1, wide, wide + 2 * S5_HALF), lambda m: (m, 0, 0)),
                   pl.BlockSpec((1, 2 * S5_HALF, wide), lambda m: (m, 0, 0))],
        out_shape=[jax.ShapeDtypeStruct((S5_MBLK, wide, wide + 2 * S5_HALF), BF16),
                   jax.ShapeDtypeStruct((S5_MBLK, 2 * S5_HALF, wide), BF16)],
        compiler_params=_cparams(("parallel",)),
        name="s5_wgen",
    )(kc, bpc, cpc)


def _s5_in_kernel(u_ref, w_ref, d_ref, y_ref, b_ref):
    tr = y_ref.shape[1]
    u = jnp.concatenate([u_ref[0, pl.ds(s, tr, stride=S5_CHUNK), :] for s in range(S5_CHUNK)], axis=1)
    res = jnp.dot(u.astype(BF16), w_ref[0], preferred_element_type=F32)
    wide = y_ref.shape[2]
    y_ref[0] = res[:, :wide] + d_ref[0] * u
    b_ref[0] = res[:, wide:]


def _s5_in(proj3, w_in, d_t):
    rows = proj3.shape[1] // S5_CHUNK
    wide = S5_CHUNK * LANE
    tr = min(rows, 512)
    return pl.pallas_call(
        _s5_in_kernel,
        grid=(S5_MBLK, rows // tr),
        in_specs=[pl.BlockSpec((1, tr * S5_CHUNK, LANE), lambda m, r: (CB_SU + m, r, 0)),
                  pl.BlockSpec((1, wide, wide + 2 * S5_HALF), lambda m, r: (m, 0, 0)),
                  pl.BlockSpec((1, 1, wide), lambda m, r: (m, 0, 0))],
        out_specs=[pl.BlockSpec((1, tr, wide), lambda m, r: (m, r, 0)),
                   pl.BlockSpec((1, tr, 2 * S5_HALF), lambda m, r: (m, r, 0))],
        out_shape=[jax.ShapeDtypeStruct((S5_MBLK, rows, wide), F32),
                   jax.ShapeDtypeStruct((S5_MBLK, rows, 2 * S5_HALF), F32)],
        compiler_params=_cparams(("parallel", "parallel")),
        name="s5_in",
    )(proj3, w_in, d_t)


def _s5_scan_kernel(b_ref, a_ref, o_ref, *, bsz, nk):
    ar = a_ref[0, :, :S5_HALF]
    ai = a_ref[0, :, S5_HALF:]

    def body(k, carry):
        out = []
        for b in range(bsz):
            sr, si = carry[2 * b], carry[2 * b + 1]
            row = b * nk + k
            o_ref[0, pl.ds(row, 1), :] = jnp.concatenate([sr, si], axis=1)
            x = b_ref[0, pl.ds(row, 1), :]
            out.append(ar * sr - ai * si + x[:, :S5_HALF])
            out.append(ar * si + ai * sr + x[:, S5_HALF:])
        return tuple(out)

    zero = jnp.zeros((1, S5_HALF), F32)
    lax.fori_loop(0, nk, body, tuple(zero for _ in range(2 * bsz)))


def _s5_scan(bst, a_t, bsz):
    _, rows, wide = bst.shape
    spec = pl.BlockSpec((1, rows, wide), lambda m: (m, 0, 0))
    return pl.pallas_call(
        functools.partial(_s5_scan_kernel, bsz=bsz, nk=rows // bsz),
        grid=(S5_MBLK,),
        in_specs=[spec, pl.BlockSpec((1, 1, wide), lambda m: (m, 0, 0))],
        out_specs=spec,
        out_shape=jax.ShapeDtypeStruct(bst.shape, F32),
        compiler_params=_cparams(("parallel",)),
        name="s5_scan",
    )(bst, a_t)


def _s5_out_kernel(y_ref, s_ref, w_ref, o_ref):
    tr = y_ref.shape[1]
    y = jax.nn.gelu(y_ref[0] + jnp.dot(s_ref[0].astype(BF16), w_ref[0], preferred_element_type=F32))
    for s in range(S5_CHUNK):
        o_ref[0, pl.ds(s, tr, stride=S5_CHUNK), :] = y[:, s * LANE:(s + 1) * LANE]


def _s5_out(y_intra, s_prev, w_c):
    _, rows, wide = y_intra.shape
    tr = min(rows, 512)
    return pl.pallas_call(
        _s5_out_kernel,
        grid=(S5_MBLK, rows // tr),
        in_specs=[pl.BlockSpec((1, tr, wide), lambda m, r: (m, r, 0)),
                  pl.BlockSpec((1, tr, 2 * S5_HALF), lambda m, r: (m, r, 0)),
                  pl.BlockSpec((1, 2 * S5_HALF, wide), lambda m, r: (m, 0, 0))],
        out_specs=pl.BlockSpec((1, tr * S5_CHUNK, LANE), lambda m, r: (m, r, 0)),
        out_shape=jax.ShapeDtypeStruct((S5_MBLK, rows * S5_CHUNK, LANE), F32),
        compiler_params=_cparams(("parallel", "parallel")),
        name="s5_out",
    )(y_intra, s_prev, w_c)


def _glu_kernel(y_ref, z_ref, w_ref, o_ref):
    nb = y_ref.shape[0]
    y = jnp.concatenate([y_ref[c] for c in range(nb)], axis=1)
    z = jnp.concatenate([z_ref[c] for c in range(nb)], axis=1)
    gate = jax.nn.sigmoid(jnp.dot(y.astype(BF16), w_ref[...], preferred_element_type=F32))
    o = (y * gate * (z * jax.nn.sigmoid(z))).astype(BF16)
    for c in range(nb):
        o_ref[c] = o[:, c * LANE:(c + 1) * LANE]


def _glu(y5, proj3, glu_w):
    nb, rows, _ = y5.shape
    tm = 512
    spec = lambda blk: pl.BlockSpec((nb, tm, LANE), lambda i: (blk, i, 0))
    return pl.pallas_call(
        _glu_kernel,
        grid=(rows // tm,),
        in_specs=[spec(0), spec(CB_SZ // nb), pl.BlockSpec((S5_WIDTH, S5_WIDTH), lambda i: (0, 0))],
        out_specs=spec(0),
        out_shape=jax.ShapeDtypeStruct((nb, rows, LANE), BF16),
        compiler_params=_cparams(("parallel",)),
        name="s5_glu",
    )(y5, proj3, glu_w.astype(BF16))


def _outproj_kernel(a_ref, b_ref, c_ref, w_ref, x_ref, o_ref):
    parts = ([a_ref[h] for h in range(a_ref.shape[0])] + [b_ref[h] for h in range(b_ref.shape[0])]
             + [c_ref[h] for h in range(c_ref.shape[0])])
    mixed = jnp.concatenate(parts, axis=1)
    o_ref[...] = x_ref[...] + jnp.dot(mixed, w_ref[...], preferred_element_type=F32)


def _outproj(m_moba, m_nsa, m_s5, w_out, x2):
    rows, d = x2.shape
    tm = 512
    lspec = lambda n: pl.BlockSpec((n, tm, LANE), lambda i: (0, i, 0))
    return pl.pallas_call(
        _outproj_kernel,
        grid=(rows // tm,),
        in_specs=[lspec(m_moba.shape[0]), lspec(m_nsa.shape[0]), lspec(m_s5.shape[0]),
                  pl.BlockSpec(w_out.shape, lambda i: (0, 0)),
                  pl.BlockSpec((tm, d), lambda i: (i, 0))],
        out_specs=pl.BlockSpec((tm, d), lambda i: (i, 0)),
        out_shape=jax.ShapeDtypeStruct((rows, d), F32),
        compiler_params=_cparams(("parallel",)),
        name="outproj",
    )(m_moba, m_nsa, m_s5, w_out.astype(BF16), x2)


def _wperm_kernel(w_ref, o_ref):
    mw, nw, kvw, ng = MOBA_HEADS * HEAD_DIM, NSA_HEADS * HEAD_DIM, HEAD_DIM, 3 * NSA_HEADS
    o_kv = 4 * mw + nw
    o_ng = o_kv + 6 * kvw
    o_nz = o_ng + ng
    run = nw + 2 * S5_WIDTH

    def put(dst_blk, src, width):
        o_ref[:, dst_blk * LANE:dst_blk * LANE + width] = w_ref[:, src:src + width].astype(BF16)

    put(CB_MQ, 0, o_kv)
    put(CB_NZ, o_nz, run)
    put(CB_NKS, o_kv + 2 * kvw, 4 * kvw)
    put(CB_NKC, o_kv, 2 * kvw)
    tail = w_ref[:, o_ng:o_ng + 2 * LANE]
    o_ref[:, CB_NG * LANE:] = jnp.where(_iota(tail.shape, 1) < ng, tail, 0.0).astype(BF16)


def _permute_w_in(w_in_all, layer):
    _, d, width = w_in_all.shape
    tr = 256
    return pl.pallas_call(
        _wperm_kernel,
        grid=(d // tr,),
        in_specs=[pl.BlockSpec((None, tr, width), lambda i: (layer, i, 0))],
        out_specs=pl.BlockSpec((tr, IN_BLOCKS * LANE), lambda i: (i, 0)),
        out_shape=jax.ShapeDtypeStruct((d, IN_BLOCKS * LANE), BF16),
        compiler_params=_cparams(("parallel",)),
        name="w_in_permute",
    )(w_in_all)


def _layer(x2, bsz, seq, tabs, ctabs, overlap, w_in, norm_w, w_out, moba_q_norm, moba_k_norm, nsa_q_norm,
           nsa_kc_norm, nsa_ks_norm, nsa_kw_norm, nsa_pe_k, nsa_pe_v, nsa_cmp_k_w1, nsa_cmp_k_w2,
           nsa_cmp_v_w1, nsa_cmp_v_w2, s5_a_re, s5_a_im, s5_b_re, s5_b_im, s5_c_re, s5_c_im, s5_d,
           s5_log_dt, s5_glu_w):
    proj3 = _inproj(x2, norm_w, w_in)
    mk_n, mv_b, nk, kmean = _kprep(proj3, seq, moba_k_norm, nsa_ks_norm, nsa_kw_norm, tabs)
    m_moba = _moba(proj3, bsz, seq, moba_q_norm, tabs, kmean.transpose(1, 0, 2), mk_n, mv_b)
    k_cmp, v_cmp = _compress(proj3, bsz, seq, nsa_pe_k, nsa_pe_v, nsa_cmp_k_w1, nsa_cmp_k_w2,
                             nsa_cmp_v_w1, nsa_cmp_v_w2, nsa_kc_norm, ctabs)
    m_nsa = _nsa(proj3, bsz, seq, nsa_q_norm, tabs, k_cmp, v_cmp, overlap, nk)
    w_s5_in, w_s5_out, a_t = _s5_weights(s5_a_re, s5_a_im, s5_b_re, s5_b_im, s5_c_re, s5_c_im, s5_log_dt)
    d_t = jnp.tile(s5_d.astype(F32).reshape(S5_MBLK, 1, LANE), (1, 1, S5_CHUNK))
    y_intra, bst = _s5_in(proj3, w_s5_in, d_t)
    y5 = _s5_out(y_intra, _s5_scan(bst, a_t, bsz), w_s5_out)
    m_s5 = _glu(y5, proj3, s5_glu_w)
    return _outproj(m_moba, m_nsa, m_s5, w_out, x2)


def kernel(x, norm_w, w_in, w_out, moba_q_norm, moba_k_norm, nsa_q_norm, nsa_kc_norm, nsa_ks_norm, nsa_kw_norm, nsa_pe_k, nsa_pe_v, nsa_cmp_k_w1, nsa_cmp_k_w2, nsa_cmp_v_w1, nsa_cmp_v_w2, s5_a_re, s5_a_im, s5_b_re, s5_b_im, s5_c_re, s5_c_im, s5_d, s5_log_dt, s5_glu_w):
    bsz, seq, d = x.shape
    tabs = _rope_tables(jnp.arange(seq, dtype=F32))
    ncmp = seq // NSA_CMP_STRIDE
    ctabs = _rope_tables(jnp.arange(ncmp, dtype=F32) * NSA_CMP_STRIDE + (NSA_CMP_LEN - 1))
    nsel = seq // NSA_SEL_BLOCK
    ci = np.arange(ncmp)[:, None] * NSA_CMP_STRIDE
    sj = np.arange(nsel)[None, :] * NSA_SEL_BLOCK
    overlap = np.zeros((ncmp, LANE), np.float32)
    overlap[:, :nsel] = (ci < sj + NSA_SEL_BLOCK) & (ci + NSA_CMP_LEN > sj)
    overlap = jnp.asarray(overlap)
    params = (norm_w, w_out, moba_q_norm, moba_k_norm, nsa_q_norm, nsa_kc_norm, nsa_ks_norm,
              nsa_kw_norm, nsa_pe_k, nsa_pe_v, nsa_cmp_k_w1, nsa_cmp_k_w2, nsa_cmp_v_w1, nsa_cmp_v_w2,
              s5_a_re, s5_a_im, s5_b_re, s5_b_im, s5_c_re, s5_c_im, s5_d, s5_log_dt, s5_glu_w)
    x2 = x.reshape(bsz * seq, d)
    for layer in range(norm_w.shape[0]):
        x2 = _layer(x2, bsz, seq, tabs, ctabs, overlap, _permute_w_in(w_in, layer), *[p[layer] for p in params])
    return x2.reshape(bsz, seq, d)
```

```python
import functools
import math

import numpy as np
import jax
import jax.numpy as jnp
from jax import lax
from jax.experimental import pallas as pl
from jax.experimental.pallas import tpu as pltpu

F32 = jnp.float32
BF16 = jnp.bfloat16
HIGHEST = lax.Precision.HIGHEST

LANE = 128
HEAD_DIM = 128
ROPE_DIM = HEAD_DIM // 4
ROPE_HALF = ROPE_DIM // 2
ROPE_THETA = 500000.0
EPS = 1e-6
SCALE = HEAD_DIM ** -0.5
NEG = -1e30

MOBA_HEADS = 4
MOBA_BLOCK = 256
MOBA_TOPK = 3

NSA_HEADS = 4
NSA_CMP_LEN = 32
NSA_CMP_STRIDE = 16
NSA_SEL_BLOCK = 64
NSA_SEL_TOPN = 16
NSA_WINDOW = 512
NSA_TQ = 256
NSA_SEL_PER_TILE = NSA_TQ // NSA_SEL_BLOCK
assert NSA_WINDOW == 2 * NSA_TQ

S5_WIDTH = 1024
S5_GROUP = 16
S5_GROUPS = S5_WIDTH // S5_GROUP
S5_STATE = 64
S5_CHUNK = 16
S5_MBLK = S5_WIDTH // LANE
S5_GPB = LANE // S5_GROUP
S5_HALF = S5_GPB * S5_STATE

CB_MQ, CB_MK, CB_MV, CB_MZ, CB_NQ, CB_NZ, CB_SU, CB_SZ = 0, 4, 8, 12, 16, 20, 24, 32
CB_NKS, CB_NVS, CB_NKW, CB_NVW, CB_NKC, CB_NVC, CB_NG = 40, 41, 42, 43, 44, 45, 46
IN_BLOCKS = 48

VMEM_LIMIT = 56 * 1024 * 1024


def _cparams(sem):
    return pltpu.CompilerParams(dimension_semantics=sem, vmem_limit_bytes=VMEM_LIMIT)


def _iota(shape, dim):
    return lax.broadcasted_iota(jnp.int32, shape, dim)


def _head_norm(x, w):
    return x * lax.rsqrt(jnp.mean(x * x, axis=-1, keepdims=True) + EPS) * w


def _rope(x, c, s1, s2):
    return x * c + pltpu.roll(x, LANE - ROPE_HALF, 1) * s1 + pltpu.roll(x, ROPE_HALF, 1) * s2


def _dot_nt(a, b):
    return lax.dot_general(a, b, (((1,), (1,)), ((), ())), preferred_element_type=F32)


def _rope_tables(pos):
    inv = ROPE_THETA ** (-jnp.arange(0, ROPE_DIM, 2, dtype=F32) / ROPE_DIM)
    ang = pos.astype(F32)[:, None] * inv[None, :]
    cos, sin = jnp.cos(ang), jnp.sin(ang)
    n = pos.shape[0]
    c = jnp.concatenate([cos, cos, jnp.ones((n, LANE - ROPE_DIM), F32)], axis=1)
    s1 = jnp.concatenate([-sin, jnp.zeros((n, LANE - ROPE_HALF), F32)], axis=1)
    s2 = jnp.concatenate([jnp.zeros((n, ROPE_HALF), F32), sin, jnp.zeros((n, LANE - ROPE_DIM), F32)], axis=1)
    return c, s1, s2


def _inproj_kernel(x_ref, nw_ref, w_ref, o_ref, h_ref):
    @pl.when(pl.program_id(1) == 0)
    def _():
        x = x_ref[...]
        ms = jnp.mean(x * x, axis=-1, keepdims=True)
        h_ref[...] = (x * lax.rsqrt(ms + EPS) * nw_ref[...]).astype(BF16)

    res = jnp.dot(h_ref[...], w_ref[...], preferred_element_type=F32)
    for c in range(o_ref.shape[0]):
        o_ref[c] = res[:, c * LANE:(c + 1) * LANE]


def _inproj(x2, norm_w, w_perm):
    rows, d = x2.shape
    tm, tn = 1024, 1024
    nb = tn // LANE
    return pl.pallas_call(
        _inproj_kernel,
        grid=(rows // tm, IN_BLOCKS // nb),
        in_specs=[pl.BlockSpec((tm, d), lambda i, j: (i, 0)),
                  pl.BlockSpec((1, d), lambda i, j: (0, 0)),
                  pl.BlockSpec((d, tn), lambda i, j: (0, j))],
        out_specs=pl.BlockSpec((nb, tm, LANE), lambda i, j: (j, i, 0)),
        out_shape=jax.ShapeDtypeStruct((IN_BLOCKS, rows, LANE), F32),
        scratch_shapes=[pltpu.VMEM((tm, d), BF16)],
        compiler_params=_cparams(("parallel", "arbitrary")),
        name="inproj",
    )(x2, norm_w.reshape(1, d), w_perm)


def _kprep_kernel(mk_ref, mv_ref, nk_ref, mkw_ref, ksw_ref, kww_ref, c_ref, s1_ref, s2_ref,
                  mk_o, mv_o, nk_o, km_o):
    c, s1, s2 = c_ref[...], s1_ref[...], s2_ref[...]
    means = []
    for h in range(MOBA_HEADS):
        k = _rope(_head_norm(mk_ref[h], mkw_ref[...]), c, s1, s2)
        mk_o[h] = k.astype(BF16)
        mv_o[h] = mv_ref[h].T.astype(BF16)
        means.append(jnp.mean(k, axis=0, keepdims=True))
    km_o[0] = jnp.concatenate(means, axis=0)
    nk_o[0] = _rope(_head_norm(nk_ref[0], ksw_ref[...]), c, s1, s2).astype(BF16)
    nk_o[1] = nk_ref[1].astype(BF16)
    nk_o[2] = _rope(_head_norm(nk_ref[2], kww_ref[...]), c, s1, s2).astype(BF16)
    nk_o[3] = nk_ref[3].astype(BF16)


def _kprep(proj3, seq, mk_w, ks_w, kw_w, tabs):
    rows = proj3.shape[1]
    t = MOBA_BLOCK
    nt = seq // t
    blk4 = lambda cb: pl.BlockSpec((4, t, LANE), lambda i: (cb // 4, i, 0))
    wspec = pl.BlockSpec((1, LANE), lambda i: (0, 0))
    tspec = pl.BlockSpec((t, LANE), lambda i: (i % nt, 0))
    out4 = pl.BlockSpec((4, t, LANE), lambda i: (0, i, 0))
    return pl.pallas_call(
        _kprep_kernel,
        grid=(rows // t,),
        in_specs=[blk4(CB_MK), blk4(CB_MV), blk4(CB_NKS), wspec, wspec, wspec, tspec, tspec, tspec],
        out_specs=[out4, pl.BlockSpec((4, None, LANE, t), lambda i: (0, i // nt, 0, i % nt)), out4,
                   pl.BlockSpec((1, MOBA_HEADS, LANE), lambda i: (i, 0, 0))],
        out_shape=[jax.ShapeDtypeStruct((4, rows, LANE), BF16),
                   jax.ShapeDtypeStruct((4, rows // seq, LANE, seq), BF16),
                   jax.ShapeDtypeStruct((4, rows, LANE), BF16),
                   jax.ShapeDtypeStruct((rows // t, MOBA_HEADS, LANE), F32)],
        compiler_params=_cparams(("parallel",)),
        name="kprep",
    )(proj3, proj3, proj3, mk_w.reshape(1, LANE), ks_w.reshape(1, LANE), kw_w.reshape(1, LANE), *tabs)


def _lanes(x, n):
    return x if n == LANE else jnp.concatenate([x] * (n // LANE), axis=1)


def _softmax_part(s, m):
    m_new = jnp.maximum(m, jnp.max(s, axis=1, keepdims=True))
    return m_new, jnp.exp(s - _lanes(m_new, s.shape[1])), jnp.exp(m - m_new)


def _flash_step(s, v, m, l, acc):
    m_new, p, alpha = _softmax_part(s, m)
    l_new = alpha * l + jnp.sum(p, axis=1, keepdims=True)
    acc_new = alpha * acc + jnp.dot(p.astype(BF16), v, preferred_element_type=F32)
    return m_new, l_new, acc_new


def _pad_rows(x, rows):
    return x if x.shape[0] == rows else jnp.concatenate(
        [x, jnp.zeros((rows - x.shape[0], x.shape[1]), x.dtype)], axis=0)


def _topk_mask_t(score_t, row_idx, k):
    n = score_t.shape[0]
    rank = jnp.zeros(score_t.shape, F32)
    for mm in range(n):
        cm = score_t[mm:mm + 1, :]
        beats = (cm > score_t) | ((cm == score_t) & (row_idx > mm))
        rank = rank + jnp.where(beats, 1.0, 0.0)
    return rank < k


def _moba_kernel(q_ref, z_ref, qw_ref, c_ref, s1_ref, s2_ref, km_ref, k_ref, vt_ref, o_ref,
                 qt_s, sel_s, s_s, p_s, acc_s):
    i = pl.program_id(1)
    t = MOBA_BLOCK
    nb = km_ref.shape[1]
    nh = q_ref.shape[0]
    npairs = (i + 1) // 2
    blk_t = _iota((nb, t), 0)
    past = blk_t < i
    causal = _iota((t, t), 0) <= _iota((t, t), 1)
    start = pl.multiple_of(i * t, t)

    def scores(h, pair):
        off = pl.multiple_of(pair * (2 * t), 2 * t)
        return jnp.dot(k_ref[h, pl.ds(off, 2 * t), :], qt_s[h], preferred_element_type=F32)

    def weighted_values(h, pair):
        off = pl.multiple_of(pair * (2 * t), 2 * t)
        off_b = pl.multiple_of(off + t, t)
        ea, eb = sel_s[h, pl.ds(2 * pair, 1), :], sel_s[h, pl.ds(2 * pair + 1, 1), :]
        return (ea * jnp.dot(vt_ref[h, :, pl.ds(off, t)], p_s[h, :t, :], preferred_element_type=F32)
                + eb * jnp.dot(vt_ref[h, :, pl.ds(off_b, t)], p_s[h, t:, :], preferred_element_type=F32))

    state = []
    for h in range(nh):
        qf = _rope(_head_norm(q_ref[h], qw_ref[...]), c_ref[...], s1_ref[...], s2_ref[...])
        qt = (qf * SCALE).T.astype(BF16)
        qt_s[h] = qt
        gate_t = lax.dot_general(_pad_rows(km_ref[h], LANE), qf, (((1,), (1,)), ((), ())), precision=HIGHEST,
                                 preferred_element_type=F32)[:nb]
        top = _topk_mask_t(jnp.where(past, gate_t, -jnp.inf), blk_t, MOBA_TOPK)
        sel_s[h] = jnp.where(top & past, 1.0, 0.0)
        s = jnp.where(causal, jnp.dot(k_ref[h, pl.ds(start, t), :], qt, preferred_element_type=F32), NEG)
        m = jnp.max(s, axis=0, keepdims=True)
        p = jnp.exp(s - m)
        acc_s[h] = jnp.dot(vt_ref[h, :, pl.ds(start, t)], p.astype(BF16), preferred_element_type=F32)
        state += [m, jnp.sum(p, axis=0, keepdims=True)]
        s_s[0, h] = scores(h, 0)
        p_s[h] = jnp.zeros((2 * t, t), BF16)

    def body(pair, carry):
        slot = pair % 2
        nxt = jnp.minimum(pair + 1, npairs - 1)
        prev = jnp.maximum(pair - 1, 0)
        out = []
        for h in range(nh):
            m, l = carry[2 * h:2 * h + 2]
            s_s[1 - slot, h] = scores(h, nxt)
            acc = acc_s[h] + weighted_values(h, prev)
            s = s_s[slot, h]
            m_new = jnp.maximum(m, jnp.max(s, axis=0, keepdims=True))
            p = jnp.exp(s - m_new)
            alpha = jnp.exp(m - m_new)
            ea, eb = sel_s[h, pl.ds(2 * pair, 1), :], sel_s[h, pl.ds(2 * pair + 1, 1), :]
            l_new = (alpha * l + ea * jnp.sum(p[:t], axis=0, keepdims=True)
                     + eb * jnp.sum(p[t:], axis=0, keepdims=True))
            acc_s[h] = alpha * acc
            p_s[h] = p.astype(BF16)
            out += [m_new, l_new]
        return tuple(out)

    state = lax.fori_loop(0, npairs, body, tuple(state))
    last = jnp.maximum(npairs - 1, 0)
    for h in range(nh):
        l = state[2 * h + 1]
        acc = acc_s[h] + weighted_values(h, last)
        z = z_ref[h]
        o_ref[h] = ((acc * (1.0 / l)).T * (z * jax.nn.sigmoid(z))).astype(BF16)


def _moba(proj3, bsz, seq, q_w, tabs, km_t, mk_n, mv_t):
    rows = proj3.shape[1]
    t = MOBA_BLOCK
    nb = seq // t
    nh = MOBA_HEADS
    qspec = lambda cb: pl.BlockSpec((nh, t, LANE), lambda b, i: (cb // nh, b * nb + i, 0))
    tspec = pl.BlockSpec((t, LANE), lambda b, i: (i, 0))
    return pl.pallas_call(
        _moba_kernel,
        grid=(bsz, nb),
        in_specs=[qspec(CB_MQ), qspec(CB_MZ), pl.BlockSpec((1, LANE), lambda b, i: (0, 0)),
                  tspec, tspec, tspec,
                  pl.BlockSpec((nh, nb, LANE), lambda b, i: (0, b, 0)),
                  pl.BlockSpec((nh, seq, LANE), lambda b, i: (0, b, 0)),
                  pl.BlockSpec((nh, None, LANE, seq), lambda b, i: (0, b, 0, 0))],
        out_specs=pl.BlockSpec((nh, t, LANE), lambda b, i: (0, b * nb + i, 0)),
        out_shape=jax.ShapeDtypeStruct((nh, rows, LANE), BF16),
        scratch_shapes=[pltpu.VMEM((nh, LANE, t), BF16),
                        pltpu.VMEM((nh, nb, t), F32),
                        pltpu.VMEM((2, nh, 2 * t, t), F32),
                        pltpu.VMEM((nh, 2 * t, t), BF16),
                        pltpu.VMEM((nh, LANE, t), F32)],
        compiler_params=_cparams(("parallel", "parallel")),
        name="moba",
    )(proj3, proj3, q_w.reshape(1, LANE), *tabs, km_t, mk_n, mv_t)


def _cmp_kernel(hk_ref, hv_ref, pek_ref, pev_ref, w1k_ref, w2k_ref, w1v_ref, w2v_ref, nw_ref,
                c_ref, s1_ref, s2_ref, ko_ref, vo_ref):
    st = NSA_CMP_STRIDE
    nh = ko_ref.shape[1]

    def compress(x_ref, pe_ref, w1_ref, w2_ref):
        a = jnp.zeros((nh, LANE), F32)
        b = jnp.zeros((nh, LANE), F32)
        for l in range(st):
            x = x_ref[0, pl.ds(l, nh, stride=st), :]
            a = a + jnp.dot((x + pe_ref[l:l + 1, :]).astype(BF16), w1_ref[l * LANE:(l + 1) * LANE, :],
                            preferred_element_type=F32)
            b = b + jnp.dot((x + pe_ref[st + l:st + l + 1, :]).astype(BF16),
                            w1_ref[(st + l) * LANE:(st + l + 1) * LANE, :], preferred_element_type=F32)
        pre = a + pltpu.roll(b, nh - 1, 0)
        return jnp.dot(jax.nn.gelu(pre).astype(BF16), w2_ref[...], preferred_element_type=F32)

    kc = compress(hk_ref, pek_ref, w1k_ref, w2k_ref)
    vc = compress(hv_ref, pev_ref, w1v_ref, w2v_ref)
    ko_ref[0] = _rope(_head_norm(kc, nw_ref[...]), c_ref[...], s1_ref[...], s2_ref[...]).astype(BF16)
    vo_ref[0] = vc.astype(BF16)


def _compress(proj3, bsz, seq, pe_k, pe_v, w1k, w2k, w1v, w2v, kc_w, ctabs):
    nh = seq // NSA_CMP_STRIDE
    wide = NSA_CMP_LEN * HEAD_DIM
    hspec = lambda cb: pl.BlockSpec((1, seq, LANE), lambda b: (cb, b, 0))
    full = lambda shape: pl.BlockSpec(shape, lambda b: tuple(0 for _ in shape))
    ospec = pl.BlockSpec((1, nh, LANE), lambda b: (b, 0, 0))
    return pl.pallas_call(
        _cmp_kernel,
        grid=(bsz,),
        in_specs=[hspec(CB_NKC), hspec(CB_NVC), full((NSA_CMP_LEN, LANE)), full((NSA_CMP_LEN, LANE)),
                  full((wide, LANE)), full((LANE, LANE)), full((wide, LANE)), full((LANE, LANE)),
                  full((1, LANE)), full((nh, LANE)), full((nh, LANE)), full((nh, LANE))],
        out_specs=[ospec, ospec],
        out_shape=[jax.ShapeDtypeStruct((bsz, nh, LANE), BF16)] * 2,
        compiler_params=_cparams(("parallel",)),
        name="nsa_compress",
    )(proj3, proj3, pe_k, pe_v, w1k.astype(BF16), w2k.astype(BF16),
      w1v.astype(BF16), w2v.astype(BF16), kc_w.reshape(1, LANE), *ctabs)


def _nsa_kernel(q_ref, z_ref, g_ref, qw_ref, c_ref, s1_ref, s2_ref, kc_ref, vc_ref, ov_ref,
                ks_ref, vs_ref, kw_ref, vw_ref, o_ref, qb_s, m_s, l_s, acc_s, ocmp_s, osel_s):
    i = pl.program_id(1)
    t = NSA_TQ
    ncmp = kc_ref.shape[1]
    nsel = ks_ref.shape[0] // NSA_SEL_BLOCK
    sel_shift = int(math.log2(NSA_SEL_BLOCK))
    trow = i * t + _iota((t, 1), 0)
    row = _iota((t, t), 0)
    col = _iota((t, t), 1)
    causal = col <= row

    c, s1, s2 = c_ref[...], s1_ref[...], s2_ref[...]
    for h in range(NSA_HEADS):
        qb_s[h] = (_rope(_head_norm(q_ref[h], qw_ref[...]), c, s1, s2) * SCALE).astype(BF16)

    valid = (_iota((t, ncmp), 1) * NSA_CMP_STRIDE + (NSA_CMP_LEN - 1)) <= trow
    kc, vc = kc_ref[0], vc_ref[0]
    psum = jnp.zeros((t, ncmp), F32)
    for h in range(NSA_HEADS):
        s = jnp.where(valid, _dot_nt(qb_s[h], kc), NEG)
        mx = jnp.max(s, axis=1, keepdims=True)
        e = jnp.exp(s - mx)
        p = e * jnp.where(mx > 0.5 * NEG, 1.0 / jnp.sum(e, axis=1, keepdims=True), 0.0)
        ocmp_s[h] = jnp.dot(p.astype(BF16), vc, preferred_element_type=F32)
        psum = psum + p

    imp_t = jnp.dot(psum, ov_ref[...], precision=HIGHEST, preferred_element_type=F32).T[:nsel]
    jj = _iota((nsel, t), 0)
    cur = (i * t + _iota((1, t), 1)) >> sel_shift
    score = jnp.where(jj <= cur, imp_t, -jnp.inf)
    score = jnp.where((jj == 0) | (jj == cur) | (jj == cur - 1), jnp.inf, score)
    top = _topk_mask_t(score, jj, NSA_SEL_TOPN)
    selbias = _pad_rows(jnp.where(top, 0.0, NEG), LANE).T.astype(BF16)

    def block_bias(kt):
        onehot = jnp.where(_iota((LANE, t), 0) == kt * NSA_SEL_PER_TILE + (_iota((LANE, t), 1) >> sel_shift),
                           1.0, 0.0).astype(BF16)
        return jnp.dot(selbias, onehot, preferred_element_type=F32)

    own = pl.multiple_of(i * t, t)
    bias = block_bias(i)
    k, v = ks_ref[pl.ds(own, t), :], vs_ref[pl.ds(own, t), :]
    for h in range(NSA_HEADS):
        s = jnp.where(causal, _dot_nt(qb_s[h], k) + bias, NEG)
        m = jnp.max(s, axis=1, keepdims=True) + jnp.zeros((t, LANE), F32)
        p = jnp.exp(s - _lanes(m, t))
        m_s[h] = m
        l_s[h] = jnp.sum(p, axis=1, keepdims=True) + jnp.zeros((t, LANE), F32)
        acc_s[h] = jnp.dot(p.astype(BF16), v, preferred_element_type=F32)

    def sel_body(kt, carry):
        off = pl.multiple_of(kt * t, t)
        bias = block_bias(kt)
        k, v = ks_ref[pl.ds(off, t), :], vs_ref[pl.ds(off, t), :]
        for h in range(NSA_HEADS):
            m, l, acc = _flash_step(_dot_nt(qb_s[h], k) + bias, v, m_s[h], l_s[h], acc_s[h])
            m_s[h] = m
            l_s[h] = l
            acc_s[h] = acc
        return carry

    lax.fori_loop(0, i, sel_body, 0)
    for h in range(NSA_HEADS):
        osel_s[h] = acc_s[h] * (1.0 / l_s[h])

    far = pl.multiple_of(jnp.maximum(i - 2, 0) * t, t)
    mid = pl.multiple_of(jnp.maximum(i - 1, 0) * t, t)
    k3 = jnp.concatenate([kw_ref[pl.ds(far, t), :], kw_ref[pl.ds(mid, t), :], kw_ref[pl.ds(own, t), :]], axis=0)
    v3 = jnp.concatenate([vw_ref[pl.ds(far, t), :], vw_ref[pl.ds(mid, t), :], vw_ref[pl.ds(own, t), :]], axis=0)
    allowed = jnp.concatenate([(col > row) & (i >= 2), jnp.full((t, t), True) & (i >= 1), causal], axis=1)
    g = jax.nn.sigmoid(g_ref[0])
    for h in range(NSA_HEADS):
        s = jnp.where(allowed, _dot_nt(qb_s[h], k3), NEG)
        p = jnp.exp(s - jnp.max(s, axis=1, keepdims=True))
        o_win = jnp.dot(p.astype(BF16), v3, preferred_element_type=F32) * (1.0 / jnp.sum(p, axis=1, keepdims=True))
        o = (g[:, 3 * h:3 * h + 1] * ocmp_s[h] + g[:, 3 * h + 1:3 * h + 2] * osel_s[h]
             + g[:, 3 * h + 2:3 * h + 3] * o_win)
        z = z_ref[h]
        o_ref[h] = (o * (z * jax.nn.sigmoid(z))).astype(BF16)


def _nsa(proj3, bsz, seq, q_w, tabs, k_cmp, v_cmp, overlap, nk):
    rows = proj3.shape[1]
    t = NSA_TQ
    nt = seq // t
    ncmp = seq // NSA_CMP_STRIDE
    nsel = seq // NSA_SEL_BLOCK
    blk4 = lambda cb: pl.BlockSpec((4, t, LANE), lambda b, i: (cb // 4, b * nt + i, 0))
    tspec = pl.BlockSpec((t, LANE), lambda b, i: (i, 0))
    cspec = pl.BlockSpec((1, ncmp, LANE), lambda b, i: (b, 0, 0))
    kvspec = lambda which: pl.BlockSpec((None, seq, LANE), lambda b, i: (which, b, 0))
    return pl.pallas_call(
        _nsa_kernel,
        grid=(bsz, nt),
        in_specs=[blk4(CB_NQ), blk4(CB_NZ),
                  pl.BlockSpec((1, t, LANE), lambda b, i: (CB_NG, b * nt + i, 0)),
                  pl.BlockSpec((1, LANE), lambda b, i: (0, 0)), tspec, tspec, tspec, cspec, cspec,
                  pl.BlockSpec((ncmp, LANE), lambda b, i: (0, 0)),
                  kvspec(0), kvspec(1), kvspec(2), kvspec(3)],
        out_specs=pl.BlockSpec((4, t, LANE), lambda b, i: (0, b * nt + i, 0)),
        out_shape=jax.ShapeDtypeStruct((NSA_HEADS, rows, LANE), BF16),
        scratch_shapes=[pltpu.VMEM((NSA_HEADS, t, LANE), BF16),
                        pltpu.VMEM((NSA_HEADS, t, LANE), F32),
                        pltpu.VMEM((NSA_HEADS, t, LANE), F32),
                        pltpu.VMEM((NSA_HEADS, t, LANE), F32),
                        pltpu.VMEM((NSA_HEADS, t, LANE), F32),
                        pltpu.VMEM((NSA_HEADS, t, LANE), F32)],
        compiler_params=_cparams(("parallel", "parallel")),
        name="nsa",
    )(proj3, proj3, proj3, q_w.reshape(1, LANE), *tabs, k_cmp, v_cmp, overlap, nk, nk, nk, nk)


def _s5_weights(a_re, a_im, b_re, b_im, c_re, c_im, log_dt):
    t = S5_CHUNK
    dt = jnp.exp(log_dt.astype(F32))[:, None]
    ar, ai = a_re.astype(F32), a_im.astype(F32)
    ang = dt * ai
    mag = jnp.exp(dt * ar)
    abar_r, abar_i = mag * jnp.cos(ang), mag * jnp.sin(ang)
    nr, ni = abar_r - 1.0, abar_i
    den = ar * ar + ai * ai
    fr = (nr * ar + ni * ai) / den
    fi = (ni * ar - nr * ai) / den
    bt_r, bt_i = b_re.astype(F32).transpose(0, 2, 1), b_im.astype(F32).transpose(0, 2, 1)
    bbar_r = fr[:, None, :] * bt_r - fi[:, None, :] * bt_i
    bbar_i = fr[:, None, :] * bt_i + fi[:, None, :] * bt_r

    def powers(tau):
        tau = jnp.asarray(tau, F32)[:, None, None]
        pmag = jnp.exp(tau * (dt * ar)[None])
        return pmag * jnp.cos(tau * ang[None]), pmag * jnp.sin(tau * ang[None])

    pw_r, pw_i = powers(np.arange(t + 1))
    cr, ci = c_re.astype(F32), c_im.astype(F32)
    cp_r = cr[None] * pw_r[:, :, None, :] - ci[None] * pw_i[:, :, None, :]
    cp_i = cr[None] * pw_i[:, :, None, :] + ci[None] * pw_r[:, :, None, :]
    klag = jnp.sum(cp_r[:t, :, None, :, :] * bbar_r[None, :, :, None, :]
                   - cp_i[:t, :, None, :, :] * bbar_i[None, :, :, None, :], axis=-1)
    kc = (klag.reshape(t, S5_MBLK, S5_GPB, S5_GROUP, S5_GROUP).transpose(1, 0, 2, 3, 4)
          .reshape(S5_MBLK, t * LANE, S5_GROUP))
    rev_r, rev_i = powers(t - 1 - np.arange(t))
    bp_r = rev_r[:, :, None, :] * bbar_r[None] - rev_i[:, :, None, :] * bbar_i[None]
    bp_i = rev_r[:, :, None, :] * bbar_i[None] + rev_i[:, :, None, :] * bbar_r[None]
    bpc = (jnp.stack([bp_r, bp_i], axis=3).reshape(t, S5_MBLK, LANE, 2 * S5_STATE).transpose(1, 0, 2, 3)
           .reshape(S5_MBLK, t * LANE, 2 * S5_STATE))
    cpo = jnp.stack([cp_r[1:], -cp_i[1:]]).reshape(2, t, S5_MBLK, S5_GPB, S5_GROUP, S5_STATE)
    cpc = cpo.transpose(2, 0, 3, 5, 1, 4).reshape(S5_MBLK, 2 * S5_HALF, t * S5_GROUP)
    a_t = jnp.concatenate([pw_r[t].reshape(S5_MBLK, 1, S5_HALF), pw_i[t].reshape(S5_MBLK, 1, S5_HALF)], axis=2)
    w_in, w_c = _s5_wgen(kc, bpc, cpc)
    return w_in, w_c, a_t


def _s5_wgen_kernel(kc_ref, bp_ref, cp_ref, win_ref, wc_ref):
    t = S5_CHUNK
    wide = t * LANE
    gs, ps = int(math.log2(S5_GROUP)), int(math.log2(S5_STATE))

    def expand(x, copy_mask, group_mask):
        ex = jnp.where(copy_mask, 1.0, 0.0).astype(BF16)
        return jnp.where(group_mask, jnp.dot(x.astype(BF16), ex, preferred_element_type=F32), 0.0).astype(BF16)

    r, c = _iota((S5_GROUP, LANE), 0), _iota((S5_GROUP, LANE), 1)
    copy_k = r == (c & (S5_GROUP - 1))
    r, c = _iota((LANE, LANE), 0), _iota((LANE, LANE), 1)
    same_k = (r >> gs) == (c >> gs)
    zero = jnp.zeros((LANE, LANE), BF16)
    for lag in range(t):
        bd = expand(kc_ref[0, lag * LANE:(lag + 1) * LANE, :], copy_k, same_k)
        for s in range(t - lag):
            win_ref[0, s * LANE:(s + 1) * LANE, (s + lag) * LANE:(s + lag + 1) * LANE] = bd
    for s in range(1, t):
        for tt in range(s):
            win_ref[0, s * LANE:(s + 1) * LANE, tt * LANE:(tt + 1) * LANE] = zero
    r, c = _iota((LANE, 2 * S5_HALF), 0), _iota((LANE, 2 * S5_HALF), 1)
    copy_b = ((r >> ps) == (c >> (ps + 3))) & ((r & (S5_STATE - 1)) == (c & (S5_STATE - 1)))
    same_b = (r >> gs) == ((c >> ps) & (S5_GPB - 1))
    for s in range(t):
        win_ref[0, s * LANE:(s + 1) * LANE, wide:] = expand(bp_ref[0, s * LANE:(s + 1) * LANE, :],
                                                            copy_b, same_b)
    rows = 2 * S5_STATE * 2
    r, c = _iota((t * S5_GROUP, wide), 0), _iota((t * S5_GROUP, wide), 1)
    copy_c = ((r >> gs) == (c >> (gs + 3))) & ((r & (S5_GROUP - 1)) == (c & (S5_GROUP - 1)))
    for ch in range(2 * S5_HALF // rows):
        r, c = ch * rows + _iota((rows, wide), 0), _iota((rows, wide), 1)
        same_c = ((r >> ps) & (S5_GPB - 1)) == ((c >> gs) & (S5_GPB - 1))
        wc_ref[0, ch * rows:(ch + 1) * rows, :] = expand(cp_ref[0, ch * rows:(ch + 1) * rows, :],
                                                         copy_c, same_c)


def _s5_wgen(kc, bpc, cpc):
    t = S5_CHUNK
    wide = t * LANE
    spec = lambda a: pl.BlockSpec((1,) + a.shape[1:], lambda m: (m, 0, 0))
    return pl.pallas_call(
        _s5_wgen_kernel,
        grid=(S5_MBLK,),
        in_specs=[spec(kc), spec(bpc), spec(cpc)],
        out_specs=[pl.BlockSpec((1, wide, wide + 2 * S5_HALF), lambda m: (m, 0, 0)),
                   pl.BlockSpec((1, 2 * S5_HALF, wide), lambda m: (m, 0, 0))],
        out_shape=[jax.ShapeDtypeStruct((S5_MBLK, wide, wide + 2 * S5_HALF), BF16),
                   jax.ShapeDtypeStruct((S5_MBLK, 2 * S5_HALF, wide), BF16)],
        compiler_params=_cparams(("parallel",)),
        name="s5_wgen",
    )(kc, bpc, cpc)


def _s5_in_kernel(u_ref, w_ref, d_ref, y_ref, b_ref):
    tr = y_ref.shape[1]
    u = jnp.concatenate([u_ref[0, pl.ds(s, tr, stride=S5_CHUNK), :] for s in range(S5_CHUNK)], axis=1)
    res = jnp.dot(u.astype(BF16), w_ref[0], preferred_element_type=F32)
    wide = y_ref.shape[2]
    y_ref[0] = res[:, :wide] + d_ref[0] * u
    b_ref[0] = res[:, wide:]


def _s5_in(proj3, w_in, d_t):
    rows = proj3.shape[1] // S5_CHUNK
    wide = S5_CHUNK * LANE
    tr = min(rows, 512)
    return pl.pallas_call(
        _s5_in_kernel,
        grid=(S5_MBLK, rows // tr),
        in_specs=[pl.BlockSpec((1, tr * S5_CHUNK, LANE), lambda m, r: (CB_SU + m, r, 0)),
                  pl.BlockSpec((1, wide, wide + 2 * S5_HALF), lambda m, r: (m, 0, 0)),
                  pl.BlockSpec((1, 1, wide), lambda m, r: (m, 0, 0))],
        out_specs=[pl.BlockSpec((1, tr, wide), lambda m, r: (m, r, 0)),
                   pl.BlockSpec((1, tr, 2 * S5_HALF), lambda m, r: (m, r, 0))],
        out_shape=[jax.ShapeDtypeStruct((S5_MBLK, rows, wide), F32),
                   jax.ShapeDtypeStruct((S5_MBLK, rows, 2 * S5_HALF), F32)],
        compiler_params=_cparams(("parallel", "parallel")),
        name="s5_in",
    )(proj3, w_in, d_t)


def _s5_scan_kernel(b_ref, a_ref, o_ref, *, bsz, nk):
    ar = a_ref[0, :, :S5_HALF]
    ai = a_ref[0, :, S5_HALF:]

    def body(k, carry):
        out = []
        for b in range(bsz):
            sr, si = carry[2 * b], carry[2 * b + 1]
            row = b * nk + k
            o_ref[0, pl.ds(row, 1), :] = jnp.concatenate([sr, si], axis=1)
            x = b_ref[0, pl.ds(row, 1), :]
            out.append(ar * sr - ai * si + x[:, :S5_HALF])
            out.append(ar * si + ai * sr + x[:, S5_HALF:])
        return tuple(out)

    zero = jnp.zeros((1, S5_HALF), F32)
    lax.fori_loop(0, nk, body, tuple(zero for _ in range(2 * bsz)))


def _s5_scan(bst, a_t, bsz):
    _, rows, wide = bst.shape
    spec = pl.BlockSpec((1, rows, wide), lambda m: (m, 0, 0))
    return pl.pallas_call(
        functools.partial(_s5_scan_kernel, bsz=bsz, nk=rows // bsz),
        grid=(S5_MBLK,),
        in_specs=[spec, pl.BlockSpec((1, 1, wide), lambda m: (m, 0, 0))],
        out_specs=spec,
        out_shape=jax.ShapeDtypeStruct(bst.shape, F32),
        compiler_params=_cparams(("parallel",)),
        name="s5_scan",
    )(bst, a_t)


def _s5_out_kernel(y_ref, s_ref, w_ref, o_ref):
    tr = y_ref.shape[1]
    y = jax.nn.gelu(y_ref[0] + jnp.dot(s_ref[0].astype(BF16), w_ref[0], preferred_element_type=F32))
    for s in range(S5_CHUNK):
        o_ref[0, pl.ds(s, tr, stride=S5_CHUNK), :] = y[:, s * LANE:(s + 1) * LANE]


def _s5_out(y_intra, s_prev, w_c):
    _, rows, wide = y_intra.shape
    tr = min(rows, 512)
    return pl.pallas_call(
        _s5_out_kernel,
        grid=(S5_MBLK, rows // tr),
        in_specs=[pl.BlockSpec((1, tr, wide), lambda m, r: (m, r, 0)),
                  pl.BlockSpec((1, tr, 2 * S5_HALF), lambda m, r: (m, r, 0)),
                  pl.BlockSpec((1, 2 * S5_HALF, wide), lambda m, r: (m, 0, 0))],
        out_specs=pl.BlockSpec((1, tr * S5_CHUNK, LANE), lambda m, r: (m, r, 0)),
        out_shape=jax.ShapeDtypeStruct((S5_MBLK, rows * S5_CHUNK, LANE), F32),
        compiler_params=_cparams(("parallel", "parallel")),
        name="s5_out",
    )(y_intra, s_prev, w_c)


def _glu_kernel(y_ref, z_ref, w_ref, o_ref):
    nb = y_ref.shape[0]
    y = jnp.concatenate([y_ref[c] for c in range(nb)], axis=1)
    z = jnp.concatenate([z_ref[c] for c in range(nb)], axis=1)
    gate = jax.nn.sigmoid(jnp.dot(y.astype(BF16), w_ref[...], preferred_element_type=F32))
    o = (y * gate * (z * jax.nn.sigmoid(z))).astype(BF16)
    for c in range(nb):
        o_ref[c] = o[:, c * LANE:(c + 1) * LANE]


def _glu(y5, proj3, glu_w):
    nb, rows, _ = y5.shape
    tm = 512
    spec = lambda blk: pl.BlockSpec((nb, tm, LANE), lambda i: (blk, i, 0))
    return pl.pallas_call(
        _glu_kernel,
        grid=(rows // tm,),
        in_specs=[spec(0), spec(CB_SZ // nb), pl.BlockSpec((S5_WIDTH, S5_WIDTH), lambda i: (0, 0))],
        out_specs=spec(0),
        out_shape=jax.ShapeDtypeStruct((nb, rows, LANE), BF16),
        compiler_params=_cparams(("parallel",)),
        name="s5_glu",
    )(y5, proj3, glu_w.astype(BF16))


def _outproj_kernel(a_ref, b_ref, c_ref, w_ref, x_ref, o_ref):
    parts = ([a_ref[h] for h in range(a_ref.shape[0])] + [b_ref[h] for h in range(b_ref.shape[0])]
             + [c_ref[h] for h in range(c_ref.shape[0])])
    mixed = jnp.concatenate(parts, axis=1)
    o_ref[...] = x_ref[...] + jnp.dot(mixed, w_ref[...], preferred_element_type=F32)


def _outproj(m_moba, m_nsa, m_s5, w_out, x2):
    rows, d = x2.shape
    tm = 512
    lspec = lambda n: pl.BlockSpec((n, tm, LANE), lambda i: (0, i, 0))
    return pl.pallas_call(
        _outproj_kernel,
        grid=(rows // tm,),
        in_specs=[lspec(m_moba.shape[0]), lspec(m_nsa.shape[0]), lspec(m_s5.shape[0]),
                  pl.BlockSpec(w_out.shape, lambda i: (0, 0)),
                  pl.BlockSpec((tm, d), lambda i: (i, 0))],
        out_specs=pl.BlockSpec((tm, d), lambda i: (i, 0)),
        out_shape=jax.ShapeDtypeStruct((rows, d), F32),
        compiler_params=_cparams(("parallel",)),
        name="outproj",
    )(m_moba, m_nsa, m_s5, w_out.astype(BF16), x2)


def _w_in_source(blk):
    mw, nw, kvw, ng = MOBA_HEADS * HEAD_DIM, NSA_HEADS * HEAD_DIM, HEAD_DIM, 3 * NSA_HEADS
    o_kv = 4 * mw + nw
    o_ng = o_kv + 6 * kvw
    o_nz = o_ng + ng
    src = jnp.where(blk < CB_NZ, blk * LANE,
          jnp.where(blk < CB_NKS, o_nz + (blk - CB_NZ) * LANE,
          jnp.where(blk < CB_NKC, o_kv + 2 * kvw + (blk - CB_NKS) * LANE,
          jnp.where(blk < CB_NG, o_kv + (blk - CB_NKC) * LANE, o_ng))))
    real = jnp.where(blk < CB_NG, LANE, jnp.where(blk == CB_NG, ng, 0))
    return src, real


def _wperm_kernel(w_ref, o_ref, *, layer):
    _, real = _w_in_source(pl.program_id(0))
    x = w_ref[:, layer, :]
    x = jnp.where(_iota(x.shape, 0) < real, x, 0.0)
    o_ref[...] = x.T.astype(BF16)


def _permute_w_in(w_in_all, layer):
    wt = jnp.transpose(w_in_all, (2, 0, 1))
    _, nl, d = wt.shape
    return pl.pallas_call(
        functools.partial(_wperm_kernel, layer=layer),
        grid=(IN_BLOCKS,),
        in_specs=[pl.BlockSpec((pl.Element(LANE), pl.Element(nl), pl.Element(d)),
                               lambda c: (_w_in_source(c)[0], 0, 0))],
        out_specs=pl.BlockSpec((d, LANE), lambda c: (0, c)),
        out_shape=jax.ShapeDtypeStruct((d, IN_BLOCKS * LANE), BF16),
        compiler_params=_cparams(("parallel",)),
        name="w_in_permute",
    )(wt)


def _layer(x2, bsz, seq, tabs, ctabs, overlap, w_in, norm_w, w_out, moba_q_norm, moba_k_norm, nsa_q_norm,
           nsa_kc_norm, nsa_ks_norm, nsa_kw_norm, nsa_pe_k, nsa_pe_v, nsa_cmp_k_w1, nsa_cmp_k_w2,
           nsa_cmp_v_w1, nsa_cmp_v_w2, s5_a_re, s5_a_im, s5_b_re, s5_b_im, s5_c_re, s5_c_im, s5_d,
           s5_log_dt, s5_glu_w):
    proj3 = _inproj(x2, norm_w, w_in)
    mk_n, mv_b, nk, kmean = _kprep(proj3, seq, moba_k_norm, nsa_ks_norm, nsa_kw_norm, tabs)
    m_moba = _moba(proj3, bsz, seq, moba_q_norm, tabs, kmean.transpose(1, 0, 2), mk_n, mv_b)
    k_cmp, v_cmp = _compress(proj3, bsz, seq, nsa_pe_k, nsa_pe_v, nsa_cmp_k_w1, nsa_cmp_k_w2,
                             nsa_cmp_v_w1, nsa_cmp_v_w2, nsa_kc_norm, ctabs)
    m_nsa = _nsa(proj3, bsz, seq, nsa_q_norm, tabs, k_cmp, v_cmp, overlap, nk)
    w_s5_in, w_s5_out, a_t = _s5_weights(s5_a_re, s5_a_im, s5_b_re, s5_b_im, s5_c_re, s5_c_im, s5_log_dt)
    d_t = jnp.tile(s5_d.astype(F32).reshape(S5_MBLK, 1, LANE), (1, 1, S5_CHUNK))
    y_intra, bst = _s5_in(proj3, w_s5_in, d_t)
    y5 = _s5_out(y_intra, _s5_scan(bst, a_t, bsz), w_s5_out)
    m_s5 = _glu(y5, proj3, s5_glu_w)
    return _outproj(m_moba, m_nsa, m_s5, w_out, x2)


def kernel(x, norm_w, w_in, w_out, moba_q_norm, moba_k_norm, nsa_q_norm, nsa_kc_norm, nsa_ks_norm, nsa_kw_norm, nsa_pe_k, nsa_pe_v, nsa_cmp_k_w1, nsa_cmp_k_w2, nsa_cmp_v_w1, nsa_cmp_v_w2, s5_a_re, s5_a_im, s5_b_re, s5_b_im, s5_c_re, s5_c_im, s5_d, s5_log_dt, s5_glu_w):
    bsz, seq, d = x.shape
    tabs = _rope_tables(jnp.arange(seq, dtype=F32))
    ncmp = seq // NSA_CMP_STRIDE
    ctabs = _rope_tables(jnp.arange(ncmp, dtype=F32) * NSA_CMP_STRIDE + (NSA_CMP_LEN - 1))
    nsel = seq // NSA_SEL_BLOCK
    ci = np.arange(ncmp)[:, None] * NSA_CMP_STRIDE
    sj = np.arange(nsel)[None, :] * NSA_SEL_BLOCK
    overlap = np.zeros((ncmp, LANE), np.float32)
    overlap[:, :nsel] = (ci < sj + NSA_SEL_BLOCK) & (ci + NSA_CMP_LEN > sj)
    overlap = jnp.asarray(overlap)
    params = (norm_w, w_out, moba_q_norm, moba_k_norm, nsa_q_norm, nsa_kc_norm, nsa_ks_norm,
              nsa_kw_norm, nsa_pe_k, nsa_pe_v, nsa_cmp_k_w1, nsa_cmp_k_w2, nsa_cmp_v_w1, nsa_cmp_v_w2,
              s5_a_re, s5_a_im, s5_b_re, s5_b_im, s5_c_re, s5_c_im, s5_d, s5_log_dt, s5_glu_w)
    x2 = x.reshape(bsz * seq, d)
    for layer in range(norm_w.shape[0]):
        x2 = _layer(x2, bsz, seq, tabs, ctabs, overlap, _permute_w_in(w_in, layer), *[p[layer] for p in params])
    return x2.reshape(bsz, seq, d)
```

```python
import functools
import math

import numpy as np
import jax
import jax.numpy as jnp
from jax import lax
from jax.experimental import pallas as pl
from jax.experimental.pallas import tpu as pltpu

F32 = jnp.float32
BF16 = jnp.bfloat16
HIGHEST = lax.Precision.HIGHEST

LANE = 128
SUBLANE = 8
HEAD_DIM = 128
ROPE_DIM = HEAD_DIM // 4
ROPE_HALF = ROPE_DIM // 2
ROPE_THETA = 500000.0
EPS = 1e-6
SCALE = HEAD_DIM ** -0.5
NEG = -1e30

MOBA_HEADS = 4
MOBA_BLOCK = 256
MOBA_TOPK = 3

NSA_HEADS = 4
NSA_CMP_LEN = 32
NSA_CMP_STRIDE = 16
NSA_SEL_BLOCK = 64
NSA_SEL_TOPN = 16
NSA_WINDOW = 512
NSA_TQ = 256
NSA_SEL_PER_TILE = NSA_TQ // NSA_SEL_BLOCK
assert NSA_WINDOW == 2 * NSA_TQ

S5_WIDTH = 1024
S5_GROUP = 16
S5_GROUPS = S5_WIDTH // S5_GROUP
S5_STATE = 64
S5_CHUNK = 16
S5_MBLK = S5_WIDTH // LANE
S5_GPB = LANE // S5_GROUP
S5_HALF = S5_GPB * S5_STATE

CB_MQ, CB_MK, CB_MV, CB_MZ, CB_NQ, CB_NZ, CB_SU, CB_SZ = 0, 4, 8, 12, 16, 20, 24, 32
CB_NKS, CB_NVS, CB_NKW, CB_NVW, CB_NKC, CB_NVC, CB_NG = 40, 41, 42, 43, 44, 45, 46
IN_BLOCKS = 48

VMEM_LIMIT = 56 * 1024 * 1024


def _cparams(sem):
    return pltpu.CompilerParams(dimension_semantics=sem, vmem_limit_bytes=VMEM_LIMIT)


def _iota(shape, dim):
    return lax.broadcasted_iota(jnp.int32, shape, dim)


def _head_norm(x, w):
    return x * lax.rsqrt(jnp.mean(x * x, axis=-1, keepdims=True) + EPS) * w


def _head_norm_mxu(x, w):
    sq = x * x
    hi = sq.astype(BF16)
    lo = (sq - hi.astype(F32)).astype(BF16)
    avg = jnp.full((LANE, LANE), 1.0 / HEAD_DIM, BF16)
    ms = jnp.dot(hi, avg, preferred_element_type=F32) + jnp.dot(lo, avg, preferred_element_type=F32)
    return x * lax.rsqrt(ms + EPS) * w


def _rope(x, c, s1, s2):
    return x * c + pltpu.roll(x, LANE - ROPE_HALF, 1) * s1 + pltpu.roll(x, ROPE_HALF, 1) * s2


def _dot_nt(a, b):
    return lax.dot_general(a, b, (((1,), (1,)), ((), ())), preferred_element_type=F32)


def _rope_tables(pos):
    inv = ROPE_THETA ** (-jnp.arange(0, ROPE_DIM, 2, dtype=F32) / ROPE_DIM)
    ang = pos.astype(F32)[:, None] * inv[None, :]
    cos, sin = jnp.cos(ang), jnp.sin(ang)
    n = pos.shape[0]
    c = jnp.concatenate([cos, cos, jnp.ones((n, LANE - ROPE_DIM), F32)], axis=1)
    s1 = jnp.concatenate([-sin, jnp.zeros((n, LANE - ROPE_HALF), F32)], axis=1)
    s2 = jnp.concatenate([jnp.zeros((n, ROPE_HALF), F32), sin, jnp.zeros((n, LANE - ROPE_DIM), F32)], axis=1)
    return c, s1, s2


def _inproj_kernel(x_ref, nw_ref, w_ref, o_ref, h_ref):
    @pl.when(pl.program_id(1) == 0)
    def _():
        x = x_ref[...]
        ms = jnp.mean(x * x, axis=-1, keepdims=True)
        h_ref[...] = (x * lax.rsqrt(ms + EPS) * nw_ref[...]).astype(BF16)

    res = jnp.dot(h_ref[...], w_ref[...], preferred_element_type=F32)
    for c in range(o_ref.shape[0]):
        o_ref[c] = res[:, c * LANE:(c + 1) * LANE]


def _inproj(x2, norm_w, w_perm, layer):
    rows, d = x2.shape
    tm, tn = 1024, 1024
    nb = tn // LANE
    return pl.pallas_call(
        _inproj_kernel,
        grid=(rows // tm, IN_BLOCKS // nb),
        in_specs=[pl.BlockSpec((tm, d), lambda i, j: (i, 0)),
                  pl.BlockSpec((1, d), lambda i, j: (0, 0)),
                  pl.BlockSpec((None, d, tn), lambda i, j: (layer, 0, j))],
        out_specs=pl.BlockSpec((nb, tm, LANE), lambda i, j: (j, i, 0)),
        out_shape=jax.ShapeDtypeStruct((IN_BLOCKS, rows, LANE), F32),
        scratch_shapes=[pltpu.VMEM((tm, d), BF16)],
        compiler_params=_cparams(("parallel", "arbitrary")),
        name="inproj",
    )(x2, norm_w.reshape(1, d), w_perm)


def _kprep_kernel(mk_ref, mv_ref, nk_ref, mkw_ref, ksw_ref, kww_ref, c_ref, s1_ref, s2_ref,
                  mk_o, mv_o, nk_o, km_o):
    c, s1, s2 = c_ref[...], s1_ref[...], s2_ref[...]
    means = []
    for h in range(MOBA_HEADS):
        k = _rope(_head_norm_mxu(mk_ref[h], mkw_ref[...]), c, s1, s2)
        mk_o[h] = k.astype(BF16)
        mv_o[h] = mv_ref[h].T.astype(BF16)
        means.append(jnp.mean(k, axis=0, keepdims=True))
    km_o[0] = jnp.concatenate(means, axis=0)
    nk_o[0] = _rope(_head_norm_mxu(nk_ref[0], ksw_ref[...]), c, s1, s2).astype(BF16)
    nk_o[1] = nk_ref[1].astype(BF16)
    nk_o[2] = _rope(_head_norm_mxu(nk_ref[2], kww_ref[...]), c, s1, s2).astype(BF16)
    nk_o[3] = nk_ref[3].astype(BF16)


def _kprep(proj3, seq, mk_w, ks_w, kw_w, tabs):
    rows = proj3.shape[1]
    t = MOBA_BLOCK
    nt = seq // t
    blk4 = lambda cb: pl.BlockSpec((4, t, LANE), lambda i: (cb // 4, i, 0))
    wspec = pl.BlockSpec((1, LANE), lambda i: (0, 0))
    tspec = pl.BlockSpec((t, LANE), lambda i: (i % nt, 0))
    out4 = pl.BlockSpec((4, t, LANE), lambda i: (0, i, 0))
    return pl.pallas_call(
        _kprep_kernel,
        grid=(rows // t,),
        in_specs=[blk4(CB_MK), blk4(CB_MV), blk4(CB_NKS), wspec, wspec, wspec, tspec, tspec, tspec],
        out_specs=[out4, pl.BlockSpec((4, None, LANE, t), lambda i: (0, i // nt, 0, i % nt)), out4,
                   pl.BlockSpec((1, MOBA_HEADS, LANE), lambda i: (i, 0, 0))],
        out_shape=[jax.ShapeDtypeStruct((4, rows, LANE), BF16),
                   jax.ShapeDtypeStruct((4, rows // seq, LANE, seq), BF16),
                   jax.ShapeDtypeStruct((4, rows, LANE), BF16),
                   jax.ShapeDtypeStruct((rows // t, MOBA_HEADS, LANE), F32)],
        compiler_params=_cparams(("parallel",)),
        name="kprep",
    )(proj3, proj3, proj3, mk_w.reshape(1, LANE), ks_w.reshape(1, LANE), kw_w.reshape(1, LANE), *tabs)


def _lanes(x, n):
    return x if n == LANE else jnp.concatenate([x] * (n // LANE), axis=1)


def _softmax_part(s, m):
    m_new = jnp.maximum(m, jnp.max(s, axis=1, keepdims=True))
    return m_new, jnp.exp(s - _lanes(m_new, s.shape[1])), jnp.exp(m - m_new)


def _flash_step(s, v, m, l, acc):
    m_new, p, alpha = _softmax_part(s, m)
    l_new = alpha * l + jnp.sum(p, axis=1, keepdims=True)
    acc_new = alpha * acc + jnp.dot(p.astype(BF16), v, preferred_element_type=F32)
    return m_new, l_new, acc_new


def _pad_rows(x, rows):
    return x if x.shape[0] == rows else jnp.concatenate(
        [x, jnp.zeros((rows - x.shape[0], x.shape[1]), x.dtype)], axis=0)


def _topk_mask_t(score_t, k):
    n, cols = score_t.shape
    assert n % SUBLANE == 0
    row_in = _iota((SUBLANE, cols), 0)
    groups = [score_t[g:g + SUBLANE] for g in range(0, n, SUBLANE)]
    ranks = [jnp.zeros((SUBLANE, cols), F32) for _ in groups]
    for mm in range(n):
        cm = score_t[mm:mm + 1, :]
        for gi, sc in enumerate(groups):
            lo = gi * SUBLANE
            if lo > mm:
                inc = jnp.where(cm >= sc, 1.0, 0.0)
            elif lo + SUBLANE - 1 <= mm:
                inc = jnp.where(cm > sc, 1.0, 0.0)
            else:
                inc = jnp.where(row_in > mm - lo, jnp.where(cm >= sc, 1.0, 0.0), jnp.where(cm > sc, 1.0, 0.0))
            ranks[gi] = ranks[gi] + inc
    return jnp.concatenate(ranks, axis=0) < k


def _moba_kernel(q_ref, z_ref, qw_ref, c_ref, s1_ref, s2_ref, km_ref, k_ref, vt_ref, o_ref,
                 qt_s, sel_s, s_s, p_s, acc_s):
    i = pl.program_id(1)
    t = MOBA_BLOCK
    nb = km_ref.shape[1]
    nh = q_ref.shape[0]
    npairs = (i + 1) // 2
    blk_t = _iota((nb, t), 0)
    past = blk_t < i
    causal = _iota((t, t), 0) <= _iota((t, t), 1)
    start = pl.multiple_of(i * t, t)

    def scores(h, pair):
        off = pl.multiple_of(pair * (2 * t), 2 * t)
        return jnp.dot(k_ref[h, pl.ds(off, 2 * t), :], qt_s[h], preferred_element_type=F32)

    def weighted_values(h, pair):
        off = pl.multiple_of(pair * (2 * t), 2 * t)
        off_b = pl.multiple_of(off + t, t)
        ea, eb = sel_s[h, pl.ds(2 * pair, 1), :], sel_s[h, pl.ds(2 * pair + 1, 1), :]
        return (ea * jnp.dot(vt_ref[h, :, pl.ds(off, t)], p_s[h, :t, :], preferred_element_type=F32)
                + eb * jnp.dot(vt_ref[h, :, pl.ds(off_b, t)], p_s[h, t:, :], preferred_element_type=F32))

    state = []
    for h in range(nh):
        qf = _rope(_head_norm(q_ref[h], qw_ref[...]), c_ref[...], s1_ref[...], s2_ref[...])
        qt = (qf * SCALE).T.astype(BF16)
        qt_s[h] = qt
        gate_t = lax.dot_general(_pad_rows(km_ref[h], LANE), qf, (((1,), (1,)), ((), ())), precision=HIGHEST,
                                 preferred_element_type=F32)[:nb]
        top = _topk_mask_t(jnp.where(past, gate_t, -jnp.inf), MOBA_TOPK)
        sel_s[h] = jnp.where(top & past, 1.0, 0.0)
        s = jnp.where(causal, jnp.dot(k_ref[h, pl.ds(start, t), :], qt, preferred_element_type=F32), NEG)
        m = jnp.max(s, axis=0, keepdims=True)
        p = jnp.exp(s - m)
        acc_s[h] = jnp.dot(vt_ref[h, :, pl.ds(start, t)], p.astype(BF16), preferred_element_type=F32)
        state += [m, jnp.sum(p, axis=0, keepdims=True)]
        s_s[0, h] = scores(h, 0)
        p_s[h] = jnp.zeros((2 * t, t), BF16)

    def body(pair, carry):
        slot = pair % 2
        nxt = jnp.minimum(pair + 1, npairs - 1)
        prev = jnp.maximum(pair - 1, 0)
        out = []
        for h in range(nh):
            m, l = carry[2 * h:2 * h + 2]
            s_s[1 - slot, h] = scores(h, nxt)
            acc = acc_s[h] + weighted_values(h, prev)
            s = s_s[slot, h]
            m_new = jnp.maximum(m, jnp.max(s, axis=0, keepdims=True))
            p = jnp.exp(s - m_new)
            alpha = jnp.exp(m - m_new)
            ea, eb = sel_s[h, pl.ds(2 * pair, 1), :], sel_s[h, pl.ds(2 * pair + 1, 1), :]
            l_new = (alpha * l + ea * jnp.sum(p[:t], axis=0, keepdims=True)
                     + eb * jnp.sum(p[t:], axis=0, keepdims=True))
            acc_s[h] = alpha * acc
            p_s[h] = p.astype(BF16)
            out += [m_new, l_new]
        return tuple(out)

    state = lax.fori_loop(0, npairs, body, tuple(state))
    last = jnp.maximum(npairs - 1, 0)
    for h in range(nh):
        l = state[2 * h + 1]
        acc = acc_s[h] + weighted_values(h, last)
        z = z_ref[h]
        o_ref[h] = ((acc * (1.0 / l)).T * (z * jax.nn.sigmoid(z))).astype(BF16)


def _moba(proj3, bsz, seq, q_w, tabs, km_t, mk_n, mv_t):
    rows = proj3.shape[1]
    t = MOBA_BLOCK
    nb = seq // t
    nh = MOBA_HEADS
    qspec = lambda cb: pl.BlockSpec((nh, t, LANE), lambda b, i: (cb // nh, b * nb + i, 0))
    tspec = pl.BlockSpec((t, LANE), lambda b, i: (i, 0))
    return pl.pallas_call(
        _moba_kernel,
        grid=(bsz, nb),
        in_specs=[qspec(CB_MQ), qspec(CB_MZ), pl.BlockSpec((1, LANE), lambda b, i: (0, 0)),
                  tspec, tspec, tspec,
                  pl.BlockSpec((nh, nb, LANE), lambda b, i: (0, b, 0)),
                  pl.BlockSpec((nh, seq, LANE), lambda b, i: (0, b, 0)),
                  pl.BlockSpec((nh, None, LANE, seq), lambda b, i: (0, b, 0, 0))],
        out_specs=pl.BlockSpec((nh, t, LANE), lambda b, i: (0, b * nb + i, 0)),
        out_shape=jax.ShapeDtypeStruct((nh, rows, LANE), BF16),
        scratch_shapes=[pltpu.VMEM((nh, LANE, t), BF16),
                        pltpu.VMEM((nh, nb, t), F32),
                        pltpu.VMEM((2, nh, 2 * t, t), F32),
                        pltpu.VMEM((nh, 2 * t, t), BF16),
                        pltpu.VMEM((nh, LANE, t), F32)],
        compiler_params=_cparams(("parallel", "parallel")),
        name="moba",
    )(proj3, proj3, q_w.reshape(1, LANE), *tabs, km_t, mk_n, mv_t)


def _cmp_kernel(hk_ref, hv_ref, pek_ref, pev_ref, w1k_ref, w2k_ref, w1v_ref, w2v_ref, nw_ref,
                c_ref, s1_ref, s2_ref, ko_ref, vo_ref):
    st = NSA_CMP_STRIDE
    nh = ko_ref.shape[1]

    def compress(x_ref, pe_ref, w1_ref, w2_ref):
        a = jnp.zeros((nh, LANE), F32)
        b = jnp.zeros((nh, LANE), F32)
        for l in range(st):
            x = x_ref[0, pl.ds(l, nh, stride=st), :]
            a = a + jnp.dot((x + pe_ref[l:l + 1, :]).astype(BF16), w1_ref[l * LANE:(l + 1) * LANE, :],
                            preferred_element_type=F32)
            b = b + jnp.dot((x + pe_ref[st + l:st + l + 1, :]).astype(BF16),
                            w1_ref[(st + l) * LANE:(st + l + 1) * LANE, :], preferred_element_type=F32)
        pre = a + pltpu.roll(b, nh - 1, 0)
        return jnp.dot(jax.nn.gelu(pre).astype(BF16), w2_ref[...], preferred_element_type=F32)

    kc = compress(hk_ref, pek_ref, w1k_ref, w2k_ref)
    vc = compress(hv_ref, pev_ref, w1v_ref, w2v_ref)
    ko_ref[0] = _rope(_head_norm(kc, nw_ref[...]), c_ref[...], s1_ref[...], s2_ref[...]).astype(BF16)
    vo_ref[0] = vc.astype(BF16)


def _compress(proj3, bsz, seq, pe_k, pe_v, w1k, w2k, w1v, w2v, kc_w, ctabs):
    nh = seq // NSA_CMP_STRIDE
    wide = NSA_CMP_LEN * HEAD_DIM
    hspec = lambda cb: pl.BlockSpec((1, seq, LANE), lambda b: (cb, b, 0))
    full = lambda shape: pl.BlockSpec(shape, lambda b: tuple(0 for _ in shape))
    ospec = pl.BlockSpec((1, nh, LANE), lambda b: (b, 0, 0))
    return pl.pallas_call(
        _cmp_kernel,
        grid=(bsz,),
        in_specs=[hspec(CB_NKC), hspec(CB_NVC), full((NSA_CMP_LEN, LANE)), full((NSA_CMP_LEN, LANE)),
                  full((wide, LANE)), full((LANE, LANE)), full((wide, LANE)), full((LANE, LANE)),
                  full((1, LANE)), full((nh, LANE)), full((nh, LANE)), full((nh, LANE))],
        out_specs=[ospec, ospec],
        out_shape=[jax.ShapeDtypeStruct((bsz, nh, LANE), BF16)] * 2,
        compiler_params=_cparams(("parallel",)),
        name="nsa_compress",
    )(proj3, proj3, pe_k, pe_v, w1k.astype(BF16), w2k.astype(BF16),
      w1v.astype(BF16), w2v.astype(BF16), kc_w.reshape(1, LANE), *ctabs)


def _nsa_kernel(q_ref, z_ref, g_ref, qw_ref, c_ref, s1_ref, s2_ref, kc_ref, vc_ref, ov_ref,
                ks_ref, vs_ref, kw_ref, vw_ref, o_ref, qb_s, m_s, l_s, acc_s, ocmp_s, osel_s):
    i = pl.program_id(1)
    t = NSA_TQ
    ncmp = kc_ref.shape[1]
    nsel = ks_ref.shape[0] // NSA_SEL_BLOCK
    sel_shift = int(math.log2(NSA_SEL_BLOCK))
    trow = i * t + _iota((t, 1), 0)
    row = _iota((t, t), 0)
    col = _iota((t, t), 1)
    causal = col <= row

    c, s1, s2 = c_ref[...], s1_ref[...], s2_ref[...]
    for h in range(NSA_HEADS):
        qb_s[h] = (_rope(_head_norm_mxu(q_ref[h], qw_ref[...]), c, s1, s2) * SCALE).astype(BF16)

    valid = (_iota((t, ncmp), 1) * NSA_CMP_STRIDE + (NSA_CMP_LEN - 1)) <= trow
    kc, vc = kc_ref[0], vc_ref[0]
    psum = jnp.zeros((t, ncmp), F32)
    for h in range(NSA_HEADS):
        s = jnp.where(valid, _dot_nt(qb_s[h], kc), NEG)
        mx = jnp.max(s, axis=1, keepdims=True)
        e = jnp.exp(s - mx)
        p = e * jnp.where(mx > 0.5 * NEG, 1.0 / jnp.sum(e, axis=1, keepdims=True), 0.0)
        ocmp_s[h] = jnp.dot(p.astype(BF16), vc, preferred_element_type=F32)
        psum = psum + p

    imp_t = jnp.dot(psum, ov_ref[...], precision=HIGHEST, preferred_element_type=F32).T[:nsel]
    jj = _iota((nsel, t), 0)
    cur = (i * t + _iota((1, t), 1)) >> sel_shift
    score = jnp.where(jj <= cur, imp_t, -jnp.inf)
    score = jnp.where((jj == 0) | (jj == cur) | (jj == cur - 1), jnp.inf, score)
    top = _topk_mask_t(score, NSA_SEL_TOPN)
    selbias = _pad_rows(jnp.where(top, 0.0, NEG), LANE).T.astype(BF16)

    def block_bias(kt):
        onehot = jnp.where(_iota((LANE, t), 0) == kt * NSA_SEL_PER_TILE + (_iota((LANE, t), 1) >> sel_shift),
                           1.0, 0.0).astype(BF16)
        return jnp.dot(selbias, onehot, preferred_element_type=F32)

    own = pl.multiple_of(i * t, t)
    bias = block_bias(i)
    k, v = ks_ref[pl.ds(own, t), :], vs_ref[pl.ds(own, t), :]
    for h in range(NSA_HEADS):
        s = jnp.where(causal, _dot_nt(qb_s[h], k) + bias, NEG)
        m = jnp.max(s, axis=1, keepdims=True) + jnp.zeros((t, LANE), F32)
        p = jnp.exp(s - _lanes(m, t))
        m_s[h] = m
        l_s[h] = jnp.sum(p, axis=1, keepdims=True) + jnp.zeros((t, LANE), F32)
        acc_s[h] = jnp.dot(p.astype(BF16), v, preferred_element_type=F32)

    def sel_body(kt, carry):
        off = pl.multiple_of(kt * t, t)
        bias = block_bias(kt)
        k, v = ks_ref[pl.ds(off, t), :], vs_ref[pl.ds(off, t), :]
        for h in range(NSA_HEADS):
            m, l, acc = _flash_step(_dot_nt(qb_s[h], k) + bias, v, m_s[h], l_s[h], acc_s[h])
            m_s[h] = m
            l_s[h] = l
            acc_s[h] = acc
        return carry

    lax.fori_loop(0, i, sel_body, 0)
    for h in range(NSA_HEADS):
        osel_s[h] = acc_s[h] * (1.0 / l_s[h])

    far = pl.multiple_of(jnp.maximum(i - 2, 0) * t, t)
    mid = pl.multiple_of(jnp.maximum(i - 1, 0) * t, t)
    k3 = jnp.concatenate([kw_ref[pl.ds(far, t), :], kw_ref[pl.ds(mid, t), :], kw_ref[pl.ds(own, t), :]], axis=0)
    v3 = jnp.concatenate([vw_ref[pl.ds(far, t), :], vw_ref[pl.ds(mid, t), :], vw_ref[pl.ds(own, t), :]], axis=0)
    allowed = jnp.concatenate([(col > row) & (i >= 2), jnp.full((t, t), True) & (i >= 1), causal], axis=1)
    g = jax.nn.sigmoid(g_ref[0])
    for h in range(NSA_HEADS):
        s = jnp.where(allowed, _dot_nt(qb_s[h], k3), NEG)
        p = jnp.exp(s - jnp.max(s, axis=1, keepdims=True))
        o_win = jnp.dot(p.astype(BF16), v3, preferred_element_type=F32) * (1.0 / jnp.sum(p, axis=1, keepdims=True))
        o = (g[:, 3 * h:3 * h + 1] * ocmp_s[h] + g[:, 3 * h + 1:3 * h + 2] * osel_s[h]
             + g[:, 3 * h + 2:3 * h + 3] * o_win)
        z = z_ref[h]
        o_ref[h] = (o * (z * jax.nn.sigmoid(z))).astype(BF16)


def _nsa(proj3, bsz, seq, q_w, tabs, k_cmp, v_cmp, overlap, nk):
    rows = proj3.shape[1]
    t = NSA_TQ
    nt = seq // t
    ncmp = seq // NSA_CMP_STRIDE
    nsel = seq // NSA_SEL_BLOCK
    blk4 = lambda cb: pl.BlockSpec((4, t, LANE), lambda b, i: (cb // 4, b * nt + i, 0))
    tspec = pl.BlockSpec((t, LANE), lambda b, i: (i, 0))
    cspec = pl.BlockSpec((1, ncmp, LANE), lambda b, i: (b, 0, 0))
    kvspec = lambda which: pl.BlockSpec((None, seq, LANE), lambda b, i: (which, b, 0))
    return pl.pallas_call(
        _nsa_kernel,
        grid=(bsz, nt),
        in_specs=[blk4(CB_NQ), blk4(CB_NZ),
                  pl.BlockSpec((1, t, LANE), lambda b, i: (CB_NG, b * nt + i, 0)),
                  pl.BlockSpec((1, LANE), lambda b, i: (0, 0)), tspec, tspec, tspec, cspec, cspec,
                  pl.BlockSpec((ncmp, LANE), lambda b, i: (0, 0)),
                  kvspec(0), kvspec(1), kvspec(2), kvspec(3)],
        out_specs=pl.BlockSpec((4, t, LANE), lambda b, i: (0, b * nt + i, 0)),
        out_shape=jax.ShapeDtypeStruct((NSA_HEADS, rows, LANE), BF16),
        scratch_shapes=[pltpu.VMEM((NSA_HEADS, t, LANE), BF16),
                        pltpu.VMEM((NSA_HEADS, t, LANE), F32),
                        pltpu.VMEM((NSA_HEADS, t, LANE), F32),
                        pltpu.VMEM((NSA_HEADS, t, LANE), F32),
                        pltpu.VMEM((NSA_HEADS, t, LANE), F32),
                        pltpu.VMEM((NSA_HEADS, t, LANE), F32)],
        compiler_params=_cparams(("parallel", "parallel")),
        name="nsa",
    )(proj3, proj3, proj3, q_w.reshape(1, LANE), *tabs, k_cmp, v_cmp, overlap, nk, nk, nk, nk)


def _s5_weights(a_re, a_im, b_re, b_im, c_re, c_im, log_dt):
    t = S5_CHUNK
    dt = jnp.exp(log_dt.astype(F32))[:, None]
    ar, ai = a_re.astype(F32), a_im.astype(F32)
    ang = dt * ai
    mag = jnp.exp(dt * ar)
    abar_r, abar_i = mag * jnp.cos(ang), mag * jnp.sin(ang)
    nr, ni = abar_r - 1.0, abar_i
    den = ar * ar + ai * ai
    fr = (nr * ar + ni * ai) / den
    fi = (ni * ar - nr * ai) / den
    bt_r, bt_i = b_re.astype(F32).transpose(0, 2, 1), b_im.astype(F32).transpose(0, 2, 1)
    bbar_r = fr[:, None, :] * bt_r - fi[:, None, :] * bt_i
    bbar_i = fr[:, None, :] * bt_i + fi[:, None, :] * bt_r

    def powers(tau):
        tau = jnp.asarray(tau, F32)[:, None, None]
        pmag = jnp.exp(tau * (dt * ar)[None])
        return pmag * jnp.cos(tau * ang[None]), pmag * jnp.sin(tau * ang[None])

    pw_r, pw_i = powers(np.arange(t + 1))
    cr, ci = c_re.astype(F32), c_im.astype(F32)
    cp_r = cr[None] * pw_r[:, :, None, :] - ci[None] * pw_i[:, :, None, :]
    cp_i = cr[None] * pw_i[:, :, None, :] + ci[None] * pw_r[:, :, None, :]
    klag = jnp.sum(cp_r[:t, :, None, :, :] * bbar_r[None, :, :, None, :]
                   - cp_i[:t, :, None, :, :] * bbar_i[None, :, :, None, :], axis=-1)
    kc = (klag.reshape(t, S5_MBLK, S5_GPB, S5_GROUP, S5_GROUP).transpose(1, 0, 2, 3, 4)
          .reshape(S5_MBLK, t * LANE, S5_GROUP))
    rev_r, rev_i = powers(t - 1 - np.arange(t))
    bp_r = rev_r[:, :, None, :] * bbar_r[None] - rev_i[:, :, None, :] * bbar_i[None]
    bp_i = rev_r[:, :, None, :] * bbar_i[None] + rev_i[:, :, None, :] * bbar_r[None]
    bpc = (jnp.stack([bp_r, bp_i], axis=3).reshape(t, S5_MBLK, LANE, 2 * S5_STATE).transpose(1, 0, 2, 3)
           .reshape(S5_MBLK, t * LANE, 2 * S5_STATE))
    cpo = jnp.stack([cp_r[1:], -cp_i[1:]]).reshape(2, t, S5_MBLK, S5_GPB, S5_GROUP, S5_STATE)
    cpc = cpo.transpose(2, 0, 3, 5, 1, 4).reshape(S5_MBLK, 2 * S5_HALF, t * S5_GROUP)
    a_t = jnp.concatenate([pw_r[t].reshape(S5_MBLK, 1, S5_HALF), pw_i[t].reshape(S5_MBLK, 1, S5_HALF)], axis=2)
    w_in, w_c = _s5_wgen(kc, bpc, cpc)
    return w_in, w_c, a_t


def _s5_wgen_kernel(kc_ref, bp_ref, cp_ref, win_ref, wc_ref):
    t = S5_CHUNK
    wide = t * LANE
    gs, ps = int(math.log2(S5_GROUP)), int(math.log2(S5_STATE))

    def expand(x, copy_mask, group_mask):
        ex = jnp.where(copy_mask, 1.0, 0.0).astype(BF16)
        return jnp.where(group_mask, jnp.dot(x.astype(BF16), ex, preferred_element_type=F32), 0.0).astype(BF16)

    r, c = _iota((S5_GROUP, LANE), 0), _iota((S5_GROUP, LANE), 1)
    copy_k = r == (c & (S5_GROUP - 1))
    r, c = _iota((LANE, LANE), 0), _iota((LANE, LANE), 1)
    same_k = (r >> gs) == (c >> gs)
    zero = jnp.zeros((LANE, LANE), BF16)
    for lag in range(t):
        bd = expand(kc_ref[0, lag * LANE:(lag + 1) * LANE, :], copy_k, same_k)
        for s in range(t - lag):
            win_ref[0, s * LANE:(s + 1) * LANE, (s + lag) * LANE:(s + lag + 1) * LANE] = bd
    for s in range(1, t):
        for tt in range(s):
            win_ref[0, s * LANE:(s + 1) * LANE, tt * LANE:(tt + 1) * LANE] = zero
    r, c = _iota((LANE, 2 * S5_HALF), 0), _iota((LANE, 2 * S5_HALF), 1)
    copy_b = ((r >> ps) == (c >> (ps + 3))) & ((r & (S5_STATE - 1)) == (c & (S5_STATE - 1)))
    same_b = (r >> gs) == ((c >> ps) & (S5_GPB - 1))
    for s in range(t):
        win_ref[0, s * LANE:(s + 1) * LANE, wide:] = expand(bp_ref[0, s * LANE:(s + 1) * LANE, :],
                                                            copy_b, same_b)
    rows = 2 * S5_STATE * 2
    r, c = _iota((t * S5_GROUP, wide), 0), _iota((t * S5_GROUP, wide), 1)
    copy_c = ((r >> gs) == (c >> (gs + 3))) & ((r & (S5_GROUP - 1)) == (c & (S5_GROUP - 1)))
    for ch in range(2 * S5_HALF // rows):
        r, c = ch * rows + _iota((rows, wide), 0), _iota((rows, wide), 1)
        same_c = ((r >> ps) & (S5_GPB - 1)) == ((c >> gs) & (S5_GPB - 1))
        wc_ref[0, ch * rows:(ch + 1) * rows, :] = expand(cp_ref[0, ch * rows:(ch + 1) * rows, :],
                                                         copy_c, same_c)


def _s5_wgen(kc, bpc, cpc):
    t = S5_CHUNK
    wide = t * LANE
    spec = lambda a: pl.BlockSpec((1,) + a.shape[1:], lambda m: (m, 0, 0))
    return pl.pallas_call(
        _s5_wgen_kernel,
        grid=(S5_MBLK,),
        in_specs=[spec(kc), spec(bpc), spec(cpc)],
        out_specs=[pl.BlockSpec((1, wide, wide + 2 * S5_HALF), lambda m: (m, 0, 0)),
                   pl.BlockSpec((1, 2 * S5_HALF, wide), lambda m: (m, 0, 0))],
        out_shape=[jax.ShapeDtypeStruct((S5_MBLK, wide, wide + 2 * S5_HALF), BF16),
                   jax.ShapeDtypeStruct((S5_MBLK, 2 * S5_HALF, wide), BF16)],
        compiler_params=_cparams(("parallel",)),
        name="s5_wgen",
    )(kc, bpc, cpc)


def _s5_in_kernel(u_ref, w_ref, d_ref, y_ref, b_ref):
    tr = y_ref.shape[1]
    u = jnp.concatenate([u_ref[0, pl.ds(s, tr, stride=S5_CHUNK), :] for s in range(S5_CHUNK)], axis=1)
    res = jnp.dot(u.astype(BF16), w_ref[0], preferred_element_type=F32)
    wide = y_ref.shape[2]
    y_ref[0] = res[:, :wide] + d_ref[0] * u
    b_ref[0] = res[:, wide:]


def _s5_in(proj3, w_in, d_t):
    rows = proj3.shape[1] // S5_CHUNK
    wide = S5_CHUNK * LANE
    tr = min(rows, 512)
    return pl.pallas_call(
        _s5_in_kernel,
        grid=(S5_MBLK, rows // tr),
        in_specs=[pl.BlockSpec((1, tr * S5_CHUNK, LANE), lambda m, r: (CB_SU + m, r, 0)),
                  pl.BlockSpec((1, wide, wide + 2 * S5_HALF), lambda m, r: (m, 0, 0)),
                  pl.BlockSpec((1, 1, wide), lambda m, r: (m, 0, 0))],
        out_specs=[pl.BlockSpec((1, tr, wide), lambda m, r: (m, r, 0)),
                   pl.BlockSpec((1, tr, 2 * S5_HALF), lambda m, r: (m, r, 0))],
        out_shape=[jax.ShapeDtypeStruct((S5_MBLK, rows, wide), F32),
                   jax.ShapeDtypeStruct((S5_MBLK, rows, 2 * S5_HALF), F32)],
        compiler_params=_cparams(("parallel", "parallel")),
        name="s5_in",
    )(proj3, w_in, d_t)


def _s5_scan_kernel(b_ref, a_ref, o_ref, *, bsz, nk):
    ar = a_ref[0, :, :S5_HALF]
    ai = a_ref[0, :, S5_HALF:]

    def body(k, carry):
        out = []
        for b in range(bsz):
            sr, si = carry[2 * b], carry[2 * b + 1]
            row = b * nk + k
            o_ref[0, pl.ds(row, 1), :] = jnp.concatenate([sr, si], axis=1)
            x = b_ref[0, pl.ds(row, 1), :]
            out.append(ar * sr - ai * si + x[:, :S5_HALF])
            out.append(ar * si + ai * sr + x[:, S5_HALF:])
        return tuple(out)

    zero = jnp.zeros((1, S5_HALF), F32)
    lax.fori_loop(0, nk, body, tuple(zero for _ in range(2 * bsz)))


def _s5_scan(bst, a_t, bsz):
    _, rows, wide = bst.shape
    spec = pl.BlockSpec((1, rows, wide), lambda m: (m, 0, 0))
    return pl.pallas_call(
        functools.partial(_s5_scan_kernel, bsz=bsz, nk=rows // bsz),
        grid=(S5_MBLK,),
        in_specs=[spec, pl.BlockSpec((1, 1, wide), lambda m: (m, 0, 0))],
        out_specs=spec,
        out_shape=jax.ShapeDtypeStruct(bst.shape, F32),
        compiler_params=_cparams(("parallel",)),
        name="s5_scan",
    )(bst, a_t)


def _s5_out_kernel(y_ref, s_ref, w_ref, o_ref):
    tr = y_ref.shape[1]
    y = jax.nn.gelu(y_ref[0] + jnp.dot(s_ref[0].astype(BF16), w_ref[0], preferred_element_type=F32))
    for s in range(S5_CHUNK):
        o_ref[0, pl.ds(s, tr, stride=S5_CHUNK), :] = y[:, s * LANE:(s + 1) * LANE]


def _s5_out(y_intra, s_prev, w_c):
    _, rows, wide = y_intra.shape
    tr = min(rows, 512)
    return pl.pallas_call(
        _s5_out_kernel,
        grid=(S5_MBLK, rows // tr),
        in_specs=[pl.BlockSpec((1, tr, wide), lambda m, r: (m, r, 0)),
                  pl.BlockSpec((1, tr, 2 * S5_HALF), lambda m, r: (m, r, 0)),
                  pl.BlockSpec((1, 2 * S5_HALF, wide), lambda m, r: (m, 0, 0))],
        out_specs=pl.BlockSpec((1, tr * S5_CHUNK, LANE), lambda m, r: (m, r, 0)),
        out_shape=jax.ShapeDtypeStruct((S5_MBLK, rows * S5_CHUNK, LANE), F32),
        compiler_params=_cparams(("parallel", "parallel")),
        name="s5_out",
    )(y_intra, s_prev, w_c)


def _glu_kernel(y_ref, z_ref, w_ref, o_ref):
    nb = y_ref.shape[0]
    y = jnp.concatenate([y_ref[c] for c in range(nb)], axis=1)
    z = jnp.concatenate([z_ref[c] for c in range(nb)], axis=1)
    gate = jax.nn.sigmoid(jnp.dot(y.astype(BF16), w_ref[...], preferred_element_type=F32))
    o = (y * gate * (z * jax.nn.sigmoid(z))).astype(BF16)
    for c in range(nb):
        o_ref[c] = o[:, c * LANE:(c + 1) * LANE]


def _glu(y5, proj3, glu_w):
    nb, rows, _ = y5.shape
    tm = 512
    spec = lambda blk: pl.BlockSpec((nb, tm, LANE), lambda i: (blk, i, 0))
    return pl.pallas_call(
        _glu_kernel,
        grid=(rows // tm,),
        in_specs=[spec(0), spec(CB_SZ // nb), pl.BlockSpec((S5_WIDTH, S5_WIDTH), lambda i: (0, 0))],
        out_specs=spec(0),
        out_shape=jax.ShapeDtypeStruct((nb, rows, LANE), BF16),
        compiler_params=_cparams(("parallel",)),
        name="s5_glu",
    )(y5, proj3, glu_w.astype(BF16))


def _outproj_kernel(a_ref, b_ref, c_ref, w_ref, x_ref, o_ref):
    parts = ([a_ref[h] for h in range(a_ref.shape[0])] + [b_ref[h] for h in range(b_ref.shape[0])]
             + [c_ref[h] for h in range(c_ref.shape[0])])
    mixed = jnp.concatenate(parts, axis=1)
    o_ref[...] = x_ref[...] + jnp.dot(mixed, w_ref[...], preferred_element_type=F32)


def _outproj(m_moba, m_nsa, m_s5, w_out, x2):
    rows, d = x2.shape
    tm = 512
    lspec = lambda n: pl.BlockSpec((n, tm, LANE), lambda i: (0, i, 0))
    return pl.pallas_call(
        _outproj_kernel,
        grid=(rows // tm,),
        in_specs=[lspec(m_moba.shape[0]), lspec(m_nsa.shape[0]), lspec(m_s5.shape[0]),
                  pl.BlockSpec(w_out.shape, lambda i: (0, 0)),
                  pl.BlockSpec((tm, d), lambda i: (i, 0))],
        out_specs=pl.BlockSpec((tm, d), lambda i: (i, 0)),
        out_shape=jax.ShapeDtypeStruct((rows, d), F32),
        compiler_params=_cparams(("parallel",)),
        name="outproj",
    )(m_moba, m_nsa, m_s5, w_out.astype(BF16), x2)


def _w_in_source(blk):
    mw, nw, kvw, ng = MOBA_HEADS * HEAD_DIM, NSA_HEADS * HEAD_DIM, HEAD_DIM, 3 * NSA_HEADS
    o_kv = 4 * mw + nw
    o_ng = o_kv + 6 * kvw
    o_nz = o_ng + ng
    src = jnp.where(blk < CB_NZ, blk * LANE,
          jnp.where(blk < CB_NKS, o_nz + (blk - CB_NZ) * LANE,
          jnp.where(blk < CB_NKC, o_kv + 2 * kvw + (blk - CB_NKS) * LANE,
          jnp.where(blk < CB_NG, o_kv + (blk - CB_NKC) * LANE, o_ng))))
    real = jnp.where(blk < CB_NG, LANE, jnp.where(blk == CB_NG, ng, 0))
    return src, real


def _wperm_kernel(w_ref, o_ref):
    _, real = _w_in_source(pl.program_id(0))
    for layer in range(o_ref.shape[0]):
        x = w_ref[:, layer, :]
        x = jnp.where(_iota(x.shape, 0) < real, x, 0.0)
        o_ref[layer] = x.T.astype(BF16)


def _permute_w_in(w_in_all):
    wt = jnp.transpose(w_in_all, (2, 0, 1))
    _, nl, d = wt.shape
    return pl.pallas_call(
        _wperm_kernel,
        grid=(IN_BLOCKS,),
        in_specs=[pl.BlockSpec((pl.Element(LANE), pl.Element(nl), pl.Element(d)),
                               lambda c: (_w_in_source(c)[0], 0, 0))],
        out_specs=pl.BlockSpec((nl, d, LANE), lambda c: (0, 0, c)),
        out_shape=jax.ShapeDtypeStruct((nl, d, IN_BLOCKS * LANE), BF16),
        compiler_params=_cparams(("parallel",)),
        name="w_in_permute",
    )(wt)


def _layer(x2, bsz, seq, tabs, ctabs, overlap, w_in, layer, norm_w, w_out, moba_q_norm, moba_k_norm, nsa_q_norm,
           nsa_kc_norm, nsa_ks_norm, nsa_kw_norm, nsa_pe_k, nsa_pe_v, nsa_cmp_k_w1, nsa_cmp_k_w2,
           nsa_cmp_v_w1, nsa_cmp_v_w2, s5_a_re, s5_a_im, s5_b_re, s5_b_im, s5_c_re, s5_c_im, s5_d,
           s5_log_dt, s5_glu_w):
    proj3 = _inproj(x2, norm_w, w_in, layer)
    mk_n, mv_b, nk, kmean = _kprep(proj3, seq, moba_k_norm, nsa_ks_norm, nsa_kw_norm, tabs)
    m_moba = _moba(proj3, bsz, seq, moba_q_norm, tabs, kmean.transpose(1, 0, 2), mk_n, mv_b)
    k_cmp, v_cmp = _compress(proj3, bsz, seq, nsa_pe_k, nsa_pe_v, nsa_cmp_k_w1, nsa_cmp_k_w2,
                             nsa_cmp_v_w1, nsa_cmp_v_w2, nsa_kc_norm, ctabs)
    m_nsa = _nsa(proj3, bsz, seq, nsa_q_norm, tabs, k_cmp, v_cmp, overlap, nk)
    w_s5_in, w_s5_out, a_t = _s5_weights(s5_a_re, s5_a_im, s5_b_re, s5_b_im, s5_c_re, s5_c_im, s5_log_dt)
    d_t = jnp.tile(s5_d.astype(F32).reshape(S5_MBLK, 1, LANE), (1, 1, S5_CHUNK))
    y_intra, bst = _s5_in(proj3, w_s5_in, d_t)
    y5 = _s5_out(y_intra, _s5_scan(bst, a_t, bsz), w_s5_out)
    m_s5 = _glu(y5, proj3, s5_glu_w)
    return _outproj(m_moba, m_nsa, m_s5, w_out, x2)


def kernel(x, norm_w, w_in, w_out, moba_q_norm, moba_k_norm, nsa_q_norm, nsa_kc_norm, nsa_ks_norm, nsa_kw_norm, nsa_pe_k, nsa_pe_v, nsa_cmp_k_w1, nsa_cmp_k_w2, nsa_cmp_v_w1, nsa_cmp_v_w2, s5_a_re, s5_a_im, s5_b_re, s5_b_im, s5_c_re, s5_c_im, s5_d, s5_log_dt, s5_glu_w):
    bsz, seq, d = x.shape
    tabs = _rope_tables(jnp.arange(seq, dtype=F32))
    ncmp = seq // NSA_CMP_STRIDE
    ctabs = _rope_tables(jnp.arange(ncmp, dtype=F32) * NSA_CMP_STRIDE + (NSA_CMP_LEN - 1))
    nsel = seq // NSA_SEL_BLOCK
    ci = np.arange(ncmp)[:, None] * NSA_CMP_STRIDE
    sj = np.arange(nsel)[None, :] * NSA_SEL_BLOCK
    overlap = np.zeros((ncmp, LANE), np.float32)
    overlap[:, :nsel] = (ci < sj + NSA_SEL_BLOCK) & (ci + NSA_CMP_LEN > sj)
    overlap = jnp.asarray(overlap)
    params = (norm_w, w_out, moba_q_norm, moba_k_norm, nsa_q_norm, nsa_kc_norm, nsa_ks_norm,
              nsa_kw_norm, nsa_pe_k, nsa_pe_v, nsa_cmp_k_w1, nsa_cmp_k_w2, nsa_cmp_v_w1, nsa_cmp_v_w2,
              s5_a_re, s5_a_im, s5_b_re, s5_b_im, s5_c_re, s5_c_im, s5_d, s5_log_dt, s5_glu_w)
    x2 = x.reshape(bsz * seq, d)
    w_perm = _permute_w_in(w_in)
    for layer in range(norm_w.shape[0]):
        x2 = _layer(x2, bsz, seq, tabs, ctabs, overlap, w_perm, layer, *[p[layer] for p in params])
    return x2.reshape(bsz, seq, d)
```

```python
import functools
import math

import numpy as np
import jax
import jax.numpy as jnp
from jax import lax
from jax.experimental import pallas as pl
from jax.experimental.pallas import tpu as pltpu

F32 = jnp.float32
BF16 = jnp.bfloat16
HIGHEST = lax.Precision.HIGHEST

LANE = 128
SUBLANE = 8
HEAD_DIM = 128
ROPE_DIM = HEAD_DIM // 4
ROPE_HALF = ROPE_DIM // 2
ROPE_THETA = 500000.0
EPS = 1e-6
SCALE = HEAD_DIM ** -0.5
NEG = -1e30

MOBA_HEADS = 4
MOBA_BLOCK = 256
MOBA_TOPK = 3

NSA_HEADS = 4
NSA_CMP_LEN = 32
NSA_CMP_STRIDE = 16
NSA_SEL_BLOCK = 64
NSA_SEL_TOPN = 16
NSA_WINDOW = 512
NSA_TQ = 256
NSA_SEL_PER_TILE = NSA_TQ // NSA_SEL_BLOCK
assert NSA_WINDOW == 2 * NSA_TQ

S5_WIDTH = 1024
S5_GROUP = 16
S5_GROUPS = S5_WIDTH // S5_GROUP
S5_STATE = 64
S5_CHUNK = 16
S5_MBLK = S5_WIDTH // LANE
S5_GPB = LANE // S5_GROUP
S5_HALF = S5_GPB * S5_STATE

CB_MQ, CB_MK, CB_MV, CB_MZ, CB_NQ, CB_NZ, CB_SU, CB_SZ = 0, 4, 8, 12, 16, 20, 24, 32
CB_NKS, CB_NVS, CB_NKW, CB_NVW, CB_NKC, CB_NVC, CB_NG = 40, 41, 42, 43, 44, 45, 46
IN_BLOCKS = 48

VMEM_LIMIT = 56 * 1024 * 1024


def _cparams(sem):
    return pltpu.CompilerParams(dimension_semantics=sem, vmem_limit_bytes=VMEM_LIMIT)


def _iota(shape, dim):
    return lax.broadcasted_iota(jnp.int32, shape, dim)


def _head_norm(x, w):
    return x * lax.rsqrt(jnp.mean(x * x, axis=-1, keepdims=True) + EPS) * w


def _head_norm_mxu(x, w):
    sq = x * x
    hi = sq.astype(BF16)
    lo = (sq - hi.astype(F32)).astype(BF16)
    avg = jnp.full((LANE, LANE), 1.0 / HEAD_DIM, BF16)
    ms = jnp.dot(hi, avg, preferred_element_type=F32) + jnp.dot(lo, avg, preferred_element_type=F32)
    return x * lax.rsqrt(ms + EPS) * w


def _rope(x, c, s1, s2):
    return x * c + pltpu.roll(x, LANE - ROPE_HALF, 1) * s1 + pltpu.roll(x, ROPE_HALF, 1) * s2


def _dot_nt(a, b):
    return lax.dot_general(a, b, (((1,), (1,)), ((), ())), preferred_element_type=F32)


def _rope_tables(pos):
    inv = ROPE_THETA ** (-jnp.arange(0, ROPE_DIM, 2, dtype=F32) / ROPE_DIM)
    ang = pos.astype(F32)[:, None] * inv[None, :]
    cos, sin = jnp.cos(ang), jnp.sin(ang)
    n = pos.shape[0]
    c = jnp.concatenate([cos, cos, jnp.ones((n, LANE - ROPE_DIM), F32)], axis=1)
    s1 = jnp.concatenate([-sin, jnp.zeros((n, LANE - ROPE_HALF), F32)], axis=1)
    s2 = jnp.concatenate([jnp.zeros((n, ROPE_HALF), F32), sin, jnp.zeros((n, LANE - ROPE_DIM), F32)], axis=1)
    return c, s1, s2


def _inproj_kernel(x_ref, nw_ref, w_ref, o_ref, h_ref):
    @pl.when(pl.program_id(1) == 0)
    def _():
        x = x_ref[...]
        ms = jnp.mean(x * x, axis=-1, keepdims=True)
        h_ref[...] = (x * lax.rsqrt(ms + EPS) * nw_ref[...]).astype(BF16)

    res = jnp.dot(h_ref[...], w_ref[...], preferred_element_type=F32)
    for c in range(o_ref.shape[0]):
        o_ref[c] = res[:, c * LANE:(c + 1) * LANE]


def _inproj(x2, norm_w, w_perm, layer):
    rows, d = x2.shape
    tm, tn = 1024, 1024
    nb = tn // LANE
    return pl.pallas_call(
        _inproj_kernel,
        grid=(rows // tm, IN_BLOCKS // nb),
        in_specs=[pl.BlockSpec((tm, d), lambda i, j: (i, 0)),
                  pl.BlockSpec((1, d), lambda i, j: (0, 0)),
                  pl.BlockSpec((None, d, tn), lambda i, j: (layer, 0, j))],
        out_specs=pl.BlockSpec((nb, tm, LANE), lambda i, j: (j, i, 0)),
        out_shape=jax.ShapeDtypeStruct((IN_BLOCKS, rows, LANE), F32),
        scratch_shapes=[pltpu.VMEM((tm, d), BF16)],
        compiler_params=_cparams(("parallel", "arbitrary")),
        name="inproj",
    )(x2, norm_w.reshape(1, d), w_perm)


def _kprep_kernel(mk_ref, mv_ref, nk_ref, mkw_ref, ksw_ref, kww_ref, c_ref, s1_ref, s2_ref,
                  mk_o, mv_o, nk_o, km_o):
    c, s1, s2 = c_ref[...], s1_ref[...], s2_ref[...]
    means = []
    for h in range(MOBA_HEADS):
        k = _rope(_head_norm_mxu(mk_ref[h], mkw_ref[...]), c, s1, s2)
        mk_o[h] = k.astype(BF16)
        mv_o[h] = mv_ref[h].T.astype(BF16)
        means.append(jnp.mean(k, axis=0, keepdims=True))
    km_o[0] = jnp.concatenate(means, axis=0)
    nk_o[0] = _rope(_head_norm_mxu(nk_ref[0], ksw_ref[...]), c, s1, s2).astype(BF16)
    nk_o[1] = nk_ref[1].astype(BF16)
    nk_o[2] = _rope(_head_norm_mxu(nk_ref[2], kww_ref[...]), c, s1, s2).astype(BF16)
    nk_o[3] = nk_ref[3].astype(BF16)


def _kprep(proj3, seq, mk_w, ks_w, kw_w, tabs):
    rows = proj3.shape[1]
    t = MOBA_BLOCK
    nt = seq // t
    blk4 = lambda cb: pl.BlockSpec((4, t, LANE), lambda i: (cb // 4, i, 0))
    wspec = pl.BlockSpec((1, LANE), lambda i: (0, 0))
    tspec = pl.BlockSpec((t, LANE), lambda i: (i % nt, 0))
    out4 = pl.BlockSpec((4, t, LANE), lambda i: (0, i, 0))
    return pl.pallas_call(
        _kprep_kernel,
        grid=(rows // t,),
        in_specs=[blk4(CB_MK), blk4(CB_MV), blk4(CB_NKS), wspec, wspec, wspec, tspec, tspec, tspec],
        out_specs=[out4, pl.BlockSpec((4, None, LANE, t), lambda i: (0, i // nt, 0, i % nt)), out4,
                   pl.BlockSpec((1, MOBA_HEADS, LANE), lambda i: (i, 0, 0))],
        out_shape=[jax.ShapeDtypeStruct((4, rows, LANE), BF16),
                   jax.ShapeDtypeStruct((4, rows // seq, LANE, seq), BF16),
                   jax.ShapeDtypeStruct((4, rows, LANE), BF16),
                   jax.ShapeDtypeStruct((rows // t, MOBA_HEADS, LANE), F32)],
        compiler_params=_cparams(("parallel",)),
        name="kprep",
    )(proj3, proj3, proj3, mk_w.reshape(1, LANE), ks_w.reshape(1, LANE), kw_w.reshape(1, LANE), *tabs)


def _lanes(x, n):
    return x if n == LANE else jnp.concatenate([x] * (n // LANE), axis=1)


def _softmax_part(s, m):
    m_new = jnp.maximum(m, jnp.max(s, axis=1, keepdims=True))
    return m_new, jnp.exp(s - _lanes(m_new, s.shape[1])), jnp.exp(m - m_new)


def _flash_step(s, v, m, l, acc):
    m_new, p, alpha = _softmax_part(s, m)
    l_new = alpha * l + jnp.sum(p, axis=1, keepdims=True)
    acc_new = alpha * acc + jnp.dot(p.astype(BF16), v, preferred_element_type=F32)
    return m_new, l_new, acc_new


def _pad_rows(x, rows):
    return x if x.shape[0] == rows else jnp.concatenate(
        [x, jnp.zeros((rows - x.shape[0], x.shape[1]), x.dtype)], axis=0)


def _topk_mask_t(score_t, k):
    n, cols = score_t.shape
    assert n % SUBLANE == 0
    row_in = _iota((SUBLANE, cols), 0)
    groups = [score_t[g:g + SUBLANE] for g in range(0, n, SUBLANE)]
    ranks = [jnp.zeros((SUBLANE, cols), F32) for _ in groups]
    for mm in range(n):
        cm = score_t[mm:mm + 1, :]
        for gi, sc in enumerate(groups):
            lo = gi * SUBLANE
            if lo > mm:
                inc = jnp.where(cm >= sc, 1.0, 0.0)
            elif lo + SUBLANE - 1 <= mm:
                inc = jnp.where(cm > sc, 1.0, 0.0)
            else:
                inc = jnp.where(row_in > mm - lo, jnp.where(cm >= sc, 1.0, 0.0), jnp.where(cm > sc, 1.0, 0.0))
            ranks[gi] = ranks[gi] + inc
    return jnp.concatenate(ranks, axis=0) < k


def _moba_kernel(q_ref, z_ref, qw_ref, c_ref, s1_ref, s2_ref, km_ref, k_ref, vt_ref, o_ref,
                 qt_s, sel_s, s_s, p_s, acc_s):
    i = pl.program_id(1)
    t = MOBA_BLOCK
    nb = km_ref.shape[1]
    nh = q_ref.shape[0]
    npairs = (i + 1) // 2
    blk_t = _iota((nb, t), 0)
    past = blk_t < i
    causal = _iota((t, t), 0) <= _iota((t, t), 1)
    start = pl.multiple_of(i * t, t)

    def put_scores(slot, h, pair):
        src = jnp.minimum(pair, nb // 2 - 1)
        off = pl.multiple_of(src * (2 * t), 2 * t)
        sc = jnp.dot(k_ref[h, pl.ds(off, 2 * t), :], qt_s[h], preferred_element_type=F32)
        row = jnp.minimum(2 * pair, nb - 2)
        s_s[slot, h, :t, :] = sc[:t] + sel_s[h, pl.ds(row, 1), :]
        s_s[slot, h, t:, :] = sc[t:] + sel_s[h, pl.ds(row + 1, 1), :]

    def weighted_values(h, pair):
        off = pl.multiple_of(pair * (2 * t), 2 * t)
        return jnp.dot(vt_ref[h, :, pl.ds(off, 2 * t)], p_s[h], preferred_element_type=F32)

    state = []
    for h in range(nh):
        qf = _rope(_head_norm(q_ref[h], qw_ref[...]), c_ref[...], s1_ref[...], s2_ref[...])
        qt = (qf * SCALE).T.astype(BF16)
        qt_s[h] = qt
        gate_t = lax.dot_general(_pad_rows(km_ref[h], LANE), qf, (((1,), (1,)), ((), ())), precision=HIGHEST,
                                 preferred_element_type=F32)[:nb]
        top = _topk_mask_t(jnp.where(past, gate_t, -jnp.inf), MOBA_TOPK)
        sel_s[h] = jnp.where(top & past, 0.0, NEG)
        s = jnp.where(causal, jnp.dot(k_ref[h, pl.ds(start, t), :], qt, preferred_element_type=F32), NEG)
        m = jnp.max(s, axis=0, keepdims=True)
        p = jnp.exp(s - m)
        acc_s[h] = jnp.dot(vt_ref[h, :, pl.ds(start, t)], p.astype(BF16), preferred_element_type=F32)
        state += [m, jnp.sum(p, axis=0, keepdims=True)]
        put_scores(0, h, 0)
        p_s[h] = jnp.zeros((2 * t, t), BF16)

    def trip(pair, carry, src, dst):
        prev = jnp.maximum(pair - 1, 0)
        out = []
        for h in range(nh):
            m, l = carry[2 * h:2 * h + 2]
            put_scores(dst, h, pair + 1)
            acc = acc_s[h] + weighted_values(h, prev)
            s = s_s[src, h]
            m_new = jnp.maximum(m, jnp.max(s, axis=0, keepdims=True))
            p = jnp.exp(s - m_new)
            alpha = jnp.exp(m - m_new)
            l_new = alpha * l + jnp.sum(p, axis=0, keepdims=True)
            acc_s[h] = alpha * acc
            p_s[h] = p.astype(BF16)
            out += [m_new, l_new]
        return tuple(out)

    def body(q, carry):
        return trip(2 * q + 1, trip(2 * q, carry, 0, 1), 1, 0)

    ntrips = 2 * ((npairs + 1) // 2)
    state = lax.fori_loop(0, ntrips // 2, body, tuple(state))
    last = jnp.maximum(ntrips - 1, 0)
    for h in range(nh):
        l = state[2 * h + 1]
        acc = acc_s[h] + weighted_values(h, last)
        z = z_ref[h]
        o_ref[h] = ((acc * (1.0 / l)).T * (z * jax.nn.sigmoid(z))).astype(BF16)


def _moba(proj3, bsz, seq, q_w, tabs, km_t, mk_n, mv_t):
    rows = proj3.shape[1]
    t = MOBA_BLOCK
    nb = seq // t
    nh = MOBA_HEADS
    assert nb % 4 == 0
    qspec = lambda cb: pl.BlockSpec((nh, t, LANE), lambda b, i: (cb // nh, b * nb + i, 0))
    tspec = pl.BlockSpec((t, LANE), lambda b, i: (i, 0))
    return pl.pallas_call(
        _moba_kernel,
        grid=(bsz, nb),
        in_specs=[qspec(CB_MQ), qspec(CB_MZ), pl.BlockSpec((1, LANE), lambda b, i: (0, 0)),
                  tspec, tspec, tspec,
                  pl.BlockSpec((nh, nb, LANE), lambda b, i: (0, b, 0)),
                  pl.BlockSpec((nh, seq, LANE), lambda b, i: (0, b, 0)),
                  pl.BlockSpec((nh, None, LANE, seq), lambda b, i: (0, b, 0, 0))],
        out_specs=pl.BlockSpec((nh, t, LANE), lambda b, i: (0, b * nb + i, 0)),
        out_shape=jax.ShapeDtypeStruct((nh, rows, LANE), BF16),
        scratch_shapes=[pltpu.VMEM((nh, LANE, t), BF16),
                        pltpu.VMEM((nh, nb, t), F32),
                        pltpu.VMEM((2, nh, 2 * t, t), F32),
                        pltpu.VMEM((nh, 2 * t, t), BF16),
                        pltpu.VMEM((nh, LANE, t), F32)],
        compiler_params=_cparams(("parallel", "parallel")),
        name="moba",
    )(proj3, proj3, q_w.reshape(1, LANE), *tabs, km_t, mk_n, mv_t)


def _cmp_kernel(hk_ref, hv_ref, pek_ref, pev_ref, w1k_ref, w2k_ref, w1v_ref, w2v_ref, nw_ref,
                c_ref, s1_ref, s2_ref, ko_ref, vo_ref):
    st = NSA_CMP_STRIDE
    nh = ko_ref.shape[1]

    def compress(x_ref, pe_ref, w1_ref, w2_ref):
        a = jnp.zeros((nh, LANE), F32)
        b = jnp.zeros((nh, LANE), F32)
        for l in range(st):
            x = x_ref[0, pl.ds(l, nh, stride=st), :]
            a = a + jnp.dot((x + pe_ref[l:l + 1, :]).astype(BF16), w1_ref[l * LANE:(l + 1) * LANE, :],
                            preferred_element_type=F32)
            b = b + jnp.dot((x + pe_ref[st + l:st + l + 1, :]).astype(BF16),
                            w1_ref[(st + l) * LANE:(st + l + 1) * LANE, :], preferred_element_type=F32)
        pre = a + pltpu.roll(b, nh - 1, 0)
        return jnp.dot(jax.nn.gelu(pre).astype(BF16), w2_ref[...], preferred_element_type=F32)

    kc = compress(hk_ref, pek_ref, w1k_ref, w2k_ref)
    vc = compress(hv_ref, pev_ref, w1v_ref, w2v_ref)
    ko_ref[0] = _rope(_head_norm(kc, nw_ref[...]), c_ref[...], s1_ref[...], s2_ref[...]).astype(BF16)
    vo_ref[0] = vc.astype(BF16)


def _compress(proj3, bsz, seq, pe_k, pe_v, w1k, w2k, w1v, w2v, kc_w, ctabs):
    nh = seq // NSA_CMP_STRIDE
    wide = NSA_CMP_LEN * HEAD_DIM
    hspec = lambda cb: pl.BlockSpec((1, seq, LANE), lambda b: (cb, b, 0))
    full = lambda shape: pl.BlockSpec(shape, lambda b: tuple(0 for _ in shape))
    ospec = pl.BlockSpec((1, nh, LANE), lambda b: (b, 0, 0))
    return pl.pallas_call(
        _cmp_kernel,
        grid=(bsz,),
        in_specs=[hspec(CB_NKC), hspec(CB_NVC), full((NSA_CMP_LEN, LANE)), full((NSA_CMP_LEN, LANE)),
                  full((wide, LANE)), full((LANE, LANE)), full((wide, LANE)), full((LANE, LANE)),
                  full((1, LANE)), full((nh, LANE)), full((nh, LANE)), full((nh, LANE))],
        out_specs=[ospec, ospec],
        out_shape=[jax.ShapeDtypeStruct((bsz, nh, LANE), BF16)] * 2,
        compiler_params=_cparams(("parallel",)),
        name="nsa_compress",
    )(proj3, proj3, pe_k, pe_v, w1k.astype(BF16), w2k.astype(BF16),
      w1v.astype(BF16), w2v.astype(BF16), kc_w.reshape(1, LANE), *ctabs)


def _nsa_kernel(q_ref, z_ref, g_ref, qw_ref, c_ref, s1_ref, s2_ref, kc_ref, vc_ref, ov_ref,
                ks_ref, vs_ref, kw_ref, vw_ref, o_ref, qb_s, m_s, l_s, acc_s, ocmp_s, osel_s):
    i = pl.program_id(1)
    t = NSA_TQ
    ncmp = kc_ref.shape[1]
    nsel = ks_ref.shape[0] // NSA_SEL_BLOCK
    sel_shift = int(math.log2(NSA_SEL_BLOCK))
    trow = i * t + _iota((t, 1), 0)
    row = _iota((t, t), 0)
    col = _iota((t, t), 1)
    causal = col <= row

    c, s1, s2 = c_ref[...], s1_ref[...], s2_ref[...]
    for h in range(NSA_HEADS):
        qb_s[h] = (_rope(_head_norm_mxu(q_ref[h], qw_ref[...]), c, s1, s2) * SCALE).astype(BF16)

    valid = (_iota((t, ncmp), 1) * NSA_CMP_STRIDE + (NSA_CMP_LEN - 1)) <= trow
    kc, vc = kc_ref[0], vc_ref[0]
    psum = jnp.zeros((t, ncmp), F32)
    for h in range(NSA_HEADS):
        s = jnp.where(valid, _dot_nt(qb_s[h], kc), NEG)
        mx = jnp.max(s, axis=1, keepdims=True)
        e = jnp.exp(s - mx)
        p = e * jnp.where(mx > 0.5 * NEG, 1.0 / jnp.sum(e, axis=1, keepdims=True), 0.0)
        ocmp_s[h] = jnp.dot(p.astype(BF16), vc, preferred_element_type=F32)
        psum = psum + p

    imp_t = jnp.dot(psum, ov_ref[...], precision=HIGHEST, preferred_element_type=F32).T[:nsel]
    jj = _iota((nsel, t), 0)
    cur = (i * t + _iota((1, t), 1)) >> sel_shift
    score = jnp.where(jj <= cur, imp_t, -jnp.inf)
    score = jnp.where((jj == 0) | (jj == cur) | (jj == cur - 1), jnp.inf, score)
    top = _topk_mask_t(score, NSA_SEL_TOPN)
    selbias = _pad_rows(jnp.where(top, 0.0, NEG), LANE).T.astype(BF16)

    def block_bias(kt):
        onehot = jnp.where(_iota((LANE, t), 0) == kt * NSA_SEL_PER_TILE + (_iota((LANE, t), 1) >> sel_shift),
                           1.0, 0.0).astype(BF16)
        return jnp.dot(selbias, onehot, preferred_element_type=F32)

    own = pl.multiple_of(i * t, t)
    bias = block_bias(i)
    k, v = ks_ref[pl.ds(own, t), :], vs_ref[pl.ds(own, t), :]
    for h in range(NSA_HEADS):
        s = jnp.where(causal, _dot_nt(qb_s[h], k) + bias, NEG)
        m = jnp.max(s, axis=1, keepdims=True) + jnp.zeros((t, LANE), F32)
        p = jnp.exp(s - _lanes(m, t))
        m_s[h] = m
        l_s[h] = jnp.sum(p, axis=1, keepdims=True) + jnp.zeros((t, LANE), F32)
        acc_s[h] = jnp.dot(p.astype(BF16), v, preferred_element_type=F32)

    def sel_body(kt, carry):
        off = pl.multiple_of(kt * t, t)
        bias = block_bias(kt)
        k, v = ks_ref[pl.ds(off, t), :], vs_ref[pl.ds(off, t), :]
        for h in range(NSA_HEADS):
            m, l, acc = _flash_step(_dot_nt(qb_s[h], k) + bias, v, m_s[h], l_s[h], acc_s[h])
            m_s[h] = m
            l_s[h] = l
            acc_s[h] = acc
        return carry

    lax.fori_loop(0, i, sel_body, 0)
    for h in range(NSA_HEADS):
        osel_s[h] = acc_s[h] * (1.0 / l_s[h])

    far = pl.multiple_of(jnp.maximum(i - 2, 0) * t, t)
    mid = pl.multiple_of(jnp.maximum(i - 1, 0) * t, t)
    k3 = jnp.concatenate([kw_ref[pl.ds(far, t), :], kw_ref[pl.ds(mid, t), :], kw_ref[pl.ds(own, t), :]], axis=0)
    v3 = jnp.concatenate([vw_ref[pl.ds(far, t), :], vw_ref[pl.ds(mid, t), :], vw_ref[pl.ds(own, t), :]], axis=0)
    allowed = jnp.concatenate([(col > row) & (i >= 2), jnp.full((t, t), True) & (i >= 1), causal], axis=1)
    g = jax.nn.sigmoid(g_ref[0])
    for h in range(NSA_HEADS):
        s = jnp.where(allowed, _dot_nt(qb_s[h], k3), NEG)
        p = jnp.exp(s - jnp.max(s, axis=1, keepdims=True))
        o_win = jnp.dot(p.astype(BF16), v3, preferred_element_type=F32) * (1.0 / jnp.sum(p, axis=1, keepdims=True))
        o = (g[:, 3 * h:3 * h + 1] * ocmp_s[h] + g[:, 3 * h + 1:3 * h + 2] * osel_s[h]
             + g[:, 3 * h + 2:3 * h + 3] * o_win)
        z = z_ref[h]
        o_ref[h] = (o * (z * jax.nn.sigmoid(z))).astype(BF16)


def _nsa(proj3, bsz, seq, q_w, tabs, k_cmp, v_cmp, overlap, nk):
    rows = proj3.shape[1]
    t = NSA_TQ
    nt = seq // t
    ncmp = seq // NSA_CMP_STRIDE
    nsel = seq // NSA_SEL_BLOCK
    blk4 = lambda cb: pl.BlockSpec((4, t, LANE), lambda b, i: (cb // 4, b * nt + i, 0))
    tspec = pl.BlockSpec((t, LANE), lambda b, i: (i, 0))
    cspec = pl.BlockSpec((1, ncmp, LANE), lambda b, i: (b, 0, 0))
    kvspec = lambda which: pl.BlockSpec((None, seq, LANE), lambda b, i: (which, b, 0))
    return pl.pallas_call(
        _nsa_kernel,
        grid=(bsz, nt),
        in_specs=[blk4(CB_NQ), blk4(CB_NZ),
                  pl.BlockSpec((1, t, LANE), lambda b, i: (CB_NG, b * nt + i, 0)),
                  pl.BlockSpec((1, LANE), lambda b, i: (0, 0)), tspec, tspec, tspec, cspec, cspec,
                  pl.BlockSpec((ncmp, LANE), lambda b, i: (0, 0)),
                  kvspec(0), kvspec(1), kvspec(2), kvspec(3)],
        out_specs=pl.BlockSpec((4, t, LANE), lambda b, i: (0, b * nt + i, 0)),
        out_shape=jax.ShapeDtypeStruct((NSA_HEADS, rows, LANE), BF16),
        scratch_shapes=[pltpu.VMEM((NSA_HEADS, t, LANE), BF16),
                        pltpu.VMEM((NSA_HEADS, t, LANE), F32),
                        pltpu.VMEM((NSA_HEADS, t, LANE), F32),
                        pltpu.VMEM((NSA_HEADS, t, LANE), F32),
                        pltpu.VMEM((NSA_HEADS, t, LANE), F32),
                        pltpu.VMEM((NSA_HEADS, t, LANE), F32)],
        compiler_params=_cparams(("parallel", "parallel")),
        name="nsa",
    )(proj3, proj3, proj3, q_w.reshape(1, LANE), *tabs, k_cmp, v_cmp, overlap, nk, nk, nk, nk)


def _s5_weights(a_re, a_im, b_re, b_im, c_re, c_im, log_dt):
    t = S5_CHUNK
    dt = jnp.exp(log_dt.astype(F32))[:, None]
    ar, ai = a_re.astype(F32), a_im.astype(F32)
    ang = dt * ai
    mag = jnp.exp(dt * ar)
    abar_r, abar_i = mag * jnp.cos(ang), mag * jnp.sin(ang)
    nr, ni = abar_r - 1.0, abar_i
    den = ar * ar + ai * ai
    fr = (nr * ar + ni * ai) / den
    fi = (ni * ar - nr * ai) / den
    bt_r, bt_i = b_re.astype(F32).transpose(0, 2, 1), b_im.astype(F32).transpose(0, 2, 1)
    bbar_r = fr[:, None, :] * bt_r - fi[:, None, :] * bt_i
    bbar_i = fr[:, None, :] * bt_i + fi[:, None, :] * bt_r

    def powers(tau):
        tau = jnp.asarray(tau, F32)[:, None, None]
        pmag = jnp.exp(tau * (dt * ar)[None])
        return pmag * jnp.cos(tau * ang[None]), pmag * jnp.sin(tau * ang[None])

    pw_r, pw_i = powers(np.arange(t + 1))
    cr, ci = c_re.astype(F32), c_im.astype(F32)
    cp_r = cr[None] * pw_r[:, :, None, :] - ci[None] * pw_i[:, :, None, :]
    cp_i = cr[None] * pw_i[:, :, None, :] + ci[None] * pw_r[:, :, None, :]
    klag = jnp.sum(cp_r[:t, :, None, :, :] * bbar_r[None, :, :, None, :]
                   - cp_i[:t, :, None, :, :] * bbar_i[None, :, :, None, :], axis=-1)
    kc = (klag.reshape(t, S5_MBLK, S5_GPB, S5_GROUP, S5_GROUP).transpose(1, 0, 2, 3, 4)
          .reshape(S5_MBLK, t * LANE, S5_GROUP))
    rev_r, rev_i = powers(t - 1 - np.arange(t))
    bp_r = rev_r[:, :, None, :] * bbar_r[None] - rev_i[:, :, None, :] * bbar_i[None]
    bp_i = rev_r[:, :, None, :] * bbar_i[None] + rev_i[:, :, None, :] * bbar_r[None]
    bpc = (jnp.stack([bp_r, bp_i], axis=3).reshape(t, S5_MBLK, LANE, 2 * S5_STATE).transpose(1, 0, 2, 3)
           .reshape(S5_MBLK, t * LANE, 2 * S5_STATE))
    cpo = jnp.stack([cp_r[1:], -cp_i[1:]]).reshape(2, t, S5_MBLK, S5_GPB, S5_GROUP, S5_STATE)
    cpc = cpo.transpose(2, 0, 3, 5, 1, 4).reshape(S5_MBLK, 2 * S5_HALF, t * S5_GROUP)
    a_t = jnp.concatenate([pw_r[t].reshape(S5_MBLK, 1, S5_HALF), pw_i[t].reshape(S5_MBLK, 1, S5_HALF)], axis=2)
    w_in, w_c = _s5_wgen(kc, bpc, cpc)
    return w_in, w_c, a_t


def _s5_wgen_kernel(kc_ref, bp_ref, cp_ref, win_ref, wc_ref):
    t = S5_CHUNK
    wide = t * LANE
    gs, ps = int(math.log2(S5_GROUP)), int(math.log2(S5_STATE))

    def expand(x, copy_mask, group_mask):
        ex = jnp.where(copy_mask, 1.0, 0.0).astype(BF16)
        return jnp.where(group_mask, jnp.dot(x.astype(BF16), ex, preferred_element_type=F32), 0.0).astype(BF16)

    r, c = _iota((S5_GROUP, LANE), 0), _iota((S5_GROUP, LANE), 1)
    copy_k = r == (c & (S5_GROUP - 1))
    r, c = _iota((LANE, LANE), 0), _iota((LANE, LANE), 1)
    same_k = (r >> gs) == (c >> gs)
    zero = jnp.zeros((LANE, LANE), BF16)
    for lag in range(t):
        bd = expand(kc_ref[0, lag * LANE:(lag + 1) * LANE, :], copy_k, same_k)
        for s in range(t - lag):
            win_ref[0, s * LANE:(s + 1) * LANE, (s + lag) * LANE:(s + lag + 1) * LANE] = bd
    for s in range(1, t):
        for tt in range(s):
            win_ref[0, s * LANE:(s + 1) * LANE, tt * LANE:(tt + 1) * LANE] = zero
    r, c = _iota((LANE, 2 * S5_HALF), 0), _iota((LANE, 2 * S5_HALF), 1)
    copy_b = ((r >> ps) == (c >> (ps + 3))) & ((r & (S5_STATE - 1)) == (c & (S5_STATE - 1)))
    same_b = (r >> gs) == ((c >> ps) & (S5_GPB - 1))
    for s in range(t):
        win_ref[0, s * LANE:(s + 1) * LANE, wide:] = expand(bp_ref[0, s * LANE:(s + 1) * LANE, :],
                                                            copy_b, same_b)
    rows = 2 * S5_STATE * 2
    r, c = _iota((t * S5_GROUP, wide), 0), _iota((t * S5_GROUP, wide), 1)
    copy_c = ((r >> gs) == (c >> (gs + 3))) & ((r & (S5_GROUP - 1)) == (c & (S5_GROUP - 1)))
    for ch in range(2 * S5_HALF // rows):
        r, c = ch * rows + _iota((rows, wide), 0), _iota((rows, wide), 1)
        same_c = ((r >> ps) & (S5_GPB - 1)) == ((c >> gs) & (S5_GPB - 1))
        wc_ref[0, ch * rows:(ch + 1) * rows, :] = expand(cp_ref[0, ch * rows:(ch + 1) * rows, :],
                                                         copy_c, same_c)


def _s5_wgen(kc, bpc, cpc):
    t = S5_CHUNK
    wide = t * LANE
    spec = lambda a: pl.BlockSpec((1,) + a.shape[1:], lambda m: (m, 0, 0))
    return pl.pallas_call(
        _s5_wgen_kernel,
        grid=(S5_MBLK,),
        in_specs=[spec(kc), spec(bpc), spec(cpc)],
        out_specs=[pl.BlockSpec((1, wide, wide + 2 * S5_HALF), lambda m: (m, 0, 0)),
                   pl.BlockSpec((1, 2 * S5_HALF, wide), lambda m: (m, 0, 0))],
        out_shape=[jax.ShapeDtypeStruct((S5_MBLK, wide, wide + 2 * S5_HALF), BF16),
                   jax.ShapeDtypeStruct((S5_MBLK, 2 * S5_HALF, wide), BF16)],
        compiler_params=_cparams(("parallel",)),
        name="s5_wgen",
    )(kc, bpc, cpc)


def _s5_in_kernel(u_ref, w_ref, d_ref, y_ref, b_ref):
    tr = y_ref.shape[1]
    u = jnp.concatenate([u_ref[0, pl.ds(s, tr, stride=S5_CHUNK), :] for s in range(S5_CHUNK)], axis=1)
    res = jnp.dot(u.astype(BF16), w_ref[0], preferred_element_type=F32)
    wide = y_ref.shape[2]
    y_ref[0] = res[:, :wide] + d_ref[0] * u
    b_ref[0] = res[:, wide:]


def _s5_in(proj3, w_in, d_t):
    rows = proj3.shape[1] // S5_CHUNK
    wide = S5_CHUNK * LANE
    tr = min(rows, 512)
    return pl.pallas_call(
        _s5_in_kernel,
        grid=(S5_MBLK, rows // tr),
        in_specs=[pl.BlockSpec((1, tr * S5_CHUNK, LANE), lambda m, r: (CB_SU + m, r, 0)),
                  pl.BlockSpec((1, wide, wide + 2 * S5_HALF), lambda m, r: (m, 0, 0)),
                  pl.BlockSpec((1, 1, wide), lambda m, r: (m, 0, 0))],
        out_specs=[pl.BlockSpec((1, tr, wide), lambda m, r: (m, r, 0)),
                   pl.BlockSpec((1, tr, 2 * S5_HALF), lambda m, r: (m, r, 0))],
        out_shape=[jax.ShapeDtypeStruct((S5_MBLK, rows, wide), F32),
                   jax.ShapeDtypeStruct((S5_MBLK, rows, 2 * S5_HALF), F32)],
        compiler_params=_cparams(("parallel", "parallel")),
        name="s5_in",
    )(proj3, w_in, d_t)


def _s5_scan_kernel(b_ref, a_ref, o_ref, *, bsz, nk):
    ar = a_ref[0, :, :S5_HALF]
    ai = a_ref[0, :, S5_HALF:]

    def body(k, carry):
        out = []
        for b in range(bsz):
            sr, si = carry[2 * b], carry[2 * b + 1]
            row = b * nk + k
            o_ref[0, pl.ds(row, 1), :] = jnp.concatenate([sr, si], axis=1)
            x = b_ref[0, pl.ds(row, 1), :]
            out.append(ar * sr - ai * si + x[:, :S5_HALF])
            out.append(ar * si + ai * sr + x[:, S5_HALF:])
        return tuple(out)

    zero = jnp.zeros((1, S5_HALF), F32)
    lax.fori_loop(0, nk, body, tuple(zero for _ in range(2 * bsz)))


def _s5_scan(bst, a_t, bsz):
    _, rows, wide = bst.shape
    spec = pl.BlockSpec((1, rows, wide), lambda m: (m, 0, 0))
    return pl.pallas_call(
        functools.partial(_s5_scan_kernel, bsz=bsz, nk=rows // bsz),
        grid=(S5_MBLK,),
        in_specs=[spec, pl.BlockSpec((1, 1, wide), lambda m: (m, 0, 0))],
        out_specs=spec,
        out_shape=jax.ShapeDtypeStruct(bst.shape, F32),
        compiler_params=_cparams(("parallel",)),
        name="s5_scan",
    )(bst, a_t)


def _s5_out_kernel(y_ref, s_ref, w_ref, o_ref):
    tr = y_ref.shape[1]
    y = jax.nn.gelu(y_ref[0] + jnp.dot(s_ref[0].astype(BF16), w_ref[0], preferred_element_type=F32))
    for s in range(S5_CHUNK):
        o_ref[0, pl.ds(s, tr, stride=S5_CHUNK), :] = y[:, s * LANE:(s + 1) * LANE]


def _s5_out(y_intra, s_prev, w_c):
    _, rows, wide = y_intra.shape
    tr = min(rows, 512)
    return pl.pallas_call(
        _s5_out_kernel,
        grid=(S5_MBLK, rows // tr),
        in_specs=[pl.BlockSpec((1, tr, wide), lambda m, r: (m, r, 0)),
                  pl.BlockSpec((1, tr, 2 * S5_HALF), lambda m, r: (m, r, 0)),
                  pl.BlockSpec((1, 2 * S5_HALF, wide), lambda m, r: (m, 0, 0))],
        out_specs=pl.BlockSpec((1, tr * S5_CHUNK, LANE), lambda m, r: (m, r, 0)),
        out_shape=jax.ShapeDtypeStruct((S5_MBLK, rows * S5_CHUNK, LANE), F32),
        compiler_params=_cparams(("parallel", "parallel")),
        name="s5_out",
    )(y_intra, s_prev, w_c)


def _glu_kernel(y_ref, z_ref, w_ref, o_ref):
    nb = y_ref.shape[0]
    y = jnp.concatenate([y_ref[c] for c in range(nb)], axis=1)
    z = jnp.concatenate([z_ref[c] for c in range(nb)], axis=1)
    gate = jax.nn.sigmoid(jnp.dot(y.astype(BF16), w_ref[...], preferred_element_type=F32))
    o = (y * gate * (z * jax.nn.sigmoid(z))).astype(BF16)
    for c in range(nb):
        o_ref[c] = o[:, c * LANE:(c + 1) * LANE]


def _glu(y5, proj3, glu_w):
    nb, rows, _ = y5.shape
    tm = 512
    spec = lambda blk: pl.BlockSpec((nb, tm, LANE), lambda i: (blk, i, 0))
    return pl.pallas_call(
        _glu_kernel,
        grid=(rows // tm,),
        in_specs=[spec(0), spec(CB_SZ // nb), pl.BlockSpec((S5_WIDTH, S5_WIDTH), lambda i: (0, 0))],
        out_specs=spec(0),
        out_shape=jax.ShapeDtypeStruct((nb, rows, LANE), BF16),
        compiler_params=_cparams(("parallel",)),
        name="s5_glu",
    )(y5, proj3, glu_w.astype(BF16))


def _outproj_kernel(a_ref, b_ref, c_ref, w_ref, x_ref, o_ref):
    parts = ([a_ref[h] for h in range(a_ref.shape[0])] + [b_ref[h] for h in range(b_ref.shape[0])]
             + [c_ref[h] for h in range(c_ref.shape[0])])
    mixed = jnp.concatenate(parts, axis=1)
    o_ref[...] = x_ref[...] + jnp.dot(mixed, w_ref[...], preferred_element_type=F32)


def _outproj(m_moba, m_nsa, m_s5, w_out, x2):
    rows, d = x2.shape
    tm = 512
    lspec = lambda n: pl.BlockSpec((n, tm, LANE), lambda i: (0, i, 0))
    return pl.pallas_call(
        _outproj_kernel,
        grid=(rows // tm,),
        in_specs=[lspec(m_moba.shape[0]), lspec(m_nsa.shape[0]), lspec(m_s5.shape[0]),
                  pl.BlockSpec(w_out.shape, lambda i: (0, 0)),
                  pl.BlockSpec((tm, d), lambda i: (i, 0))],
        out_specs=pl.BlockSpec((tm, d), lambda i: (i, 0)),
        out_shape=jax.ShapeDtypeStruct((rows, d), F32),
        compiler_params=_cparams(("parallel",)),
        name="outproj",
    )(m_moba, m_nsa, m_s5, w_out.astype(BF16), x2)


def _w_in_source(blk):
    mw, nw, kvw, ng = MOBA_HEADS * HEAD_DIM, NSA_HEADS * HEAD_DIM, HEAD_DIM, 3 * NSA_HEADS
    o_kv = 4 * mw + nw
    o_ng = o_kv + 6 * kvw
    o_nz = o_ng + ng
    src = jnp.where(blk < CB_NZ, blk * LANE,
          jnp.where(blk < CB_NKS, o_nz + (blk - CB_NZ) * LANE,
          jnp.where(blk < CB_NKC, o_kv + 2 * kvw + (blk - CB_NKS) * LANE,
          jnp.where(blk < CB_NG, o_kv + (blk - CB_NKC) * LANE, o_ng))))
    real = jnp.where(blk < CB_NG, LANE, jnp.where(blk == CB_NG, ng, 0))
    return src, real


def _wperm_kernel(w_ref, o_ref):
    _, real = _w_in_source(pl.program_id(0))
    for layer in range(o_ref.shape[0]):
        x = w_ref[:, layer, :]
        x = jnp.where(_iota(x.shape, 0) < real, x, 0.0)
        o_ref[layer] = x.T.astype(BF16)


def _permute_w_in(w_in_all):
    wt = jnp.transpose(w_in_all, (2, 0, 1))
    _, nl, d = wt.shape
    return pl.pallas_call(
        _wperm_kernel,
        grid=(IN_BLOCKS,),
        in_specs=[pl.BlockSpec((pl.Element(LANE), pl.Element(nl), pl.Element(d)),
                               lambda c: (_w_in_source(c)[0], 0, 0))],
        out_specs=pl.BlockSpec((nl, d, LANE), lambda c: (0, 0, c)),
        out_shape=jax.ShapeDtypeStruct((nl, d, IN_BLOCKS * LANE), BF16),
        compiler_params=_cparams(("parallel",)),
        name="w_in_permute",
    )(wt)


def _layer(x2, bsz, seq, tabs, ctabs, overlap, w_in, layer, norm_w, w_out, moba_q_norm, moba_k_norm, nsa_q_norm,
           nsa_kc_norm, nsa_ks_norm, nsa_kw_norm, nsa_pe_k, nsa_pe_v, nsa_cmp_k_w1, nsa_cmp_k_w2,
           nsa_cmp_v_w1, nsa_cmp_v_w2, s5_a_re, s5_a_im, s5_b_re, s5_b_im, s5_c_re, s5_c_im, s5_d,
           s5_log_dt, s5_glu_w):
    proj3 = _inproj(x2, norm_w, w_in, layer)
    mk_n, mv_b, nk, kmean = _kprep(proj3, seq, moba_k_norm, nsa_ks_norm, nsa_kw_norm, tabs)
    m_moba = _moba(proj3, bsz, seq, moba_q_norm, tabs, kmean.transpose(1, 0, 2), mk_n, mv_b)
    k_cmp, v_cmp = _compress(proj3, bsz, seq, nsa_pe_k, nsa_pe_v, nsa_cmp_k_w1, nsa_cmp_k_w2,
                             nsa_cmp_v_w1, nsa_cmp_v_w2, nsa_kc_norm, ctabs)
    m_nsa = _nsa(proj3, bsz, seq, nsa_q_norm, tabs, k_cmp, v_cmp, overlap, nk)
    w_s5_in, w_s5_out, a_t = _s5_weights(s5_a_re, s5_a_im, s5_b_re, s5_b_im, s5_c_re, s5_c_im, s5_log_dt)
    d_t = jnp.tile(s5_d.astype(F32).reshape(S5_MBLK, 1, LANE), (1, 1, S5_CHUNK))
    y_intra, bst = _s5_in(proj3, w_s5_in, d_t)
    y5 = _s5_out(y_intra, _s5_scan(bst, a_t, bsz), w_s5_out)
    m_s5 = _glu(y5, proj3, s5_glu_w)
    return _outproj(m_moba, m_nsa, m_s5, w_out, x2)


def kernel(x, norm_w, w_in, w_out, moba_q_norm, moba_k_norm, nsa_q_norm, nsa_kc_norm, nsa_ks_norm, nsa_kw_norm, nsa_pe_k, nsa_pe_v, nsa_cmp_k_w1, nsa_cmp_k_w2, nsa_cmp_v_w1, nsa_cmp_v_w2, s5_a_re, s5_a_im, s5_b_re, s5_b_im, s5_c_re, s5_c_im, s5_d, s5_log_dt, s5_glu_w):
    bsz, seq, d = x.shape
    tabs = _rope_tables(jnp.arange(seq, dtype=F32))
    ncmp = seq // NSA_CMP_STRIDE
    ctabs = _rope_tables(jnp.arange(ncmp, dtype=F32) * NSA_CMP_STRIDE + (NSA_CMP_LEN - 1))
    nsel = seq // NSA_SEL_BLOCK
    ci = np.arange(ncmp)[:, None] * NSA_CMP_STRIDE
    sj = np.arange(nsel)[None, :] * NSA_SEL_BLOCK
    overlap = np.zeros((ncmp, LANE), np.float32)
    overlap[:, :nsel] = (ci < sj + NSA_SEL_BLOCK) & (ci + NSA_CMP_LEN > sj)
    overlap = jnp.asarray(overlap)
    params = (norm_w, w_out, moba_q_norm, moba_k_norm, nsa_q_norm, nsa_kc_norm, nsa_ks_norm,
              nsa_kw_norm, nsa_pe_k, nsa_pe_v, nsa_cmp_k_w1, nsa_cmp_k_w2, nsa_cmp_v_w1, nsa_cmp_v_w2,
              s5_a_re, s5_a_im, s5_b_re, s5_b_im, s5_c_re, s5_c_im, s5_d, s5_log_dt, s5_glu_w)
    x2 = x.reshape(bsz * seq, d)
    w_perm = _permute_w_in(w_in)
    for layer in range(norm_w.shape[0]):
        x2 = _layer(x2, bsz, seq, tabs, ctabs, overlap, w_perm, layer, *[p[layer] for p in params])
    return x2.reshape(bsz, seq, d)
```

```python
import functools
import math

import numpy as np
import jax
import jax.numpy as jnp
from jax import lax
from jax.experimental import pallas as pl
from jax.experimental.pallas import tpu as pltpu

F32 = jnp.float32
BF16 = jnp.bfloat16
HIGHEST = lax.Precision.HIGHEST

LANE = 128
SUBLANE = 8
HEAD_DIM = 128
ROPE_DIM = HEAD_DIM // 4
ROPE_HALF = ROPE_DIM // 2
ROPE_THETA = 500000.0
EPS = 1e-6
SCALE = HEAD_DIM ** -0.5
NEG = -1e30

MOBA_HEADS = 4
MOBA_BLOCK = 256
MOBA_TOPK = 3

NSA_HEADS = 4
NSA_CMP_LEN = 32
NSA_CMP_STRIDE = 16
NSA_SEL_BLOCK = 64
NSA_SEL_TOPN = 16
NSA_WINDOW = 512
NSA_TQ = 256
NSA_SEL_PER_TILE = NSA_TQ // NSA_SEL_BLOCK
assert NSA_WINDOW == 2 * NSA_TQ

S5_WIDTH = 1024
S5_GROUP = 16
S5_GROUPS = S5_WIDTH // S5_GROUP
S5_STATE = 64
S5_CHUNK = 16
S5_MBLK = S5_WIDTH // LANE
S5_GPB = LANE // S5_GROUP
S5_HALF = S5_GPB * S5_STATE

CB_MQ, CB_MK, CB_MV, CB_MZ, CB_NQ, CB_NZ, CB_SU, CB_SZ = 0, 4, 8, 12, 16, 20, 24, 32
CB_NKS, CB_NVS, CB_NKW, CB_NVW, CB_NKC, CB_NVC, CB_NG = 40, 41, 42, 43, 44, 45, 46
IN_BLOCKS = 48

VMEM_LIMIT = 56 * 1024 * 1024


def _cparams(sem):
    return pltpu.CompilerParams(dimension_semantics=sem, vmem_limit_bytes=VMEM_LIMIT)


def _iota(shape, dim):
    return lax.broadcasted_iota(jnp.int32, shape, dim)


def _head_norm(x, w):
    return x * lax.rsqrt(jnp.mean(x * x, axis=-1, keepdims=True) + EPS) * w


def _head_norm_mxu(x, w):
    sq = x * x
    hi = sq.astype(BF16)
    lo = (sq - hi.astype(F32)).astype(BF16)
    avg = jnp.full((LANE, LANE), 1.0 / HEAD_DIM, BF16)
    ms = jnp.dot(hi, avg, preferred_element_type=F32) + jnp.dot(lo, avg, preferred_element_type=F32)
    return x * lax.rsqrt(ms + EPS) * w


def _rope(x, c, s1, s2):
    return x * c + pltpu.roll(x, LANE - ROPE_HALF, 1) * s1 + pltpu.roll(x, ROPE_HALF, 1) * s2


def _dot_nt(a, b):
    return lax.dot_general(a, b, (((1,), (1,)), ((), ())), preferred_element_type=F32)


def _rope_tables(pos):
    inv = ROPE_THETA ** (-jnp.arange(0, ROPE_DIM, 2, dtype=F32) / ROPE_DIM)
    ang = pos.astype(F32)[:, None] * inv[None, :]
    cos, sin = jnp.cos(ang), jnp.sin(ang)
    n = pos.shape[0]
    c = jnp.concatenate([cos, cos, jnp.ones((n, LANE - ROPE_DIM), F32)], axis=1)
    s1 = jnp.concatenate([-sin, jnp.zeros((n, LANE - ROPE_HALF), F32)], axis=1)
    s2 = jnp.concatenate([jnp.zeros((n, ROPE_HALF), F32), sin, jnp.zeros((n, LANE - ROPE_DIM), F32)], axis=1)
    return c, s1, s2


def _inproj_kernel(x_ref, nw_ref, w_ref, o_ref, h_ref):
    @pl.when(pl.program_id(1) == 0)
    def _():
        x = x_ref[...]
        ms = jnp.mean(x * x, axis=-1, keepdims=True)
        h_ref[...] = (x * lax.rsqrt(ms + EPS) * nw_ref[...]).astype(BF16)

    res = jnp.dot(h_ref[...], w_ref[...], preferred_element_type=F32)
    for c in range(o_ref.shape[0]):
        o_ref[c] = res[:, c * LANE:(c + 1) * LANE]


def _inproj(x2, norm_w, w_perm, layer):
    rows, d = x2.shape
    tm, tn = 1024, 1024
    nb = tn // LANE
    return pl.pallas_call(
        _inproj_kernel,
        grid=(rows // tm, IN_BLOCKS // nb),
        in_specs=[pl.BlockSpec((tm, d), lambda i, j: (i, 0)),
                  pl.BlockSpec((1, d), lambda i, j: (0, 0)),
                  pl.BlockSpec((None, d, tn), lambda i, j: (layer, 0, j))],
        out_specs=pl.BlockSpec((nb, tm, LANE), lambda i, j: (j, i, 0)),
        out_shape=jax.ShapeDtypeStruct((IN_BLOCKS, rows, LANE), F32),
        scratch_shapes=[pltpu.VMEM((tm, d), BF16)],
        compiler_params=_cparams(("parallel", "arbitrary")),
        name="inproj",
    )(x2, norm_w.reshape(1, d), w_perm)


def _kprep_kernel(mk_ref, mv_ref, nk_ref, mkw_ref, ksw_ref, kww_ref, c_ref, s1_ref, s2_ref,
                  mk_o, mv_o, nk_o, km_o, vst_o):
    c, s1, s2 = c_ref[...], s1_ref[...], s2_ref[...]
    means = []
    for h in range(MOBA_HEADS):
        k = _rope(_head_norm_mxu(mk_ref[h], mkw_ref[...]), c, s1, s2)
        mk_o[h] = k.astype(BF16)
        mv_o[h] = mv_ref[h].T.astype(BF16)
        means.append(jnp.mean(k, axis=0, keepdims=True))
    km_o[0] = jnp.concatenate(means, axis=0)
    nk_o[0] = _rope(_head_norm_mxu(nk_ref[0], ksw_ref[...]), c, s1, s2).astype(BF16)
    nk_o[1] = nk_ref[1].astype(BF16)
    vst_o[...] = nk_ref[1].T.astype(BF16)
    nk_o[2] = _rope(_head_norm_mxu(nk_ref[2], kww_ref[...]), c, s1, s2).astype(BF16)
    nk_o[3] = nk_ref[3].astype(BF16)


def _kprep(proj3, seq, mk_w, ks_w, kw_w, tabs):
    rows = proj3.shape[1]
    t = MOBA_BLOCK
    nt = seq // t
    blk4 = lambda cb: pl.BlockSpec((4, t, LANE), lambda i: (cb // 4, i, 0))
    wspec = pl.BlockSpec((1, LANE), lambda i: (0, 0))
    tspec = pl.BlockSpec((t, LANE), lambda i: (i % nt, 0))
    out4 = pl.BlockSpec((4, t, LANE), lambda i: (0, i, 0))
    return pl.pallas_call(
        _kprep_kernel,
        grid=(rows // t,),
        in_specs=[blk4(CB_MK), blk4(CB_MV), blk4(CB_NKS), wspec, wspec, wspec, tspec, tspec, tspec],
        out_specs=[out4, pl.BlockSpec((4, None, LANE, t), lambda i: (0, i // nt, 0, i % nt)), out4,
                   pl.BlockSpec((1, MOBA_HEADS, LANE), lambda i: (i, 0, 0)),
                   pl.BlockSpec((None, LANE, t), lambda i: (i // nt, 0, i % nt))],
        out_shape=[jax.ShapeDtypeStruct((4, rows, LANE), BF16),
                   jax.ShapeDtypeStruct((4, rows // seq, LANE, seq), BF16),
                   jax.ShapeDtypeStruct((4, rows, LANE), BF16),
                   jax.ShapeDtypeStruct((rows // t, MOBA_HEADS, LANE), F32),
                   jax.ShapeDtypeStruct((rows // seq, LANE, seq), BF16)],
        compiler_params=_cparams(("parallel",)),
        name="kprep",
    )(proj3, proj3, proj3, mk_w.reshape(1, LANE), ks_w.reshape(1, LANE), kw_w.reshape(1, LANE), *tabs)


def _pad_rows(x, rows):
    return x if x.shape[0] == rows else jnp.concatenate(
        [x, jnp.zeros((rows - x.shape[0], x.shape[1]), x.dtype)], axis=0)


def _topk_mask_t(score_t, k):
    n, cols = score_t.shape
    assert n % SUBLANE == 0
    row_in = _iota((SUBLANE, cols), 0)
    groups = [score_t[g:g + SUBLANE] for g in range(0, n, SUBLANE)]
    ranks = [jnp.zeros((SUBLANE, cols), F32) for _ in groups]
    for mm in range(n):
        cm = score_t[mm:mm + 1, :]
        for gi, sc in enumerate(groups):
            lo = gi * SUBLANE
            if lo > mm:
                inc = jnp.where(cm >= sc, 1.0, 0.0)
            elif lo + SUBLANE - 1 <= mm:
                inc = jnp.where(cm > sc, 1.0, 0.0)
            else:
                inc = jnp.where(row_in > mm - lo, jnp.where(cm >= sc, 1.0, 0.0), jnp.where(cm > sc, 1.0, 0.0))
            ranks[gi] = ranks[gi] + inc
    return jnp.concatenate(ranks, axis=0) < k


def _moba_kernel(q_ref, z_ref, qw_ref, c_ref, s1_ref, s2_ref, km_ref, k_ref, vt_ref, o_ref,
                 qt_s, sel_s, s_s, p_s, acc_s):
    i = pl.program_id(1)
    t = MOBA_BLOCK
    nb = km_ref.shape[1]
    nh = q_ref.shape[0]
    npairs = (i + 1) // 2
    blk_t = _iota((nb, t), 0)
    past = blk_t < i
    causal = _iota((t, t), 0) <= _iota((t, t), 1)
    start = pl.multiple_of(i * t, t)

    def put_scores(slot, h, pair):
        src = jnp.minimum(pair, nb // 2 - 1)
        off = pl.multiple_of(src * (2 * t), 2 * t)
        sc = jnp.dot(k_ref[h, pl.ds(off, 2 * t), :], qt_s[h], preferred_element_type=F32)
        row = jnp.minimum(2 * pair, nb - 2)
        s_s[slot, h, :t, :] = sc[:t] + sel_s[h, pl.ds(row, 1), :]
        s_s[slot, h, t:, :] = sc[t:] + sel_s[h, pl.ds(row + 1, 1), :]

    def weighted_values(h, pair):
        off = pl.multiple_of(pair * (2 * t), 2 * t)
        return jnp.dot(vt_ref[h, :, pl.ds(off, 2 * t)], p_s[h], preferred_element_type=F32)

    state = []
    for h in range(nh):
        qf = _rope(_head_norm(q_ref[h], qw_ref[...]), c_ref[...], s1_ref[...], s2_ref[...])
        qt = (qf * SCALE).T.astype(BF16)
        qt_s[h] = qt
        gate_t = lax.dot_general(_pad_rows(km_ref[h], LANE), qf, (((1,), (1,)), ((), ())), precision=HIGHEST,
                                 preferred_element_type=F32)[:nb]
        top = _topk_mask_t(jnp.where(past, gate_t, -jnp.inf), MOBA_TOPK)
        sel_s[h] = jnp.where(top & past, 0.0, NEG)
        s = jnp.where(causal, jnp.dot(k_ref[h, pl.ds(start, t), :], qt, preferred_element_type=F32), NEG)
        m = jnp.max(s, axis=0, keepdims=True)
        p = jnp.exp(s - m)
        acc_s[h] = jnp.dot(vt_ref[h, :, pl.ds(start, t)], p.astype(BF16), preferred_element_type=F32)
        state += [m, jnp.sum(p, axis=0, keepdims=True)]
        put_scores(0, h, 0)
        p_s[h] = jnp.zeros((2 * t, t), BF16)

    def trip(pair, carry, src, dst):
        prev = jnp.maximum(pair - 1, 0)
        out = []
        for h in range(nh):
            m, l = carry[2 * h:2 * h + 2]
            put_scores(dst, h, pair + 1)
            acc = acc_s[h] + weighted_values(h, prev)
            s = s_s[src, h]
            m_new = jnp.maximum(m, jnp.max(s, axis=0, keepdims=True))
            p = jnp.exp(s - m_new)
            alpha = jnp.exp(m - m_new)
            l_new = alpha * l + jnp.sum(p, axis=0, keepdims=True)
            acc_s[h] = alpha * acc
            p_s[h] = p.astype(BF16)
            out += [m_new, l_new]
        return tuple(out)

    def body(q, carry):
        return trip(2 * q + 1, trip(2 * q, carry, 0, 1), 1, 0)

    ntrips = 2 * ((npairs + 1) // 2)
    state = lax.fori_loop(0, ntrips // 2, body, tuple(state))
    last = jnp.maximum(ntrips - 1, 0)
    for h in range(nh):
        l = state[2 * h + 1]
        acc = acc_s[h] + weighted_values(h, last)
        z = z_ref[h]
        o_ref[h] = ((acc * (1.0 / l)).T * (z * jax.nn.sigmoid(z))).astype(BF16)


def _moba(proj3, bsz, seq, q_w, tabs, km_t, mk_n, mv_t):
    rows = proj3.shape[1]
    t = MOBA_BLOCK
    nb = seq // t
    nh = MOBA_HEADS
    assert nb % 4 == 0
    qspec = lambda cb: pl.BlockSpec((nh, t, LANE), lambda b, i: (cb // nh, b * nb + i, 0))
    tspec = pl.BlockSpec((t, LANE), lambda b, i: (i, 0))
    return pl.pallas_call(
        _moba_kernel,
        grid=(bsz, nb),
        in_specs=[qspec(CB_MQ), qspec(CB_MZ), pl.BlockSpec((1, LANE), lambda b, i: (0, 0)),
                  tspec, tspec, tspec,
                  pl.BlockSpec((nh, nb, LANE), lambda b, i: (0, b, 0)),
                  pl.BlockSpec((nh, seq, LANE), lambda b, i: (0, b, 0)),
                  pl.BlockSpec((nh, None, LANE, seq), lambda b, i: (0, b, 0, 0))],
        out_specs=pl.BlockSpec((nh, t, LANE), lambda b, i: (0, b * nb + i, 0)),
        out_shape=jax.ShapeDtypeStruct((nh, rows, LANE), BF16),
        scratch_shapes=[pltpu.VMEM((nh, LANE, t), BF16),
                        pltpu.VMEM((nh, nb, t), F32),
                        pltpu.VMEM((2, nh, 2 * t, t), F32),
                        pltpu.VMEM((nh, 2 * t, t), BF16),
                        pltpu.VMEM((nh, LANE, t), F32)],
        compiler_params=_cparams(("parallel", "parallel")),
        name="moba",
    )(proj3, proj3, q_w.reshape(1, LANE), *tabs, km_t, mk_n, mv_t)


def _cmp_kernel(hk_ref, hv_ref, pek_ref, pev_ref, w1k_ref, w2k_ref, w1v_ref, w2v_ref, nw_ref,
                c_ref, s1_ref, s2_ref, ko_ref, vo_ref):
    st = NSA_CMP_STRIDE
    nh = ko_ref.shape[1]

    def compress(x_ref, pe_ref, w1_ref, w2_ref):
        a = jnp.zeros((nh, LANE), F32)
        b = jnp.zeros((nh, LANE), F32)
        for l in range(st):
            x = x_ref[0, pl.ds(l, nh, stride=st), :]
            a = a + jnp.dot((x + pe_ref[l:l + 1, :]).astype(BF16), w1_ref[l * LANE:(l + 1) * LANE, :],
                            preferred_element_type=F32)
            b = b + jnp.dot((x + pe_ref[st + l:st + l + 1, :]).astype(BF16),
                            w1_ref[(st + l) * LANE:(st + l + 1) * LANE, :], preferred_element_type=F32)
        pre = a + pltpu.roll(b, nh - 1, 0)
        return jnp.dot(jax.nn.gelu(pre).astype(BF16), w2_ref[...], preferred_element_type=F32)

    kc = compress(hk_ref, pek_ref, w1k_ref, w2k_ref)
    vc = compress(hv_ref, pev_ref, w1v_ref, w2v_ref)
    ko_ref[0] = _rope(_head_norm(kc, nw_ref[...]), c_ref[...], s1_ref[...], s2_ref[...]).astype(BF16)
    vo_ref[0] = vc.astype(BF16)


def _compress(proj3, bsz, seq, pe_k, pe_v, w1k, w2k, w1v, w2v, kc_w, ctabs):
    nh = seq // NSA_CMP_STRIDE
    wide = NSA_CMP_LEN * HEAD_DIM
    hspec = lambda cb: pl.BlockSpec((1, seq, LANE), lambda b: (cb, b, 0))
    full = lambda shape: pl.BlockSpec(shape, lambda b: tuple(0 for _ in shape))
    ospec = pl.BlockSpec((1, nh, LANE), lambda b: (b, 0, 0))
    return pl.pallas_call(
        _cmp_kernel,
        grid=(bsz,),
        in_specs=[hspec(CB_NKC), hspec(CB_NVC), full((NSA_CMP_LEN, LANE)), full((NSA_CMP_LEN, LANE)),
                  full((wide, LANE)), full((LANE, LANE)), full((wide, LANE)), full((LANE, LANE)),
                  full((1, LANE)), full((nh, LANE)), full((nh, LANE)), full((nh, LANE))],
        out_specs=[ospec, ospec],
        out_shape=[jax.ShapeDtypeStruct((bsz, nh, LANE), BF16)] * 2,
        compiler_params=_cparams(("parallel",)),
        name="nsa_compress",
    )(proj3, proj3, pe_k, pe_v, w1k.astype(BF16), w2k.astype(BF16),
      w1v.astype(BF16), w2v.astype(BF16), kc_w.reshape(1, LANE), *ctabs)


def _nsa_kernel(q_ref, z_ref, g_ref, qw_ref, c_ref, s1_ref, s2_ref, kc_ref, vc_ref, ov_ref,
                ks_ref, vst_ref, kw_ref, vw_ref, o_ref, qb_s, qt_s, acc_s, ocmp_s, osel_s, s_s, p_s):
    i = pl.program_id(1)
    t = NSA_TQ
    ncmp = kc_ref.shape[1]
    nsel = ks_ref.shape[0] // NSA_SEL_BLOCK
    sel_shift = int(math.log2(NSA_SEL_BLOCK))
    trow = i * t + _iota((t, 1), 0)
    row = _iota((t, t), 0)
    col = _iota((t, t), 1)
    causal = col <= row

    c, s1, s2 = c_ref[...], s1_ref[...], s2_ref[...]
    for h in range(NSA_HEADS):
        qb_s[h] = (_rope(_head_norm_mxu(q_ref[h], qw_ref[...]), c, s1, s2) * SCALE).astype(BF16)

    valid = (_iota((t, ncmp), 1) * NSA_CMP_STRIDE + (NSA_CMP_LEN - 1)) <= trow
    kc, vc = kc_ref[0], vc_ref[0]
    psum = jnp.zeros((t, ncmp), F32)
    for h in range(NSA_HEADS):
        s = jnp.where(valid, _dot_nt(qb_s[h], kc), NEG)
        mx = jnp.max(s, axis=1, keepdims=True)
        e = jnp.exp(s - mx)
        p = e * jnp.where(mx > 0.5 * NEG, 1.0 / jnp.sum(e, axis=1, keepdims=True), 0.0)
        ocmp_s[h] = jnp.dot(p.astype(BF16), vc, preferred_element_type=F32)
        psum = psum + p

    imp_t = jnp.dot(psum, ov_ref[...], precision=HIGHEST, preferred_element_type=F32).T[:nsel]
    jj = _iota((nsel, t), 0)
    cur = (i * t + _iota((1, t), 1)) >> sel_shift
    score = jnp.where(jj <= cur, imp_t, -jnp.inf)
    score = jnp.where((jj == 0) | (jj == cur) | (jj == cur - 1), jnp.inf, score)
    top = _topk_mask_t(score, NSA_SEL_TOPN)
    selbias_t = jnp.concatenate([jnp.where(top, 0.0, NEG), jnp.zeros((LANE - nsel - SUBLANE, t), F32),
                                 jnp.full((SUBLANE, t), NEG, F32)], axis=0).astype(BF16)
    ntiles = ks_ref.shape[0] // t
    key_blk = _iota((t, LANE), 0) >> sel_shift

    def tile_bias(kt):
        blk = jnp.where(kt < i, kt * NSA_SEL_PER_TILE + key_blk, LANE - 1)
        onehot = jnp.where(_iota((t, LANE), 1) == blk, 1.0, 0.0).astype(BF16)
        return jnp.dot(onehot, selbias_t, preferred_element_type=F32)

    def score_inputs(kt):
        src = jnp.minimum(kt, ntiles - 1)
        return ks_ref[pl.ds(pl.multiple_of(src * t, t), t), :], tile_bias(kt)

    def put_scores(slot, h, k, bias):
        s_s[slot, h] = jnp.dot(k, qt_s[h], preferred_element_type=F32) + bias

    own = pl.multiple_of(i * t, t)
    own_blk = i * NSA_SEL_PER_TILE + key_blk
    own_bias = jnp.dot(jnp.where(_iota((t, LANE), 1) == own_blk, 1.0, 0.0).astype(BF16), selbias_t,
                       preferred_element_type=F32)
    k_own, vt_own = ks_ref[pl.ds(own, t), :], vst_ref[:, pl.ds(own, t)]
    state = []
    for h in range(NSA_HEADS):
        qt_s[h] = qb_s[h].T
        s = jnp.where(row <= col, jnp.dot(k_own, qt_s[h], preferred_element_type=F32) + own_bias, NEG)
        m = jnp.max(s, axis=0, keepdims=True)
        p = jnp.exp(s - m)
        acc_s[h] = jnp.dot(vt_own, p.astype(BF16), preferred_element_type=F32)
        state += [m, jnp.sum(p, axis=0, keepdims=True)]
        p_s[h] = jnp.zeros((t, t), BF16)
    k_first, bias_first = score_inputs(0)
    for h in range(NSA_HEADS):
        put_scores(0, h, k_first, bias_first)

    def trip(kt, carry, src, dst):
        k_next, bias_next = score_inputs(kt + 1)
        vt_prev = vst_ref[:, pl.ds(pl.multiple_of(jnp.clip(kt - 1, 0, ntiles - 1) * t, t), t)]
        out = []
        for h in range(NSA_HEADS):
            m, l = carry[2 * h:2 * h + 2]
            put_scores(dst, h, k_next, bias_next)
            acc = acc_s[h] + jnp.dot(vt_prev, p_s[h], preferred_element_type=F32)
            s = s_s[src, h]
            m_new = jnp.maximum(m, jnp.max(s, axis=0, keepdims=True))
            p = jnp.exp(s - m_new)
            alpha = jnp.exp(m - m_new)
            acc_s[h] = alpha * acc
            p_s[h] = p.astype(BF16)
            out += [m_new, alpha * l + jnp.sum(p, axis=0, keepdims=True)]
        return tuple(out)

    def sel_body(q, carry):
        return trip(2 * q + 1, trip(2 * q, carry, 0, 1), 1, 0)

    ntrips = 2 * ((i + 1) // 2)
    state = lax.fori_loop(0, ntrips // 2, sel_body, tuple(state))
    vt_last = vst_ref[:, pl.ds(pl.multiple_of(jnp.clip(ntrips - 1, 0, ntiles - 1) * t, t), t)]
    for h in range(NSA_HEADS):
        acc = acc_s[h] + jnp.dot(vt_last, p_s[h], preferred_element_type=F32)
        osel_s[h] = (acc * (1.0 / state[2 * h + 1])).T

    far = pl.multiple_of(jnp.maximum(i - 2, 0) * t, t)
    mid = pl.multiple_of(jnp.maximum(i - 1, 0) * t, t)
    k3 = jnp.concatenate([kw_ref[pl.ds(far, t), :], kw_ref[pl.ds(mid, t), :], kw_ref[pl.ds(own, t), :]], axis=0)
    v3 = jnp.concatenate([vw_ref[pl.ds(far, t), :], vw_ref[pl.ds(mid, t), :], vw_ref[pl.ds(own, t), :]], axis=0)
    allowed = jnp.concatenate([(col > row) & (i >= 2), jnp.full((t, t), True) & (i >= 1), causal], axis=1)
    g = jax.nn.sigmoid(g_ref[0])
    for h in range(NSA_HEADS):
        s = jnp.where(allowed, _dot_nt(qb_s[h], k3), NEG)
        p = jnp.exp(s - jnp.max(s, axis=1, keepdims=True))
        o_win = jnp.dot(p.astype(BF16), v3, preferred_element_type=F32) * (1.0 / jnp.sum(p, axis=1, keepdims=True))
        o = (g[:, 3 * h:3 * h + 1] * ocmp_s[h] + g[:, 3 * h + 1:3 * h + 2] * osel_s[h]
             + g[:, 3 * h + 2:3 * h + 3] * o_win)
        z = z_ref[h]
        o_ref[h] = (o * (z * jax.nn.sigmoid(z))).astype(BF16)


def _nsa(proj3, bsz, seq, q_w, tabs, k_cmp, v_cmp, overlap, nk, vs_t):
    rows = proj3.shape[1]
    t = NSA_TQ
    nt = seq // t
    ncmp = seq // NSA_CMP_STRIDE
    nsel = seq // NSA_SEL_BLOCK
    blk4 = lambda cb: pl.BlockSpec((4, t, LANE), lambda b, i: (cb // 4, b * nt + i, 0))
    tspec = pl.BlockSpec((t, LANE), lambda b, i: (i, 0))
    cspec = pl.BlockSpec((1, ncmp, LANE), lambda b, i: (b, 0, 0))
    kvspec = lambda which: pl.BlockSpec((None, seq, LANE), lambda b, i: (which, b, 0))
    return pl.pallas_call(
        _nsa_kernel,
        grid=(bsz, nt),
        in_specs=[blk4(CB_NQ), blk4(CB_NZ),
                  pl.BlockSpec((1, t, LANE), lambda b, i: (CB_NG, b * nt + i, 0)),
                  pl.BlockSpec((1, LANE), lambda b, i: (0, 0)), tspec, tspec, tspec, cspec, cspec,
                  pl.BlockSpec((ncmp, LANE), lambda b, i: (0, 0)),
                  kvspec(0), pl.BlockSpec((None, LANE, seq), lambda b, i: (b, 0, 0)), kvspec(2), kvspec(3)],
        out_specs=pl.BlockSpec((4, t, LANE), lambda b, i: (0, b * nt + i, 0)),
        out_shape=jax.ShapeDtypeStruct((NSA_HEADS, rows, LANE), BF16),
        scratch_shapes=[pltpu.VMEM((NSA_HEADS, t, LANE), BF16),
                        pltpu.VMEM((NSA_HEADS, LANE, t), BF16),
                        pltpu.VMEM((NSA_HEADS, LANE, t), F32),
                        pltpu.VMEM((NSA_HEADS, t, LANE), F32),
                        pltpu.VMEM((NSA_HEADS, t, LANE), F32),
                        pltpu.VMEM((2, NSA_HEADS, t, t), F32),
                        pltpu.VMEM((NSA_HEADS, t, t), BF16)],
        compiler_params=_cparams(("parallel", "parallel")),
        name="nsa",
    )(proj3, proj3, proj3, q_w.reshape(1, LANE), *tabs, k_cmp, v_cmp, overlap, nk, vs_t, nk, nk)


def _s5_weights(a_re, a_im, b_re, b_im, c_re, c_im, log_dt):
    t = S5_CHUNK
    dt = jnp.exp(log_dt.astype(F32))[:, None]
    ar, ai = a_re.astype(F32), a_im.astype(F32)
    ang = dt * ai
    mag = jnp.exp(dt * ar)
    abar_r, abar_i = mag * jnp.cos(ang), mag * jnp.sin(ang)
    nr, ni = abar_r - 1.0, abar_i
    den = ar * ar + ai * ai
    fr = (nr * ar + ni * ai) / den
    fi = (ni * ar - nr * ai) / den
    bt_r, bt_i = b_re.astype(F32).transpose(0, 2, 1), b_im.astype(F32).transpose(0, 2, 1)
    bbar_r = fr[:, None, :] * bt_r - fi[:, None, :] * bt_i
    bbar_i = fr[:, None, :] * bt_i + fi[:, None, :] * bt_r

    def powers(tau):
        tau = jnp.asarray(tau, F32)[:, None, None]
        pmag = jnp.exp(tau * (dt * ar)[None])
        return pmag * jnp.cos(tau * ang[None]), pmag * jnp.sin(tau * ang[None])

    pw_r, pw_i = powers(np.arange(t + 1))
    cr, ci = c_re.astype(F32), c_im.astype(F32)
    cp_r = cr[None] * pw_r[:, :, None, :] - ci[None] * pw_i[:, :, None, :]
    cp_i = cr[None] * pw_i[:, :, None, :] + ci[None] * pw_r[:, :, None, :]
    klag = jnp.sum(cp_r[:t, :, None, :, :] * bbar_r[None, :, :, None, :]
                   - cp_i[:t, :, None, :, :] * bbar_i[None, :, :, None, :], axis=-1)
    kc = (klag.reshape(t, S5_MBLK, S5_GPB, S5_GROUP, S5_GROUP).transpose(1, 0, 2, 3, 4)
          .reshape(S5_MBLK, t * LANE, S5_GROUP))
    rev_r, rev_i = powers(t - 1 - np.arange(t))
    bp_r = rev_r[:, :, None, :] * bbar_r[None] - rev_i[:, :, None, :] * bbar_i[None]
    bp_i = rev_r[:, :, None, :] * bbar_i[None] + rev_i[:, :, None, :] * bbar_r[None]
    bpc = (jnp.stack([bp_r, bp_i], axis=3).reshape(t, S5_MBLK, LANE, 2 * S5_STATE).transpose(1, 0, 2, 3)
           .reshape(S5_MBLK, t * LANE, 2 * S5_STATE))
    cpo = jnp.stack([cp_r[1:], -cp_i[1:]]).reshape(2, t, S5_MBLK, S5_GPB, S5_GROUP, S5_STATE)
    cpc = cpo.transpose(2, 0, 3, 5, 1, 4).reshape(S5_MBLK, 2 * S5_HALF, t * S5_GROUP)
    a_t = jnp.concatenate([pw_r[t].reshape(S5_MBLK, 1, S5_HALF), pw_i[t].reshape(S5_MBLK, 1, S5_HALF)], axis=2)
    w_in, w_c = _s5_wgen(kc, bpc, cpc)
    return w_in, w_c, a_t


def _s5_wgen_kernel(kc_ref, bp_ref, cp_ref, win_ref, wc_ref):
    t = S5_CHUNK
    wide = t * LANE
    gs, ps = int(math.log2(S5_GROUP)), int(math.log2(S5_STATE))

    def expand(x, copy_mask, group_mask):
        ex = jnp.where(copy_mask, 1.0, 0.0).astype(BF16)
        return jnp.where(group_mask, jnp.dot(x.astype(BF16), ex, preferred_element_type=F32), 0.0).astype(BF16)

    r, c = _iota((S5_GROUP, LANE), 0), _iota((S5_GROUP, LANE), 1)
    copy_k = r == (c & (S5_GROUP - 1))
    r, c = _iota((LANE, LANE), 0), _iota((LANE, LANE), 1)
    same_k = (r >> gs) == (c >> gs)
    zero = jnp.zeros((LANE, LANE), BF16)
    for lag in range(t):
        bd = expand(kc_ref[0, lag * LANE:(lag + 1) * LANE, :], copy_k, same_k)
        for s in range(t - lag):
            win_ref[0, s * LANE:(s + 1) * LANE, (s + lag) * LANE:(s + lag + 1) * LANE] = bd
    for s in range(1, t):
        for tt in range(s):
            win_ref[0, s * LANE:(s + 1) * LANE, tt * LANE:(tt + 1) * LANE] = zero
    r, c = _iota((LANE, 2 * S5_HALF), 0), _iota((LANE, 2 * S5_HALF), 1)
    copy_b = ((r >> ps) == (c >> (ps + 3))) & ((r & (S5_STATE - 1)) == (c & (S5_STATE - 1)))
    same_b = (r >> gs) == ((c >> ps) & (S5_GPB - 1))
    for s in range(t):
        win_ref[0, s * LANE:(s + 1) * LANE, wide:] = expand(bp_ref[0, s * LANE:(s + 1) * LANE, :],
                                                            copy_b, same_b)
    rows = 2 * S5_STATE * 2
    r, c = _iota((t * S5_GROUP, wide), 0), _iota((t * S5_GROUP, wide), 1)
    copy_c = ((r >> gs) == (c >> (gs + 3))) & ((r & (S5_GROUP - 1)) == (c & (S5_GROUP - 1)))
    for ch in range(2 * S5_HALF // rows):
        r, c = ch * rows + _iota((rows, wide), 0), _iota((rows, wide), 1)
        same_c = ((r >> ps) & (S5_GPB - 1)) == ((c >> gs) & (S5_GPB - 1))
        wc_ref[0, ch * rows:(ch + 1) * rows, :] = expand(cp_ref[0, ch * rows:(ch + 1) * rows, :],
                                                         copy_c, same_c)


def _s5_wgen(kc, bpc, cpc):
    t = S5_CHUNK
    wide = t * LANE
    spec = lambda a: pl.BlockSpec((1,) + a.shape[1:], lambda m: (m, 0, 0))
    return pl.pallas_call(
        _s5_wgen_kernel,
        grid=(S5_MBLK,),
        in_specs=[spec(kc), spec(bpc), spec(cpc)],
        out_specs=[pl.BlockSpec((1, wide, wide + 2 * S5_HALF), lambda m: (m, 0, 0)),
                   pl.BlockSpec((1, 2 * S5_HALF, wide), lambda m: (m, 0, 0))],
        out_shape=[jax.ShapeDtypeStruct((S5_MBLK, wide, wide + 2 * S5_HALF), BF16),
                   jax.ShapeDtypeStruct((S5_MBLK, 2 * S5_HALF, wide), BF16)],
        compiler_params=_cparams(("parallel",)),
        name="s5_wgen",
    )(kc, bpc, cpc)


def _s5_in_kernel(u_ref, w_ref, d_ref, y_ref, b_ref):
    tr = y_ref.shape[1]
    u = jnp.concatenate([u_ref[0, pl.ds(s, tr, stride=S5_CHUNK), :] for s in range(S5_CHUNK)], axis=1)
    res = jnp.dot(u.astype(BF16), w_ref[0], preferred_element_type=F32)
    wide = y_ref.shape[2]
    y_ref[0] = res[:, :wide] + d_ref[0] * u
    b_ref[0] = res[:, wide:]


def _s5_in(proj3, w_in, d_t):
    rows = proj3.shape[1] // S5_CHUNK
    wide = S5_CHUNK * LANE
    tr = min(rows, 512)
    return pl.pallas_call(
        _s5_in_kernel,
        grid=(S5_MBLK, rows // tr),
        in_specs=[pl.BlockSpec((1, tr * S5_CHUNK, LANE), lambda m, r: (CB_SU + m, r, 0)),
                  pl.BlockSpec((1, wide, wide + 2 * S5_HALF), lambda m, r: (m, 0, 0)),
                  pl.BlockSpec((1, 1, wide), lambda m, r: (m, 0, 0))],
        out_specs=[pl.BlockSpec((1, tr, wide), lambda m, r: (m, r, 0)),
                   pl.BlockSpec((1, tr, 2 * S5_HALF), lambda m, r: (m, r, 0))],
        out_shape=[jax.ShapeDtypeStruct((S5_MBLK, rows, wide), F32),
                   jax.ShapeDtypeStruct((S5_MBLK, rows, 2 * S5_HALF), F32)],
        compiler_params=_cparams(("parallel", "parallel")),
        name="s5_in",
    )(proj3, w_in, d_t)


def _s5_scan_kernel(b_ref, a_ref, o_ref, *, bsz, nk):
    ar = a_ref[0, :, :S5_HALF]
    ai = a_ref[0, :, S5_HALF:]

    def body(k, carry):
        out = []
        for b in range(bsz):
            sr, si = carry[2 * b], carry[2 * b + 1]
            row = b * nk + k
            o_ref[0, pl.ds(row, 1), :] = jnp.concatenate([sr, si], axis=1)
            x = b_ref[0, pl.ds(row, 1), :]
            out.append(ar * sr - ai * si + x[:, :S5_HALF])
            out.append(ar * si + ai * sr + x[:, S5_HALF:])
        return tuple(out)

    zero = jnp.zeros((1, S5_HALF), F32)
    lax.fori_loop(0, nk, body, tuple(zero for _ in range(2 * bsz)))


def _s5_scan(bst, a_t, bsz):
    _, rows, wide = bst.shape
    spec = pl.BlockSpec((1, rows, wide), lambda m: (m, 0, 0))
    return pl.pallas_call(
        functools.partial(_s5_scan_kernel, bsz=bsz, nk=rows // bsz),
        grid=(S5_MBLK,),
        in_specs=[spec, pl.BlockSpec((1, 1, wide), lambda m: (m, 0, 0))],
        out_specs=spec,
        out_shape=jax.ShapeDtypeStruct(bst.shape, F32),
        compiler_params=_cparams(("parallel",)),
        name="s5_scan",
    )(bst, a_t)


def _s5_out_kernel(y_ref, s_ref, w_ref, o_ref):
    tr = y_ref.shape[1]
    y = jax.nn.gelu(y_ref[0] + jnp.dot(s_ref[0].astype(BF16), w_ref[0], preferred_element_type=F32))
    for s in range(S5_CHUNK):
        o_ref[0, pl.ds(s, tr, stride=S5_CHUNK), :] = y[:, s * LANE:(s + 1) * LANE]


def _s5_out(y_intra, s_prev, w_c):
    _, rows, wide = y_intra.shape
    tr = min(rows, 512)
    return pl.pallas_call(
        _s5_out_kernel,
        grid=(S5_MBLK, rows // tr),
        in_specs=[pl.BlockSpec((1, tr, wide), lambda m, r: (m, r, 0)),
                  pl.BlockSpec((1, tr, 2 * S5_HALF), lambda m, r: (m, r, 0)),
                  pl.BlockSpec((1, 2 * S5_HALF, wide), lambda m, r: (m, 0, 0))],
        out_specs=pl.BlockSpec((1, tr * S5_CHUNK, LANE), lambda m, r: (m, r, 0)),
        out_shape=jax.ShapeDtypeStruct((S5_MBLK, rows * S5_CHUNK, LANE), F32),
        compiler_params=_cparams(("parallel", "parallel")),
        name="s5_out",
    )(y_intra, s_prev, w_c)


def _glu_kernel(y_ref, z_ref, w_ref, o_ref):
    nb = y_ref.shape[0]
    y = jnp.concatenate([y_ref[c] for c in range(nb)], axis=1)
    z = jnp.concatenate([z_ref[c] for c in range(nb)], axis=1)
    gate = jax.nn.sigmoid(jnp.dot(y.astype(BF16), w_ref[...], preferred_element_type=F32))
    o = (y * gate * (z * jax.nn.sigmoid(z))).astype(BF16)
    for c in range(nb):
        o_ref[c] = o[:, c * LANE:(c + 1) * LANE]


def _glu(y5, proj3, glu_w):
    nb, rows, _ = y5.shape
    tm = 512
    spec = lambda blk: pl.BlockSpec((nb, tm, LANE), lambda i: (blk, i, 0))
    return pl.pallas_call(
        _glu_kernel,
        grid=(rows // tm,),
        in_specs=[spec(0), spec(CB_SZ // nb), pl.BlockSpec((S5_WIDTH, S5_WIDTH), lambda i: (0, 0))],
        out_specs=spec(0),
        out_shape=jax.ShapeDtypeStruct((nb, rows, LANE), BF16),
        compiler_params=_cparams(("parallel",)),
        name="s5_glu",
    )(y5, proj3, glu_w.astype(BF16))


def _outproj_kernel(a_ref, b_ref, c_ref, w_ref, x_ref, o_ref):
    parts = ([a_ref[h] for h in range(a_ref.shape[0])] + [b_ref[h] for h in range(b_ref.shape[0])]
             + [c_ref[h] for h in range(c_ref.shape[0])])
    mixed = jnp.concatenate(parts, axis=1)
    o_ref[...] = x_ref[...] + jnp.dot(mixed, w_ref[...], preferred_element_type=F32)


def _outproj(m_moba, m_nsa, m_s5, w_out, x2):
    rows, d = x2.shape
    tm = 512
    lspec = lambda n: pl.BlockSpec((n, tm, LANE), lambda i: (0, i, 0))
    return pl.pallas_call(
        _outproj_kernel,
        grid=(rows // tm,),
        in_specs=[lspec(m_moba.shape[0]), lspec(m_nsa.shape[0]), lspec(m_s5.shape[0]),
                  pl.BlockSpec(w_out.shape, lambda i: (0, 0)),
                  pl.BlockSpec((tm, d), lambda i: (i, 0))],
        out_specs=pl.BlockSpec((tm, d), lambda i: (i, 0)),
        out_shape=jax.ShapeDtypeStruct((rows, d), F32),
        compiler_params=_cparams(("parallel",)),
        name="outproj",
    )(m_moba, m_nsa, m_s5, w_out.astype(BF16), x2)


def _w_in_source(blk):
    mw, nw, kvw, ng = MOBA_HEADS * HEAD_DIM, NSA_HEADS * HEAD_DIM, HEAD_DIM, 3 * NSA_HEADS
    o_kv = 4 * mw + nw
    o_ng = o_kv + 6 * kvw
    o_nz = o_ng + ng
    src = jnp.where(blk < CB_NZ, blk * LANE,
          jnp.where(blk < CB_NKS, o_nz + (blk - CB_NZ) * LANE,
          jnp.where(blk < CB_NKC, o_kv + 2 * kvw + (blk - CB_NKS) * LANE,
          jnp.where(blk < CB_NG, o_kv + (blk - CB_NKC) * LANE, o_ng))))
    real = jnp.where(blk < CB_NG, LANE, jnp.where(blk == CB_NG, ng, 0))
    return src, real


def _wperm_kernel(w_ref, o_ref):
    _, real = _w_in_source(pl.program_id(0))
    for layer in range(o_ref.shape[0]):
        x = w_ref[:, layer, :]
        x = jnp.where(_iota(x.shape, 0) < real, x, 0.0)
        o_ref[layer] = x.T.astype(BF16)


def _permute_w_in(w_in_all):
    wt = jnp.transpose(w_in_all, (2, 0, 1))
    _, nl, d = wt.shape
    return pl.pallas_call(
        _wperm_kernel,
        grid=(IN_BLOCKS,),
        in_specs=[pl.BlockSpec((pl.Element(LANE), pl.Element(nl), pl.Element(d)),
                               lambda c: (_w_in_source(c)[0], 0, 0))],
        out_specs=pl.BlockSpec((nl, d, LANE), lambda c: (0, 0, c)),
        out_shape=jax.ShapeDtypeStruct((nl, d, IN_BLOCKS * LANE), BF16),
        compiler_params=_cparams(("parallel",)),
        name="w_in_permute",
    )(wt)


def _layer(x2, bsz, seq, tabs, ctabs, overlap, w_in, layer, norm_w, w_out, moba_q_norm, moba_k_norm, nsa_q_norm,
           nsa_kc_norm, nsa_ks_norm, nsa_kw_norm, nsa_pe_k, nsa_pe_v, nsa_cmp_k_w1, nsa_cmp_k_w2,
           nsa_cmp_v_w1, nsa_cmp_v_w2, s5_a_re, s5_a_im, s5_b_re, s5_b_im, s5_c_re, s5_c_im, s5_d,
           s5_log_dt, s5_glu_w):
    proj3 = _inproj(x2, norm_w, w_in, layer)
    mk_n, mv_b, nk, kmean, vs_t = _kprep(proj3, seq, moba_k_norm, nsa_ks_norm, nsa_kw_norm, tabs)
    m_moba = _moba(proj3, bsz, seq, moba_q_norm, tabs, kmean.transpose(1, 0, 2), mk_n, mv_b)
    k_cmp, v_cmp = _compress(proj3, bsz, seq, nsa_pe_k, nsa_pe_v, nsa_cmp_k_w1, nsa_cmp_k_w2,
                             nsa_cmp_v_w1, nsa_cmp_v_w2, nsa_kc_norm, ctabs)
    m_nsa = _nsa(proj3, bsz, seq, nsa_q_norm, tabs, k_cmp, v_cmp, overlap, nk, vs_t)
    w_s5_in, w_s5_out, a_t = _s5_weights(s5_a_re, s5_a_im, s5_b_re, s5_b_im, s5_c_re, s5_c_im, s5_log_dt)
    d_t = jnp.tile(s5_d.astype(F32).reshape(S5_MBLK, 1, LANE), (1, 1, S5_CHUNK))
    y_intra, bst = _s5_in(proj3, w_s5_in, d_t)
    y5 = _s5_out(y_intra, _s5_scan(bst, a_t, bsz), w_s5_out)
    m_s5 = _glu(y5, proj3, s5_glu_w)
    return _outproj(m_moba, m_nsa, m_s5, w_out, x2)


def kernel(x, norm_w, w_in, w_out, moba_q_norm, moba_k_norm, nsa_q_norm, nsa_kc_norm, nsa_ks_norm, nsa_kw_norm, nsa_pe_k, nsa_pe_v, nsa_cmp_k_w1, nsa_cmp_k_w2, nsa_cmp_v_w1, nsa_cmp_v_w2, s5_a_re, s5_a_im, s5_b_re, s5_b_im, s5_c_re, s5_c_im, s5_d, s5_log_dt, s5_glu_w):
    bsz, seq, d = x.shape
    tabs = _rope_tables(jnp.arange(seq, dtype=F32))
    ncmp = seq // NSA_CMP_STRIDE
    ctabs = _rope_tables(jnp.arange(ncmp, dtype=F32) * NSA_CMP_STRIDE + (NSA_CMP_LEN - 1))
    nsel = seq // NSA_SEL_BLOCK
    ci = np.arange(ncmp)[:, None] * NSA_CMP_STRIDE
    sj = np.arange(nsel)[None, :] * NSA_SEL_BLOCK
    overlap = np.zeros((ncmp, LANE), np.float32)
    overlap[:, :nsel] = (ci < sj + NSA_SEL_BLOCK) & (ci + NSA_CMP_LEN > sj)
    overlap = jnp.asarray(overlap)
    params = (norm_w, w_out, moba_q_norm, moba_k_norm, nsa_q_norm, nsa_kc_norm, nsa_ks_norm,
              nsa_kw_norm, nsa_pe_k, nsa_pe_v, nsa_cmp_k_w1, nsa_cmp_k_w2, nsa_cmp_v_w1, nsa_cmp_v_w2,
              s5_a_re, s5_a_im, s5_b_re, s5_b_im, s5_c_re, s5_c_im, s5_d, s5_log_dt, s5_glu_w)
    x2 = x.reshape(bsz * seq, d)
    w_perm = _permute_w_in(w_in)
    for layer in range(norm_w.shape[0]):
        x2 = _layer(x2, bsz, seq, tabs, ctabs, overlap, w_perm, layer, *[p[layer] for p in params])
    return x2.reshape(bsz, seq, d)
```

```python
import functools
import math

import numpy as np
import jax
import jax.numpy as jnp
from jax import lax
from jax.experimental import pallas as pl
from jax.experimental.pallas import tpu as pltpu

F32 = jnp.float32
BF16 = jnp.bfloat16
HIGHEST = lax.Precision.HIGHEST

LANE = 128
SUBLANE = 8
HEAD_DIM = 128
ROPE_DIM = HEAD_DIM // 4
ROPE_HALF = ROPE_DIM // 2
ROPE_THETA = 500000.0
EPS = 1e-6
SCALE = HEAD_DIM ** -0.5
NEG = -1e30

MOBA_HEADS = 4
MOBA_BLOCK = 256
MOBA_TOPK = 3

NSA_HEADS = 4
NSA_CMP_LEN = 32
NSA_CMP_STRIDE = 16
NSA_SEL_BLOCK = 64
NSA_SEL_TOPN = 16
NSA_WINDOW = 512
NSA_TQ = 256
NSA_SEL_PER_TILE = NSA_TQ // NSA_SEL_BLOCK
assert NSA_WINDOW == 2 * NSA_TQ

S5_WIDTH = 1024
S5_GROUP = 16
S5_GROUPS = S5_WIDTH // S5_GROUP
S5_STATE = 64
S5_CHUNK = 16
S5_MBLK = S5_WIDTH // LANE
S5_GPB = LANE // S5_GROUP
S5_HALF = S5_GPB * S5_STATE

CB_MQ, CB_MK, CB_MV, CB_MZ, CB_NQ, CB_NZ, CB_SU, CB_SZ = 0, 4, 8, 12, 16, 20, 24, 32
CB_NKS, CB_NVS, CB_NKW, CB_NVW, CB_NKC, CB_NVC, CB_NG = 40, 41, 42, 43, 44, 45, 46
IN_BLOCKS = 48

VMEM_LIMIT = 56 * 1024 * 1024


def _cparams(sem):
    return pltpu.CompilerParams(dimension_semantics=sem, vmem_limit_bytes=VMEM_LIMIT)


def _iota(shape, dim):
    return lax.broadcasted_iota(jnp.int32, shape, dim)


def _head_norm(x, w):
    return x * lax.rsqrt(jnp.mean(x * x, axis=-1, keepdims=True) + EPS) * w


def _head_norm_mxu(x, w):
    sq = x * x
    hi = sq.astype(BF16)
    lo = (sq - hi.astype(F32)).astype(BF16)
    avg = jnp.full((LANE, LANE), 1.0 / HEAD_DIM, BF16)
    ms = jnp.dot(hi, avg, preferred_element_type=F32) + jnp.dot(lo, avg, preferred_element_type=F32)
    return x * lax.rsqrt(ms + EPS) * w


def _rope(x, c, s1, s2):
    return x * c + pltpu.roll(x, LANE - ROPE_HALF, 1) * s1 + pltpu.roll(x, ROPE_HALF, 1) * s2


def _dot_nt(a, b):
    return lax.dot_general(a, b, (((1,), (1,)), ((), ())), preferred_element_type=F32)


def _rope_tables(pos):
    inv = ROPE_THETA ** (-jnp.arange(0, ROPE_DIM, 2, dtype=F32) / ROPE_DIM)
    ang = pos.astype(F32)[:, None] * inv[None, :]
    cos, sin = jnp.cos(ang), jnp.sin(ang)
    n = pos.shape[0]
    c = jnp.concatenate([cos, cos, jnp.ones((n, LANE - ROPE_DIM), F32)], axis=1)
    s1 = jnp.concatenate([-sin, jnp.zeros((n, LANE - ROPE_HALF), F32)], axis=1)
    s2 = jnp.concatenate([jnp.zeros((n, ROPE_HALF), F32), sin, jnp.zeros((n, LANE - ROPE_DIM), F32)], axis=1)
    return c, s1, s2


def _inproj_kernel(x_ref, nw_ref, w_ref, o_ref, h_ref):
    @pl.when(pl.program_id(1) == 0)
    def _():
        x = x_ref[...]
        ms = jnp.mean(x * x, axis=-1, keepdims=True)
        h_ref[...] = (x * lax.rsqrt(ms + EPS) * nw_ref[...]).astype(BF16)

    res = jnp.dot(h_ref[...], w_ref[...], preferred_element_type=F32)
    for c in range(o_ref.shape[0]):
        o_ref[c] = res[:, c * LANE:(c + 1) * LANE]


def _inproj(x2, norm_w, w_perm, layer):
    rows, d = x2.shape
    tm, tn = 1024, 1024
    nb = tn // LANE
    return pl.pallas_call(
        _inproj_kernel,
        grid=(rows // tm, IN_BLOCKS // nb),
        in_specs=[pl.BlockSpec((tm, d), lambda i, j: (i, 0)),
                  pl.BlockSpec((1, d), lambda i, j: (0, 0)),
                  pl.BlockSpec((None, d, tn), lambda i, j: (layer, 0, j))],
        out_specs=pl.BlockSpec((nb, tm, LANE), lambda i, j: (j, i, 0)),
        out_shape=jax.ShapeDtypeStruct((IN_BLOCKS, rows, LANE), F32),
        scratch_shapes=[pltpu.VMEM((tm, d), BF16)],
        compiler_params=_cparams(("parallel", "arbitrary")),
        name="inproj",
    )(x2, norm_w.reshape(1, d), w_perm)


def _kprep_kernel(mk_ref, mv_ref, nk_ref, mkw_ref, ksw_ref, kww_ref, c_ref, s1_ref, s2_ref,
                  mk_o, mv_o, nk_o, km_o, vst_o):
    c, s1, s2 = c_ref[...], s1_ref[...], s2_ref[...]
    means = []
    for h in range(MOBA_HEADS):
        k = _rope(_head_norm_mxu(mk_ref[h], mkw_ref[...]), c, s1, s2)
        mk_o[h] = k.astype(BF16)
        mv_o[h] = mv_ref[h].T.astype(BF16)
        means.append(jnp.mean(k, axis=0, keepdims=True))
    km_o[0] = jnp.concatenate(means, axis=0)
    nk_o[0] = _rope(_head_norm_mxu(nk_ref[0], ksw_ref[...]), c, s1, s2).astype(BF16)
    nk_o[1] = nk_ref[1].astype(BF16)
    vst_o[...] = nk_ref[1].T.astype(BF16)
    nk_o[2] = _rope(_head_norm_mxu(nk_ref[2], kww_ref[...]), c, s1, s2).astype(BF16)
    nk_o[3] = nk_ref[3].astype(BF16)


def _kprep(proj3, seq, mk_w, ks_w, kw_w, tabs):
    rows = proj3.shape[1]
    t = MOBA_BLOCK
    nt = seq // t
    blk4 = lambda cb: pl.BlockSpec((4, t, LANE), lambda i: (cb // 4, i, 0))
    wspec = pl.BlockSpec((1, LANE), lambda i: (0, 0))
    tspec = pl.BlockSpec((t, LANE), lambda i: (i % nt, 0))
    out4 = pl.BlockSpec((4, t, LANE), lambda i: (0, i, 0))
    return pl.pallas_call(
        _kprep_kernel,
        grid=(rows // t,),
        in_specs=[blk4(CB_MK), blk4(CB_MV), blk4(CB_NKS), wspec, wspec, wspec, tspec, tspec, tspec],
        out_specs=[out4, pl.BlockSpec((4, None, LANE, t), lambda i: (0, i // nt, 0, i % nt)), out4,
                   pl.BlockSpec((1, MOBA_HEADS, LANE), lambda i: (i, 0, 0)),
                   pl.BlockSpec((None, LANE, t), lambda i: (i // nt, 0, i % nt))],
        out_shape=[jax.ShapeDtypeStruct((4, rows, LANE), BF16),
                   jax.ShapeDtypeStruct((4, rows // seq, LANE, seq), BF16),
                   jax.ShapeDtypeStruct((4, rows, LANE), BF16),
                   jax.ShapeDtypeStruct((rows // t, MOBA_HEADS, LANE), F32),
                   jax.ShapeDtypeStruct((rows // seq, LANE, seq), BF16)],
        compiler_params=_cparams(("parallel",)),
        name="kprep",
    )(proj3, proj3, proj3, mk_w.reshape(1, LANE), ks_w.reshape(1, LANE), kw_w.reshape(1, LANE), *tabs)


def _topk_mask_t(score_t, k):
    n, cols = score_t.shape
    assert n % SUBLANE == 0
    row_in = _iota((SUBLANE, cols), 0)
    groups = [score_t[g:g + SUBLANE] for g in range(0, n, SUBLANE)]
    ranks = [jnp.zeros((SUBLANE, cols), F32) for _ in groups]
    for mm in range(n):
        cm = score_t[mm:mm + 1, :]
        for gi, sc in enumerate(groups):
            lo = gi * SUBLANE
            if lo > mm:
                inc = jnp.where(cm >= sc, 1.0, 0.0)
            elif lo + SUBLANE - 1 <= mm:
                inc = jnp.where(cm > sc, 1.0, 0.0)
            else:
                inc = jnp.where(row_in > mm - lo, jnp.where(cm >= sc, 1.0, 0.0), jnp.where(cm > sc, 1.0, 0.0))
            ranks[gi] = ranks[gi] + inc
    return jnp.concatenate(ranks, axis=0) < k


def _moba_kernel(q_ref, z_ref, qw_ref, c_ref, s1_ref, s2_ref, km_ref, k_ref, vt_ref, o_ref,
                 qt_s, sel_s, s_s, p_s, acc_s):
    i = pl.program_id(1)
    t = MOBA_BLOCK
    nb = km_ref.shape[1]
    nh = q_ref.shape[0]
    npairs = (i + 1) // 2
    blk_t = _iota((nb, t), 0)
    past = blk_t < i
    causal = _iota((t, t), 0) <= _iota((t, t), 1)
    start = pl.multiple_of(i * t, t)

    def put_scores(slot, h, pair):
        src = jnp.minimum(pair, nb // 2 - 1)
        off = pl.multiple_of(src * (2 * t), 2 * t)
        sc = jnp.dot(k_ref[h, pl.ds(off, 2 * t), :], qt_s[h], preferred_element_type=F32)
        row = jnp.minimum(2 * pair, nb - 2)
        s_s[slot, h, :t, :] = sc[:t] + sel_s[h, pl.ds(row, 1), :]
        s_s[slot, h, t:, :] = sc[t:] + sel_s[h, pl.ds(row + 1, 1), :]

    def weighted_values(h, pair):
        off = pl.multiple_of(pair * (2 * t), 2 * t)
        return jnp.dot(vt_ref[h, :, pl.ds(off, 2 * t)], p_s[h], preferred_element_type=F32)

    state = []
    for h in range(nh):
        qf = _rope(_head_norm(q_ref[h], qw_ref[...]), c_ref[...], s1_ref[...], s2_ref[...])
        qt = (qf * SCALE).T.astype(BF16)
        qt_s[h] = qt
        gate_t = lax.dot_general(km_ref[h], qf, (((1,), (1,)), ((), ())), precision=HIGHEST,
                                 preferred_element_type=F32)
        top = _topk_mask_t(jnp.where(past, gate_t, -jnp.inf), MOBA_TOPK)
        sel_s[h] = jnp.where(top & past, 0.0, NEG)
        s = jnp.where(causal, jnp.dot(k_ref[h, pl.ds(start, t), :], qt, preferred_element_type=F32), NEG)
        m = jnp.max(s, axis=0, keepdims=True)
        p = jnp.exp(s - m)
        acc_s[h] = jnp.dot(vt_ref[h, :, pl.ds(start, t)], p.astype(BF16), preferred_element_type=F32)
        state += [m, jnp.sum(p, axis=0, keepdims=True)]
        put_scores(0, h, 0)
        p_s[h] = jnp.zeros((2 * t, t), BF16)

    def trip(pair, carry, src, dst):
        prev = jnp.maximum(pair - 1, 0)
        out = []
        for h in range(nh):
            m, l = carry[2 * h:2 * h + 2]
            put_scores(dst, h, pair + 1)
            acc = acc_s[h] + weighted_values(h, prev)
            s = s_s[src, h]
            m_new = jnp.maximum(m, jnp.max(s, axis=0, keepdims=True))
            p = jnp.exp(s - m_new)
            alpha = jnp.exp(m - m_new)
            l_new = alpha * l + jnp.sum(p, axis=0, keepdims=True)
            acc_s[h] = alpha * acc
            p_s[h] = p.astype(BF16)
            out += [m_new, l_new]
        return tuple(out)

    def body(q, carry):
        return trip(2 * q + 1, trip(2 * q, carry, 0, 1), 1, 0)

    ntrips = 2 * ((npairs + 1) // 2)
    state = lax.fori_loop(0, ntrips // 2, body, tuple(state))
    last = jnp.maximum(ntrips - 1, 0)
    for h in range(nh):
        l = state[2 * h + 1]
        acc = acc_s[h] + weighted_values(h, last)
        z = z_ref[h]
        o_ref[h] = ((acc * (1.0 / l)).T * (z * jax.nn.sigmoid(z))).astype(BF16)


def _moba(proj3, bsz, seq, q_w, tabs, km_t, mk_n, mv_t):
    rows = proj3.shape[1]
    t = MOBA_BLOCK
    nb = seq // t
    nh = MOBA_HEADS
    assert nb % 4 == 0
    qspec = lambda cb: pl.BlockSpec((nh, t, LANE), lambda b, i: (cb // nh, b * nb + i, 0))
    tspec = pl.BlockSpec((t, LANE), lambda b, i: (i, 0))
    return pl.pallas_call(
        _moba_kernel,
        grid=(bsz, nb),
        in_specs=[qspec(CB_MQ), qspec(CB_MZ), pl.BlockSpec((1, LANE), lambda b, i: (0, 0)),
                  tspec, tspec, tspec,
                  pl.BlockSpec((nh, nb, LANE), lambda b, i: (0, b, 0)),
                  pl.BlockSpec((nh, seq, LANE), lambda b, i: (0, b, 0)),
                  pl.BlockSpec((nh, None, LANE, seq), lambda b, i: (0, b, 0, 0))],
        out_specs=pl.BlockSpec((nh, t, LANE), lambda b, i: (0, b * nb + i, 0)),
        out_shape=jax.ShapeDtypeStruct((nh, rows, LANE), BF16),
        scratch_shapes=[pltpu.VMEM((nh, LANE, t), BF16),
                        pltpu.VMEM((nh, nb, t), F32),
                        pltpu.VMEM((2, nh, 2 * t, t), F32),
                        pltpu.VMEM((nh, 2 * t, t), BF16),
                        pltpu.VMEM((nh, LANE, t), F32)],
        compiler_params=_cparams(("parallel", "parallel")),
        name="moba",
    )(proj3, proj3, q_w.reshape(1, LANE), *tabs, km_t, mk_n, mv_t)


def _cmp_kernel(hk_ref, hv_ref, pek_ref, pev_ref, w1k_ref, w2k_ref, w1v_ref, w2v_ref, nw_ref,
                c_ref, s1_ref, s2_ref, ko_ref, vo_ref):
    st = NSA_CMP_STRIDE
    nh = ko_ref.shape[1]

    def compress(x_ref, pe_ref, w1_ref, w2_ref):
        a = jnp.zeros((nh, LANE), F32)
        b = jnp.zeros((nh, LANE), F32)
        for l in range(st):
            x = x_ref[0, pl.ds(l, nh, stride=st), :]
            a = a + jnp.dot((x + pe_ref[l:l + 1, :]).astype(BF16), w1_ref[l * LANE:(l + 1) * LANE, :],
                            preferred_element_type=F32)
            b = b + jnp.dot((x + pe_ref[st + l:st + l + 1, :]).astype(BF16),
                            w1_ref[(st + l) * LANE:(st + l + 1) * LANE, :], preferred_element_type=F32)
        pre = a + pltpu.roll(b, nh - 1, 0)
        return jnp.dot(jax.nn.gelu(pre).astype(BF16), w2_ref[...], preferred_element_type=F32)

    kc = compress(hk_ref, pek_ref, w1k_ref, w2k_ref)
    vc = compress(hv_ref, pev_ref, w1v_ref, w2v_ref)
    ko_ref[0] = _rope(_head_norm(kc, nw_ref[...]), c_ref[...], s1_ref[...], s2_ref[...]).astype(BF16)
    vo_ref[0] = vc.astype(BF16)


def _compress(proj3, bsz, seq, pe_k, pe_v, w1k, w2k, w1v, w2v, kc_w, ctabs):
    nh = seq // NSA_CMP_STRIDE
    wide = NSA_CMP_LEN * HEAD_DIM
    hspec = lambda cb: pl.BlockSpec((1, seq, LANE), lambda b: (cb, b, 0))
    full = lambda shape: pl.BlockSpec(shape, lambda b: tuple(0 for _ in shape))
    ospec = pl.BlockSpec((1, nh, LANE), lambda b: (b, 0, 0))
    return pl.pallas_call(
        _cmp_kernel,
        grid=(bsz,),
        in_specs=[hspec(CB_NKC), hspec(CB_NVC), full((NSA_CMP_LEN, LANE)), full((NSA_CMP_LEN, LANE)),
                  full((wide, LANE)), full((LANE, LANE)), full((wide, LANE)), full((LANE, LANE)),
                  full((1, LANE)), full((nh, LANE)), full((nh, LANE)), full((nh, LANE))],
        out_specs=[ospec, ospec],
        out_shape=[jax.ShapeDtypeStruct((bsz, nh, LANE), BF16)] * 2,
        compiler_params=_cparams(("parallel",)),
        name="nsa_compress",
    )(proj3, proj3, pe_k, pe_v, w1k.astype(BF16), w2k.astype(BF16),
      w1v.astype(BF16), w2v.astype(BF16), kc_w.reshape(1, LANE), *ctabs)


def _nsa_kernel(q_ref, z_ref, g_ref, qw_ref, c_ref, s1_ref, s2_ref, kc_ref, vc_ref, ov_ref,
                ks_ref, vst_ref, kw_ref, vw_ref, o_ref, qb_s, qt_s, acc_s, ocmp_s, osel_s, s_s, p_s):
    i = pl.program_id(1)
    t = NSA_TQ
    ncmp = kc_ref.shape[1]
    nsel = ks_ref.shape[0] // NSA_SEL_BLOCK
    sel_shift = int(math.log2(NSA_SEL_BLOCK))
    trow = i * t + _iota((t, 1), 0)
    row = _iota((t, t), 0)
    col = _iota((t, t), 1)
    causal = col <= row

    c, s1, s2 = c_ref[...], s1_ref[...], s2_ref[...]
    for h in range(NSA_HEADS):
        qb_s[h] = (_rope(_head_norm_mxu(q_ref[h], qw_ref[...]), c, s1, s2) * SCALE).astype(BF16)

    valid = (_iota((t, ncmp), 1) * NSA_CMP_STRIDE + (NSA_CMP_LEN - 1)) <= trow
    kc, vc = kc_ref[0], vc_ref[0]
    psum = jnp.zeros((t, ncmp), F32)
    for h in range(NSA_HEADS):
        s = jnp.where(valid, _dot_nt(qb_s[h], kc), NEG)
        mx = jnp.max(s, axis=1, keepdims=True)
        e = jnp.exp(s - mx)
        p = e * jnp.where(mx > 0.5 * NEG, 1.0 / jnp.sum(e, axis=1, keepdims=True), 0.0)
        ocmp_s[h] = jnp.dot(p.astype(BF16), vc, preferred_element_type=F32)
        psum = psum + p

    imp_t = jnp.dot(psum, ov_ref[...], precision=HIGHEST, preferred_element_type=F32).T[:nsel]
    jj = _iota((nsel, t), 0)
    cur = (i * t + _iota((1, t), 1)) >> sel_shift
    score = jnp.where(jj <= cur, imp_t, -jnp.inf)
    score = jnp.where((jj == 0) | (jj == cur) | (jj == cur - 1), jnp.inf, score)
    top = _topk_mask_t(score, NSA_SEL_TOPN)
    selbias_t = jnp.concatenate([jnp.where(top, 0.0, NEG), jnp.zeros((LANE - nsel - SUBLANE, t), F32),
                                 jnp.full((SUBLANE, t), NEG, F32)], axis=0).astype(BF16)
    ntiles = ks_ref.shape[0] // t
    key_blk = _iota((t, LANE), 0) >> sel_shift

    def tile_bias(kt):
        blk = jnp.where(kt < i, kt * NSA_SEL_PER_TILE + key_blk, LANE - 1)
        onehot = jnp.where(_iota((t, LANE), 1) == blk, 1.0, 0.0).astype(BF16)
        return jnp.dot(onehot, selbias_t, preferred_element_type=F32)

    def score_inputs(kt):
        src = jnp.minimum(kt, ntiles - 1)
        return ks_ref[pl.ds(pl.multiple_of(src * t, t), t), :], tile_bias(kt)

    def put_scores(slot, h, k, bias):
        s_s[slot, h] = jnp.dot(k, qt_s[h], preferred_element_type=F32) + bias

    own = pl.multiple_of(i * t, t)
    own_blk = i * NSA_SEL_PER_TILE + key_blk
    own_bias = jnp.dot(jnp.where(_iota((t, LANE), 1) == own_blk, 1.0, 0.0).astype(BF16), selbias_t,
                       preferred_element_type=F32)
    k_own, vt_own = ks_ref[pl.ds(own, t), :], vst_ref[:, pl.ds(own, t)]
    state = []
    for h in range(NSA_HEADS):
        qt_s[h] = qb_s[h].T
        s = jnp.where(row <= col, jnp.dot(k_own, qt_s[h], preferred_element_type=F32) + own_bias, NEG)
        m = jnp.max(s, axis=0, keepdims=True)
        p = jnp.exp(s - m)
        acc_s[h] = jnp.dot(vt_own, p.astype(BF16), preferred_element_type=F32)
        state += [m, jnp.sum(p, axis=0, keepdims=True)]
        p_s[h] = jnp.zeros((t, t), BF16)
    k_first, bias_first = score_inputs(0)
    for h in range(NSA_HEADS):
        put_scores(0, h, k_first, bias_first)

    def trip(kt, carry, src, dst):
        k_next, bias_next = score_inputs(kt + 1)
        vt_prev = vst_ref[:, pl.ds(pl.multiple_of(jnp.clip(kt - 1, 0, ntiles - 1) * t, t), t)]
        out = []
        for h in range(NSA_HEADS):
            m, l = carry[2 * h:2 * h + 2]
            put_scores(dst, h, k_next, bias_next)
            acc = acc_s[h] + jnp.dot(vt_prev, p_s[h], preferred_element_type=F32)
            s = s_s[src, h]
            m_new = jnp.maximum(m, jnp.max(s, axis=0, keepdims=True))
            p = jnp.exp(s - m_new)
            alpha = jnp.exp(m - m_new)
            acc_s[h] = alpha * acc
            p_s[h] = p.astype(BF16)
            out += [m_new, alpha * l + jnp.sum(p, axis=0, keepdims=True)]
        return tuple(out)

    def sel_body(q, carry):
        return trip(2 * q + 1, trip(2 * q, carry, 0, 1), 1, 0)

    ntrips = 2 * ((i + 1) // 2)
    state = lax.fori_loop(0, ntrips // 2, sel_body, tuple(state))
    vt_last = vst_ref[:, pl.ds(pl.multiple_of(jnp.clip(ntrips - 1, 0, ntiles - 1) * t, t), t)]
    for h in range(NSA_HEADS):
        acc = acc_s[h] + jnp.dot(vt_last, p_s[h], preferred_element_type=F32)
        osel_s[h] = (acc * (1.0 / state[2 * h + 1])).T

    far = pl.multiple_of(jnp.maximum(i - 2, 0) * t, t)
    mid = pl.multiple_of(jnp.maximum(i - 1, 0) * t, t)
    k3 = jnp.concatenate([kw_ref[pl.ds(far, t), :], kw_ref[pl.ds(mid, t), :], kw_ref[pl.ds(own, t), :]], axis=0)
    v3 = jnp.concatenate([vw_ref[pl.ds(far, t), :], vw_ref[pl.ds(mid, t), :], vw_ref[pl.ds(own, t), :]], axis=0)
    allowed = jnp.concatenate([(col > row) & (i >= 2), jnp.full((t, t), True) & (i >= 1), causal], axis=1)
    g = jax.nn.sigmoid(g_ref[0])
    for h in range(NSA_HEADS):
        s = jnp.where(allowed, _dot_nt(qb_s[h], k3), NEG)
        p = jnp.exp(s - jnp.max(s, axis=1, keepdims=True))
        o_win = jnp.dot(p.astype(BF16), v3, preferred_element_type=F32) * (1.0 / jnp.sum(p, axis=1, keepdims=True))
        o = (g[:, 3 * h:3 * h + 1] * ocmp_s[h] + g[:, 3 * h + 1:3 * h + 2] * osel_s[h]
             + g[:, 3 * h + 2:3 * h + 3] * o_win)
        z = z_ref[h]
        o_ref[h] = (o * (z * jax.nn.sigmoid(z))).astype(BF16)


def _nsa(proj3, bsz, seq, q_w, tabs, k_cmp, v_cmp, overlap, nk, vs_t):
    rows = proj3.shape[1]
    t = NSA_TQ
    nt = seq // t
    ncmp = seq // NSA_CMP_STRIDE
    nsel = seq // NSA_SEL_BLOCK
    blk4 = lambda cb: pl.BlockSpec((4, t, LANE), lambda b, i: (cb // 4, b * nt + i, 0))
    tspec = pl.BlockSpec((t, LANE), lambda b, i: (i, 0))
    cspec = pl.BlockSpec((1, ncmp, LANE), lambda b, i: (b, 0, 0))
    kvspec = lambda which: pl.BlockSpec((None, seq, LANE), lambda b, i: (which, b, 0))
    return pl.pallas_call(
        _nsa_kernel,
        grid=(bsz, nt),
        in_specs=[blk4(CB_NQ), blk4(CB_NZ),
                  pl.BlockSpec((1, t, LANE), lambda b, i: (CB_NG, b * nt + i, 0)),
                  pl.BlockSpec((1, LANE), lambda b, i: (0, 0)), tspec, tspec, tspec, cspec, cspec,
                  pl.BlockSpec((ncmp, LANE), lambda b, i: (0, 0)),
                  kvspec(0), pl.BlockSpec((None, LANE, seq), lambda b, i: (b, 0, 0)), kvspec(2), kvspec(3)],
        out_specs=pl.BlockSpec((4, t, LANE), lambda b, i: (0, b * nt + i, 0)),
        out_shape=jax.ShapeDtypeStruct((NSA_HEADS, rows, LANE), BF16),
        scratch_shapes=[pltpu.VMEM((NSA_HEADS, t, LANE), BF16),
                        pltpu.VMEM((NSA_HEADS, LANE, t), BF16),
                        pltpu.VMEM((NSA_HEADS, LANE, t), F32),
                        pltpu.VMEM((NSA_HEADS, t, LANE), F32),
                        pltpu.VMEM((NSA_HEADS, t, LANE), F32),
                        pltpu.VMEM((2, NSA_HEADS, t, t), F32),
                        pltpu.VMEM((NSA_HEADS, t, t), BF16)],
        compiler_params=_cparams(("parallel", "parallel")),
        name="nsa",
    )(proj3, proj3, proj3, q_w.reshape(1, LANE), *tabs, k_cmp, v_cmp, overlap, nk, vs_t, nk, nk)


def _s5_factors(a_re, a_im, b_re, b_im, c_re, c_im, log_dt):
    t = S5_CHUNK
    dt = jnp.exp(log_dt.astype(F32))[:, None]
    ar, ai = a_re.astype(F32), a_im.astype(F32)
    ang = dt * ai
    mag = jnp.exp(dt * ar)
    abar_r, abar_i = mag * jnp.cos(ang), mag * jnp.sin(ang)
    nr, ni = abar_r - 1.0, abar_i
    den = ar * ar + ai * ai
    fr = (nr * ar + ni * ai) / den
    fi = (ni * ar - nr * ai) / den
    bt_r, bt_i = b_re.astype(F32).transpose(0, 2, 1), b_im.astype(F32).transpose(0, 2, 1)
    bbar_r = fr[:, None, :] * bt_r - fi[:, None, :] * bt_i
    bbar_i = fr[:, None, :] * bt_i + fi[:, None, :] * bt_r

    def powers(tau):
        tau = jnp.asarray(tau, F32)[:, None, None]
        pmag = jnp.exp(tau * (dt * ar)[None])
        return pmag * jnp.cos(tau * ang[None]), pmag * jnp.sin(tau * ang[None])

    pw_r, pw_i = powers(np.arange(t + 1))
    cr, ci = c_re.astype(F32), c_im.astype(F32)
    cp_r = cr[None] * pw_r[:, :, None, :] - ci[None] * pw_i[:, :, None, :]
    cp_i = cr[None] * pw_i[:, :, None, :] + ci[None] * pw_r[:, :, None, :]
    klag = jnp.sum(cp_r[:t, :, None, :, :] * bbar_r[None, :, :, None, :]
                   - cp_i[:t, :, None, :, :] * bbar_i[None, :, :, None, :], axis=-1)
    kc = (klag.reshape(t, S5_MBLK, S5_GPB, S5_GROUP, S5_GROUP).transpose(1, 0, 2, 3, 4)
          .reshape(S5_MBLK, t * LANE, S5_GROUP))
    rev_r, rev_i = powers(t - 1 - np.arange(t))
    bp_r = rev_r[:, :, None, :] * bbar_r[None] - rev_i[:, :, None, :] * bbar_i[None]
    bp_i = rev_r[:, :, None, :] * bbar_i[None] + rev_i[:, :, None, :] * bbar_r[None]
    bpc = (jnp.stack([bp_r, bp_i], axis=3).reshape(t, S5_MBLK, LANE, 2 * S5_STATE).transpose(1, 0, 2, 3)
           .reshape(S5_MBLK, t * LANE, 2 * S5_STATE))
    cpo = jnp.stack([cp_r[1:], -cp_i[1:]]).reshape(2, t, S5_MBLK, S5_GPB, S5_GROUP, S5_STATE)
    cpc = cpo.transpose(2, 0, 3, 5, 1, 4).reshape(S5_MBLK, 2 * S5_HALF, t * S5_GROUP)
    a_t = jnp.concatenate([pw_r[t].reshape(S5_MBLK, 1, S5_HALF), pw_i[t].reshape(S5_MBLK, 1, S5_HALF)], axis=2)
    return kc, bpc, cpc, a_t


def _s5_wgen_kernel(kc_ref, bp_ref, cp_ref, win_ref, wc_ref):
    t = S5_CHUNK
    wide = t * LANE
    gs, ps = int(math.log2(S5_GROUP)), int(math.log2(S5_STATE))

    def expand(x, copy_mask, group_mask):
        ex = jnp.where(copy_mask, 1.0, 0.0).astype(BF16)
        return jnp.where(group_mask, jnp.dot(x.astype(BF16), ex, preferred_element_type=F32), 0.0).astype(BF16)

    r, c = _iota((S5_GROUP, LANE), 0), _iota((S5_GROUP, LANE), 1)
    copy_k = r == (c & (S5_GROUP - 1))
    r, c = _iota((LANE, LANE), 0), _iota((LANE, LANE), 1)
    same_k = (r >> gs) == (c >> gs)
    zero = jnp.zeros((LANE, LANE), BF16)
    for lag in range(t):
        bd = expand(kc_ref[0, lag * LANE:(lag + 1) * LANE, :], copy_k, same_k)
        for s in range(t - lag):
            win_ref[0, s * LANE:(s + 1) * LANE, (s + lag) * LANE:(s + lag + 1) * LANE] = bd
    for s in range(1, t):
        for tt in range(s):
            win_ref[0, s * LANE:(s + 1) * LANE, tt * LANE:(tt + 1) * LANE] = zero
    r, c = _iota((LANE, 2 * S5_HALF), 0), _iota((LANE, 2 * S5_HALF), 1)
    copy_b = ((r >> ps) == (c >> (ps + 3))) & ((r & (S5_STATE - 1)) == (c & (S5_STATE - 1)))
    same_b = (r >> gs) == ((c >> ps) & (S5_GPB - 1))
    for s in range(t):
        win_ref[0, s * LANE:(s + 1) * LANE, wide:] = expand(bp_ref[0, s * LANE:(s + 1) * LANE, :],
                                                            copy_b, same_b)
    rows = 2 * S5_STATE * 2
    r, c = _iota((t * S5_GROUP, wide), 0), _iota((t * S5_GROUP, wide), 1)
    copy_c = ((r >> gs) == (c >> (gs + 3))) & ((r & (S5_GROUP - 1)) == (c & (S5_GROUP - 1)))
    for ch in range(2 * S5_HALF // rows):
        r, c = ch * rows + _iota((rows, wide), 0), _iota((rows, wide), 1)
        same_c = ((r >> ps) & (S5_GPB - 1)) == ((c >> gs) & (S5_GPB - 1))
        wc_ref[0, ch * rows:(ch + 1) * rows, :] = expand(cp_ref[0, ch * rows:(ch + 1) * rows, :],
                                                         copy_c, same_c)


def _s5_wgen(kc, bpc, cpc):
    t = S5_CHUNK
    wide = t * LANE
    spec = lambda a: pl.BlockSpec((1,) + a.shape[1:], lambda m: (m, 0, 0))
    return pl.pallas_call(
        _s5_wgen_kernel,
        grid=(kc.shape[0],),
        in_specs=[spec(kc), spec(bpc), spec(cpc)],
        out_specs=[pl.BlockSpec((1, wide, wide + 2 * S5_HALF), lambda m: (m, 0, 0)),
                   pl.BlockSpec((1, 2 * S5_HALF, wide), lambda m: (m, 0, 0))],
        out_shape=[jax.ShapeDtypeStruct((kc.shape[0], wide, wide + 2 * S5_HALF), BF16),
                   jax.ShapeDtypeStruct((kc.shape[0], 2 * S5_HALF, wide), BF16)],
        compiler_params=_cparams(("parallel",)),
        name="s5_wgen",
    )(kc, bpc, cpc)


def _s5_in_kernel(u_ref, w_ref, d_ref, y_ref, b_ref):
    tr = y_ref.shape[1]
    u = jnp.concatenate([u_ref[0, pl.ds(s, tr, stride=S5_CHUNK), :] for s in range(S5_CHUNK)], axis=1)
    res = jnp.dot(u.astype(BF16), w_ref[0], preferred_element_type=F32)
    wide = y_ref.shape[2]
    y_ref[0] = res[:, :wide] + d_ref[0] * u
    b_ref[0] = res[:, wide:]


def _s5_in(proj3, w_in, d_t, layer):
    rows = proj3.shape[1] // S5_CHUNK
    wide = S5_CHUNK * LANE
    tr = min(rows, 512)
    return pl.pallas_call(
        _s5_in_kernel,
        grid=(S5_MBLK, rows // tr),
        in_specs=[pl.BlockSpec((1, tr * S5_CHUNK, LANE), lambda m, r: (CB_SU + m, r, 0)),
                  pl.BlockSpec((1, wide, wide + 2 * S5_HALF), lambda m, r: (layer * S5_MBLK + m, 0, 0)),
                  pl.BlockSpec((1, 1, wide), lambda m, r: (m, 0, 0))],
        out_specs=[pl.BlockSpec((1, tr, wide), lambda m, r: (m, r, 0)),
                   pl.BlockSpec((1, tr, 2 * S5_HALF), lambda m, r: (m, r, 0))],
        out_shape=[jax.ShapeDtypeStruct((S5_MBLK, rows, wide), F32),
                   jax.ShapeDtypeStruct((S5_MBLK, rows, 2 * S5_HALF), F32)],
        compiler_params=_cparams(("parallel", "parallel")),
        name="s5_in",
    )(proj3, w_in, d_t)


def _s5_scan_kernel(b_ref, a_ref, o_ref, *, bsz, nk):
    ar = a_ref[0, :, :S5_HALF]
    ai = a_ref[0, :, S5_HALF:]

    def body(k, carry):
        out = []
        for b in range(bsz):
            sr, si = carry[2 * b], carry[2 * b + 1]
            row = b * nk + k
            o_ref[0, pl.ds(row, 1), :] = jnp.concatenate([sr, si], axis=1)
            x = b_ref[0, pl.ds(row, 1), :]
            out.append(ar * sr - ai * si + x[:, :S5_HALF])
            out.append(ar * si + ai * sr + x[:, S5_HALF:])
        return tuple(out)

    zero = jnp.zeros((1, S5_HALF), F32)
    lax.fori_loop(0, nk, body, tuple(zero for _ in range(2 * bsz)))


def _s5_scan(bst, a_t, bsz):
    _, rows, wide = bst.shape
    spec = pl.BlockSpec((1, rows, wide), lambda m: (m, 0, 0))
    return pl.pallas_call(
        functools.partial(_s5_scan_kernel, bsz=bsz, nk=rows // bsz),
        grid=(S5_MBLK,),
        in_specs=[spec, pl.BlockSpec((1, 1, wide), lambda m: (m, 0, 0))],
        out_specs=spec,
        out_shape=jax.ShapeDtypeStruct(bst.shape, F32),
        compiler_params=_cparams(("parallel",)),
        name="s5_scan",
    )(bst, a_t)


def _s5_out_kernel(y_ref, s_ref, w_ref, o_ref):
    tr = y_ref.shape[1]
    y = jax.nn.gelu(y_ref[0] + jnp.dot(s_ref[0].astype(BF16), w_ref[0], preferred_element_type=F32))
    for s in range(S5_CHUNK):
        o_ref[0, pl.ds(s, tr, stride=S5_CHUNK), :] = y[:, s * LANE:(s + 1) * LANE]


def _s5_out(y_intra, s_prev, w_c, layer):
    _, rows, wide = y_intra.shape
    tr = min(rows, 512)
    return pl.pallas_call(
        _s5_out_kernel,
        grid=(S5_MBLK, rows // tr),
        in_specs=[pl.BlockSpec((1, tr, wide), lambda m, r: (m, r, 0)),
                  pl.BlockSpec((1, tr, 2 * S5_HALF), lambda m, r: (m, r, 0)),
                  pl.BlockSpec((1, 2 * S5_HALF, wide), lambda m, r: (layer * S5_MBLK + m, 0, 0))],
        out_specs=pl.BlockSpec((1, tr * S5_CHUNK, LANE), lambda m, r: (m, r, 0)),
        out_shape=jax.ShapeDtypeStruct((S5_MBLK, rows * S5_CHUNK, LANE), F32),
        compiler_params=_cparams(("parallel", "parallel")),
        name="s5_out",
    )(y_intra, s_prev, w_c)


def _glu_kernel(y_ref, z_ref, w_ref, o_ref):
    nb = y_ref.shape[0]
    y = jnp.concatenate([y_ref[c] for c in range(nb)], axis=1)
    z = jnp.concatenate([z_ref[c] for c in range(nb)], axis=1)
    gate = jax.nn.sigmoid(jnp.dot(y.astype(BF16), w_ref[...], preferred_element_type=F32))
    o = (y * gate * (z * jax.nn.sigmoid(z))).astype(BF16)
    for c in range(nb):
        o_ref[c] = o[:, c * LANE:(c + 1) * LANE]


def _glu(y5, proj3, glu_w):
    nb, rows, _ = y5.shape
    tm = 512
    spec = lambda blk: pl.BlockSpec((nb, tm, LANE), lambda i: (blk, i, 0))
    return pl.pallas_call(
        _glu_kernel,
        grid=(rows // tm,),
        in_specs=[spec(0), spec(CB_SZ // nb), pl.BlockSpec((S5_WIDTH, S5_WIDTH), lambda i: (0, 0))],
        out_specs=spec(0),
        out_shape=jax.ShapeDtypeStruct((nb, rows, LANE), BF16),
        compiler_params=_cparams(("parallel",)),
        name="s5_glu",
    )(y5, proj3, glu_w.astype(BF16))


def _outproj_kernel(a_ref, b_ref, c_ref, w_ref, x_ref, o_ref):
    parts = ([a_ref[h] for h in range(a_ref.shape[0])] + [b_ref[h] for h in range(b_ref.shape[0])]
             + [c_ref[h] for h in range(c_ref.shape[0])])
    mixed = jnp.concatenate(parts, axis=1)
    o_ref[...] = x_ref[...] + jnp.dot(mixed, w_ref[...], preferred_element_type=F32)


def _outproj(m_moba, m_nsa, m_s5, w_out, x2):
    rows, d = x2.shape
    tm = 512
    lspec = lambda n: pl.BlockSpec((n, tm, LANE), lambda i: (0, i, 0))
    return pl.pallas_call(
        _outproj_kernel,
        grid=(rows // tm,),
        in_specs=[lspec(m_moba.shape[0]), lspec(m_nsa.shape[0]), lspec(m_s5.shape[0]),
                  pl.BlockSpec(w_out.shape, lambda i: (0, 0)),
                  pl.BlockSpec((tm, d), lambda i: (i, 0))],
        out_specs=pl.BlockSpec((tm, d), lambda i: (i, 0)),
        out_shape=jax.ShapeDtypeStruct((rows, d), F32),
        compiler_params=_cparams(("parallel",)),
        name="outproj",
    )(m_moba, m_nsa, m_s5, w_out.astype(BF16), x2)


def _w_in_source(blk):
    mw, nw, kvw, ng = MOBA_HEADS * HEAD_DIM, NSA_HEADS * HEAD_DIM, HEAD_DIM, 3 * NSA_HEADS
    o_kv = 4 * mw + nw
    o_ng = o_kv + 6 * kvw
    o_nz = o_ng + ng
    src = jnp.where(blk < CB_NZ, blk * LANE,
          jnp.where(blk < CB_NKS, o_nz + (blk - CB_NZ) * LANE,
          jnp.where(blk < CB_NKC, o_kv + 2 * kvw + (blk - CB_NKS) * LANE,
          jnp.where(blk < CB_NG, o_kv + (blk - CB_NKC) * LANE, o_ng))))
    real = jnp.where(blk < CB_NG, LANE, jnp.where(blk == CB_NG, ng, 0))
    return src, real


def _wperm_kernel(w_ref, o_ref):
    _, real = _w_in_source(pl.program_id(0))
    for layer in range(o_ref.shape[0]):
        x = w_ref[:, layer, :]
        x = jnp.where(_iota(x.shape, 0) < real, x, 0.0)
        o_ref[layer] = x.T.astype(BF16)


def _permute_w_in(w_in_all):
    wt = jnp.transpose(w_in_all, (2, 0, 1))
    _, nl, d = wt.shape
    return pl.pallas_call(
        _wperm_kernel,
        grid=(IN_BLOCKS,),
        in_specs=[pl.BlockSpec((pl.Element(LANE), pl.Element(nl), pl.Element(d)),
                               lambda c: (_w_in_source(c)[0], 0, 0))],
        out_specs=pl.BlockSpec((nl, d, LANE), lambda c: (0, 0, c)),
        out_shape=jax.ShapeDtypeStruct((nl, d, IN_BLOCKS * LANE), BF16),
        compiler_params=_cparams(("parallel",)),
        name="w_in_permute",
    )(wt)


def _layer(x2, bsz, seq, tabs, ctabs, overlap, w_in, s5_w, layer, norm_w, w_out, moba_q_norm, moba_k_norm,
           nsa_q_norm, nsa_kc_norm, nsa_ks_norm, nsa_kw_norm, nsa_pe_k, nsa_pe_v, nsa_cmp_k_w1, nsa_cmp_k_w2,
           nsa_cmp_v_w1, nsa_cmp_v_w2, s5_d, s5_glu_w):
    proj3 = _inproj(x2, norm_w, w_in, layer)
    mk_n, mv_b, nk, kmean, vs_t = _kprep(proj3, seq, moba_k_norm, nsa_ks_norm, nsa_kw_norm, tabs)
    m_moba = _moba(proj3, bsz, seq, moba_q_norm, tabs, kmean.transpose(1, 0, 2), mk_n, mv_b)
    k_cmp, v_cmp = _compress(proj3, bsz, seq, nsa_pe_k, nsa_pe_v, nsa_cmp_k_w1, nsa_cmp_k_w2,
                             nsa_cmp_v_w1, nsa_cmp_v_w2, nsa_kc_norm, ctabs)
    m_nsa = _nsa(proj3, bsz, seq, nsa_q_norm, tabs, k_cmp, v_cmp, overlap, nk, vs_t)
    w_s5_in, w_s5_out, a_t = s5_w
    d_t = jnp.tile(s5_d.astype(F32).reshape(S5_MBLK, 1, LANE), (1, 1, S5_CHUNK))
    y_intra, bst = _s5_in(proj3, w_s5_in, d_t, layer)
    y5 = _s5_out(y_intra, _s5_scan(bst, a_t[layer], bsz), w_s5_out, layer)
    m_s5 = _glu(y5, proj3, s5_glu_w)
    return _outproj(m_moba, m_nsa, m_s5, w_out, x2)


def kernel(x, norm_w, w_in, w_out, moba_q_norm, moba_k_norm, nsa_q_norm, nsa_kc_norm, nsa_ks_norm, nsa_kw_norm, nsa_pe_k, nsa_pe_v, nsa_cmp_k_w1, nsa_cmp_k_w2, nsa_cmp_v_w1, nsa_cmp_v_w2, s5_a_re, s5_a_im, s5_b_re, s5_b_im, s5_c_re, s5_c_im, s5_d, s5_log_dt, s5_glu_w):
    bsz, seq, d = x.shape
    tabs = _rope_tables(jnp.arange(seq, dtype=F32))
    ncmp = seq // NSA_CMP_STRIDE
    ctabs = _rope_tables(jnp.arange(ncmp, dtype=F32) * NSA_CMP_STRIDE + (NSA_CMP_LEN - 1))
    nsel = seq // NSA_SEL_BLOCK
    ci = np.arange(ncmp)[:, None] * NSA_CMP_STRIDE
    sj = np.arange(nsel)[None, :] * NSA_SEL_BLOCK
    overlap = np.zeros((ncmp, LANE), np.float32)
    overlap[:, :nsel] = (ci < sj + NSA_SEL_BLOCK) & (ci + NSA_CMP_LEN > sj)
    overlap = jnp.asarray(overlap)
    params = (norm_w, w_out, moba_q_norm, moba_k_norm, nsa_q_norm, nsa_kc_norm, nsa_ks_norm,
              nsa_kw_norm, nsa_pe_k, nsa_pe_v, nsa_cmp_k_w1, nsa_cmp_k_w2, nsa_cmp_v_w1, nsa_cmp_v_w2,
              s5_d, s5_glu_w)
    kc, bpc, cpc, a_t = jax.vmap(_s5_factors)(s5_a_re, s5_a_im, s5_b_re, s5_b_im, s5_c_re, s5_c_im, s5_log_dt)
    merge = lambda a: a.reshape((a.shape[0] * a.shape[1],) + a.shape[2:])
    s5_w = tuple(_s5_wgen(merge(kc), merge(bpc), merge(cpc))) + (a_t,)
    x2 = x.reshape(bsz * seq, d)
    w_perm = _permute_w_in(w_in)
    for layer in range(norm_w.shape[0]):
        x2 = _layer(x2, bsz, seq, tabs, ctabs, overlap, w_perm, s5_w, layer, *[p[layer] for p in params])
    return x2.reshape(bsz, seq, d)
```

```python
import functools
import math

import numpy as np
import jax
import jax.numpy as jnp
from jax import lax
from jax.experimental import pallas as pl
from jax.experimental.pallas import tpu as pltpu

F32 = jnp.float32
BF16 = jnp.bfloat16
HIGHEST = lax.Precision.HIGHEST

LANE = 128
SUBLANE = 8
HEAD_DIM = 128
ROPE_DIM = HEAD_DIM // 4
ROPE_HALF = ROPE_DIM // 2
ROPE_THETA = 500000.0
EPS = 1e-6
SCALE = HEAD_DIM ** -0.5
NEG = -1e30

MOBA_HEADS = 4
MOBA_BLOCK = 256
MOBA_TOPK = 3

NSA_HEADS = 4
NSA_CMP_LEN = 32
NSA_CMP_STRIDE = 16
NSA_SEL_BLOCK = 64
NSA_SEL_TOPN = 16
NSA_WINDOW = 512
NSA_TQ = 256
NSA_SEL_PER_TILE = NSA_TQ // NSA_SEL_BLOCK
assert NSA_WINDOW == 2 * NSA_TQ

S5_WIDTH = 1024
S5_GROUP = 16
S5_GROUPS = S5_WIDTH // S5_GROUP
S5_STATE = 64
S5_CHUNK = 16
S5_MBLK = S5_WIDTH // LANE
S5_GPB = LANE // S5_GROUP
S5_HALF = S5_GPB * S5_STATE

CB_MQ, CB_MK, CB_MV, CB_MZ, CB_NQ, CB_NZ, CB_SU, CB_SZ = 0, 4, 8, 12, 16, 20, 24, 32
CB_NKS, CB_NVS, CB_NKW, CB_NVW, CB_NKC, CB_NVC, CB_NG = 40, 41, 42, 43, 44, 45, 46
IN_BLOCKS = 48

VMEM_LIMIT = 56 * 1024 * 1024


def _cparams(sem):
    return pltpu.CompilerParams(dimension_semantics=sem, vmem_limit_bytes=VMEM_LIMIT)


def _iota(shape, dim):
    return lax.broadcasted_iota(jnp.int32, shape, dim)


def _head_norm(x, w):
    return x * lax.rsqrt(jnp.mean(x * x, axis=-1, keepdims=True) + EPS) * w


def _head_norm_mxu(x, w):
    sq = x * x
    hi = sq.astype(BF16)
    lo = (sq - hi.astype(F32)).astype(BF16)
    avg = jnp.full((LANE, LANE), 1.0 / HEAD_DIM, BF16)
    ms = jnp.dot(hi, avg, preferred_element_type=F32) + jnp.dot(lo, avg, preferred_element_type=F32)
    return x * lax.rsqrt(ms + EPS) * w


def _rope(x, c, s1, s2):
    return x * c + pltpu.roll(x, LANE - ROPE_HALF, 1) * s1 + pltpu.roll(x, ROPE_HALF, 1) * s2


def _dot_nt(a, b):
    return lax.dot_general(a, b, (((1,), (1,)), ((), ())), preferred_element_type=F32)


def _rope_tables(pos):
    inv = ROPE_THETA ** (-jnp.arange(0, ROPE_DIM, 2, dtype=F32) / ROPE_DIM)
    ang = pos.astype(F32)[:, None] * inv[None, :]
    cos, sin = jnp.cos(ang), jnp.sin(ang)
    n = pos.shape[0]
    c = jnp.concatenate([cos, cos, jnp.ones((n, LANE - ROPE_DIM), F32)], axis=1)
    s1 = jnp.concatenate([-sin, jnp.zeros((n, LANE - ROPE_HALF), F32)], axis=1)
    s2 = jnp.concatenate([jnp.zeros((n, ROPE_HALF), F32), sin, jnp.zeros((n, LANE - ROPE_DIM), F32)], axis=1)
    return c, s1, s2


def _inproj_kernel(x_ref, nw_ref, w_ref, o_ref, h_ref):
    @pl.when(pl.program_id(1) == 0)
    def _():
        x = x_ref[...]
        ms = jnp.mean(x * x, axis=-1, keepdims=True)
        h_ref[...] = (x * lax.rsqrt(ms + EPS) * nw_ref[...]).astype(BF16)

    res = jnp.dot(h_ref[...], w_ref[...], preferred_element_type=F32)
    for c in range(o_ref.shape[0]):
        o_ref[c] = res[:, c * LANE:(c + 1) * LANE]


def _inproj(x2, norm_w, w_perm, layer):
    rows, d = x2.shape
    tm, tn = 1024, 1536
    nb = tn // LANE
    return pl.pallas_call(
        _inproj_kernel,
        grid=(rows // tm, IN_BLOCKS // nb),
        in_specs=[pl.BlockSpec((tm, d), lambda i, j: (i, 0)),
                  pl.BlockSpec((1, d), lambda i, j: (0, 0)),
                  pl.BlockSpec((None, d, tn), lambda i, j: (layer, 0, j))],
        out_specs=pl.BlockSpec((nb, tm, LANE), lambda i, j: (j, i, 0)),
        out_shape=jax.ShapeDtypeStruct((IN_BLOCKS, rows, LANE), F32),
        scratch_shapes=[pltpu.VMEM((tm, d), BF16)],
        compiler_params=_cparams(("parallel", "arbitrary")),
        name="inproj",
    )(x2, norm_w.reshape(1, d), w_perm)


def _kprep_kernel(mk_ref, mv_ref, nk_ref, mkw_ref, ksw_ref, kww_ref, c_ref, s1_ref, s2_ref,
                  mk_o, mv_o, nk_o, km_o, vst_o):
    c, s1, s2 = c_ref[...], s1_ref[...], s2_ref[...]
    means = []
    for h in range(MOBA_HEADS):
        k = _rope(_head_norm_mxu(mk_ref[h], mkw_ref[...]), c, s1, s2)
        mk_o[h] = k.astype(BF16)
        mv_o[h] = mv_ref[h].T.astype(BF16)
        means.append(jnp.mean(k, axis=0, keepdims=True))
    km_o[0] = jnp.concatenate(means, axis=0)
    nk_o[0] = _rope(_head_norm_mxu(nk_ref[0], ksw_ref[...]), c, s1, s2).astype(BF16)
    nk_o[1] = nk_ref[1].astype(BF16)
    vst_o[...] = nk_ref[1].T.astype(BF16)
    nk_o[2] = _rope(_head_norm_mxu(nk_ref[2], kww_ref[...]), c, s1, s2).astype(BF16)
    nk_o[3] = nk_ref[3].astype(BF16)


def _kprep(proj3, seq, mk_w, ks_w, kw_w, tabs):
    rows = proj3.shape[1]
    t = MOBA_BLOCK
    nt = seq // t
    blk4 = lambda cb: pl.BlockSpec((4, t, LANE), lambda i: (cb // 4, i, 0))
    wspec = pl.BlockSpec((1, LANE), lambda i: (0, 0))
    tspec = pl.BlockSpec((t, LANE), lambda i: (i % nt, 0))
    out4 = pl.BlockSpec((4, t, LANE), lambda i: (0, i, 0))
    return pl.pallas_call(
        _kprep_kernel,
        grid=(rows // t,),
        in_specs=[blk4(CB_MK), blk4(CB_MV), blk4(CB_NKS), wspec, wspec, wspec, tspec, tspec, tspec],
        out_specs=[out4, pl.BlockSpec((4, None, LANE, t), lambda i: (0, i // nt, 0, i % nt)), out4,
                   pl.BlockSpec((1, MOBA_HEADS, LANE), lambda i: (i, 0, 0)),
                   pl.BlockSpec((None, LANE, t), lambda i: (i // nt, 0, i % nt))],
        out_shape=[jax.ShapeDtypeStruct((4, rows, LANE), BF16),
                   jax.ShapeDtypeStruct((4, rows // seq, LANE, seq), BF16),
                   jax.ShapeDtypeStruct((4, rows, LANE), BF16),
                   jax.ShapeDtypeStruct((rows // t, MOBA_HEADS, LANE), F32),
                   jax.ShapeDtypeStruct((rows // seq, LANE, seq), BF16)],
        compiler_params=_cparams(("parallel",)),
        name="kprep",
    )(proj3, proj3, proj3, mk_w.reshape(1, LANE), ks_w.reshape(1, LANE), kw_w.reshape(1, LANE), *tabs)


def _topk_mask_t(score_t, k):
    n, cols = score_t.shape
    assert n % SUBLANE == 0
    row_in = _iota((SUBLANE, cols), 0)
    groups = [score_t[g:g + SUBLANE] for g in range(0, n, SUBLANE)]
    ranks = [jnp.zeros((SUBLANE, cols), F32) for _ in groups]
    for mm in range(n):
        cm = score_t[mm:mm + 1, :]
        for gi, sc in enumerate(groups):
            lo = gi * SUBLANE
            if lo > mm:
                inc = jnp.where(cm >= sc, 1.0, 0.0)
            elif lo + SUBLANE - 1 <= mm:
                inc = jnp.where(cm > sc, 1.0, 0.0)
            else:
                inc = jnp.where(row_in > mm - lo, jnp.where(cm >= sc, 1.0, 0.0), jnp.where(cm > sc, 1.0, 0.0))
            ranks[gi] = ranks[gi] + inc
    return jnp.concatenate(ranks, axis=0) < k


def _moba_kernel(q_ref, z_ref, qw_ref, c_ref, s1_ref, s2_ref, km_ref, k_ref, vt_ref, o_ref,
                 qt_s, sel_s, s_s, p_s, acc_s):
    i = pl.program_id(1)
    t = MOBA_BLOCK
    nb = km_ref.shape[1]
    nh = q_ref.shape[0]
    npairs = (i + 1) // 2
    blk_t = _iota((nb, t), 0)
    past = blk_t < i
    causal = _iota((t, t), 0) <= _iota((t, t), 1)
    start = pl.multiple_of(i * t, t)

    def put_scores(slot, h, pair):
        src = jnp.minimum(pair, nb // 2 - 1)
        off = pl.multiple_of(src * (2 * t), 2 * t)
        sc = jnp.dot(k_ref[h, pl.ds(off, 2 * t), :], qt_s[h], preferred_element_type=F32)
        row = jnp.minimum(2 * pair, nb - 2)
        s_s[slot, h, :t, :] = sc[:t] + sel_s[h, pl.ds(row, 1), :]
        s_s[slot, h, t:, :] = sc[t:] + sel_s[h, pl.ds(row + 1, 1), :]

    def weighted_values(h, pair):
        off = pl.multiple_of(pair * (2 * t), 2 * t)
        return jnp.dot(vt_ref[h, :, pl.ds(off, 2 * t)], p_s[h], preferred_element_type=F32)

    state = []
    for h in range(nh):
        qf = _rope(_head_norm(q_ref[h], qw_ref[...]), c_ref[...], s1_ref[...], s2_ref[...])
        qt = (qf * SCALE).T.astype(BF16)
        qt_s[h] = qt
        gate_t = lax.dot_general(km_ref[h], qf, (((1,), (1,)), ((), ())), precision=HIGHEST,
                                 preferred_element_type=F32)
        top = _topk_mask_t(jnp.where(past, gate_t, -jnp.inf), MOBA_TOPK)
        sel_s[h] = jnp.where(top & past, 0.0, NEG)
        s = jnp.where(causal, jnp.dot(k_ref[h, pl.ds(start, t), :], qt, preferred_element_type=F32), NEG)
        m = jnp.max(s, axis=0, keepdims=True)
        p = jnp.exp(s - m)
        acc_s[h] = jnp.dot(vt_ref[h, :, pl.ds(start, t)], p.astype(BF16), preferred_element_type=F32)
        state += [m, jnp.sum(p, axis=0, keepdims=True)]
        put_scores(0, h, 0)
        p_s[h] = jnp.zeros((2 * t, t), BF16)

    def trip(pair, carry, src, dst):
        prev = jnp.maximum(pair - 1, 0)
        out = []
        for h in range(nh):
            m, l = carry[2 * h:2 * h + 2]
            put_scores(dst, h, pair + 1)
            acc = acc_s[h] + weighted_values(h, prev)
            s = s_s[src, h]
            m_new = jnp.maximum(m, jnp.max(s, axis=0, keepdims=True))
            p = jnp.exp(s - m_new)
            alpha = jnp.exp(m - m_new)
            l_new = alpha * l + jnp.sum(p, axis=0, keepdims=True)
            acc_s[h] = alpha * acc
            p_s[h] = p.astype(BF16)
            out += [m_new, l_new]
        return tuple(out)

    def body(q, carry):
        return trip(2 * q + 1, trip(2 * q, carry, 0, 1), 1, 0)

    ntrips = 2 * ((npairs + 1) // 2)
    state = lax.fori_loop(0, ntrips // 2, body, tuple(state))
    last = jnp.maximum(ntrips - 1, 0)
    for h in range(nh):
        l = state[2 * h + 1]
        acc = acc_s[h] + weighted_values(h, last)
        z = z_ref[h]
        o_ref[h] = ((acc * (1.0 / l)).T * (z * jax.nn.sigmoid(z))).astype(BF16)


def _moba(proj3, bsz, seq, q_w, tabs, km_t, mk_n, mv_t):
    rows = proj3.shape[1]
    t = MOBA_BLOCK
    nb = seq // t
    nh = MOBA_HEADS
    assert nb % 4 == 0
    qspec = lambda cb: pl.BlockSpec((nh, t, LANE), lambda b, i: (cb // nh, b * nb + i, 0))
    tspec = pl.BlockSpec((t, LANE), lambda b, i: (i, 0))
    return pl.pallas_call(
        _moba_kernel,
        grid=(bsz, nb),
        in_specs=[qspec(CB_MQ), qspec(CB_MZ), pl.BlockSpec((1, LANE), lambda b, i: (0, 0)),
                  tspec, tspec, tspec,
                  pl.BlockSpec((nh, nb, LANE), lambda b, i: (0, b, 0)),
                  pl.BlockSpec((nh, seq, LANE), lambda b, i: (0, b, 0)),
                  pl.BlockSpec((nh, None, LANE, seq), lambda b, i: (0, b, 0, 0))],
        out_specs=pl.BlockSpec((nh, t, LANE), lambda b, i: (0, b * nb + i, 0)),
        out_shape=jax.ShapeDtypeStruct((nh, rows, LANE), BF16),
        scratch_shapes=[pltpu.VMEM((nh, LANE, t), BF16),
                        pltpu.VMEM((nh, nb, t), F32),
                        pltpu.VMEM((2, nh, 2 * t, t), F32),
                        pltpu.VMEM((nh, 2 * t, t), BF16),
                        pltpu.VMEM((nh, LANE, t), F32)],
        compiler_params=_cparams(("parallel", "parallel")),
        name="moba",
    )(proj3, proj3, q_w.reshape(1, LANE), *tabs, km_t, mk_n, mv_t)


def _cmp_kernel(hk_ref, hv_ref, pek_ref, pev_ref, w1k_ref, w2k_ref, w1v_ref, w2v_ref, nw_ref,
                c_ref, s1_ref, s2_ref, ko_ref, vo_ref):
    st = NSA_CMP_STRIDE
    nh = ko_ref.shape[1]

    def compress(x_ref, pe_ref, w1_ref, w2_ref):
        a = jnp.zeros((nh, LANE), F32)
        b = jnp.zeros((nh, LANE), F32)
        for l in range(st):
            x = x_ref[0, pl.ds(l, nh, stride=st), :]
            a = a + jnp.dot((x + pe_ref[l:l + 1, :]).astype(BF16), w1_ref[l * LANE:(l + 1) * LANE, :],
                            preferred_element_type=F32)
            b = b + jnp.dot((x + pe_ref[st + l:st + l + 1, :]).astype(BF16),
                            w1_ref[(st + l) * LANE:(st + l + 1) * LANE, :], preferred_element_type=F32)
        pre = a + pltpu.roll(b, nh - 1, 0)
        return jnp.dot(jax.nn.gelu(pre).astype(BF16), w2_ref[...], preferred_element_type=F32)

    kc = compress(hk_ref, pek_ref, w1k_ref, w2k_ref)
    vc = compress(hv_ref, pev_ref, w1v_ref, w2v_ref)
    ko_ref[0] = _rope(_head_norm(kc, nw_ref[...]), c_ref[...], s1_ref[...], s2_ref[...]).astype(BF16)
    vo_ref[0] = vc.astype(BF16)


def _compress(proj3, bsz, seq, pe_k, pe_v, w1k, w2k, w1v, w2v, kc_w, ctabs):
    nh = seq // NSA_CMP_STRIDE
    wide = NSA_CMP_LEN * HEAD_DIM
    hspec = lambda cb: pl.BlockSpec((1, seq, LANE), lambda b: (cb, b, 0))
    full = lambda shape: pl.BlockSpec(shape, lambda b: tuple(0 for _ in shape))
    ospec = pl.BlockSpec((1, nh, LANE), lambda b: (b, 0, 0))
    return pl.pallas_call(
        _cmp_kernel,
        grid=(bsz,),
        in_specs=[hspec(CB_NKC), hspec(CB_NVC), full((NSA_CMP_LEN, LANE)), full((NSA_CMP_LEN, LANE)),
                  full((wide, LANE)), full((LANE, LANE)), full((wide, LANE)), full((LANE, LANE)),
                  full((1, LANE)), full((nh, LANE)), full((nh, LANE)), full((nh, LANE))],
        out_specs=[ospec, ospec],
        out_shape=[jax.ShapeDtypeStruct((bsz, nh, LANE), BF16)] * 2,
        compiler_params=_cparams(("parallel",)),
        name="nsa_compress",
    )(proj3, proj3, pe_k, pe_v, w1k.astype(BF16), w2k.astype(BF16),
      w1v.astype(BF16), w2v.astype(BF16), kc_w.reshape(1, LANE), *ctabs)


def _nsa_kernel(q_ref, z_ref, g_ref, qw_ref, c_ref, s1_ref, s2_ref, kc_ref, vc_ref, ov_ref,
                ks_ref, vst_ref, kw_ref, vw_ref, o_ref, qb_s, qt_s, acc_s, ocmp_s, osel_s, s_s, p_s):
    i = pl.program_id(1)
    t = NSA_TQ
    ncmp = kc_ref.shape[1]
    nsel = ks_ref.shape[0] // NSA_SEL_BLOCK
    sel_shift = int(math.log2(NSA_SEL_BLOCK))
    trow = i * t + _iota((t, 1), 0)
    row = _iota((t, t), 0)
    col = _iota((t, t), 1)
    causal = col <= row

    c, s1, s2 = c_ref[...], s1_ref[...], s2_ref[...]
    for h in range(NSA_HEADS):
        qb_s[h] = (_rope(_head_norm_mxu(q_ref[h], qw_ref[...]), c, s1, s2) * SCALE).astype(BF16)

    valid = (_iota((t, ncmp), 1) * NSA_CMP_STRIDE + (NSA_CMP_LEN - 1)) <= trow
    kc, vc = kc_ref[0], vc_ref[0]
    psum = jnp.zeros((t, ncmp), F32)
    for h in range(NSA_HEADS):
        s = jnp.where(valid, _dot_nt(qb_s[h], kc), NEG)
        mx = jnp.max(s, axis=1, keepdims=True)
        e = jnp.exp(s - mx)
        p = e * jnp.where(mx > 0.5 * NEG, 1.0 / jnp.sum(e, axis=1, keepdims=True), 0.0)
        ocmp_s[h] = jnp.dot(p.astype(BF16), vc, preferred_element_type=F32)
        psum = psum + p

    imp_t = jnp.dot(psum, ov_ref[...], precision=HIGHEST, preferred_element_type=F32).T[:nsel]
    jj = _iota((nsel, t), 0)
    cur = (i * t + _iota((1, t), 1)) >> sel_shift
    score = jnp.where(jj <= cur, imp_t, -jnp.inf)
    score = jnp.where((jj == 0) | (jj == cur) | (jj == cur - 1), jnp.inf, score)
    top = _topk_mask_t(score, NSA_SEL_TOPN)
    selbias_t = jnp.concatenate([jnp.where(top, 0.0, NEG), jnp.zeros((LANE - nsel - SUBLANE, t), F32),
                                 jnp.full((SUBLANE, t), NEG, F32)], axis=0).astype(BF16)
    ntiles = ks_ref.shape[0] // t
    key_blk = _iota((t, LANE), 0) >> sel_shift

    def tile_bias(kt):
        blk = jnp.where(kt < i, kt * NSA_SEL_PER_TILE + key_blk, LANE - 1)
        onehot = jnp.where(_iota((t, LANE), 1) == blk, 1.0, 0.0).astype(BF16)
        return jnp.dot(onehot, selbias_t, preferred_element_type=F32)

    def score_inputs(kt):
        src = jnp.minimum(kt, ntiles - 1)
        return ks_ref[pl.ds(pl.multiple_of(src * t, t), t), :], tile_bias(kt)

    def put_scores(slot, h, k, bias):
        s_s[slot, h] = jnp.dot(k, qt_s[h], preferred_element_type=F32) + bias

    own = pl.multiple_of(i * t, t)
    own_blk = i * NSA_SEL_PER_TILE + key_blk
    own_bias = jnp.dot(jnp.where(_iota((t, LANE), 1) == own_blk, 1.0, 0.0).astype(BF16), selbias_t,
                       preferred_element_type=F32)
    k_own, vt_own = ks_ref[pl.ds(own, t), :], vst_ref[:, pl.ds(own, t)]
    state = []
    for h in range(NSA_HEADS):
        qt_s[h] = qb_s[h].T
        s = jnp.where(row <= col, jnp.dot(k_own, qt_s[h], preferred_element_type=F32) + own_bias, NEG)
        m = jnp.max(s, axis=0, keepdims=True)
        p = jnp.exp(s - m)
        acc_s[h] = jnp.dot(vt_own, p.astype(BF16), preferred_element_type=F32)
        state += [m, jnp.sum(p, axis=0, keepdims=True)]
        p_s[h] = jnp.zeros((t, t), BF16)
    k_first, bias_first = score_inputs(0)
    for h in range(NSA_HEADS):
        put_scores(0, h, k_first, bias_first)

    def trip(kt, carry, src, dst):
        k_next, bias_next = score_inputs(kt + 1)
        vt_prev = vst_ref[:, pl.ds(pl.multiple_of(jnp.clip(kt - 1, 0, ntiles - 1) * t, t), t)]
        out = []
        for h in range(NSA_HEADS):
            m, l = carry[2 * h:2 * h + 2]
            put_scores(dst, h, k_next, bias_next)
            acc = acc_s[h] + jnp.dot(vt_prev, p_s[h], preferred_element_type=F32)
            s = s_s[src, h]
            m_new = jnp.maximum(m, jnp.max(s, axis=0, keepdims=True))
            p = jnp.exp(s - m_new)
            alpha = jnp.exp(m - m_new)
            acc_s[h] = alpha * acc
            p_s[h] = p.astype(BF16)
            out += [m_new, alpha * l + jnp.sum(p, axis=0, keepdims=True)]
        return tuple(out)

    def sel_body(q, carry):
        return trip(2 * q + 1, trip(2 * q, carry, 0, 1), 1, 0)

    ntrips = 2 * ((i + 1) // 2)
    state = lax.fori_loop(0, ntrips // 2, sel_body, tuple(state))
    vt_last = vst_ref[:, pl.ds(pl.multiple_of(jnp.clip(ntrips - 1, 0, ntiles - 1) * t, t), t)]
    for h in range(NSA_HEADS):
        acc = acc_s[h] + jnp.dot(vt_last, p_s[h], preferred_element_type=F32)
        osel_s[h] = (acc * (1.0 / state[2 * h + 1])).T

    far = pl.multiple_of(jnp.maximum(i - 2, 0) * t, t)
    mid = pl.multiple_of(jnp.maximum(i - 1, 0) * t, t)
    k3 = jnp.concatenate([kw_ref[pl.ds(far, t), :], kw_ref[pl.ds(mid, t), :], kw_ref[pl.ds(own, t), :]], axis=0)
    v3 = jnp.concatenate([vw_ref[pl.ds(far, t), :], vw_ref[pl.ds(mid, t), :], vw_ref[pl.ds(own, t), :]], axis=0)
    allowed = jnp.concatenate([(col > row) & (i >= 2), jnp.full((t, t), True) & (i >= 1), causal], axis=1)
    g = jax.nn.sigmoid(g_ref[0])
    for h in range(NSA_HEADS):
        s = jnp.where(allowed, _dot_nt(qb_s[h], k3), NEG)
        p = jnp.exp(s - jnp.max(s, axis=1, keepdims=True))
        o_win = jnp.dot(p.astype(BF16), v3, preferred_element_type=F32) * (1.0 / jnp.sum(p, axis=1, keepdims=True))
        o = (g[:, 3 * h:3 * h + 1] * ocmp_s[h] + g[:, 3 * h + 1:3 * h + 2] * osel_s[h]
             + g[:, 3 * h + 2:3 * h + 3] * o_win)
        z = z_ref[h]
        o_ref[h] = (o * (z * jax.nn.sigmoid(z))).astype(BF16)


def _nsa(proj3, bsz, seq, q_w, tabs, k_cmp, v_cmp, overlap, nk, vs_t):
    rows = proj3.shape[1]
    t = NSA_TQ
    nt = seq // t
    ncmp = seq // NSA_CMP_STRIDE
    nsel = seq // NSA_SEL_BLOCK
    blk4 = lambda cb: pl.BlockSpec((4, t, LANE), lambda b, i: (cb // 4, b * nt + i, 0))
    tspec = pl.BlockSpec((t, LANE), lambda b, i: (i, 0))
    cspec = pl.BlockSpec((1, ncmp, LANE), lambda b, i: (b, 0, 0))
    kvspec = lambda which: pl.BlockSpec((None, seq, LANE), lambda b, i: (which, b, 0))
    return pl.pallas_call(
        _nsa_kernel,
        grid=(bsz, nt),
        in_specs=[blk4(CB_NQ), blk4(CB_NZ),
                  pl.BlockSpec((1, t, LANE), lambda b, i: (CB_NG, b * nt + i, 0)),
                  pl.BlockSpec((1, LANE), lambda b, i: (0, 0)), tspec, tspec, tspec, cspec, cspec,
                  pl.BlockSpec((ncmp, LANE), lambda b, i: (0, 0)),
                  kvspec(0), pl.BlockSpec((None, LANE, seq), lambda b, i: (b, 0, 0)), kvspec(2), kvspec(3)],
        out_specs=pl.BlockSpec((4, t, LANE), lambda b, i: (0, b * nt + i, 0)),
        out_shape=jax.ShapeDtypeStruct((NSA_HEADS, rows, LANE), BF16),
        scratch_shapes=[pltpu.VMEM((NSA_HEADS, t, LANE), BF16),
                        pltpu.VMEM((NSA_HEADS, LANE, t), BF16),
                        pltpu.VMEM((NSA_HEADS, LANE, t), F32),
                        pltpu.VMEM((NSA_HEADS, t, LANE), F32),
                        pltpu.VMEM((NSA_HEADS, t, LANE), F32),
                        pltpu.VMEM((2, NSA_HEADS, t, t), F32),
                        pltpu.VMEM((NSA_HEADS, t, t), BF16)],
        compiler_params=_cparams(("parallel", "parallel")),
        name="nsa",
    )(proj3, proj3, proj3, q_w.reshape(1, LANE), *tabs, k_cmp, v_cmp, overlap, nk, vs_t, nk, nk)


def _s5_factors(a_re, a_im, b_re, b_im, c_re, c_im, log_dt):
    t = S5_CHUNK
    dt = jnp.exp(log_dt.astype(F32))[:, None]
    ar, ai = a_re.astype(F32), a_im.astype(F32)
    ang = dt * ai
    mag = jnp.exp(dt * ar)
    abar_r, abar_i = mag * jnp.cos(ang), mag * jnp.sin(ang)
    nr, ni = abar_r - 1.0, abar_i
    den = ar * ar + ai * ai
    fr = (nr * ar + ni * ai) / den
    fi = (ni * ar - nr * ai) / den
    bt_r, bt_i = b_re.astype(F32).transpose(0, 2, 1), b_im.astype(F32).transpose(0, 2, 1)
    bbar_r = fr[:, None, :] * bt_r - fi[:, None, :] * bt_i
    bbar_i = fr[:, None, :] * bt_i + fi[:, None, :] * bt_r

    def powers(tau):
        tau = jnp.asarray(tau, F32)[:, None, None]
        pmag = jnp.exp(tau * (dt * ar)[None])
        return pmag * jnp.cos(tau * ang[None]), pmag * jnp.sin(tau * ang[None])

    pw_r, pw_i = powers(np.arange(t + 1))
    cr, ci = c_re.astype(F32), c_im.astype(F32)
    cp_r = cr[None] * pw_r[:, :, None, :] - ci[None] * pw_i[:, :, None, :]
    cp_i = cr[None] * pw_i[:, :, None, :] + ci[None] * pw_r[:, :, None, :]
    klag = jnp.sum(cp_r[:t, :, None, :, :] * bbar_r[None, :, :, None, :]
                   - cp_i[:t, :, None, :, :] * bbar_i[None, :, :, None, :], axis=-1)
    kc = (klag.reshape(t, S5_MBLK, S5_GPB, S5_GROUP, S5_GROUP).transpose(1, 0, 2, 3, 4)
          .reshape(S5_MBLK, t * LANE, S5_GROUP))
    rev_r, rev_i = powers(t - 1 - np.arange(t))
    bp_r = rev_r[:, :, None, :] * bbar_r[None] - rev_i[:, :, None, :] * bbar_i[None]
    bp_i = rev_r[:, :, None, :] * bbar_i[None] + rev_i[:, :, None, :] * bbar_r[None]
    bpc = (jnp.stack([bp_r, bp_i], axis=3).reshape(t, S5_MBLK, LANE, 2 * S5_STATE).transpose(1, 0, 2, 3)
           .reshape(S5_MBLK, t * LANE, 2 * S5_STATE))
    cpo = jnp.stack([cp_r[1:], -cp_i[1:]]).reshape(2, t, S5_MBLK, S5_GPB, S5_GROUP, S5_STATE)
    cpc = cpo.transpose(2, 0, 3, 5, 1, 4).reshape(S5_MBLK, 2 * S5_HALF, t * S5_GROUP)
    a_t = jnp.concatenate([pw_r[t].reshape(S5_MBLK, 1, S5_HALF), pw_i[t].reshape(S5_MBLK, 1, S5_HALF)], axis=2)
    return kc, bpc, cpc, a_t


def _s5_wgen_kernel(kc_ref, bp_ref, cp_ref, win_ref, wc_ref):
    t = S5_CHUNK
    wide = t * LANE
    gs, ps = int(math.log2(S5_GROUP)), int(math.log2(S5_STATE))

    def expand(x, copy_mask, group_mask):
        ex = jnp.where(copy_mask, 1.0, 0.0).astype(BF16)
        return jnp.where(group_mask, jnp.dot(x.astype(BF16), ex, preferred_element_type=F32), 0.0).astype(BF16)

    r, c = _iota((S5_GROUP, LANE), 0), _iota((S5_GROUP, LANE), 1)
    copy_k = r == (c & (S5_GROUP - 1))
    r, c = _iota((LANE, LANE), 0), _iota((LANE, LANE), 1)
    same_k = (r >> gs) == (c >> gs)
    zero = jnp.zeros((LANE, LANE), BF16)
    for lag in range(t):
        bd = expand(kc_ref[0, lag * LANE:(lag + 1) * LANE, :], copy_k, same_k)
        for s in range(t - lag):
            win_ref[0, s * LANE:(s + 1) * LANE, (s + lag) * LANE:(s + lag + 1) * LANE] = bd
    for s in range(1, t):
        for tt in range(s):
            win_ref[0, s * LANE:(s + 1) * LANE, tt * LANE:(tt + 1) * LANE] = zero
    r, c = _iota((LANE, 2 * S5_HALF), 0), _iota((LANE, 2 * S5_HALF), 1)
    copy_b = ((r >> ps) == (c >> (ps + 3))) & ((r & (S5_STATE - 1)) == (c & (S5_STATE - 1)))
    same_b = (r >> gs) == ((c >> ps) & (S5_GPB - 1))
    for s in range(t):
        win_ref[0, s * LANE:(s + 1) * LANE, wide:] = expand(bp_ref[0, s * LANE:(s + 1) * LANE, :],
                                                            copy_b, same_b)
    rows = 2 * S5_STATE * 2
    r, c = _iota((t * S5_GROUP, wide), 0), _iota((t * S5_GROUP, wide), 1)
    copy_c = ((r >> gs) == (c >> (gs + 3))) & ((r & (S5_GROUP - 1)) == (c & (S5_GROUP - 1)))
    for ch in range(2 * S5_HALF // rows):
        r, c = ch * rows + _iota((rows, wide), 0), _iota((rows, wide), 1)
        same_c = ((r >> ps) & (S5_GPB - 1)) == ((c >> gs) & (S5_GPB - 1))
        wc_ref[0, ch * rows:(ch + 1) * rows, :] = expand(cp_ref[0, ch * rows:(ch + 1) * rows, :],
                                                         copy_c, same_c)


def _s5_wgen(kc, bpc, cpc):
    t = S5_CHUNK
    wide = t * LANE
    spec = lambda a: pl.BlockSpec((1,) + a.shape[1:], lambda m: (m, 0, 0))
    return pl.pallas_call(
        _s5_wgen_kernel,
        grid=(kc.shape[0],),
        in_specs=[spec(kc), spec(bpc), spec(cpc)],
        out_specs=[pl.BlockSpec((1, wide, wide + 2 * S5_HALF), lambda m: (m, 0, 0)),
                   pl.BlockSpec((1, 2 * S5_HALF, wide), lambda m: (m, 0, 0))],
        out_shape=[jax.ShapeDtypeStruct((kc.shape[0], wide, wide + 2 * S5_HALF), BF16),
                   jax.ShapeDtypeStruct((kc.shape[0], 2 * S5_HALF, wide), BF16)],
        compiler_params=_cparams(("parallel",)),
        name="s5_wgen",
    )(kc, bpc, cpc)


def _s5_in_kernel(u_ref, w_ref, d_ref, y_ref, b_ref):
    tr = y_ref.shape[1]
    u = jnp.concatenate([u_ref[0, pl.ds(s, tr, stride=S5_CHUNK), :] for s in range(S5_CHUNK)], axis=1)
    ub = u.astype(BF16)
    wide = y_ref.shape[2]
    for j in range(S5_CHUNK // 2):
        kk = (2 * j + 2) * LANE
        cols = slice(2 * j * LANE, kk)
        y_ref[0, :, cols] = (jnp.dot(ub[:, :kk], w_ref[0, :kk, cols], preferred_element_type=F32)
                             + d_ref[0, :, cols] * u[:, cols])
    b_ref[0] = jnp.dot(ub, w_ref[0, :, wide:], preferred_element_type=F32)


def _s5_in(proj3, w_in, d_t, layer):
    rows = proj3.shape[1] // S5_CHUNK
    wide = S5_CHUNK * LANE
    tr = min(rows, 512)
    return pl.pallas_call(
        _s5_in_kernel,
        grid=(S5_MBLK, rows // tr),
        in_specs=[pl.BlockSpec((1, tr * S5_CHUNK, LANE), lambda m, r: (CB_SU + m, r, 0)),
                  pl.BlockSpec((1, wide, wide + 2 * S5_HALF), lambda m, r: (layer * S5_MBLK + m, 0, 0)),
                  pl.BlockSpec((1, 1, wide), lambda m, r: (m, 0, 0))],
        out_specs=[pl.BlockSpec((1, tr, wide), lambda m, r: (m, r, 0)),
                   pl.BlockSpec((1, tr, 2 * S5_HALF), lambda m, r: (m, r, 0))],
        out_shape=[jax.ShapeDtypeStruct((S5_MBLK, rows, wide), F32),
                   jax.ShapeDtypeStruct((S5_MBLK, rows, 2 * S5_HALF), F32)],
        compiler_params=_cparams(("parallel", "parallel")),
        name="s5_in",
    )(proj3, w_in, d_t)


def _s5_scan_kernel(b_ref, a_ref, o_ref, *, bsz, nk):
    nm = b_ref.shape[0]

    def body(k, carry):
        out = []
        for mi in range(nm):
            ar = a_ref[mi, :, :S5_HALF]
            ai = a_ref[mi, :, S5_HALF:]
            for b in range(bsz):
                sr, si = carry[2 * (mi * bsz + b)], carry[2 * (mi * bsz + b) + 1]
                row = b * nk + k
                o_ref[mi, pl.ds(row, 1), :] = jnp.concatenate([sr, si], axis=1)
                x = b_ref[mi, pl.ds(row, 1), :]
                out.append(ar * sr - ai * si + x[:, :S5_HALF])
                out.append(ar * si + ai * sr + x[:, S5_HALF:])
        return tuple(out)

    zero = jnp.zeros((1, S5_HALF), F32)
    lax.fori_loop(0, nk, body, tuple(zero for _ in range(2 * bsz * nm)))


def _s5_scan(bst, a_t, bsz):
    _, rows, wide = bst.shape
    nm = 2
    spec = pl.BlockSpec((nm, rows, wide), lambda m: (m, 0, 0))
    return pl.pallas_call(
        functools.partial(_s5_scan_kernel, bsz=bsz, nk=rows // bsz),
        grid=(S5_MBLK // nm,),
        in_specs=[spec, pl.BlockSpec((nm, 1, wide), lambda m: (m, 0, 0))],
        out_specs=spec,
        out_shape=jax.ShapeDtypeStruct(bst.shape, F32),
        compiler_params=_cparams(("parallel",)),
        name="s5_scan",
    )(bst, a_t)


def _s5_out_kernel(y_ref, s_ref, w_ref, o_ref):
    tr = y_ref.shape[1]
    y = jax.nn.gelu(y_ref[0] + jnp.dot(s_ref[0].astype(BF16), w_ref[0], preferred_element_type=F32))
    for s in range(S5_CHUNK):
        o_ref[0, pl.ds(s, tr, stride=S5_CHUNK), :] = y[:, s * LANE:(s + 1) * LANE]


def _s5_out(y_intra, s_prev, w_c, layer):
    _, rows, wide = y_intra.shape
    tr = min(rows, 512)
    return pl.pallas_call(
        _s5_out_kernel,
        grid=(S5_MBLK, rows // tr),
        in_specs=[pl.BlockSpec((1, tr, wide), lambda m, r: (m, r, 0)),
                  pl.BlockSpec((1, tr, 2 * S5_HALF), lambda m, r: (m, r, 0)),
                  pl.BlockSpec((1, 2 * S5_HALF, wide), lambda m, r: (layer * S5_MBLK + m, 0, 0))],
        out_specs=pl.BlockSpec((1, tr * S5_CHUNK, LANE), lambda m, r: (m, r, 0)),
        out_shape=jax.ShapeDtypeStruct((S5_MBLK, rows * S5_CHUNK, LANE), F32),
        compiler_params=_cparams(("parallel", "parallel")),
        name="s5_out",
    )(y_intra, s_prev, w_c)


def _glu_kernel(y_ref, z_ref, w_ref, o_ref):
    nb = y_ref.shape[0]
    y = jnp.concatenate([y_ref[c] for c in range(nb)], axis=1)
    z = jnp.concatenate([z_ref[c] for c in range(nb)], axis=1)
    gate = jax.nn.sigmoid(jnp.dot(y.astype(BF16), w_ref[...], preferred_element_type=F32))
    o = (y * gate * (z * jax.nn.sigmoid(z))).astype(BF16)
    for c in range(nb):
        o_ref[c] = o[:, c * LANE:(c + 1) * LANE]


def _glu(y5, proj3, glu_w):
    nb, rows, _ = y5.shape
    tm = 512
    spec = lambda blk: pl.BlockSpec((nb, tm, LANE), lambda i: (blk, i, 0))
    return pl.pallas_call(
        _glu_kernel,
        grid=(rows // tm,),
        in_specs=[spec(0), spec(CB_SZ // nb), pl.BlockSpec((S5_WIDTH, S5_WIDTH), lambda i: (0, 0))],
        out_specs=spec(0),
        out_shape=jax.ShapeDtypeStruct((nb, rows, LANE), BF16),
        compiler_params=_cparams(("parallel",)),
        name="s5_glu",
    )(y5, proj3, glu_w.astype(BF16))


def _outproj_kernel(a_ref, b_ref, c_ref, w_ref, x_ref, o_ref):
    parts = ([a_ref[h] for h in range(a_ref.shape[0])] + [b_ref[h] for h in range(b_ref.shape[0])]
             + [c_ref[h] for h in range(c_ref.shape[0])])
    mixed = jnp.concatenate(parts, axis=1)
    o_ref[...] = x_ref[...] + jnp.dot(mixed, w_ref[...], preferred_element_type=F32)


def _outproj(m_moba, m_nsa, m_s5, w_out, x2):
    rows, d = x2.shape
    tm = 512
    lspec = lambda n: pl.BlockSpec((n, tm, LANE), lambda i: (0, i, 0))
    return pl.pallas_call(
        _outproj_kernel,
        grid=(rows // tm,),
        in_specs=[lspec(m_moba.shape[0]), lspec(m_nsa.shape[0]), lspec(m_s5.shape[0]),
                  pl.BlockSpec(w_out.shape, lambda i: (0, 0)),
                  pl.BlockSpec((tm, d), lambda i: (i, 0))],
        out_specs=pl.BlockSpec((tm, d), lambda i: (i, 0)),
        out_shape=jax.ShapeDtypeStruct((rows, d), F32),
        compiler_params=_cparams(("parallel",)),
        name="outproj",
    )(m_moba, m_nsa, m_s5, w_out.astype(BF16), x2)


def _w_in_source(blk):
    mw, nw, kvw, ng = MOBA_HEADS * HEAD_DIM, NSA_HEADS * HEAD_DIM, HEAD_DIM, 3 * NSA_HEADS
    o_kv = 4 * mw + nw
    o_ng = o_kv + 6 * kvw
    o_nz = o_ng + ng
    src = jnp.where(blk < CB_NZ, blk * LANE,
          jnp.where(blk < CB_NKS, o_nz + (blk - CB_NZ) * LANE,
          jnp.where(blk < CB_NKC, o_kv + 2 * kvw + (blk - CB_NKS) * LANE,
          jnp.where(blk < CB_NG, o_kv + (blk - CB_NKC) * LANE, o_ng))))
    real = jnp.where(blk < CB_NG, LANE, jnp.where(blk == CB_NG, ng, 0))
    return src, real


def _wperm_kernel(w_ref, o_ref):
    _, real = _w_in_source(pl.program_id(0))
    for layer in range(o_ref.shape[0]):
        x = w_ref[:, layer, :]
        x = jnp.where(_iota(x.shape, 0) < real, x, 0.0)
        o_ref[layer] = x.T.astype(BF16)


def _permute_w_in(w_in_all):
    wt = jnp.transpose(w_in_all, (2, 0, 1))
    _, nl, d = wt.shape
    return pl.pallas_call(
        _wperm_kernel,
        grid=(IN_BLOCKS,),
        in_specs=[pl.BlockSpec((pl.Element(LANE), pl.Element(nl), pl.Element(d)),
                               lambda c: (_w_in_source(c)[0], 0, 0))],
        out_specs=pl.BlockSpec((nl, d, LANE), lambda c: (0, 0, c)),
        out_shape=jax.ShapeDtypeStruct((nl, d, IN_BLOCKS * LANE), BF16),
        compiler_params=_cparams(("parallel",)),
        name="w_in_permute",
    )(wt)


def _layer(x2, bsz, seq, tabs, ctabs, overlap, w_in, s5_w, layer, norm_w, w_out, moba_q_norm, moba_k_norm,
           nsa_q_norm, nsa_kc_norm, nsa_ks_norm, nsa_kw_norm, nsa_pe_k, nsa_pe_v, nsa_cmp_k_w1, nsa_cmp_k_w2,
           nsa_cmp_v_w1, nsa_cmp_v_w2, s5_d, s5_glu_w):
    proj3 = _inproj(x2, norm_w, w_in, layer)
    mk_n, mv_b, nk, kmean, vs_t = _kprep(proj3, seq, moba_k_norm, nsa_ks_norm, nsa_kw_norm, tabs)
    m_moba = _moba(proj3, bsz, seq, moba_q_norm, tabs, kmean.transpose(1, 0, 2), mk_n, mv_b)
    k_cmp, v_cmp = _compress(proj3, bsz, seq, nsa_pe_k, nsa_pe_v, nsa_cmp_k_w1, nsa_cmp_k_w2,
                             nsa_cmp_v_w1, nsa_cmp_v_w2, nsa_kc_norm, ctabs)
    m_nsa = _nsa(proj3, bsz, seq, nsa_q_norm, tabs, k_cmp, v_cmp, overlap, nk, vs_t)
    w_s5_in, w_s5_out, a_t = s5_w
    d_t = jnp.tile(s5_d.astype(F32).reshape(S5_MBLK, 1, LANE), (1, 1, S5_CHUNK))
    y_intra, bst = _s5_in(proj3, w_s5_in, d_t, layer)
    y5 = _s5_out(y_intra, _s5_scan(bst, a_t[layer], bsz), w_s5_out, layer)
    m_s5 = _glu(y5, proj3, s5_glu_w)
    return _outproj(m_moba, m_nsa, m_s5, w_out, x2)


def kernel(x, norm_w, w_in, w_out, moba_q_norm, moba_k_norm, nsa_q_norm, nsa_kc_norm, nsa_ks_norm, nsa_kw_norm, nsa_pe_k, nsa_pe_v, nsa_cmp_k_w1, nsa_cmp_k_w2, nsa_cmp_v_w1, nsa_cmp_v_w2, s5_a_re, s5_a_im, s5_b_re, s5_b_im, s5_c_re, s5_c_im, s5_d, s5_log_dt, s5_glu_w):
    bsz, seq, d = x.shape
    tabs = _rope_tables(jnp.arange(seq, dtype=F32))
    ncmp = seq // NSA_CMP_STRIDE
    ctabs = _rope_tables(jnp.arange(ncmp, dtype=F32) * NSA_CMP_STRIDE + (NSA_CMP_LEN - 1))
    nsel = seq // NSA_SEL_BLOCK
    ci = np.arange(ncmp)[:, None] * NSA_CMP_STRIDE
    sj = np.arange(nsel)[None, :] * NSA_SEL_BLOCK
    overlap = np.zeros((ncmp, LANE), np.float32)
    overlap[:, :nsel] = (ci < sj + NSA_SEL_BLOCK) & (ci + NSA_CMP_LEN > sj)
    overlap = jnp.asarray(overlap)
    params = (norm_w, w_out, moba_q_norm, moba_k_norm, nsa_q_norm, nsa_kc_norm, nsa_ks_norm,
              nsa_kw_norm, nsa_pe_k, nsa_pe_v, nsa_cmp_k_w1, nsa_cmp_k_w2, nsa_cmp_v_w1, nsa_cmp_v_w2,
              s5_d, s5_glu_w)
    kc, bpc, cpc, a_t = jax.vmap(_s5_factors)(s5_a_re, s5_a_im, s5_b_re, s5_b_im, s5_c_re, s5_c_im, s5_log_dt)
    merge = lambda a: a.reshape((a.shape[0] * a.shape[1],) + a.shape[2:])
    s5_w = tuple(_s5_wgen(merge(kc), merge(bpc), merge(cpc))) + (a_t,)
    x2 = x.reshape(bsz * seq, d)
    w_perm = _permute_w_in(w_in)
    for layer in range(norm_w.shape[0]):
        x2 = _layer(x2, bsz, seq, tabs, ctabs, overlap, w_perm, s5_w, layer, *[p[layer] for p in params])
    return x2.reshape(bsz, seq, d)
```

```python
import functools
import math

import numpy as np
import jax
import jax.numpy as jnp
from jax import lax
from jax.experimental import pallas as pl
from jax.experimental.pallas import tpu as pltpu

F32 = jnp.float32
BF16 = jnp.bfloat16
HIGHEST = lax.Precision.HIGHEST

LANE = 128
SUBLANE = 8
HEAD_DIM = 128
ROPE_DIM = HEAD_DIM // 4
ROPE_HALF = ROPE_DIM // 2
ROPE_THETA = 500000.0
EPS = 1e-6
SCALE = HEAD_DIM ** -0.5
NEG = -1e30

MOBA_HEADS = 4
MOBA_BLOCK = 256
MOBA_TOPK = 3

NSA_HEADS = 4
NSA_CMP_LEN = 32
NSA_CMP_STRIDE = 16
NSA_SEL_BLOCK = 64
NSA_SEL_TOPN = 16
NSA_WINDOW = 512
NSA_TQ = 256
NSA_SEL_PER_TILE = NSA_TQ // NSA_SEL_BLOCK
assert NSA_WINDOW == 2 * NSA_TQ

S5_WIDTH = 1024
S5_GROUP = 16
S5_GROUPS = S5_WIDTH // S5_GROUP
S5_STATE = 64
S5_CHUNK = 16
S5_MBLK = S5_WIDTH // LANE
S5_GPB = LANE // S5_GROUP
S5_HALF = S5_GPB * S5_STATE

CB_MQ, CB_MK, CB_MV, CB_MZ, CB_NQ, CB_NZ, CB_SU, CB_SZ = 0, 4, 8, 12, 16, 20, 24, 32
CB_NKS, CB_NVS, CB_NKW, CB_NVW, CB_NKC, CB_NVC, CB_NG = 40, 41, 42, 43, 44, 45, 46
IN_BLOCKS = 48

VMEM_LIMIT = 56 * 1024 * 1024
INPROJ_TM, INPROJ_TN = 1024, 1536
ROW_TILE = 512


def _cparams(sem):
    return pltpu.CompilerParams(dimension_semantics=sem, vmem_limit_bytes=VMEM_LIMIT)


def _iota(shape, dim):
    return lax.broadcasted_iota(jnp.int32, shape, dim)


def _head_norm(x, w):
    return x * lax.rsqrt(jnp.mean(x * x, axis=-1, keepdims=True) + EPS) * w


def _head_norm_mxu(x, w):
    sq = x * x
    hi = sq.astype(BF16)
    lo = (sq - hi.astype(F32)).astype(BF16)
    avg = jnp.full((LANE, LANE), 1.0 / HEAD_DIM, BF16)
    ms = jnp.dot(hi, avg, preferred_element_type=F32) + jnp.dot(lo, avg, preferred_element_type=F32)
    return x * lax.rsqrt(ms + EPS) * w


def _rope(x, c, s1, s2):
    return x * c + pltpu.roll(x, LANE - ROPE_HALF, 1) * s1 + pltpu.roll(x, ROPE_HALF, 1) * s2


def _dot_nt(a, b):
    return lax.dot_general(a, b, (((1,), (1,)), ((), ())), preferred_element_type=F32)


def _rope_tables(pos):
    inv = ROPE_THETA ** (-jnp.arange(0, ROPE_DIM, 2, dtype=F32) / ROPE_DIM)
    ang = pos.astype(F32)[:, None] * inv[None, :]
    cos, sin = jnp.cos(ang), jnp.sin(ang)
    n = pos.shape[0]
    c = jnp.concatenate([cos, cos, jnp.ones((n, LANE - ROPE_DIM), F32)], axis=1)
    s1 = jnp.concatenate([-sin, jnp.zeros((n, LANE - ROPE_HALF), F32)], axis=1)
    s2 = jnp.concatenate([jnp.zeros((n, ROPE_HALF), F32), sin, jnp.zeros((n, LANE - ROPE_DIM), F32)], axis=1)
    return c, s1, s2


def _inproj_kernel(x_ref, nw_ref, w_ref, o_ref, h_ref):
    @pl.when(pl.program_id(1) == 0)
    def _():
        x = x_ref[...]
        ms = jnp.mean(x * x, axis=-1, keepdims=True)
        h_ref[...] = (x * lax.rsqrt(ms + EPS) * nw_ref[...]).astype(BF16)

    res = jnp.dot(h_ref[...], w_ref[...], preferred_element_type=F32)
    for c in range(o_ref.shape[0]):
        o_ref[c] = res[:, c * LANE:(c + 1) * LANE]


def _inproj(x2, norm_w, w_perm, layer):
    rows, d = x2.shape
    tm, tn = INPROJ_TM, INPROJ_TN
    nb = tn // LANE
    return pl.pallas_call(
        _inproj_kernel,
        grid=(rows // tm, IN_BLOCKS // nb),
        in_specs=[pl.BlockSpec((tm, d), lambda i, j: (i, 0)),
                  pl.BlockSpec((1, d), lambda i, j: (0, 0)),
                  pl.BlockSpec((None, d, tn), lambda i, j: (layer, 0, j))],
        out_specs=pl.BlockSpec((nb, tm, LANE), lambda i, j: (j, i, 0)),
        out_shape=jax.ShapeDtypeStruct((IN_BLOCKS, rows, LANE), F32),
        scratch_shapes=[pltpu.VMEM((tm, d), BF16)],
        compiler_params=_cparams(("parallel", "arbitrary")),
        name="inproj",
    )(x2, norm_w.reshape(1, d), w_perm)


def _kprep_kernel(mk_ref, mv_ref, nk_ref, mkw_ref, ksw_ref, kww_ref, c_ref, s1_ref, s2_ref,
                  mk_o, mv_o, nk_o, km_o, vst_o):
    c, s1, s2 = c_ref[...], s1_ref[...], s2_ref[...]
    means = []
    for h in range(MOBA_HEADS):
        k = _rope(_head_norm_mxu(mk_ref[h], mkw_ref[...]), c, s1, s2)
        mk_o[h] = k.astype(BF16)
        mv_o[h] = mv_ref[h].astype(BF16).T
        means.append(jnp.mean(k, axis=0, keepdims=True))
    km_o[0] = jnp.concatenate(means, axis=0)
    nk_o[0] = _rope(_head_norm_mxu(nk_ref[0], ksw_ref[...]), c, s1, s2).astype(BF16)
    nk_o[1] = nk_ref[1].astype(BF16)
    vst_o[...] = nk_ref[1].astype(BF16).T
    nk_o[2] = _rope(_head_norm_mxu(nk_ref[2], kww_ref[...]), c, s1, s2).astype(BF16)
    nk_o[3] = nk_ref[3].astype(BF16)


def _kprep(proj3, seq, mk_w, ks_w, kw_w, tabs):
    rows = proj3.shape[1]
    t = MOBA_BLOCK
    nt = seq // t
    blk4 = lambda cb: pl.BlockSpec((4, t, LANE), lambda i: (cb // 4, i, 0))
    wspec = pl.BlockSpec((1, LANE), lambda i: (0, 0))
    tspec = pl.BlockSpec((t, LANE), lambda i: (i % nt, 0))
    out4 = pl.BlockSpec((4, t, LANE), lambda i: (0, i, 0))
    return pl.pallas_call(
        _kprep_kernel,
        grid=(rows // t,),
        in_specs=[blk4(CB_MK), blk4(CB_MV), blk4(CB_NKS), wspec, wspec, wspec, tspec, tspec, tspec],
        out_specs=[out4, pl.BlockSpec((4, None, LANE, t), lambda i: (0, i // nt, 0, i % nt)), out4,
                   pl.BlockSpec((1, MOBA_HEADS, LANE), lambda i: (i, 0, 0)),
                   pl.BlockSpec((None, LANE, t), lambda i: (i // nt, 0, i % nt))],
        out_shape=[jax.ShapeDtypeStruct((4, rows, LANE), BF16),
                   jax.ShapeDtypeStruct((4, rows // seq, LANE, seq), BF16),
                   jax.ShapeDtypeStruct((4, rows, LANE), BF16),
                   jax.ShapeDtypeStruct((rows // t, MOBA_HEADS, LANE), F32),
                   jax.ShapeDtypeStruct((rows // seq, LANE, seq), BF16)],
        compiler_params=_cparams(("parallel",)),
        name="kprep",
    )(proj3, proj3, proj3, mk_w.reshape(1, LANE), ks_w.reshape(1, LANE), kw_w.reshape(1, LANE), *tabs)


def _topk_mask_t(score_t, k):
    n, cols = score_t.shape
    assert n % SUBLANE == 0
    row_in = _iota((SUBLANE, cols), 0)
    groups = [score_t[g:g + SUBLANE] for g in range(0, n, SUBLANE)]
    ranks = [jnp.zeros((SUBLANE, cols), F32) for _ in groups]
    for mm in range(n):
        cm = score_t[mm:mm + 1, :]
        for gi, sc in enumerate(groups):
            lo = gi * SUBLANE
            if lo > mm:
                inc = jnp.where(cm >= sc, 1.0, 0.0)
            elif lo + SUBLANE - 1 <= mm:
                inc = jnp.where(cm > sc, 1.0, 0.0)
            else:
                inc = jnp.where(row_in > mm - lo, jnp.where(cm >= sc, 1.0, 0.0), jnp.where(cm > sc, 1.0, 0.0))
            ranks[gi] = ranks[gi] + inc
    return jnp.concatenate(ranks, axis=0) < k


def _moba_kernel(q_ref, z_ref, qw_ref, c_ref, s1_ref, s2_ref, km_ref, k_ref, vt_ref, o_ref,
                 qt_s, sel_s, s_s, p_s, acc_s):
    i = pl.program_id(1)
    t = MOBA_BLOCK
    nb = km_ref.shape[1]
    nh = q_ref.shape[0]
    npairs = (i + 1) // 2
    blk_t = _iota((nb, t), 0)
    past = blk_t < i
    causal = _iota((t, t), 0) <= _iota((t, t), 1)
    start = pl.multiple_of(i * t, t)

    def put_scores(slot, h, pair):
        src = jnp.minimum(pair, nb // 2 - 1)
        off = pl.multiple_of(src * (2 * t), 2 * t)
        sc = jnp.dot(k_ref[h, pl.ds(off, 2 * t), :], qt_s[h], preferred_element_type=F32)
        row = jnp.minimum(2 * pair, nb - 2)
        s_s[slot, h, :t, :] = sc[:t] + sel_s[h, pl.ds(row, 1), :]
        s_s[slot, h, t:, :] = sc[t:] + sel_s[h, pl.ds(row + 1, 1), :]

    def weighted_values(h, pair):
        off = pl.multiple_of(pair * (2 * t), 2 * t)
        return jnp.dot(vt_ref[h, :, pl.ds(off, 2 * t)], p_s[h], preferred_element_type=F32)

    state = []
    for h in range(nh):
        qf = _rope(_head_norm(q_ref[h], qw_ref[...]), c_ref[...], s1_ref[...], s2_ref[...])
        qt = (qf * SCALE).T.astype(BF16)
        qt_s[h] = qt
        gate_t = lax.dot_general(km_ref[h], qf, (((1,), (1,)), ((), ())), precision=HIGHEST,
                                 preferred_element_type=F32)
        top = _topk_mask_t(jnp.where(past, gate_t, -jnp.inf), MOBA_TOPK)
        sel_s[h] = jnp.where(top & past, 0.0, NEG)
        s = jnp.where(causal, jnp.dot(k_ref[h, pl.ds(start, t), :], qt, preferred_element_type=F32), NEG)
        m = jnp.max(s, axis=0, keepdims=True)
        p = jnp.exp(s - m)
        acc_s[h] = jnp.dot(vt_ref[h, :, pl.ds(start, t)], p.astype(BF16), preferred_element_type=F32)
        state += [m, jnp.sum(p, axis=0, keepdims=True)]
        put_scores(0, h, 0)
        p_s[h] = jnp.zeros((2 * t, t), BF16)

    def trip(pair, carry, src, dst):
        prev = jnp.maximum(pair - 1, 0)
        out = []
        for h in range(nh):
            m, l = carry[2 * h:2 * h + 2]
            put_scores(dst, h, pair + 1)
            acc = acc_s[h] + weighted_values(h, prev)
            s = s_s[src, h]
            m_new = jnp.maximum(m, jnp.max(s, axis=0, keepdims=True))
            p = jnp.exp(s - m_new)
            alpha = jnp.exp(m - m_new)
            l_new = alpha * l + jnp.sum(p, axis=0, keepdims=True)
            acc_s[h] = alpha * acc
            p_s[h] = p.astype(BF16)
            out += [m_new, l_new]
        return tuple(out)

    def body(q, carry):
        return trip(2 * q + 1, trip(2 * q, carry, 0, 1), 1, 0)

    ntrips = 2 * ((npairs + 1) // 2)
    state = lax.fori_loop(0, ntrips // 2, body, tuple(state))
    last = jnp.maximum(ntrips - 1, 0)
    for h in range(nh):
        l = state[2 * h + 1]
        acc = acc_s[h] + weighted_values(h, last)
        z = z_ref[h]
        o_ref[h] = ((acc * (1.0 / l)).T * (z * jax.nn.sigmoid(z))).astype(BF16)


def _moba(proj3, bsz, seq, q_w, tabs, km_t, mk_n, mv_t):
    rows = proj3.shape[1]
    t = MOBA_BLOCK
    nb = seq // t
    nh = MOBA_HEADS
    assert nb % 4 == 0
    qspec = lambda cb: pl.BlockSpec((nh, t, LANE), lambda b, i: (cb // nh, b * nb + i, 0))
    tspec = pl.BlockSpec((t, LANE), lambda b, i: (i, 0))
    return pl.pallas_call(
        _moba_kernel,
        grid=(bsz, nb),
        in_specs=[qspec(CB_MQ), qspec(CB_MZ), pl.BlockSpec((1, LANE), lambda b, i: (0, 0)),
                  tspec, tspec, tspec,
                  pl.BlockSpec((nh, nb, LANE), lambda b, i: (0, b, 0)),
                  pl.BlockSpec((nh, seq, LANE), lambda b, i: (0, b, 0)),
                  pl.BlockSpec((nh, None, LANE, seq), lambda b, i: (0, b, 0, 0))],
        out_specs=pl.BlockSpec((nh, t, LANE), lambda b, i: (0, b * nb + i, 0)),
        out_shape=jax.ShapeDtypeStruct((nh, rows, LANE), BF16),
        scratch_shapes=[pltpu.VMEM((nh, LANE, t), BF16),
                        pltpu.VMEM((nh, nb, t), F32),
                        pltpu.VMEM((2, nh, 2 * t, t), F32),
                        pltpu.VMEM((nh, 2 * t, t), BF16),
                        pltpu.VMEM((nh, LANE, t), F32)],
        compiler_params=_cparams(("parallel", "parallel")),
        name="moba",
    )(proj3, proj3, q_w.reshape(1, LANE), *tabs, km_t, mk_n, mv_t)


def _cmp_kernel(hk_ref, hv_ref, pek_ref, pev_ref, w1k_ref, w2k_ref, w1v_ref, w2v_ref, nw_ref,
                c_ref, s1_ref, s2_ref, ko_ref, vo_ref):
    st = NSA_CMP_STRIDE
    nh = ko_ref.shape[1]

    def compress(x_ref, pe_ref, w1_ref, w2_ref):
        a = jnp.zeros((nh, LANE), F32)
        b = jnp.zeros((nh, LANE), F32)
        for l in range(st):
            x = x_ref[0, pl.ds(l, nh, stride=st), :]
            a = a + jnp.dot((x + pe_ref[l:l + 1, :]).astype(BF16), w1_ref[l * LANE:(l + 1) * LANE, :],
                            preferred_element_type=F32)
            b = b + jnp.dot((x + pe_ref[st + l:st + l + 1, :]).astype(BF16),
                            w1_ref[(st + l) * LANE:(st + l + 1) * LANE, :], preferred_element_type=F32)
        pre = a + pltpu.roll(b, nh - 1, 0)
        return jnp.dot(jax.nn.gelu(pre).astype(BF16), w2_ref[...], preferred_element_type=F32)

    kc = compress(hk_ref, pek_ref, w1k_ref, w2k_ref)
    vc = compress(hv_ref, pev_ref, w1v_ref, w2v_ref)
    ko_ref[0] = _rope(_head_norm(kc, nw_ref[...]), c_ref[...], s1_ref[...], s2_ref[...]).astype(BF16)
    vo_ref[0] = vc.astype(BF16)


def _compress(proj3, bsz, seq, pe_k, pe_v, w1k, w2k, w1v, w2v, kc_w, ctabs):
    nh = seq // NSA_CMP_STRIDE
    wide = NSA_CMP_LEN * HEAD_DIM
    hspec = lambda cb: pl.BlockSpec((1, seq, LANE), lambda b: (cb, b, 0))
    full = lambda shape: pl.BlockSpec(shape, lambda b: tuple(0 for _ in shape))
    ospec = pl.BlockSpec((1, nh, LANE), lambda b: (b, 0, 0))
    return pl.pallas_call(
        _cmp_kernel,
        grid=(bsz,),
        in_specs=[hspec(CB_NKC), hspec(CB_NVC), full((NSA_CMP_LEN, LANE)), full((NSA_CMP_LEN, LANE)),
                  full((wide, LANE)), full((LANE, LANE)), full((wide, LANE)), full((LANE, LANE)),
                  full((1, LANE)), full((nh, LANE)), full((nh, LANE)), full((nh, LANE))],
        out_specs=[ospec, ospec],
        out_shape=[jax.ShapeDtypeStruct((bsz, nh, LANE), BF16)] * 2,
        compiler_params=_cparams(("parallel",)),
        name="nsa_compress",
    )(proj3, proj3, pe_k, pe_v, w1k.astype(BF16), w2k.astype(BF16),
      w1v.astype(BF16), w2v.astype(BF16), kc_w.reshape(1, LANE), *ctabs)


def _nsa_kernel(q_ref, z_ref, g_ref, qw_ref, c_ref, s1_ref, s2_ref, kc_ref, vc_ref, ov_ref,
                ks_ref, vst_ref, kw_ref, vw_ref, o_ref, qb_s, qt_s, acc_s, ocmp_s, osel_s, s_s, p_s):
    i = pl.program_id(1)
    t = NSA_TQ
    ncmp = kc_ref.shape[1]
    nsel = ks_ref.shape[0] // NSA_SEL_BLOCK
    sel_shift = int(math.log2(NSA_SEL_BLOCK))
    trow = i * t + _iota((t, 1), 0)
    row = _iota((t, t), 0)
    col = _iota((t, t), 1)
    causal = col <= row

    c, s1, s2 = c_ref[...], s1_ref[...], s2_ref[...]
    for h in range(NSA_HEADS):
        qb_s[h] = (_rope(_head_norm_mxu(q_ref[h], qw_ref[...]), c, s1, s2) * SCALE).astype(BF16)

    valid = (_iota((t, ncmp), 1) * NSA_CMP_STRIDE + (NSA_CMP_LEN - 1)) <= trow
    kc, vc = kc_ref[0], vc_ref[0]
    psum = jnp.zeros((t, ncmp), F32)
    for h in range(NSA_HEADS):
        s = jnp.where(valid, _dot_nt(qb_s[h], kc), NEG)
        mx = jnp.max(s, axis=1, keepdims=True)
        e = jnp.exp(s - mx)
        p = e * jnp.where(mx > 0.5 * NEG, 1.0 / jnp.sum(e, axis=1, keepdims=True), 0.0)
        ocmp_s[h] = jnp.dot(p.astype(BF16), vc, preferred_element_type=F32)
        psum = psum + p

    imp_t = jnp.dot(psum, ov_ref[...], precision=HIGHEST, preferred_element_type=F32).T[:nsel]
    jj = _iota((nsel, t), 0)
    cur = (i * t + _iota((1, t), 1)) >> sel_shift
    score = jnp.where(jj <= cur, imp_t, -jnp.inf)
    score = jnp.where((jj == 0) | (jj == cur) | (jj == cur - 1), jnp.inf, score)
    top = _topk_mask_t(score, NSA_SEL_TOPN)
    selbias_t = jnp.concatenate([jnp.where(top, 0.0, NEG), jnp.zeros((LANE - nsel - SUBLANE, t), F32),
                                 jnp.full((SUBLANE, t), NEG, F32)], axis=0).astype(BF16)
    ntiles = ks_ref.shape[0] // t
    key_blk = _iota((t, LANE), 0) >> sel_shift

    def tile_bias(kt):
        blk = jnp.where(kt < i, kt * NSA_SEL_PER_TILE + key_blk, LANE - 1)
        onehot = jnp.where(_iota((t, LANE), 1) == blk, 1.0, 0.0).astype(BF16)
        return jnp.dot(onehot, selbias_t, preferred_element_type=F32)

    def score_inputs(kt):
        src = jnp.minimum(kt, ntiles - 1)
        return ks_ref[pl.ds(pl.multiple_of(src * t, t), t), :], tile_bias(kt)

    def put_scores(slot, h, k, bias):
        s_s[slot, h] = jnp.dot(k, qt_s[h], preferred_element_type=F32) + bias

    own = pl.multiple_of(i * t, t)
    own_blk = i * NSA_SEL_PER_TILE + key_blk
    own_bias = jnp.dot(jnp.where(_iota((t, LANE), 1) == own_blk, 1.0, 0.0).astype(BF16), selbias_t,
                       preferred_element_type=F32)
    k_own, vt_own = ks_ref[pl.ds(own, t), :], vst_ref[:, pl.ds(own, t)]
    state = []
    for h in range(NSA_HEADS):
        qt_s[h] = qb_s[h].T
        s = jnp.where(row <= col, jnp.dot(k_own, qt_s[h], preferred_element_type=F32) + own_bias, NEG)
        m = jnp.max(s, axis=0, keepdims=True)
        p = jnp.exp(s - m)
        acc_s[h] = jnp.dot(vt_own, p.astype(BF16), preferred_element_type=F32)
        state += [m, jnp.sum(p, axis=0, keepdims=True)]
        p_s[h] = jnp.zeros((t, t), BF16)
    k_first, bias_first = score_inputs(0)
    for h in range(NSA_HEADS):
        put_scores(0, h, k_first, bias_first)

    def trip(kt, carry, src, dst):
        k_next, bias_next = score_inputs(kt + 1)
        vt_prev = vst_ref[:, pl.ds(pl.multiple_of(jnp.clip(kt - 1, 0, ntiles - 1) * t, t), t)]
        out = []
        for h in range(NSA_HEADS):
            m, l = carry[2 * h:2 * h + 2]
            put_scores(dst, h, k_next, bias_next)
            acc = acc_s[h] + jnp.dot(vt_prev, p_s[h], preferred_element_type=F32)
            s = s_s[src, h]
            m_new = jnp.maximum(m, jnp.max(s, axis=0, keepdims=True))
            p = jnp.exp(s - m_new)
            alpha = jnp.exp(m - m_new)
            acc_s[h] = alpha * acc
            p_s[h] = p.astype(BF16)
            out += [m_new, alpha * l + jnp.sum(p, axis=0, keepdims=True)]
        return tuple(out)

    def sel_body(q, carry):
        return trip(2 * q + 1, trip(2 * q, carry, 0, 1), 1, 0)

    ntrips = 2 * ((i + 1) // 2)
    state = lax.fori_loop(0, ntrips // 2, sel_body, tuple(state))
    vt_last = vst_ref[:, pl.ds(pl.multiple_of(jnp.clip(ntrips - 1, 0, ntiles - 1) * t, t), t)]
    for h in range(NSA_HEADS):
        acc = acc_s[h] + jnp.dot(vt_last, p_s[h], preferred_element_type=F32)
        osel_s[h] = (acc * (1.0 / state[2 * h + 1])).T

    far = pl.multiple_of(jnp.maximum(i - 2, 0) * t, t)
    mid = pl.multiple_of(jnp.maximum(i - 1, 0) * t, t)
    k3 = jnp.concatenate([kw_ref[pl.ds(far, t), :], kw_ref[pl.ds(mid, t), :], kw_ref[pl.ds(own, t), :]], axis=0)
    v3 = jnp.concatenate([vw_ref[pl.ds(far, t), :], vw_ref[pl.ds(mid, t), :], vw_ref[pl.ds(own, t), :]], axis=0)
    allowed = jnp.concatenate([(col > row) & (i >= 2), jnp.full((t, t), True) & (i >= 1), causal], axis=1)
    g = jax.nn.sigmoid(g_ref[0])
    for h in range(NSA_HEADS):
        s = jnp.where(allowed, _dot_nt(qb_s[h], k3), NEG)
        p = jnp.exp(s - jnp.max(s, axis=1, keepdims=True))
        o_win = jnp.dot(p.astype(BF16), v3, preferred_element_type=F32) * (1.0 / jnp.sum(p, axis=1, keepdims=True))
        o = (g[:, 3 * h:3 * h + 1] * ocmp_s[h] + g[:, 3 * h + 1:3 * h + 2] * osel_s[h]
             + g[:, 3 * h + 2:3 * h + 3] * o_win)
        z = z_ref[h]
        o_ref[h] = (o * (z * jax.nn.sigmoid(z))).astype(BF16)


def _nsa(proj3, bsz, seq, q_w, tabs, k_cmp, v_cmp, overlap, nk, vs_t):
    rows = proj3.shape[1]
    t = NSA_TQ
    nt = seq // t
    ncmp = seq // NSA_CMP_STRIDE
    nsel = seq // NSA_SEL_BLOCK
    blk4 = lambda cb: pl.BlockSpec((4, t, LANE), lambda b, i: (cb // 4, b * nt + i, 0))
    tspec = pl.BlockSpec((t, LANE), lambda b, i: (i, 0))
    cspec = pl.BlockSpec((1, ncmp, LANE), lambda b, i: (b, 0, 0))
    kvspec = lambda which: pl.BlockSpec((None, seq, LANE), lambda b, i: (which, b, 0))
    return pl.pallas_call(
        _nsa_kernel,
        grid=(bsz, nt),
        in_specs=[blk4(CB_NQ), blk4(CB_NZ),
                  pl.BlockSpec((1, t, LANE), lambda b, i: (CB_NG, b * nt + i, 0)),
                  pl.BlockSpec((1, LANE), lambda b, i: (0, 0)), tspec, tspec, tspec, cspec, cspec,
                  pl.BlockSpec((ncmp, LANE), lambda b, i: (0, 0)),
                  kvspec(0), pl.BlockSpec((None, LANE, seq), lambda b, i: (b, 0, 0)), kvspec(2), kvspec(3)],
        out_specs=pl.BlockSpec((4, t, LANE), lambda b, i: (0, b * nt + i, 0)),
        out_shape=jax.ShapeDtypeStruct((NSA_HEADS, rows, LANE), BF16),
        scratch_shapes=[pltpu.VMEM((NSA_HEADS, t, LANE), BF16),
                        pltpu.VMEM((NSA_HEADS, LANE, t), BF16),
                        pltpu.VMEM((NSA_HEADS, LANE, t), F32),
                        pltpu.VMEM((NSA_HEADS, t, LANE), F32),
                        pltpu.VMEM((NSA_HEADS, t, LANE), F32),
                        pltpu.VMEM((2, NSA_HEADS, t, t), F32),
                        pltpu.VMEM((NSA_HEADS, t, t), BF16)],
        compiler_params=_cparams(("parallel", "parallel")),
        name="nsa",
    )(proj3, proj3, proj3, q_w.reshape(1, LANE), *tabs, k_cmp, v_cmp, overlap, nk, vs_t, nk, nk)


def _s5_factors(a_re, a_im, b_re, b_im, c_re, c_im, log_dt):
    t = S5_CHUNK
    dt = jnp.exp(log_dt.astype(F32))[:, None]
    ar, ai = a_re.astype(F32), a_im.astype(F32)
    ang = dt * ai
    mag = jnp.exp(dt * ar)
    abar_r, abar_i = mag * jnp.cos(ang), mag * jnp.sin(ang)
    nr, ni = abar_r - 1.0, abar_i
    den = ar * ar + ai * ai
    fr = (nr * ar + ni * ai) / den
    fi = (ni * ar - nr * ai) / den
    bt_r, bt_i = b_re.astype(F32).transpose(0, 2, 1), b_im.astype(F32).transpose(0, 2, 1)
    bbar_r = fr[:, None, :] * bt_r - fi[:, None, :] * bt_i
    bbar_i = fr[:, None, :] * bt_i + fi[:, None, :] * bt_r

    def powers(tau):
        tau = jnp.asarray(tau, F32)[:, None, None]
        pmag = jnp.exp(tau * (dt * ar)[None])
        return pmag * jnp.cos(tau * ang[None]), pmag * jnp.sin(tau * ang[None])

    pw_r, pw_i = powers(np.arange(t + 1))
    cr, ci = c_re.astype(F32), c_im.astype(F32)
    cp_r = cr[None] * pw_r[:, :, None, :] - ci[None] * pw_i[:, :, None, :]
    cp_i = cr[None] * pw_i[:, :, None, :] + ci[None] * pw_r[:, :, None, :]
    klag = jnp.sum(cp_r[:t, :, None, :, :] * bbar_r[None, :, :, None, :]
                   - cp_i[:t, :, None, :, :] * bbar_i[None, :, :, None, :], axis=-1)
    kc = (klag.reshape(t, S5_MBLK, S5_GPB, S5_GROUP, S5_GROUP).transpose(1, 0, 2, 3, 4)
          .reshape(S5_MBLK, t * LANE, S5_GROUP))
    rev_r, rev_i = powers(t - 1 - np.arange(t))
    bp_r = rev_r[:, :, None, :] * bbar_r[None] - rev_i[:, :, None, :] * bbar_i[None]
    bp_i = rev_r[:, :, None, :] * bbar_i[None] + rev_i[:, :, None, :] * bbar_r[None]
    bpc = (jnp.stack([bp_r, bp_i], axis=3).reshape(t, S5_MBLK, LANE, 2 * S5_STATE).transpose(1, 0, 2, 3)
           .reshape(S5_MBLK, t * LANE, 2 * S5_STATE))
    cpo = jnp.stack([cp_r[1:], -cp_i[1:]]).reshape(2, t, S5_MBLK, S5_GPB, S5_GROUP, S5_STATE)
    cpc = cpo.transpose(2, 0, 3, 5, 1, 4).reshape(S5_MBLK, 2 * S5_HALF, t * S5_GROUP)
    a_t = jnp.concatenate([pw_r[t].reshape(S5_MBLK, 1, S5_HALF), pw_i[t].reshape(S5_MBLK, 1, S5_HALF)], axis=2)
    return kc, bpc, cpc, a_t


def _s5_wgen_kernel(kc_ref, bp_ref, cp_ref, win_ref, wc_ref):
    t = S5_CHUNK
    wide = t * LANE
    gs, ps = int(math.log2(S5_GROUP)), int(math.log2(S5_STATE))

    def expand(x, copy_mask, group_mask):
        ex = jnp.where(copy_mask, 1.0, 0.0).astype(BF16)
        return jnp.where(group_mask, jnp.dot(x.astype(BF16), ex, preferred_element_type=F32), 0.0).astype(BF16)

    r, c = _iota((S5_GROUP, LANE), 0), _iota((S5_GROUP, LANE), 1)
    copy_k = r == (c & (S5_GROUP - 1))
    r, c = _iota((LANE, LANE), 0), _iota((LANE, LANE), 1)
    same_k = (r >> gs) == (c >> gs)
    zero = jnp.zeros((LANE, LANE), BF16)
    for lag in range(t):
        bd = expand(kc_ref[0, lag * LANE:(lag + 1) * LANE, :], copy_k, same_k)
        for s in range(t - lag):
            win_ref[0, s * LANE:(s + 1) * LANE, (s + lag) * LANE:(s + lag + 1) * LANE] = bd
    for s in range(1, t):
        for tt in range(s):
            win_ref[0, s * LANE:(s + 1) * LANE, tt * LANE:(tt + 1) * LANE] = zero
    r, c = _iota((LANE, 2 * S5_HALF), 0), _iota((LANE, 2 * S5_HALF), 1)
    copy_b = ((r >> ps) == (c >> (ps + 3))) & ((r & (S5_STATE - 1)) == (c & (S5_STATE - 1)))
    same_b = (r >> gs) == ((c >> ps) & (S5_GPB - 1))
    for s in range(t):
        win_ref[0, s * LANE:(s + 1) * LANE, wide:] = expand(bp_ref[0, s * LANE:(s + 1) * LANE, :],
                                                            copy_b, same_b)
    rows = 2 * S5_STATE * 2
    r, c = _iota((t * S5_GROUP, wide), 0), _iota((t * S5_GROUP, wide), 1)
    copy_c = ((r >> gs) == (c >> (gs + 3))) & ((r & (S5_GROUP - 1)) == (c & (S5_GROUP - 1)))
    for ch in range(2 * S5_HALF // rows):
        r, c = ch * rows + _iota((rows, wide), 0), _iota((rows, wide), 1)
        same_c = ((r >> ps) & (S5_GPB - 1)) == ((c >> gs) & (S5_GPB - 1))
        wc_ref[0, ch * rows:(ch + 1) * rows, :] = expand(cp_ref[0, ch * rows:(ch + 1) * rows, :],
                                                         copy_c, same_c)


def _s5_wgen(kc, bpc, cpc):
    t = S5_CHUNK
    wide = t * LANE
    spec = lambda a: pl.BlockSpec((1,) + a.shape[1:], lambda m: (m, 0, 0))
    return pl.pallas_call(
        _s5_wgen_kernel,
        grid=(kc.shape[0],),
        in_specs=[spec(kc), spec(bpc), spec(cpc)],
        out_specs=[pl.BlockSpec((1, wide, wide + 2 * S5_HALF), lambda m: (m, 0, 0)),
                   pl.BlockSpec((1, 2 * S5_HALF, wide), lambda m: (m, 0, 0))],
        out_shape=[jax.ShapeDtypeStruct((kc.shape[0], wide, wide + 2 * S5_HALF), BF16),
                   jax.ShapeDtypeStruct((kc.shape[0], 2 * S5_HALF, wide), BF16)],
        compiler_params=_cparams(("parallel",)),
        name="s5_wgen",
    )(kc, bpc, cpc)


def _s5_in_kernel(u_ref, w_ref, d_ref, y_ref, b_ref):
    tr = y_ref.shape[1]
    u = jnp.concatenate([u_ref[0, pl.ds(s, tr, stride=S5_CHUNK), :] for s in range(S5_CHUNK)], axis=1)
    ub = u.astype(BF16)
    wide = y_ref.shape[2]
    for j in range(S5_CHUNK // 2):
        kk = (2 * j + 2) * LANE
        cols = slice(2 * j * LANE, kk)
        y_ref[0, :, cols] = (jnp.dot(ub[:, :kk], w_ref[0, :kk, cols], preferred_element_type=F32)
                             + d_ref[0, :, cols] * u[:, cols])
    b_ref[0] = jnp.dot(ub, w_ref[0, :, wide:], preferred_element_type=F32)


def _s5_in(proj3, w_in, d_t, layer):
    rows = proj3.shape[1] // S5_CHUNK
    wide = S5_CHUNK * LANE
    tr = min(rows, ROW_TILE)
    return pl.pallas_call(
        _s5_in_kernel,
        grid=(S5_MBLK, rows // tr),
        in_specs=[pl.BlockSpec((1, tr * S5_CHUNK, LANE), lambda m, r: (CB_SU + m, r, 0)),
                  pl.BlockSpec((1, wide, wide + 2 * S5_HALF), lambda m, r: (layer * S5_MBLK + m, 0, 0)),
                  pl.BlockSpec((1, 1, wide), lambda m, r: (m, 0, 0))],
        out_specs=[pl.BlockSpec((1, tr, wide), lambda m, r: (m, r, 0)),
                   pl.BlockSpec((1, tr, 2 * S5_HALF), lambda m, r: (m, r, 0))],
        out_shape=[jax.ShapeDtypeStruct((S5_MBLK, rows, wide), F32),
                   jax.ShapeDtypeStruct((S5_MBLK, rows, 2 * S5_HALF), F32)],
        compiler_params=_cparams(("parallel", "parallel")),
        name="s5_in",
    )(proj3, w_in, d_t)


def _s5_scan_kernel(b_ref, a_ref, o_ref, *, bsz, nk):
    nm = b_ref.shape[0]

    def body(k, carry):
        out = []
        for mi in range(nm):
            ar = a_ref[mi, :, :S5_HALF]
            ai = a_ref[mi, :, S5_HALF:]
            for b in range(bsz):
                sr, si = carry[2 * (mi * bsz + b)], carry[2 * (mi * bsz + b) + 1]
                row = b * nk + k
                o_ref[mi, pl.ds(row, 1), :] = jnp.concatenate([sr, si], axis=1)
                x = b_ref[mi, pl.ds(row, 1), :]
                out.append(ar * sr - ai * si + x[:, :S5_HALF])
                out.append(ar * si + ai * sr + x[:, S5_HALF:])
        return tuple(out)

    zero = jnp.zeros((1, S5_HALF), F32)
    lax.fori_loop(0, nk, body, tuple(zero for _ in range(2 * bsz * nm)))


def _s5_scan(bst, a_t, bsz):
    _, rows, wide = bst.shape
    nm = 2
    spec = pl.BlockSpec((nm, rows, wide), lambda m: (m, 0, 0))
    return pl.pallas_call(
        functools.partial(_s5_scan_kernel, bsz=bsz, nk=rows // bsz),
        grid=(S5_MBLK // nm,),
        in_specs=[spec, pl.BlockSpec((nm, 1, wide), lambda m: (m, 0, 0))],
        out_specs=spec,
        out_shape=jax.ShapeDtypeStruct(bst.shape, F32),
        compiler_params=_cparams(("parallel",)),
        name="s5_scan",
    )(bst, a_t)


def _s5_out_kernel(y_ref, s_ref, w_ref, o_ref):
    tr = y_ref.shape[1]
    y = jax.nn.gelu(y_ref[0] + jnp.dot(s_ref[0].astype(BF16), w_ref[0], preferred_element_type=F32))
    for s in range(S5_CHUNK):
        o_ref[0, pl.ds(s, tr, stride=S5_CHUNK), :] = y[:, s * LANE:(s + 1) * LANE]


def _s5_out(y_intra, s_prev, w_c, layer):
    _, rows, wide = y_intra.shape
    tr = min(rows, ROW_TILE)
    return pl.pallas_call(
        _s5_out_kernel,
        grid=(S5_MBLK, rows // tr),
        in_specs=[pl.BlockSpec((1, tr, wide), lambda m, r: (m, r, 0)),
                  pl.BlockSpec((1, tr, 2 * S5_HALF), lambda m, r: (m, r, 0)),
                  pl.BlockSpec((1, 2 * S5_HALF, wide), lambda m, r: (layer * S5_MBLK + m, 0, 0))],
        out_specs=pl.BlockSpec((1, tr * S5_CHUNK, LANE), lambda m, r: (m, r, 0)),
        out_shape=jax.ShapeDtypeStruct((S5_MBLK, rows * S5_CHUNK, LANE), F32),
        compiler_params=_cparams(("parallel", "parallel")),
        name="s5_out",
    )(y_intra, s_prev, w_c)


def _glu_kernel(y_ref, z_ref, w_ref, o_ref):
    nb = y_ref.shape[0]
    y = jnp.concatenate([y_ref[c] for c in range(nb)], axis=1)
    z = jnp.concatenate([z_ref[c] for c in range(nb)], axis=1)
    gate = jax.nn.sigmoid(jnp.dot(y.astype(BF16), w_ref[...], preferred_element_type=F32))
    o = (y * gate * (z * jax.nn.sigmoid(z))).astype(BF16)
    for c in range(nb):
        o_ref[c] = o[:, c * LANE:(c + 1) * LANE]


def _glu(y5, proj3, glu_w):
    nb, rows, _ = y5.shape
    tm = ROW_TILE
    spec = lambda blk: pl.BlockSpec((nb, tm, LANE), lambda i: (blk, i, 0))
    return pl.pallas_call(
        _glu_kernel,
        grid=(rows // tm,),
        in_specs=[spec(0), spec(CB_SZ // nb), pl.BlockSpec((S5_WIDTH, S5_WIDTH), lambda i: (0, 0))],
        out_specs=spec(0),
        out_shape=jax.ShapeDtypeStruct((nb, rows, LANE), BF16),
        compiler_params=_cparams(("parallel",)),
        name="s5_glu",
    )(y5, proj3, glu_w.astype(BF16))


def _outproj_kernel(a_ref, b_ref, c_ref, w_ref, x_ref, o_ref):
    parts = ([a_ref[h] for h in range(a_ref.shape[0])] + [b_ref[h] for h in range(b_ref.shape[0])]
             + [c_ref[h] for h in range(c_ref.shape[0])])
    mixed = jnp.concatenate(parts, axis=1)
    o_ref[...] = x_ref[...] + jnp.dot(mixed, w_ref[...], preferred_element_type=F32)


def _outproj(m_moba, m_nsa, m_s5, w_out, x2):
    rows, d = x2.shape
    tm = ROW_TILE
    lspec = lambda n: pl.BlockSpec((n, tm, LANE), lambda i: (0, i, 0))
    return pl.pallas_call(
        _outproj_kernel,
        grid=(rows // tm,),
        in_specs=[lspec(m_moba.shape[0]), lspec(m_nsa.shape[0]), lspec(m_s5.shape[0]),
                  pl.BlockSpec(w_out.shape, lambda i: (0, 0)),
                  pl.BlockSpec((tm, d), lambda i: (i, 0))],
        out_specs=pl.BlockSpec((tm, d), lambda i: (i, 0)),
        out_shape=jax.ShapeDtypeStruct((rows, d), F32),
        compiler_params=_cparams(("parallel",)),
        name="outproj",
    )(m_moba, m_nsa, m_s5, w_out.astype(BF16), x2)


def _w_in_source(blk):
    mw, nw, kvw, ng = MOBA_HEADS * HEAD_DIM, NSA_HEADS * HEAD_DIM, HEAD_DIM, 3 * NSA_HEADS
    o_kv = 4 * mw + nw
    o_ng = o_kv + 6 * kvw
    o_nz = o_ng + ng
    src = jnp.where(blk < CB_NZ, blk * LANE,
          jnp.where(blk < CB_NKS, o_nz + (blk - CB_NZ) * LANE,
          jnp.where(blk < CB_NKC, o_kv + 2 * kvw + (blk - CB_NKS) * LANE,
          jnp.where(blk < CB_NG, o_kv + (blk - CB_NKC) * LANE, o_ng))))
    real = jnp.where(blk < CB_NG, LANE, jnp.where(blk == CB_NG, ng, 0))
    return src, real


def _wperm_kernel(w_ref, o_ref):
    _, real = _w_in_source(pl.program_id(0))
    for layer in range(o_ref.shape[0]):
        x = w_ref[:, layer, :]
        x = jnp.where(_iota(x.shape, 0) < real, x, 0.0)
        o_ref[layer] = x.T.astype(BF16)


def _permute_w_in(w_in_all):
    wt = jnp.transpose(w_in_all, (2, 0, 1))
    _, nl, d = wt.shape
    return pl.pallas_call(
        _wperm_kernel,
        grid=(IN_BLOCKS,),
        in_specs=[pl.BlockSpec((pl.Element(LANE), pl.Element(nl), pl.Element(d)),
                               lambda c: (_w_in_source(c)[0], 0, 0))],
        out_specs=pl.BlockSpec((nl, d, LANE), lambda c: (0, 0, c)),
        out_shape=jax.ShapeDtypeStruct((nl, d, IN_BLOCKS * LANE), BF16),
        compiler_params=_cparams(("parallel",)),
        name="w_in_permute",
    )(wt)


def _layer(x2, bsz, seq, tabs, ctabs, overlap, w_in, s5_w, layer, norm_w, w_out, moba_q_norm, moba_k_norm,
           nsa_q_norm, nsa_kc_norm, nsa_ks_norm, nsa_kw_norm, nsa_pe_k, nsa_pe_v, nsa_cmp_k_w1, nsa_cmp_k_w2,
           nsa_cmp_v_w1, nsa_cmp_v_w2, s5_d, s5_glu_w):
    proj3 = _inproj(x2, norm_w, w_in, layer)
    mk_n, mv_b, nk, kmean, vs_t = _kprep(proj3, seq, moba_k_norm, nsa_ks_norm, nsa_kw_norm, tabs)
    m_moba = _moba(proj3, bsz, seq, moba_q_norm, tabs, kmean.transpose(1, 0, 2), mk_n, mv_b)
    k_cmp, v_cmp = _compress(proj3, bsz, seq, nsa_pe_k, nsa_pe_v, nsa_cmp_k_w1, nsa_cmp_k_w2,
                             nsa_cmp_v_w1, nsa_cmp_v_w2, nsa_kc_norm, ctabs)
    m_nsa = _nsa(proj3, bsz, seq, nsa_q_norm, tabs, k_cmp, v_cmp, overlap, nk, vs_t)
    w_s5_in, w_s5_out, a_t = s5_w
    d_t = jnp.tile(s5_d.astype(F32).reshape(S5_MBLK, 1, LANE), (1, 1, S5_CHUNK))
    y_intra, bst = _s5_in(proj3, w_s5_in, d_t, layer)
    y5 = _s5_out(y_intra, _s5_scan(bst, a_t[layer], bsz), w_s5_out, layer)
    m_s5 = _glu(y5, proj3, s5_glu_w)
    return _outproj(m_moba, m_nsa, m_s5, w_out, x2)


def kernel(x, norm_w, w_in, w_out, moba_q_norm, moba_k_norm, nsa_q_norm, nsa_kc_norm, nsa_ks_norm, nsa_kw_norm, nsa_pe_k, nsa_pe_v, nsa_cmp_k_w1, nsa_cmp_k_w2, nsa_cmp_v_w1, nsa_cmp_v_w2, s5_a_re, s5_a_im, s5_b_re, s5_b_im, s5_c_re, s5_c_im, s5_d, s5_log_dt, s5_glu_w):
    bsz, seq, d = x.shape
    tabs = _rope_tables(jnp.arange(seq, dtype=F32))
    ncmp = seq // NSA_CMP_STRIDE
    ctabs = _rope_tables(jnp.arange(ncmp, dtype=F32) * NSA_CMP_STRIDE + (NSA_CMP_LEN - 1))
    nsel = seq // NSA_SEL_BLOCK
    ci = np.arange(ncmp)[:, None] * NSA_CMP_STRIDE
    sj = np.arange(nsel)[None, :] * NSA_SEL_BLOCK
    overlap = np.zeros((ncmp, LANE), np.float32)
    overlap[:, :nsel] = (ci < sj + NSA_SEL_BLOCK) & (ci + NSA_CMP_LEN > sj)
    overlap = jnp.asarray(overlap)
    params = (norm_w, w_out, moba_q_norm, moba_k_norm, nsa_q_norm, nsa_kc_norm, nsa_ks_norm,
              nsa_kw_norm, nsa_pe_k, nsa_pe_v, nsa_cmp_k_w1, nsa_cmp_k_w2, nsa_cmp_v_w1, nsa_cmp_v_w2,
              s5_d, s5_glu_w)
    kc, bpc, cpc, a_t = jax.vmap(_s5_factors)(s5_a_re, s5_a_im, s5_b_re, s5_b_im, s5_c_re, s5_c_im, s5_log_dt)
    merge = lambda a: a.reshape((a.shape[0] * a.shape[1],) + a.shape[2:])
    s5_w = tuple(_s5_wgen(merge(kc), merge(bpc), merge(cpc))) + (a_t,)
    x2 = x.reshape(bsz * seq, d)
    w_perm = _permute_w_in(w_in)
    for layer in range(norm_w.shape[0]):
        x2 = _layer(x2, bsz, seq, tabs, ctabs, overlap, w_perm, s5_w, layer, *[p[layer] for p in params])
    return x2.reshape(bsz, seq, d)
```

```python
import functools
import math

import numpy as np
import jax
import jax.numpy as jnp
from jax import lax
from jax.experimental import pallas as pl
from jax.experimental.pallas import tpu as pltpu

F32 = jnp.float32
BF16 = jnp.bfloat16
HIGHEST = lax.Precision.HIGHEST

LANE = 128
SUBLANE = 8
HEAD_DIM = 128
ROPE_DIM = HEAD_DIM // 4
ROPE_HALF = ROPE_DIM // 2
ROPE_THETA = 500000.0
EPS = 1e-6
SCALE = HEAD_DIM ** -0.5
NEG = -1e30

MOBA_HEADS = 4
MOBA_BLOCK = 256
MOBA_TOPK = 3

NSA_HEADS = 4
NSA_CMP_LEN = 32
NSA_CMP_STRIDE = 16
NSA_SEL_BLOCK = 64
NSA_SEL_TOPN = 16
NSA_WINDOW = 512
NSA_TQ = 256
NSA_SEL_PER_TILE = NSA_TQ // NSA_SEL_BLOCK
assert NSA_WINDOW == 2 * NSA_TQ

S5_WIDTH = 1024
S5_GROUP = 16
S5_GROUPS = S5_WIDTH // S5_GROUP
S5_STATE = 64
S5_CHUNK = 16
S5_MBLK = S5_WIDTH // LANE
S5_GPB = LANE // S5_GROUP
S5_HALF = S5_GPB * S5_STATE

CB_MQ, CB_MK, CB_MV, CB_MZ, CB_NQ, CB_NZ, CB_SU, CB_SZ = 0, 4, 8, 12, 16, 20, 24, 32
CB_NKS, CB_NVS, CB_NKW, CB_NVW, CB_NKC, CB_NVC, CB_NG = 40, 41, 42, 43, 44, 45, 46
IN_BLOCKS = 48

VMEM_LIMIT = 56 * 1024 * 1024
INPROJ_TM, INPROJ_TN = 1024, 1536
ROW_TILE = 512


def _cparams(sem):
    return pltpu.CompilerParams(dimension_semantics=sem, vmem_limit_bytes=VMEM_LIMIT)


def _iota(shape, dim):
    return lax.broadcasted_iota(jnp.int32, shape, dim)


def _head_norm(x, w):
    return x * lax.rsqrt(jnp.mean(x * x, axis=-1, keepdims=True) + EPS) * w


def _head_norm_mxu(x, w):
    sq = x * x
    hi = sq.astype(BF16)
    lo = (sq - hi.astype(F32)).astype(BF16)
    avg = jnp.full((LANE, LANE), 1.0 / HEAD_DIM, BF16)
    ms = jnp.dot(hi, avg, preferred_element_type=F32) + jnp.dot(lo, avg, preferred_element_type=F32)
    return x * lax.rsqrt(ms + EPS) * w


def _rope(x, c, s1, s2):
    return x * c + pltpu.roll(x, LANE - ROPE_HALF, 1) * s1 + pltpu.roll(x, ROPE_HALF, 1) * s2


def _dot_nt(a, b):
    return lax.dot_general(a, b, (((1,), (1,)), ((), ())), preferred_element_type=F32)


def _rope_tables(pos):
    inv = ROPE_THETA ** (-jnp.arange(0, ROPE_DIM, 2, dtype=F32) / ROPE_DIM)
    ang = pos.astype(F32)[:, None] * inv[None, :]
    cos, sin = jnp.cos(ang), jnp.sin(ang)
    n = pos.shape[0]
    c = jnp.concatenate([cos, cos, jnp.ones((n, LANE - ROPE_DIM), F32)], axis=1)
    s1 = jnp.concatenate([-sin, jnp.zeros((n, LANE - ROPE_HALF), F32)], axis=1)
    s2 = jnp.concatenate([jnp.zeros((n, ROPE_HALF), F32), sin, jnp.zeros((n, LANE - ROPE_DIM), F32)], axis=1)
    return c, s1, s2


def _inproj_kernel(x_ref, nw_ref, w_ref, o_ref, h_ref):
    @pl.when(pl.program_id(1) == 0)
    def _():
        x = x_ref[...]
        ms = jnp.mean(x * x, axis=-1, keepdims=True)
        h_ref[...] = (x * lax.rsqrt(ms + EPS) * nw_ref[...]).astype(BF16)

    res = jnp.dot(h_ref[...], w_ref[...], preferred_element_type=F32)
    for c in range(o_ref.shape[0]):
        o_ref[c] = res[:, c * LANE:(c + 1) * LANE]


def _inproj(x2, norm_w, w_perm, layer):
    rows, d = x2.shape
    tm, tn = INPROJ_TM, INPROJ_TN
    nb = tn // LANE
    return pl.pallas_call(
        _inproj_kernel,
        grid=(rows // tm, IN_BLOCKS // nb),
        in_specs=[pl.BlockSpec((tm, d), lambda i, j: (i, 0)),
                  pl.BlockSpec((1, d), lambda i, j: (0, 0)),
                  pl.BlockSpec((None, d, tn), lambda i, j: (layer, 0, j))],
        out_specs=pl.BlockSpec((nb, tm, LANE), lambda i, j: (j, i, 0)),
        out_shape=jax.ShapeDtypeStruct((IN_BLOCKS, rows, LANE), F32),
        scratch_shapes=[pltpu.VMEM((tm, d), BF16)],
        compiler_params=_cparams(("parallel", "arbitrary")),
        name="inproj",
    )(x2, norm_w.reshape(1, d), w_perm)


def _kprep_kernel(mk_ref, mv_ref, nk_ref, mkw_ref, ksw_ref, kww_ref, c_ref, s1_ref, s2_ref,
                  mk_o, mv_o, nk_o, km_o, vst_o):
    c, s1, s2 = c_ref[...], s1_ref[...], s2_ref[...]
    means = []
    for h in range(MOBA_HEADS):
        k = _rope(_head_norm_mxu(mk_ref[h], mkw_ref[...]), c, s1, s2)
        mk_o[h] = k.astype(BF16)
        mv_o[h] = mv_ref[h].astype(BF16).T
        means.append(jnp.mean(k, axis=0, keepdims=True))
    km_o[0] = jnp.concatenate(means, axis=0)
    nk_o[0] = _rope(_head_norm_mxu(nk_ref[0], ksw_ref[...]), c, s1, s2).astype(BF16)
    nk_o[1] = nk_ref[1].astype(BF16)
    vst_o[...] = nk_ref[1].astype(BF16).T
    nk_o[2] = _rope(_head_norm_mxu(nk_ref[2], kww_ref[...]), c, s1, s2).astype(BF16)
    nk_o[3] = nk_ref[3].astype(BF16)


def _kprep(proj3, seq, mk_w, ks_w, kw_w, tabs):
    rows = proj3.shape[1]
    t = MOBA_BLOCK
    nt = seq // t
    blk4 = lambda cb: pl.BlockSpec((4, t, LANE), lambda i: (cb // 4, i, 0))
    wspec = pl.BlockSpec((1, LANE), lambda i: (0, 0))
    tspec = pl.BlockSpec((t, LANE), lambda i: (i % nt, 0))
    out4 = pl.BlockSpec((4, t, LANE), lambda i: (0, i, 0))
    return pl.pallas_call(
        _kprep_kernel,
        grid=(rows // t,),
        in_specs=[blk4(CB_MK), blk4(CB_MV), blk4(CB_NKS), wspec, wspec, wspec, tspec, tspec, tspec],
        out_specs=[out4, pl.BlockSpec((4, None, LANE, t), lambda i: (0, i // nt, 0, i % nt)), out4,
                   pl.BlockSpec((1, MOBA_HEADS, LANE), lambda i: (i, 0, 0)),
                   pl.BlockSpec((None, LANE, t), lambda i: (i // nt, 0, i % nt))],
        out_shape=[jax.ShapeDtypeStruct((4, rows, LANE), BF16),
                   jax.ShapeDtypeStruct((4, rows // seq, LANE, seq), BF16),
                   jax.ShapeDtypeStruct((4, rows, LANE), BF16),
                   jax.ShapeDtypeStruct((rows // t, MOBA_HEADS, LANE), F32),
                   jax.ShapeDtypeStruct((rows // seq, LANE, seq), BF16)],
        compiler_params=_cparams(("parallel",)),
        name="kprep",
    )(proj3, proj3, proj3, mk_w.reshape(1, LANE), ks_w.reshape(1, LANE), kw_w.reshape(1, LANE), *tabs)


def _topk_mask_t(score_t, k):
    n, cols = score_t.shape
    assert n % SUBLANE == 0
    row_in = _iota((SUBLANE, cols), 0)
    groups = [score_t[g:g + SUBLANE] for g in range(0, n, SUBLANE)]
    ranks = [jnp.zeros((SUBLANE, cols), F32) for _ in groups]
    for mm in range(n):
        cm = score_t[mm:mm + 1, :]
        for gi, sc in enumerate(groups):
            lo = gi * SUBLANE
            if lo > mm:
                inc = jnp.where(cm >= sc, 1.0, 0.0)
            elif lo + SUBLANE - 1 <= mm:
                inc = jnp.where(cm > sc, 1.0, 0.0)
            else:
                inc = jnp.where(row_in > mm - lo, jnp.where(cm >= sc, 1.0, 0.0), jnp.where(cm > sc, 1.0, 0.0))
            ranks[gi] = ranks[gi] + inc
    return jnp.concatenate(ranks, axis=0) < k


def _moba_kernel(q_ref, z_ref, qw_ref, c_ref, s1_ref, s2_ref, km_ref, k_ref, vt_ref, o_ref,
                 qt_s, sel_s, s_s, p_s, acc_s):
    i = pl.program_id(1)
    t = MOBA_BLOCK
    nb = km_ref.shape[1]
    nh = q_ref.shape[0]
    npairs = (i + 1) // 2
    blk_t = _iota((nb, t), 0)
    past = blk_t < i
    causal = _iota((t, t), 0) <= _iota((t, t), 1)
    start = pl.multiple_of(i * t, t)

    def put_scores(slot, h, pair):
        src = jnp.minimum(pair, nb // 2 - 1)
        off = pl.multiple_of(src * (2 * t), 2 * t)
        sc = jnp.dot(k_ref[h, pl.ds(off, 2 * t), :], qt_s[h], preferred_element_type=F32)
        row = jnp.minimum(2 * pair, nb - 2)
        s_s[slot, h, :t, :] = sc[:t] + sel_s[h, pl.ds(row, 1), :]
        s_s[slot, h, t:, :] = sc[t:] + sel_s[h, pl.ds(row + 1, 1), :]

    def weighted_values(h, pair):
        off = pl.multiple_of(pair * (2 * t), 2 * t)
        return jnp.dot(vt_ref[h, :, pl.ds(off, 2 * t)], p_s[h], preferred_element_type=F32)

    state = []
    for h in range(nh):
        qf = _rope(_head_norm(q_ref[h], qw_ref[...]), c_ref[...], s1_ref[...], s2_ref[...])
        qt = (qf * SCALE).T.astype(BF16)
        qt_s[h] = qt
        gate_t = lax.dot_general(km_ref[h], qf, (((1,), (1,)), ((), ())), precision=HIGHEST,
                                 preferred_element_type=F32)
        top = _topk_mask_t(jnp.where(past, gate_t, -jnp.inf), MOBA_TOPK)
        sel_s[h] = jnp.where(top & past, 0.0, NEG)
        s = jnp.where(causal, jnp.dot(k_ref[h, pl.ds(start, t), :], qt, preferred_element_type=F32), NEG)
        m = jnp.max(s, axis=0, keepdims=True)
        p = jnp.exp(s - m)
        acc_s[h] = jnp.dot(vt_ref[h, :, pl.ds(start, t)], p.astype(BF16), preferred_element_type=F32)
        state += [m, jnp.sum(p, axis=0, keepdims=True)]
        put_scores(0, h, 0)
        p_s[h] = jnp.zeros((2 * t, t), BF16)

    def trip(pair, carry, src, dst):
        prev = jnp.maximum(pair - 1, 0)
        out = []
        for h in range(nh):
            m, l = carry[2 * h:2 * h + 2]
            put_scores(dst, h, pair + 1)
            acc = acc_s[h] + weighted_values(h, prev)
            s = s_s[src, h]
            m_new = jnp.maximum(m, jnp.max(s, axis=0, keepdims=True))
            p = jnp.exp(s - m_new)
            alpha = jnp.exp(m - m_new)
            l_new = alpha * l + jnp.sum(p, axis=0, keepdims=True)
            acc_s[h] = alpha * acc
            p_s[h] = p.astype(BF16)
            out += [m_new, l_new]
        return tuple(out)

    def body(q, carry):
        return trip(2 * q + 1, trip(2 * q, carry, 0, 1), 1, 0)

    ntrips = 2 * ((npairs + 1) // 2)
    state = lax.fori_loop(0, ntrips // 2, body, tuple(state))
    last = jnp.maximum(ntrips - 1, 0)
    for h in range(nh):
        l = state[2 * h + 1]
        acc = acc_s[h] + weighted_values(h, last)
        z = z_ref[h]
        o_ref[h] = ((acc * (1.0 / l)).T * (z * jax.nn.sigmoid(z))).astype(BF16)


def _moba(proj3, bsz, seq, q_w, tabs, km_t, mk_n, mv_t):
    rows = proj3.shape[1]
    t = MOBA_BLOCK
    nb = seq // t
    nh = MOBA_HEADS
    assert nb % 4 == 0
    qspec = lambda cb: pl.BlockSpec((nh, t, LANE), lambda b, i: (cb // nh, b * nb + i, 0))
    tspec = pl.BlockSpec((t, LANE), lambda b, i: (i, 0))
    return pl.pallas_call(
        _moba_kernel,
        grid=(bsz, nb),
        in_specs=[qspec(CB_MQ), qspec(CB_MZ), pl.BlockSpec((1, LANE), lambda b, i: (0, 0)),
                  tspec, tspec, tspec,
                  pl.BlockSpec((nh, nb, LANE), lambda b, i: (0, b, 0)),
                  pl.BlockSpec((nh, seq, LANE), lambda b, i: (0, b, 0)),
                  pl.BlockSpec((nh, None, LANE, seq), lambda b, i: (0, b, 0, 0))],
        out_specs=pl.BlockSpec((nh, t, LANE), lambda b, i: (0, b * nb + i, 0)),
        out_shape=jax.ShapeDtypeStruct((nh, rows, LANE), BF16),
        scratch_shapes=[pltpu.VMEM((nh, LANE, t), BF16),
                        pltpu.VMEM((nh, nb, t), F32),
                        pltpu.VMEM((2, nh, 2 * t, t), F32),
                        pltpu.VMEM((nh, 2 * t, t), BF16),
                        pltpu.VMEM((nh, LANE, t), F32)],
        compiler_params=_cparams(("parallel", "parallel")),
        name="moba",
    )(proj3, proj3, q_w.reshape(1, LANE), *tabs, km_t, mk_n, mv_t)


def _cmp_kernel(hk_ref, hv_ref, pek_ref, pev_ref, w1k_ref, w2k_ref, w1v_ref, w2v_ref, nw_ref,
                c_ref, s1_ref, s2_ref, ko_ref, vo_ref):
    st = NSA_CMP_STRIDE
    nh = ko_ref.shape[1]

    def compress(x_ref, pe_ref, w1_ref, w2_ref):
        a = jnp.zeros((nh, LANE), F32)
        b = jnp.zeros((nh, LANE), F32)
        for l in range(st):
            x = x_ref[0, pl.ds(l, nh, stride=st), :]
            a = a + jnp.dot((x + pe_ref[l:l + 1, :]).astype(BF16), w1_ref[l * LANE:(l + 1) * LANE, :],
                            preferred_element_type=F32)
            b = b + jnp.dot((x + pe_ref[st + l:st + l + 1, :]).astype(BF16),
                            w1_ref[(st + l) * LANE:(st + l + 1) * LANE, :], preferred_element_type=F32)
        pre = a + pltpu.roll(b, nh - 1, 0)
        return jnp.dot(jax.nn.gelu(pre).astype(BF16), w2_ref[...], preferred_element_type=F32)

    kc = compress(hk_ref, pek_ref, w1k_ref, w2k_ref)
    vc = compress(hv_ref, pev_ref, w1v_ref, w2v_ref)
    ko_ref[0] = _rope(_head_norm(kc, nw_ref[...]), c_ref[...], s1_ref[...], s2_ref[...]).astype(BF16)
    vo_ref[0] = vc.astype(BF16)


def _compress(proj3, bsz, seq, pe_k, pe_v, w1k, w2k, w1v, w2v, kc_w, ctabs):
    nh = seq // NSA_CMP_STRIDE
    wide = NSA_CMP_LEN * HEAD_DIM
    hspec = lambda cb: pl.BlockSpec((1, seq, LANE), lambda b: (cb, b, 0))
    full = lambda shape: pl.BlockSpec(shape, lambda b: tuple(0 for _ in shape))
    ospec = pl.BlockSpec((1, nh, LANE), lambda b: (b, 0, 0))
    return pl.pallas_call(
        _cmp_kernel,
        grid=(bsz,),
        in_specs=[hspec(CB_NKC), hspec(CB_NVC), full((NSA_CMP_LEN, LANE)), full((NSA_CMP_LEN, LANE)),
                  full((wide, LANE)), full((LANE, LANE)), full((wide, LANE)), full((LANE, LANE)),
                  full((1, LANE)), full((nh, LANE)), full((nh, LANE)), full((nh, LANE))],
        out_specs=[ospec, ospec],
        out_shape=[jax.ShapeDtypeStruct((bsz, nh, LANE), BF16)] * 2,
        compiler_params=_cparams(("parallel",)),
        name="nsa_compress",
    )(proj3, proj3, pe_k, pe_v, w1k.astype(BF16), w2k.astype(BF16),
      w1v.astype(BF16), w2v.astype(BF16), kc_w.reshape(1, LANE), *ctabs)


def _nsa_kernel(q_ref, z_ref, g_ref, qw_ref, c_ref, s1_ref, s2_ref, kc_ref, vc_ref, ov_ref,
                ks_ref, vst_ref, kw_ref, vw_ref, o_ref, qb_s, qt_s, acc_s, ocmp_s, osel_s, s_s, p_s):
    i = pl.program_id(1)
    t = NSA_TQ
    ncmp = kc_ref.shape[1]
    nsel = ks_ref.shape[0] // NSA_SEL_BLOCK
    sel_shift = int(math.log2(NSA_SEL_BLOCK))
    trow = i * t + _iota((t, 1), 0)
    row = _iota((t, t), 0)
    col = _iota((t, t), 1)
    causal = col <= row

    c, s1, s2 = c_ref[...], s1_ref[...], s2_ref[...]
    for h in range(NSA_HEADS):
        qb_s[h] = (_rope(_head_norm_mxu(q_ref[h], qw_ref[...]), c, s1, s2) * SCALE).astype(BF16)

    valid = (_iota((t, ncmp), 1) * NSA_CMP_STRIDE + (NSA_CMP_LEN - 1)) <= trow
    kc, vc = kc_ref[0], vc_ref[0]
    psum = jnp.zeros((t, ncmp), F32)
    for h in range(NSA_HEADS):
        s = jnp.where(valid, _dot_nt(qb_s[h], kc), NEG)
        mx = jnp.max(s, axis=1, keepdims=True)
        e = jnp.exp(s - mx)
        p = e * jnp.where(mx > 0.5 * NEG, 1.0 / jnp.sum(e, axis=1, keepdims=True), 0.0)
        ocmp_s[h] = jnp.dot(p.astype(BF16), vc, preferred_element_type=F32)
        psum = psum + p

    imp_t = jnp.dot(psum, ov_ref[...], precision=HIGHEST, preferred_element_type=F32).T[:nsel]
    jj = _iota((nsel, t), 0)
    cur = (i * t + _iota((1, t), 1)) >> sel_shift
    score = jnp.where(jj <= cur, imp_t, -jnp.inf)
    score = jnp.where((jj == 0) | (jj == cur) | (jj == cur - 1), jnp.inf, score)
    top = _topk_mask_t(score, NSA_SEL_TOPN)
    selbias_t = jnp.concatenate([jnp.where(top, 0.0, NEG), jnp.zeros((LANE - nsel - SUBLANE, t), F32),
                                 jnp.full((SUBLANE, t), NEG, F32)], axis=0).astype(BF16)
    ntiles = ks_ref.shape[0] // t
    key_blk = _iota((t, LANE), 0) >> sel_shift

    def tile_bias(kt):
        blk = jnp.where(kt < i, kt * NSA_SEL_PER_TILE + key_blk, LANE - 1)
        onehot = jnp.where(_iota((t, LANE), 1) == blk, 1.0, 0.0).astype(BF16)
        return jnp.dot(onehot, selbias_t, preferred_element_type=F32)

    def score_inputs(kt):
        src = jnp.minimum(kt, ntiles - 1)
        return ks_ref[pl.ds(pl.multiple_of(src * t, t), t), :], tile_bias(kt)

    def put_scores(slot, h, k, bias):
        s_s[slot, h] = jnp.dot(k, qt_s[h], preferred_element_type=F32) + bias

    own = pl.multiple_of(i * t, t)
    own_blk = i * NSA_SEL_PER_TILE + key_blk
    own_bias = jnp.dot(jnp.where(_iota((t, LANE), 1) == own_blk, 1.0, 0.0).astype(BF16), selbias_t,
                       preferred_element_type=F32)
    k_own, vt_own = ks_ref[pl.ds(own, t), :], vst_ref[:, pl.ds(own, t)]
    state = []
    for h in range(NSA_HEADS):
        qt_s[h] = qb_s[h].T
        s = jnp.where(row <= col, jnp.dot(k_own, qt_s[h], preferred_element_type=F32) + own_bias, NEG)
        m = jnp.max(s, axis=0, keepdims=True)
        p = jnp.exp(s - m)
        acc_s[h] = jnp.dot(vt_own, p.astype(BF16), preferred_element_type=F32)
        state += [m, jnp.sum(p, axis=0, keepdims=True)]
        p_s[h] = jnp.zeros((t, t), BF16)
    k_first, bias_first = score_inputs(0)
    for h in range(NSA_HEADS):
        put_scores(0, h, k_first, bias_first)

    def trip(kt, carry, src, dst):
        k_next, bias_next = score_inputs(kt + 1)
        vt_prev = vst_ref[:, pl.ds(pl.multiple_of(jnp.clip(kt - 1, 0, ntiles - 1) * t, t), t)]
        out = []
        for h in range(NSA_HEADS):
            m, l = carry[2 * h:2 * h + 2]
            put_scores(dst, h, k_next, bias_next)
            acc = acc_s[h] + jnp.dot(vt_prev, p_s[h], preferred_element_type=F32)
            s = s_s[src, h]
            m_new = jnp.maximum(m, jnp.max(s, axis=0, keepdims=True))
            p = jnp.exp(s - m_new)
            alpha = jnp.exp(m - m_new)
            acc_s[h] = alpha * acc
            p_s[h] = p.astype(BF16)
            out += [m_new, alpha * l + jnp.sum(p, axis=0, keepdims=True)]
        return tuple(out)

    def sel_body(q, carry):
        return trip(2 * q + 1, trip(2 * q, carry, 0, 1), 1, 0)

    ntrips = 2 * ((i + 1) // 2)
    state = lax.fori_loop(0, ntrips // 2, sel_body, tuple(state))
    vt_last = vst_ref[:, pl.ds(pl.multiple_of(jnp.clip(ntrips - 1, 0, ntiles - 1) * t, t), t)]
    for h in range(NSA_HEADS):
        acc = acc_s[h] + jnp.dot(vt_last, p_s[h], preferred_element_type=F32)
        osel_s[h] = (acc * (1.0 / state[2 * h + 1])).T

    far = pl.multiple_of(jnp.maximum(i - 2, 0) * t, t)
    mid = pl.multiple_of(jnp.maximum(i - 1, 0) * t, t)
    k3 = jnp.concatenate([kw_ref[pl.ds(far, t), :], kw_ref[pl.ds(mid, t), :], kw_ref[pl.ds(own, t), :]], axis=0)
    v3 = jnp.concatenate([vw_ref[pl.ds(far, t), :], vw_ref[pl.ds(mid, t), :], vw_ref[pl.ds(own, t), :]], axis=0)
    allowed = jnp.concatenate([(col > row) & (i >= 2), jnp.full((t, t), True) & (i >= 1), causal], axis=1)
    g = jax.nn.sigmoid(g_ref[0])
    for h in range(NSA_HEADS):
        s = jnp.where(allowed, _dot_nt(qb_s[h], k3), NEG)
        p = jnp.exp(s - jnp.max(s, axis=1, keepdims=True))
        o_win = jnp.dot(p.astype(BF16), v3, preferred_element_type=F32) * (1.0 / jnp.sum(p, axis=1, keepdims=True))
        o = (g[:, 3 * h:3 * h + 1] * ocmp_s[h] + g[:, 3 * h + 1:3 * h + 2] * osel_s[h]
             + g[:, 3 * h + 2:3 * h + 3] * o_win)
        z = z_ref[h]
        o_ref[h] = (o * (z * jax.nn.sigmoid(z))).astype(BF16)


def _nsa(proj3, bsz, seq, q_w, tabs, k_cmp, v_cmp, overlap, nk, vs_t):
    rows = proj3.shape[1]
    t = NSA_TQ
    nt = seq // t
    ncmp = seq // NSA_CMP_STRIDE
    nsel = seq // NSA_SEL_BLOCK
    blk4 = lambda cb: pl.BlockSpec((4, t, LANE), lambda b, i: (cb // 4, b * nt + i, 0))
    tspec = pl.BlockSpec((t, LANE), lambda b, i: (i, 0))
    cspec = pl.BlockSpec((1, ncmp, LANE), lambda b, i: (b, 0, 0))
    kvspec = lambda which: pl.BlockSpec((None, seq, LANE), lambda b, i: (which, b, 0))
    return pl.pallas_call(
        _nsa_kernel,
        grid=(bsz, nt),
        in_specs=[blk4(CB_NQ), blk4(CB_NZ),
                  pl.BlockSpec((1, t, LANE), lambda b, i: (CB_NG, b * nt + i, 0)),
                  pl.BlockSpec((1, LANE), lambda b, i: (0, 0)), tspec, tspec, tspec, cspec, cspec,
                  pl.BlockSpec((ncmp, LANE), lambda b, i: (0, 0)),
                  kvspec(0), pl.BlockSpec((None, LANE, seq), lambda b, i: (b, 0, 0)), kvspec(2), kvspec(3)],
        out_specs=pl.BlockSpec((4, t, LANE), lambda b, i: (0, b * nt + i, 0)),
        out_shape=jax.ShapeDtypeStruct((NSA_HEADS, rows, LANE), BF16),
        scratch_shapes=[pltpu.VMEM((NSA_HEADS, t, LANE), BF16),
                        pltpu.VMEM((NSA_HEADS, LANE, t), BF16),
                        pltpu.VMEM((NSA_HEADS, LANE, t), F32),
                        pltpu.VMEM((NSA_HEADS, t, LANE), F32),
                        pltpu.VMEM((NSA_HEADS, t, LANE), F32),
                        pltpu.VMEM((2, NSA_HEADS, t, t), F32),
                        pltpu.VMEM((NSA_HEADS, t, t), BF16)],
        compiler_params=_cparams(("parallel", "parallel")),
        name="nsa",
    )(proj3, proj3, proj3, q_w.reshape(1, LANE), *tabs, k_cmp, v_cmp, overlap, nk, vs_t, nk, nk)


def _s5_factors(a_re, a_im, b_re, b_im, c_re, c_im, log_dt):
    t = S5_CHUNK
    dt = jnp.exp(log_dt.astype(F32))[:, None]
    ar, ai = a_re.astype(F32), a_im.astype(F32)
    ang = dt * ai
    mag = jnp.exp(dt * ar)
    abar_r, abar_i = mag * jnp.cos(ang), mag * jnp.sin(ang)
    nr, ni = abar_r - 1.0, abar_i
    den = ar * ar + ai * ai
    fr = (nr * ar + ni * ai) / den
    fi = (ni * ar - nr * ai) / den
    bt_r, bt_i = b_re.astype(F32).transpose(0, 2, 1), b_im.astype(F32).transpose(0, 2, 1)
    bbar_r = fr[:, None, :] * bt_r - fi[:, None, :] * bt_i
    bbar_i = fr[:, None, :] * bt_i + fi[:, None, :] * bt_r

    def powers(tau):
        tau = jnp.asarray(tau, F32)[:, None, None]
        pmag = jnp.exp(tau * (dt * ar)[None])
        return pmag * jnp.cos(tau * ang[None]), pmag * jnp.sin(tau * ang[None])

    pw_r, pw_i = powers(np.arange(t + 1))
    cr, ci = c_re.astype(F32), c_im.astype(F32)
    cp_r = cr[None] * pw_r[:, :, None, :] - ci[None] * pw_i[:, :, None, :]
    cp_i = cr[None] * pw_i[:, :, None, :] + ci[None] * pw_r[:, :, None, :]
    rev_r, rev_i = powers(t - 1 - np.arange(t))
    bp_r = rev_r[:, :, None, :] * bbar_r[None] - rev_i[:, :, None, :] * bbar_i[None]
    bp_i = rev_r[:, :, None, :] * bbar_i[None] + rev_i[:, :, None, :] * bbar_r[None]
    bpc = (jnp.stack([bp_r, bp_i], axis=3).reshape(t, S5_MBLK, LANE, 2 * S5_STATE).transpose(1, 0, 2, 3)
           .reshape(S5_MBLK, t * LANE, 2 * S5_STATE))
    cpo = jnp.stack([cp_r, -cp_i]).reshape(2, t + 1, S5_MBLK, S5_GPB, S5_GROUP, S5_STATE)
    cpc = cpo.transpose(2, 0, 3, 5, 1, 4).reshape(S5_MBLK, 2 * S5_HALF, (t + 1) * S5_GROUP)
    a_t = jnp.concatenate([pw_r[t].reshape(S5_MBLK, 1, S5_HALF), pw_i[t].reshape(S5_MBLK, 1, S5_HALF)], axis=2)
    return bpc, cpc, a_t


def _s5_wgen_kernel(bp_ref, cp_ref, win_ref, wc_ref):
    t = S5_CHUNK
    wide = t * LANE
    gs, ps = int(math.log2(S5_GROUP)), int(math.log2(S5_STATE))

    def expand(x, copy_mask, group_mask):
        ex = jnp.where(copy_mask, 1.0, 0.0).astype(BF16)
        return jnp.where(group_mask, jnp.dot(x.astype(BF16), ex, preferred_element_type=F32), 0.0).astype(BF16)

    def split(x):
        hi = x.astype(BF16)
        return hi, (x - hi.astype(F32)).astype(BF16)

    lagc = t * S5_GROUP
    kc = []
    for g in range(S5_GPB):
        a_hi, a_lo = split(bp_ref[0, (t - 1) * LANE + g * S5_GROUP:(t - 1) * LANE + (g + 1) * S5_GROUP, :])
        b_hi, b_lo = split(jnp.concatenate(
            [cp_ref[0, g * S5_STATE:(g + 1) * S5_STATE, :lagc],
             cp_ref[0, S5_HALF + g * S5_STATE:S5_HALF + (g + 1) * S5_STATE, :lagc]], axis=0))
        kc.append(jnp.dot(a_hi, b_hi, preferred_element_type=F32) + jnp.dot(a_hi, b_lo, preferred_element_type=F32)
                  + jnp.dot(a_lo, b_hi, preferred_element_type=F32))
    kc = jnp.concatenate(kc, axis=0)
    r, c = _iota((lagc, LANE), 0), _iota((lagc, LANE), 1)
    same_k = (_iota((LANE, LANE), 0) >> gs) == (_iota((LANE, LANE), 1) >> gs)
    zero = jnp.zeros((LANE, LANE), BF16)
    for lag in range(t):
        bd = expand(kc, ((r >> gs) == lag) & ((r & (S5_GROUP - 1)) == (c & (S5_GROUP - 1))), same_k)
        for s in range(t - lag):
            win_ref[0, s * LANE:(s + 1) * LANE, (s + lag) * LANE:(s + lag + 1) * LANE] = bd
    for s in range(1, t):
        for tt in range(s):
            win_ref[0, s * LANE:(s + 1) * LANE, tt * LANE:(tt + 1) * LANE] = zero
    r, c = _iota((LANE, 2 * S5_HALF), 0), _iota((LANE, 2 * S5_HALF), 1)
    copy_b = ((r >> ps) == (c >> (ps + 3))) & ((r & (S5_STATE - 1)) == (c & (S5_STATE - 1)))
    same_b = (r >> gs) == ((c >> ps) & (S5_GPB - 1))
    for s in range(t):
        win_ref[0, s * LANE:(s + 1) * LANE, wide:] = expand(bp_ref[0, s * LANE:(s + 1) * LANE, :],
                                                            copy_b, same_b)
    rows = 2 * S5_STATE * 2
    r, c = _iota(((t + 1) * S5_GROUP, wide), 0), _iota(((t + 1) * S5_GROUP, wide), 1)
    copy_c = ((r >> gs) == (c >> (gs + 3)) + 1) & ((r & (S5_GROUP - 1)) == (c & (S5_GROUP - 1)))
    for ch in range(2 * S5_HALF // rows):
        r, c = ch * rows + _iota((rows, wide), 0), _iota((rows, wide), 1)
        same_c = ((r >> ps) & (S5_GPB - 1)) == ((c >> gs) & (S5_GPB - 1))
        wc_ref[0, ch * rows:(ch + 1) * rows, :] = expand(cp_ref[0, ch * rows:(ch + 1) * rows, :],
                                                         copy_c, same_c)


def _s5_wgen(bpc, cpc):
    t = S5_CHUNK
    wide = t * LANE
    spec = lambda a: pl.BlockSpec((1,) + a.shape[1:], lambda m: (m, 0, 0))
    return pl.pallas_call(
        _s5_wgen_kernel,
        grid=(bpc.shape[0],),
        in_specs=[spec(bpc), spec(cpc)],
        out_specs=[pl.BlockSpec((1, wide, wide + 2 * S5_HALF), lambda m: (m, 0, 0)),
                   pl.BlockSpec((1, 2 * S5_HALF, wide), lambda m: (m, 0, 0))],
        out_shape=[jax.ShapeDtypeStruct((bpc.shape[0], wide, wide + 2 * S5_HALF), BF16),
                   jax.ShapeDtypeStruct((bpc.shape[0], 2 * S5_HALF, wide), BF16)],
        compiler_params=_cparams(("parallel",)),
        name="s5_wgen",
    )(bpc, cpc)


def _s5_in_kernel(u_ref, w_ref, d_ref, y_ref, b_ref):
    tr = y_ref.shape[1]
    u = jnp.concatenate([u_ref[0, pl.ds(s, tr, stride=S5_CHUNK), :] for s in range(S5_CHUNK)], axis=1)
    ub = u.astype(BF16)
    wide = y_ref.shape[2]
    for j in range(S5_CHUNK // 2):
        kk = (2 * j + 2) * LANE
        cols = slice(2 * j * LANE, kk)
        y_ref[0, :, cols] = (jnp.dot(ub[:, :kk], w_ref[0, :kk, cols], preferred_element_type=F32)
                             + d_ref[0, :, cols] * u[:, cols])
    b_ref[0] = jnp.dot(ub, w_ref[0, :, wide:], preferred_element_type=F32)


def _s5_in(proj3, w_in, d_t, layer):
    rows = proj3.shape[1] // S5_CHUNK
    wide = S5_CHUNK * LANE
    tr = min(rows, ROW_TILE)
    return pl.pallas_call(
        _s5_in_kernel,
        grid=(S5_MBLK, rows // tr),
        in_specs=[pl.BlockSpec((1, tr * S5_CHUNK, LANE), lambda m, r: (CB_SU + m, r, 0)),
                  pl.BlockSpec((1, wide, wide + 2 * S5_HALF), lambda m, r: (layer * S5_MBLK + m, 0, 0)),
                  pl.BlockSpec((1, 1, wide), lambda m, r: (m, 0, 0))],
        out_specs=[pl.BlockSpec((1, tr, wide), lambda m, r: (m, r, 0)),
                   pl.BlockSpec((1, tr, 2 * S5_HALF), lambda m, r: (m, r, 0))],
        out_shape=[jax.ShapeDtypeStruct((S5_MBLK, rows, wide), F32),
                   jax.ShapeDtypeStruct((S5_MBLK, rows, 2 * S5_HALF), F32)],
        compiler_params=_cparams(("parallel", "parallel")),
        name="s5_in",
    )(proj3, w_in, d_t)


def _s5_scan_kernel(b_ref, a_ref, o_ref, *, bsz, nk):
    nm = b_ref.shape[0]

    def body(k, carry):
        out = []
        for mi in range(nm):
            ar = a_ref[mi, :, :S5_HALF]
            ai = a_ref[mi, :, S5_HALF:]
            for b in range(bsz):
                sr, si = carry[2 * (mi * bsz + b)], carry[2 * (mi * bsz + b) + 1]
                row = b * nk + k
                o_ref[mi, pl.ds(row, 1), :] = jnp.concatenate([sr, si], axis=1)
                x = b_ref[mi, pl.ds(row, 1), :]
                out.append(ar * sr - ai * si + x[:, :S5_HALF])
                out.append(ar * si + ai * sr + x[:, S5_HALF:])
        return tuple(out)

    zero = jnp.zeros((1, S5_HALF), F32)
    lax.fori_loop(0, nk, body, tuple(zero for _ in range(2 * bsz * nm)))


def _s5_scan(bst, a_t, bsz):
    _, rows, wide = bst.shape
    nm = 2
    spec = pl.BlockSpec((nm, rows, wide), lambda m: (m, 0, 0))
    return pl.pallas_call(
        functools.partial(_s5_scan_kernel, bsz=bsz, nk=rows // bsz),
        grid=(S5_MBLK // nm,),
        in_specs=[spec, pl.BlockSpec((nm, 1, wide), lambda m: (m, 0, 0))],
        out_specs=spec,
        out_shape=jax.ShapeDtypeStruct(bst.shape, F32),
        compiler_params=_cparams(("parallel",)),
        name="s5_scan",
    )(bst, a_t)


def _s5_out_kernel(y_ref, s_ref, w_ref, o_ref):
    tr = y_ref.shape[1]
    y = jax.nn.gelu(y_ref[0] + jnp.dot(s_ref[0].astype(BF16), w_ref[0], preferred_element_type=F32))
    for s in range(S5_CHUNK):
        o_ref[0, pl.ds(s, tr, stride=S5_CHUNK), :] = y[:, s * LANE:(s + 1) * LANE]


def _s5_out(y_intra, s_prev, w_c, layer):
    _, rows, wide = y_intra.shape
    tr = min(rows, ROW_TILE)
    return pl.pallas_call(
        _s5_out_kernel,
        grid=(S5_MBLK, rows // tr),
        in_specs=[pl.BlockSpec((1, tr, wide), lambda m, r: (m, r, 0)),
                  pl.BlockSpec((1, tr, 2 * S5_HALF), lambda m, r: (m, r, 0)),
                  pl.BlockSpec((1, 2 * S5_HALF, wide), lambda m, r: (layer * S5_MBLK + m, 0, 0))],
        out_specs=pl.BlockSpec((1, tr * S5_CHUNK, LANE), lambda m, r: (m, r, 0)),
        out_shape=jax.ShapeDtypeStruct((S5_MBLK, rows * S5_CHUNK, LANE), F32),
        compiler_params=_cparams(("parallel", "parallel")),
        name="s5_out",
    )(y_intra, s_prev, w_c)


def _glu_kernel(y_ref, z_ref, w_ref, o_ref):
    nb = y_ref.shape[0]
    y = jnp.concatenate([y_ref[c] for c in range(nb)], axis=1)
    z = jnp.concatenate([z_ref[c] for c in range(nb)], axis=1)
    gate = jax.nn.sigmoid(jnp.dot(y.astype(BF16), w_ref[...], preferred_element_type=F32))
    o = (y * gate * (z * jax.nn.sigmoid(z))).astype(BF16)
    for c in range(nb):
        o_ref[c] = o[:, c * LANE:(c + 1) * LANE]


def _glu(y5, proj3, glu_w):
    nb, rows, _ = y5.shape
    tm = ROW_TILE
    spec = lambda blk: pl.BlockSpec((nb, tm, LANE), lambda i: (blk, i, 0))
    return pl.pallas_call(
        _glu_kernel,
        grid=(rows // tm,),
        in_specs=[spec(0), spec(CB_SZ // nb), pl.BlockSpec((S5_WIDTH, S5_WIDTH), lambda i: (0, 0))],
        out_specs=spec(0),
        out_shape=jax.ShapeDtypeStruct((nb, rows, LANE), BF16),
        compiler_params=_cparams(("parallel",)),
        name="s5_glu",
    )(y5, proj3, glu_w.astype(BF16))


def _outproj_kernel(a_ref, b_ref, c_ref, w_ref, x_ref, o_ref):
    parts = ([a_ref[h] for h in range(a_ref.shape[0])] + [b_ref[h] for h in range(b_ref.shape[0])]
             + [c_ref[h] for h in range(c_ref.shape[0])])
    mixed = jnp.concatenate(parts, axis=1)
    o_ref[...] = x_ref[...] + jnp.dot(mixed, w_ref[...], preferred_element_type=F32)


def _outproj(m_moba, m_nsa, m_s5, w_out, x2):
    rows, d = x2.shape
    tm = ROW_TILE
    lspec = lambda n: pl.BlockSpec((n, tm, LANE), lambda i: (0, i, 0))
    return pl.pallas_call(
        _outproj_kernel,
        grid=(rows // tm,),
        in_specs=[lspec(m_moba.shape[0]), lspec(m_nsa.shape[0]), lspec(m_s5.shape[0]),
                  pl.BlockSpec(w_out.shape, lambda i: (0, 0)),
                  pl.BlockSpec((tm, d), lambda i: (i, 0))],
        out_specs=pl.BlockSpec((tm, d), lambda i: (i, 0)),
        out_shape=jax.ShapeDtypeStruct((rows, d), F32),
        compiler_params=_cparams(("parallel",)),
        name="outproj",
    )(m_moba, m_nsa, m_s5, w_out.astype(BF16), x2)


def _w_in_source(blk):
    mw, nw, kvw, ng = MOBA_HEADS * HEAD_DIM, NSA_HEADS * HEAD_DIM, HEAD_DIM, 3 * NSA_HEADS
    o_kv = 4 * mw + nw
    o_ng = o_kv + 6 * kvw
    o_nz = o_ng + ng
    src = jnp.where(blk < CB_NZ, blk * LANE,
          jnp.where(blk < CB_NKS, o_nz + (blk - CB_NZ) * LANE,
          jnp.where(blk < CB_NKC, o_kv + 2 * kvw + (blk - CB_NKS) * LANE,
          jnp.where(blk < CB_NG, o_kv + (blk - CB_NKC) * LANE, o_ng))))
    real = jnp.where(blk < CB_NG, LANE, jnp.where(blk == CB_NG, ng, 0))
    return src, real


def _wperm_kernel(w_ref, o_ref):
    _, real = _w_in_source(pl.program_id(0))
    for layer in range(o_ref.shape[0]):
        x = w_ref[:, layer, :]
        x = jnp.where(_iota(x.shape, 0) < real, x, 0.0)
        o_ref[layer] = x.T.astype(BF16)


def _permute_w_in(w_in_all):
    wt = jnp.transpose(w_in_all, (2, 0, 1))
    _, nl, d = wt.shape
    return pl.pallas_call(
        _wperm_kernel,
        grid=(IN_BLOCKS,),
        in_specs=[pl.BlockSpec((pl.Element(LANE), pl.Element(nl), pl.Element(d)),
                               lambda c: (_w_in_source(c)[0], 0, 0))],
        out_specs=pl.BlockSpec((nl, d, LANE), lambda c: (0, 0, c)),
        out_shape=jax.ShapeDtypeStruct((nl, d, IN_BLOCKS * LANE), BF16),
        compiler_params=_cparams(("parallel",)),
        name="w_in_permute",
    )(wt)


def _layer(x2, bsz, seq, tabs, ctabs, overlap, w_in, s5_w, layer, norm_w, w_out, moba_q_norm, moba_k_norm,
           nsa_q_norm, nsa_kc_norm, nsa_ks_norm, nsa_kw_norm, nsa_pe_k, nsa_pe_v, nsa_cmp_k_w1, nsa_cmp_k_w2,
           nsa_cmp_v_w1, nsa_cmp_v_w2, s5_d, s5_glu_w):
    proj3 = _inproj(x2, norm_w, w_in, layer)
    mk_n, mv_b, nk, kmean, vs_t = _kprep(proj3, seq, moba_k_norm, nsa_ks_norm, nsa_kw_norm, tabs)
    m_moba = _moba(proj3, bsz, seq, moba_q_norm, tabs, kmean.transpose(1, 0, 2), mk_n, mv_b)
    k_cmp, v_cmp = _compress(proj3, bsz, seq, nsa_pe_k, nsa_pe_v, nsa_cmp_k_w1, nsa_cmp_k_w2,
                             nsa_cmp_v_w1, nsa_cmp_v_w2, nsa_kc_norm, ctabs)
    m_nsa = _nsa(proj3, bsz, seq, nsa_q_norm, tabs, k_cmp, v_cmp, overlap, nk, vs_t)
    w_s5_in, w_s5_out, a_t = s5_w
    d_t = jnp.tile(s5_d.astype(F32).reshape(S5_MBLK, 1, LANE), (1, 1, S5_CHUNK))
    y_intra, bst = _s5_in(proj3, w_s5_in, d_t, layer)
    y5 = _s5_out(y_intra, _s5_scan(bst, a_t[layer], bsz), w_s5_out, layer)
    m_s5 = _glu(y5, proj3, s5_glu_w)
    return _outproj(m_moba, m_nsa, m_s5, w_out, x2)


def kernel(x, norm_w, w_in, w_out, moba_q_norm, moba_k_norm, nsa_q_norm, nsa_kc_norm, nsa_ks_norm, nsa_kw_norm, nsa_pe_k, nsa_pe_v, nsa_cmp_k_w1, nsa_cmp_k_w2, nsa_cmp_v_w1, nsa_cmp_v_w2, s5_a_re, s5_a_im, s5_b_re, s5_b_im, s5_c_re, s5_c_im, s5_d, s5_log_dt, s5_glu_w):
    bsz, seq, d = x.shape
    tabs = _rope_tables(jnp.arange(seq, dtype=F32))
    ncmp = seq // NSA_CMP_STRIDE
    ctabs = _rope_tables(jnp.arange(ncmp, dtype=F32) * NSA_CMP_STRIDE + (NSA_CMP_LEN - 1))
    nsel = seq // NSA_SEL_BLOCK
    ci = np.arange(ncmp)[:, None] * NSA_CMP_STRIDE
    sj = np.arange(nsel)[None, :] * NSA_SEL_BLOCK
    overlap = np.zeros((ncmp, LANE), np.float32)
    overlap[:, :nsel] = (ci < sj + NSA_SEL_BLOCK) & (ci + NSA_CMP_LEN > sj)
    overlap = jnp.asarray(overlap)
    params = (norm_w, w_out, moba_q_norm, moba_k_norm, nsa_q_norm, nsa_kc_norm, nsa_ks_norm,
              nsa_kw_norm, nsa_pe_k, nsa_pe_v, nsa_cmp_k_w1, nsa_cmp_k_w2, nsa_cmp_v_w1, nsa_cmp_v_w2,
              s5_d, s5_glu_w)
    bpc, cpc, a_t = jax.vmap(_s5_factors)(s5_a_re, s5_a_im, s5_b_re, s5_b_im, s5_c_re, s5_c_im, s5_log_dt)
    merge = lambda a: a.reshape((a.shape[0] * a.shape[1],) + a.shape[2:])
    s5_w = tuple(_s5_wgen(merge(bpc), merge(cpc))) + (a_t,)
    x2 = x.reshape(bsz * seq, d)
    w_perm = _permute_w_in(w_in)
    for layer in range(norm_w.shape[0]):
        x2 = _layer(x2, bsz, seq, tabs, ctabs, overlap, w_perm, s5_w, layer, *[p[layer] for p in params])
    return x2.reshape(bsz, seq, d)
```

```python
import functools
import math

import numpy as np
import jax
import jax.numpy as jnp
from jax import lax
from jax.experimental import pallas as pl
from jax.experimental.pallas import tpu as pltpu

F32 = jnp.float32
BF16 = jnp.bfloat16
HIGHEST = lax.Precision.HIGHEST

LANE = 128
SUBLANE = 8
HEAD_DIM = 128
ROPE_DIM = HEAD_DIM // 4
ROPE_HALF = ROPE_DIM // 2
ROPE_THETA = 500000.0
EPS = 1e-6
SCALE = HEAD_DIM ** -0.5
NEG = -1e30

MOBA_HEADS = 4
MOBA_BLOCK = 256
MOBA_TOPK = 3

NSA_HEADS = 4
NSA_CMP_LEN = 32
NSA_CMP_STRIDE = 16
NSA_SEL_BLOCK = 64
NSA_SEL_TOPN = 16
NSA_WINDOW = 512
NSA_TQ = 256
NSA_SEL_PER_TILE = NSA_TQ // NSA_SEL_BLOCK
assert NSA_WINDOW == 2 * NSA_TQ

S5_WIDTH = 1024
S5_GROUP = 16
S5_GROUPS = S5_WIDTH // S5_GROUP
S5_STATE = 64
S5_CHUNK = 16
S5_MBLK = S5_WIDTH // LANE
S5_GPB = LANE // S5_GROUP
S5_HALF = S5_GPB * S5_STATE

CB_MQ, CB_MK, CB_MV, CB_MZ, CB_NQ, CB_NZ, CB_SU, CB_SZ = 0, 4, 8, 12, 16, 20, 24, 32
CB_NKS, CB_NVS, CB_NKW, CB_NVW, CB_NKC, CB_NVC, CB_NG = 40, 41, 42, 43, 44, 45, 46
IN_BLOCKS = 48

VMEM_LIMIT = 56 * 1024 * 1024
INPROJ_TM, INPROJ_TN = 1024, 1536
ROW_TILE = 512


def _cparams(sem):
    return pltpu.CompilerParams(dimension_semantics=sem, vmem_limit_bytes=VMEM_LIMIT)


def _iota(shape, dim):
    return lax.broadcasted_iota(jnp.int32, shape, dim)


def _head_norm(x, w):
    return x * lax.rsqrt(jnp.mean(x * x, axis=-1, keepdims=True) + EPS) * w


def _head_norm_mxu(x, w):
    sq = x * x
    hi = sq.astype(BF16)
    lo = (sq - hi.astype(F32)).astype(BF16)
    avg = jnp.full((LANE, LANE), 1.0 / HEAD_DIM, BF16)
    ms = jnp.dot(hi, avg, preferred_element_type=F32) + jnp.dot(lo, avg, preferred_element_type=F32)
    return x * lax.rsqrt(ms + EPS) * w


def _rope(x, c, s1, s2):
    return x * c + pltpu.roll(x, LANE - ROPE_HALF, 1) * s1 + pltpu.roll(x, ROPE_HALF, 1) * s2


def _dot_nt(a, b):
    return lax.dot_general(a, b, (((1,), (1,)), ((), ())), preferred_element_type=F32)


def _rope_tables(pos):
    inv = ROPE_THETA ** (-jnp.arange(0, ROPE_DIM, 2, dtype=F32) / ROPE_DIM)
    ang = pos.astype(F32)[:, None] * inv[None, :]
    cos, sin = jnp.cos(ang), jnp.sin(ang)
    n = pos.shape[0]
    c = jnp.concatenate([cos, cos, jnp.ones((n, LANE - ROPE_DIM), F32)], axis=1)
    s1 = jnp.concatenate([-sin, jnp.zeros((n, LANE - ROPE_HALF), F32)], axis=1)
    s2 = jnp.concatenate([jnp.zeros((n, ROPE_HALF), F32), sin, jnp.zeros((n, LANE - ROPE_DIM), F32)], axis=1)
    return c, s1, s2


def _inproj_kernel(x_ref, nw_ref, w_ref, o_ref, h_ref):
    @pl.when(pl.program_id(1) == 0)
    def _():
        x = x_ref[...]
        ms = jnp.mean(x * x, axis=-1, keepdims=True)
        h_ref[...] = (x * lax.rsqrt(ms + EPS) * nw_ref[...]).astype(BF16)

    res = jnp.dot(h_ref[...], w_ref[...], preferred_element_type=F32)
    for c in range(o_ref.shape[0]):
        o_ref[c] = res[:, c * LANE:(c + 1) * LANE]


def _inproj(x2, norm_w, w_perm, layer):
    rows, d = x2.shape
    tm, tn = INPROJ_TM, INPROJ_TN
    nb = tn // LANE
    return pl.pallas_call(
        _inproj_kernel,
        grid=(rows // tm, IN_BLOCKS // nb),
        in_specs=[pl.BlockSpec((tm, d), lambda i, j: (i, 0)),
                  pl.BlockSpec((1, d), lambda i, j: (0, 0)),
                  pl.BlockSpec((None, d, tn), lambda i, j: (layer, 0, j))],
        out_specs=pl.BlockSpec((nb, tm, LANE), lambda i, j: (j, i, 0)),
        out_shape=jax.ShapeDtypeStruct((IN_BLOCKS, rows, LANE), F32),
        scratch_shapes=[pltpu.VMEM((tm, d), BF16)],
        compiler_params=_cparams(("parallel", "arbitrary")),
        name="inproj",
    )(x2, norm_w.reshape(1, d), w_perm)


def _kprep_kernel(mk_ref, mv_ref, nk_ref, mkw_ref, ksw_ref, kww_ref, c_ref, s1_ref, s2_ref,
                  mk_o, mv_o, nk_o, km_o, vst_o):
    c, s1, s2 = c_ref[...], s1_ref[...], s2_ref[...]
    means = []
    for h in range(MOBA_HEADS):
        k = _rope(_head_norm_mxu(mk_ref[h], mkw_ref[...]), c, s1, s2)
        mk_o[h] = k.astype(BF16)
        mv_o[h] = mv_ref[h].astype(BF16).T
        means.append(jnp.mean(k, axis=0, keepdims=True))
    km_o[0] = jnp.concatenate(means, axis=0)
    nk_o[0] = _rope(_head_norm_mxu(nk_ref[0], ksw_ref[...]), c, s1, s2).astype(BF16)
    nk_o[1] = nk_ref[1].astype(BF16)
    vst_o[...] = nk_ref[1].astype(BF16).T
    nk_o[2] = _rope(_head_norm_mxu(nk_ref[2], kww_ref[...]), c, s1, s2).astype(BF16)
    nk_o[3] = nk_ref[3].astype(BF16)


def _kprep(proj3, seq, mk_w, ks_w, kw_w, tabs):
    rows = proj3.shape[1]
    t = MOBA_BLOCK
    nt = seq // t
    blk4 = lambda cb: pl.BlockSpec((4, t, LANE), lambda i: (cb // 4, i, 0))
    wspec = pl.BlockSpec((1, LANE), lambda i: (0, 0))
    tspec = pl.BlockSpec((t, LANE), lambda i: (i % nt, 0))
    out4 = pl.BlockSpec((4, t, LANE), lambda i: (0, i, 0))
    return pl.pallas_call(
        _kprep_kernel,
        grid=(rows // t,),
        in_specs=[blk4(CB_MK), blk4(CB_MV), blk4(CB_NKS), wspec, wspec, wspec, tspec, tspec, tspec],
        out_specs=[out4, pl.BlockSpec((4, None, LANE, t), lambda i: (0, i // nt, 0, i % nt)), out4,
                   pl.BlockSpec((1, MOBA_HEADS, LANE), lambda i: (i, 0, 0)),
                   pl.BlockSpec((None, LANE, t), lambda i: (i // nt, 0, i % nt))],
        out_shape=[jax.ShapeDtypeStruct((4, rows, LANE), BF16),
                   jax.ShapeDtypeStruct((4, rows // seq, LANE, seq), BF16),
                   jax.ShapeDtypeStruct((4, rows, LANE), BF16),
                   jax.ShapeDtypeStruct((rows // t, MOBA_HEADS, LANE), F32),
                   jax.ShapeDtypeStruct((rows // seq, LANE, seq), BF16)],
        compiler_params=_cparams(("parallel",)),
        name="kprep",
    )(proj3, proj3, proj3, mk_w.reshape(1, LANE), ks_w.reshape(1, LANE), kw_w.reshape(1, LANE), *tabs)


def _topk_mask_t(score_t, k):
    n, cols = score_t.shape
    assert n % SUBLANE == 0
    row_in = _iota((SUBLANE, cols), 0)
    groups = [score_t[g:g + SUBLANE] for g in range(0, n, SUBLANE)]
    ranks = [jnp.zeros((SUBLANE, cols), F32) for _ in groups]
    for mm in range(n):
        cm = score_t[mm:mm + 1, :]
        for gi, sc in enumerate(groups):
            lo = gi * SUBLANE
            if lo > mm:
                inc = jnp.where(cm >= sc, 1.0, 0.0)
            elif lo + SUBLANE - 1 <= mm:
                inc = jnp.where(cm > sc, 1.0, 0.0)
            else:
                inc = jnp.where(row_in > mm - lo, jnp.where(cm >= sc, 1.0, 0.0), jnp.where(cm > sc, 1.0, 0.0))
            ranks[gi] = ranks[gi] + inc
    return jnp.concatenate(ranks, axis=0) < k


def _moba_kernel(q_ref, z_ref, qw_ref, c_ref, s1_ref, s2_ref, km_ref, k_ref, vt_ref, o_ref,
                 qt_s, sel_s, s_s, p_s, acc_s):
    i = pl.program_id(1)
    t = MOBA_BLOCK
    nb = km_ref.shape[1]
    nh = q_ref.shape[0]
    npairs = (i + 1) // 2
    blk_t = _iota((nb, t), 0)
    past = blk_t < i
    causal = _iota((t, t), 0) <= _iota((t, t), 1)
    start = pl.multiple_of(i * t, t)

    def put_scores(slot, h, pair):
        src = jnp.minimum(pair, nb // 2 - 1)
        off = pl.multiple_of(src * (2 * t), 2 * t)
        sc = jnp.dot(k_ref[h, pl.ds(off, 2 * t), :], qt_s[h], preferred_element_type=F32)
        row = jnp.minimum(2 * pair, nb - 2)
        s_s[slot, h, :t, :] = sc[:t] + sel_s[h, pl.ds(row, 1), :]
        s_s[slot, h, t:, :] = sc[t:] + sel_s[h, pl.ds(row + 1, 1), :]

    def weighted_values(h, pair):
        off = pl.multiple_of(pair * (2 * t), 2 * t)
        return jnp.dot(vt_ref[h, :, pl.ds(off, 2 * t)], p_s[h], preferred_element_type=F32)

    state = []
    for h in range(nh):
        qf = _rope(_head_norm(q_ref[h], qw_ref[...]), c_ref[...], s1_ref[...], s2_ref[...])
        qt = (qf * SCALE).T.astype(BF16)
        qt_s[h] = qt
        gate_t = lax.dot_general(km_ref[h], qf, (((1,), (1,)), ((), ())), precision=HIGHEST,
                                 preferred_element_type=F32)
        top = _topk_mask_t(jnp.where(past, gate_t, -jnp.inf), MOBA_TOPK)
        sel_s[h] = jnp.where(top & past, 0.0, NEG)
        s = jnp.where(causal, jnp.dot(k_ref[h, pl.ds(start, t), :], qt, preferred_element_type=F32), NEG)
        m = jnp.max(s, axis=0, keepdims=True)
        p = jnp.exp(s - m)
        acc_s[h] = jnp.dot(vt_ref[h, :, pl.ds(start, t)], p.astype(BF16), preferred_element_type=F32)
        state += [m, jnp.sum(p, axis=0, keepdims=True)]
        put_scores(0, h, 0)
        p_s[h] = jnp.zeros((2 * t, t), BF16)

    def trip(pair, carry, src, dst):
        prev = jnp.maximum(pair - 1, 0)
        out = []
        for h in range(nh):
            m, l = carry[2 * h:2 * h + 2]
            put_scores(dst, h, pair + 1)
            acc = acc_s[h] + weighted_values(h, prev)
            s = s_s[src, h]
            m_new = jnp.maximum(m, jnp.max(s, axis=0, keepdims=True))
            p = jnp.exp(s - m_new)
            alpha = jnp.exp(m - m_new)
            l_new = alpha * l + jnp.sum(p, axis=0, keepdims=True)
            acc_s[h] = alpha * acc
            p_s[h] = p.astype(BF16)
            out += [m_new, l_new]
        return tuple(out)

    def body(q, carry):
        return trip(2 * q + 1, trip(2 * q, carry, 0, 1), 1, 0)

    ntrips = 2 * ((npairs + 1) // 2)
    state = lax.fori_loop(0, ntrips // 2, body, tuple(state))
    last = jnp.maximum(ntrips - 1, 0)
    for h in range(nh):
        l = state[2 * h + 1]
        acc = acc_s[h] + weighted_values(h, last)
        z = z_ref[h]
        o_ref[h] = ((acc * (1.0 / l)).T * (z * jax.nn.sigmoid(z))).astype(BF16)


def _moba(proj3, bsz, seq, q_w, tabs, km_t, mk_n, mv_t):
    rows = proj3.shape[1]
    t = MOBA_BLOCK
    nb = seq // t
    nh = MOBA_HEADS
    assert nb % 4 == 0
    qspec = lambda cb: pl.BlockSpec((nh, t, LANE), lambda b, i: (cb // nh, b * nb + i, 0))
    tspec = pl.BlockSpec((t, LANE), lambda b, i: (i, 0))
    return pl.pallas_call(
        _moba_kernel,
        grid=(bsz, nb),
        in_specs=[qspec(CB_MQ), qspec(CB_MZ), pl.BlockSpec((1, LANE), lambda b, i: (0, 0)),
                  tspec, tspec, tspec,
                  pl.BlockSpec((nh, nb, LANE), lambda b, i: (0, b, 0)),
                  pl.BlockSpec((nh, seq, LANE), lambda b, i: (0, b, 0)),
                  pl.BlockSpec((nh, None, LANE, seq), lambda b, i: (0, b, 0, 0))],
        out_specs=pl.BlockSpec((nh, t, LANE), lambda b, i: (0, b * nb + i, 0)),
        out_shape=jax.ShapeDtypeStruct((nh, rows, LANE), BF16),
        scratch_shapes=[pltpu.VMEM((nh, LANE, t), BF16),
                        pltpu.VMEM((nh, nb, t), F32),
                        pltpu.VMEM((2, nh, 2 * t, t), F32),
                        pltpu.VMEM((nh, 2 * t, t), BF16),
                        pltpu.VMEM((nh, LANE, t), F32)],
        compiler_params=_cparams(("parallel", "parallel")),
        name="moba",
    )(proj3, proj3, q_w.reshape(1, LANE), *tabs, km_t, mk_n, mv_t)


def _cmp_kernel(hk_ref, hv_ref, pek_ref, pev_ref, w1k_ref, w2k_ref, w1v_ref, w2v_ref, nw_ref,
                c_ref, s1_ref, s2_ref, ko_ref, vo_ref):
    st = NSA_CMP_STRIDE
    nh = ko_ref.shape[1]

    def compress(x_ref, pe_ref, w1_ref, w2_ref):
        a = jnp.zeros((nh, LANE), F32)
        b = jnp.zeros((nh, LANE), F32)
        for l in range(st):
            x = x_ref[0, pl.ds(l, nh, stride=st), :]
            a = a + jnp.dot((x + pe_ref[l:l + 1, :]).astype(BF16), w1_ref[l * LANE:(l + 1) * LANE, :],
                            preferred_element_type=F32)
            b = b + jnp.dot((x + pe_ref[st + l:st + l + 1, :]).astype(BF16),
                            w1_ref[(st + l) * LANE:(st + l + 1) * LANE, :], preferred_element_type=F32)
        pre = a + pltpu.roll(b, nh - 1, 0)
        return jnp.dot(jax.nn.gelu(pre).astype(BF16), w2_ref[...], preferred_element_type=F32)

    kc = compress(hk_ref, pek_ref, w1k_ref, w2k_ref)
    vc = compress(hv_ref, pev_ref, w1v_ref, w2v_ref)
    ko_ref[0] = _rope(_head_norm(kc, nw_ref[...]), c_ref[...], s1_ref[...], s2_ref[...]).astype(BF16)
    vo_ref[0] = vc.astype(BF16)


def _compress(proj3, bsz, seq, pe_k, pe_v, w1k, w2k, w1v, w2v, kc_w, ctabs):
    nh = seq // NSA_CMP_STRIDE
    wide = NSA_CMP_LEN * HEAD_DIM
    hspec = lambda cb: pl.BlockSpec((1, seq, LANE), lambda b: (cb, b, 0))
    full = lambda shape: pl.BlockSpec(shape, lambda b: tuple(0 for _ in shape))
    ospec = pl.BlockSpec((1, nh, LANE), lambda b: (b, 0, 0))
    return pl.pallas_call(
        _cmp_kernel,
        grid=(bsz,),
        in_specs=[hspec(CB_NKC), hspec(CB_NVC), full((NSA_CMP_LEN, LANE)), full((NSA_CMP_LEN, LANE)),
                  full((wide, LANE)), full((LANE, LANE)), full((wide, LANE)), full((LANE, LANE)),
                  full((1, LANE)), full((nh, LANE)), full((nh, LANE)), full((nh, LANE))],
        out_specs=[ospec, ospec],
        out_shape=[jax.ShapeDtypeStruct((bsz, nh, LANE), BF16)] * 2,
        compiler_params=_cparams(("parallel",)),
        name="nsa_compress",
    )(proj3, proj3, pe_k, pe_v, w1k.astype(BF16), w2k.astype(BF16),
      w1v.astype(BF16), w2v.astype(BF16), kc_w.reshape(1, LANE), *ctabs)


def _nsa_kernel(q_ref, z_ref, g_ref, qw_ref, c_ref, s1_ref, s2_ref, kc_ref, vc_ref, ov_ref,
                ks_ref, vst_ref, kw_ref, vw_ref, o_ref, qb_s, qt_s, acc_s, ocmp_s, osel_s, s_s, p_s):
    i = pl.program_id(1)
    t = NSA_TQ
    ncmp = kc_ref.shape[1]
    nsel = ks_ref.shape[0] // NSA_SEL_BLOCK
    sel_shift = int(math.log2(NSA_SEL_BLOCK))
    trow = i * t + _iota((t, 1), 0)
    row = _iota((t, t), 0)
    col = _iota((t, t), 1)
    causal = col <= row

    c, s1, s2 = c_ref[...], s1_ref[...], s2_ref[...]
    for h in range(NSA_HEADS):
        qb_s[h] = (_rope(_head_norm_mxu(q_ref[h], qw_ref[...]), c, s1, s2) * SCALE).astype(BF16)

    valid = (_iota((t, ncmp), 1) * NSA_CMP_STRIDE + (NSA_CMP_LEN - 1)) <= trow
    kc, vc = kc_ref[0], vc_ref[0]
    psum = jnp.zeros((t, ncmp), F32)
    for h in range(NSA_HEADS):
        s = jnp.where(valid, _dot_nt(qb_s[h], kc), NEG)
        mx = jnp.max(s, axis=1, keepdims=True)
        e = jnp.exp(s - mx)
        p = e * jnp.where(mx > 0.5 * NEG, 1.0 / jnp.sum(e, axis=1, keepdims=True), 0.0)
        ocmp_s[h] = jnp.dot(p.astype(BF16), vc, preferred_element_type=F32)
        psum = psum + p

    imp_t = jnp.dot(psum, ov_ref[...], precision=HIGHEST, preferred_element_type=F32).T[:nsel]
    jj = _iota((nsel, t), 0)
    cur = (i * t + _iota((1, t), 1)) >> sel_shift
    score = jnp.where(jj <= cur, imp_t, -jnp.inf)
    score = jnp.where((jj == 0) | (jj == cur) | (jj == cur - 1), jnp.inf, score)
    top = _topk_mask_t(score, NSA_SEL_TOPN)
    selbias_t = jnp.concatenate([jnp.where(top, 0.0, NEG), jnp.zeros((LANE - nsel - SUBLANE, t), F32),
                                 jnp.full((SUBLANE, t), NEG, F32)], axis=0).astype(BF16)
    ntiles = ks_ref.shape[0] // t
    key_blk = _iota((t, LANE), 0) >> sel_shift

    def tile_bias(kt):
        blk = jnp.where(kt < i, kt * NSA_SEL_PER_TILE + key_blk, LANE - 1)
        onehot = jnp.where(_iota((t, LANE), 1) == blk, 1.0, 0.0).astype(BF16)
        return jnp.dot(onehot, selbias_t, preferred_element_type=F32)

    def score_inputs(kt):
        src = jnp.minimum(kt, ntiles - 1)
        return ks_ref[pl.ds(pl.multiple_of(src * t, t), t), :], tile_bias(kt)

    def put_scores(slot, h, k, bias):
        s_s[slot, h] = jnp.dot(k, qt_s[h], preferred_element_type=F32) + bias

    own = pl.multiple_of(i * t, t)
    own_blk = i * NSA_SEL_PER_TILE + key_blk
    own_bias = jnp.dot(jnp.where(_iota((t, LANE), 1) == own_blk, 1.0, 0.0).astype(BF16), selbias_t,
                       preferred_element_type=F32)
    k_own, vt_own = ks_ref[pl.ds(own, t), :], vst_ref[:, pl.ds(own, t)]
    state = []
    for h in range(NSA_HEADS):
        qt_s[h] = qb_s[h].T
        s = jnp.where(row <= col, jnp.dot(k_own, qt_s[h], preferred_element_type=F32) + own_bias, NEG)
        m = jnp.max(s, axis=0, keepdims=True)
        p = jnp.exp(s - m)
        acc_s[h] = jnp.dot(vt_own, p.astype(BF16), preferred_element_type=F32)
        state += [m, jnp.sum(p, axis=0, keepdims=True)]
        p_s[h] = jnp.zeros((t, t), BF16)
    k_first, bias_first = score_inputs(0)
    for h in range(NSA_HEADS):
        put_scores(0, h, k_first, bias_first)

    def trip(kt, carry, src, dst):
        k_next, bias_next = score_inputs(kt + 1)
        vt_prev = vst_ref[:, pl.ds(pl.multiple_of(jnp.clip(kt - 1, 0, ntiles - 1) * t, t), t)]
        out = []
        for h in range(NSA_HEADS):
            m, l = carry[2 * h:2 * h + 2]
            put_scores(dst, h, k_next, bias_next)
            acc = acc_s[h] + jnp.dot(vt_prev, p_s[h], preferred_element_type=F32)
            s = s_s[src, h]
            m_new = jnp.maximum(m, jnp.max(s, axis=0, keepdims=True))
            p = jnp.exp(s - m_new)
            alpha = jnp.exp(m - m_new)
            acc_s[h] = alpha * acc
            p_s[h] = p.astype(BF16)
            out += [m_new, alpha * l + jnp.sum(p, axis=0, keepdims=True)]
        return tuple(out)

    def sel_body(q, carry):
        return trip(2 * q + 1, trip(2 * q, carry, 0, 1), 1, 0)

    ntrips = 2 * ((i + 1) // 2)
    state = lax.fori_loop(0, ntrips // 2, sel_body, tuple(state))
    vt_last = vst_ref[:, pl.ds(pl.multiple_of(jnp.clip(ntrips - 1, 0, ntiles - 1) * t, t), t)]
    for h in range(NSA_HEADS):
        acc = acc_s[h] + jnp.dot(vt_last, p_s[h], preferred_element_type=F32)
        osel_s[h] = (acc * (1.0 / state[2 * h + 1])).T

    far = pl.multiple_of(jnp.maximum(i - 2, 0) * t, t)
    mid = pl.multiple_of(jnp.maximum(i - 1, 0) * t, t)
    k3 = jnp.concatenate([kw_ref[pl.ds(far, t), :], kw_ref[pl.ds(mid, t), :], kw_ref[pl.ds(own, t), :]], axis=0)
    v3 = jnp.concatenate([vw_ref[pl.ds(far, t), :], vw_ref[pl.ds(mid, t), :], vw_ref[pl.ds(own, t), :]], axis=0)
    allowed = jnp.concatenate([(col > row) & (i >= 2), jnp.full((t, t), True) & (i >= 1), causal], axis=1)
    g = jax.nn.sigmoid(g_ref[0])
    for h in range(NSA_HEADS):
        s = jnp.where(allowed, _dot_nt(qb_s[h], k3), NEG)
        p = jnp.exp(s - jnp.max(s, axis=1, keepdims=True))
        o_win = jnp.dot(p.astype(BF16), v3, preferred_element_type=F32) * (1.0 / jnp.sum(p, axis=1, keepdims=True))
        o = (g[:, 3 * h:3 * h + 1] * ocmp_s[h] + g[:, 3 * h + 1:3 * h + 2] * osel_s[h]
             + g[:, 3 * h + 2:3 * h + 3] * o_win)
        z = z_ref[h]
        o_ref[h] = (o * (z * jax.nn.sigmoid(z))).astype(BF16)


def _nsa(proj3, bsz, seq, q_w, tabs, k_cmp, v_cmp, overlap, nk, vs_t):
    rows = proj3.shape[1]
    t = NSA_TQ
    nt = seq // t
    ncmp = seq // NSA_CMP_STRIDE
    nsel = seq // NSA_SEL_BLOCK
    blk4 = lambda cb: pl.BlockSpec((4, t, LANE), lambda b, i: (cb // 4, b * nt + i, 0))
    tspec = pl.BlockSpec((t, LANE), lambda b, i: (i, 0))
    cspec = pl.BlockSpec((1, ncmp, LANE), lambda b, i: (b, 0, 0))
    kvspec = lambda which: pl.BlockSpec((None, seq, LANE), lambda b, i: (which, b, 0))
    return pl.pallas_call(
        _nsa_kernel,
        grid=(bsz, nt),
        in_specs=[blk4(CB_NQ), blk4(CB_NZ),
                  pl.BlockSpec((1, t, LANE), lambda b, i: (CB_NG, b * nt + i, 0)),
                  pl.BlockSpec((1, LANE), lambda b, i: (0, 0)), tspec, tspec, tspec, cspec, cspec,
                  pl.BlockSpec((ncmp, LANE), lambda b, i: (0, 0)),
                  kvspec(0), pl.BlockSpec((None, LANE, seq), lambda b, i: (b, 0, 0)), kvspec(2), kvspec(3)],
        out_specs=pl.BlockSpec((4, t, LANE), lambda b, i: (0, b * nt + i, 0)),
        out_shape=jax.ShapeDtypeStruct((NSA_HEADS, rows, LANE), BF16),
        scratch_shapes=[pltpu.VMEM((NSA_HEADS, t, LANE), BF16),
                        pltpu.VMEM((NSA_HEADS, LANE, t), BF16),
                        pltpu.VMEM((NSA_HEADS, LANE, t), F32),
                        pltpu.VMEM((NSA_HEADS, t, LANE), F32),
                        pltpu.VMEM((NSA_HEADS, t, LANE), F32),
                        pltpu.VMEM((2, NSA_HEADS, t, t), F32),
                        pltpu.VMEM((NSA_HEADS, t, t), BF16)],
        compiler_params=_cparams(("parallel", "parallel")),
        name="nsa",
    )(proj3, proj3, proj3, q_w.reshape(1, LANE), *tabs, k_cmp, v_cmp, overlap, nk, vs_t, nk, nk)


def _s5_factors(a_re, a_im, b_re, b_im, c_re, c_im, log_dt):
    t = S5_CHUNK
    dt = jnp.exp(log_dt.astype(F32))[:, None]
    ar, ai = a_re.astype(F32), a_im.astype(F32)
    ang = dt * ai
    mag = jnp.exp(dt * ar)
    abar_r, abar_i = mag * jnp.cos(ang), mag * jnp.sin(ang)
    nr, ni = abar_r - 1.0, abar_i
    den = ar * ar + ai * ai
    fr = (nr * ar + ni * ai) / den
    fi = (ni * ar - nr * ai) / den
    bt_r, bt_i = b_re.astype(F32).transpose(0, 2, 1), b_im.astype(F32).transpose(0, 2, 1)
    bbar_r = fr[:, None, :] * bt_r - fi[:, None, :] * bt_i
    bbar_i = fr[:, None, :] * bt_i + fi[:, None, :] * bt_r

    def powers(tau):
        tau = jnp.asarray(tau, F32)[:, None, None]
        pmag = jnp.exp(tau * (dt * ar)[None])
        return pmag * jnp.cos(tau * ang[None]), pmag * jnp.sin(tau * ang[None])

    pw_r, pw_i = powers(np.arange(t + 1))
    cr, ci = c_re.astype(F32), c_im.astype(F32)
    cp_r = cr[None] * pw_r[:, :, None, :] - ci[None] * pw_i[:, :, None, :]
    cp_i = cr[None] * pw_i[:, :, None, :] + ci[None] * pw_r[:, :, None, :]
    rev_r, rev_i = powers(t - 1 - np.arange(t))
    bp_r = rev_r[:, :, None, :] * bbar_r[None] - rev_i[:, :, None, :] * bbar_i[None]
    bp_i = rev_r[:, :, None, :] * bbar_i[None] + rev_i[:, :, None, :] * bbar_r[None]
    to_rows = lambda a: (a.reshape(t, S5_MBLK, LANE, S5_STATE).transpose(1, 0, 2, 3)
                         .reshape(S5_MBLK, t * LANE, S5_STATE))
    bpc_r, bpc_i = to_rows(bp_r), to_rows(bp_i)
    cpo = jnp.stack([cp_r, -cp_i]).reshape(2, t + 1, S5_MBLK, S5_GPB, S5_GROUP, S5_STATE)
    cpc = cpo.transpose(2, 0, 3, 5, 1, 4).reshape(S5_MBLK, 2 * S5_HALF, (t + 1) * S5_GROUP)
    a_t = jnp.concatenate([pw_r[t].reshape(S5_MBLK, 1, S5_HALF), pw_i[t].reshape(S5_MBLK, 1, S5_HALF)], axis=2)
    return bpc_r, bpc_i, cpc, a_t


def _s5_wgen_kernel(bpr_ref, bpi_ref, cp_ref, win_ref, wc_ref):
    t = S5_CHUNK
    wide = t * LANE
    gs, ps = int(math.log2(S5_GROUP)), int(math.log2(S5_STATE))

    def expand(x, copy_mask, group_mask):
        ex = jnp.where(copy_mask, 1.0, 0.0).astype(BF16)
        return jnp.where(group_mask, jnp.dot(x.astype(BF16), ex, preferred_element_type=F32), 0.0).astype(BF16)

    def bp_rows(lo, hi):
        return jnp.concatenate([bpr_ref[0, lo:hi, :], bpi_ref[0, lo:hi, :]], axis=1)

    def split(x):
        hi = x.astype(BF16)
        return hi, (x - hi.astype(F32)).astype(BF16)

    lagc = t * S5_GROUP
    kc = []
    for g in range(S5_GPB):
        a_hi, a_lo = split(bp_rows((t - 1) * LANE + g * S5_GROUP, (t - 1) * LANE + (g + 1) * S5_GROUP))
        b_hi, b_lo = split(jnp.concatenate(
            [cp_ref[0, g * S5_STATE:(g + 1) * S5_STATE, :lagc],
             cp_ref[0, S5_HALF + g * S5_STATE:S5_HALF + (g + 1) * S5_STATE, :lagc]], axis=0))
        kc.append(jnp.dot(a_hi, b_hi, preferred_element_type=F32) + jnp.dot(a_hi, b_lo, preferred_element_type=F32)
                  + jnp.dot(a_lo, b_hi, preferred_element_type=F32))
    kc = jnp.concatenate(kc, axis=0)
    r, c = _iota((lagc, LANE), 0), _iota((lagc, LANE), 1)
    same_k = (_iota((LANE, LANE), 0) >> gs) == (_iota((LANE, LANE), 1) >> gs)
    zero = jnp.zeros((LANE, LANE), BF16)
    for lag in range(t):
        bd = expand(kc, ((r >> gs) == lag) & ((r & (S5_GROUP - 1)) == (c & (S5_GROUP - 1))), same_k)
        for s in range(t - lag):
            win_ref[0, s * LANE:(s + 1) * LANE, (s + lag) * LANE:(s + lag + 1) * LANE] = bd
    for s in range(1, t):
        for tt in range(s):
            win_ref[0, s * LANE:(s + 1) * LANE, tt * LANE:(tt + 1) * LANE] = zero
    r, c = _iota((LANE, 2 * S5_HALF), 0), _iota((LANE, 2 * S5_HALF), 1)
    copy_b = ((r >> ps) == (c >> (ps + 3))) & ((r & (S5_STATE - 1)) == (c & (S5_STATE - 1)))
    same_b = (r >> gs) == ((c >> ps) & (S5_GPB - 1))
    for s in range(t):
        win_ref[0, s * LANE:(s + 1) * LANE, wide:] = expand(bp_rows(s * LANE, (s + 1) * LANE), copy_b, same_b)
    rows = 2 * S5_STATE * 2
    r, c = _iota(((t + 1) * S5_GROUP, wide), 0), _iota(((t + 1) * S5_GROUP, wide), 1)
    copy_c = ((r >> gs) == (c >> (gs + 3)) + 1) & ((r & (S5_GROUP - 1)) == (c & (S5_GROUP - 1)))
    for ch in range(2 * S5_HALF // rows):
        r, c = ch * rows + _iota((rows, wide), 0), _iota((rows, wide), 1)
        same_c = ((r >> ps) & (S5_GPB - 1)) == ((c >> gs) & (S5_GPB - 1))
        wc_ref[0, ch * rows:(ch + 1) * rows, :] = expand(cp_ref[0, ch * rows:(ch + 1) * rows, :],
                                                         copy_c, same_c)


def _s5_wgen(bpc_r, bpc_i, cpc):
    t = S5_CHUNK
    wide = t * LANE
    spec = lambda a: pl.BlockSpec((1,) + a.shape[1:], lambda m: (m, 0, 0))
    return pl.pallas_call(
        _s5_wgen_kernel,
        grid=(cpc.shape[0],),
        in_specs=[spec(bpc_r), spec(bpc_i), spec(cpc)],
        out_specs=[pl.BlockSpec((1, wide, wide + 2 * S5_HALF), lambda m: (m, 0, 0)),
                   pl.BlockSpec((1, 2 * S5_HALF, wide), lambda m: (m, 0, 0))],
        out_shape=[jax.ShapeDtypeStruct((cpc.shape[0], wide, wide + 2 * S5_HALF), BF16),
                   jax.ShapeDtypeStruct((cpc.shape[0], 2 * S5_HALF, wide), BF16)],
        compiler_params=_cparams(("parallel",)),
        name="s5_wgen",
    )(bpc_r, bpc_i, cpc)


def _s5_in_kernel(u_ref, w_ref, d_ref, y_ref, b_ref):
    tr = y_ref.shape[1]
    u = jnp.concatenate([u_ref[0, pl.ds(s, tr, stride=S5_CHUNK), :] for s in range(S5_CHUNK)], axis=1)
    ub = u.astype(BF16)
    wide = y_ref.shape[2]
    for j in range(S5_CHUNK // 2):
        kk = (2 * j + 2) * LANE
        cols = slice(2 * j * LANE, kk)
        y_ref[0, :, cols] = (jnp.dot(ub[:, :kk], w_ref[0, :kk, cols], preferred_element_type=F32)
                             + d_ref[0, :, cols] * u[:, cols])
    b_ref[0] = jnp.dot(ub, w_ref[0, :, wide:], preferred_element_type=F32)


def _s5_in(proj3, w_in, d_t, layer):
    rows = proj3.shape[1] // S5_CHUNK
    wide = S5_CHUNK * LANE
    tr = min(rows, ROW_TILE)
    return pl.pallas_call(
        _s5_in_kernel,
        grid=(S5_MBLK, rows // tr),
        in_specs=[pl.BlockSpec((1, tr * S5_CHUNK, LANE), lambda m, r: (CB_SU + m, r, 0)),
                  pl.BlockSpec((1, wide, wide + 2 * S5_HALF), lambda m, r: (layer * S5_MBLK + m, 0, 0)),
                  pl.BlockSpec((1, 1, wide), lambda m, r: (m, 0, 0))],
        out_specs=[pl.BlockSpec((1, tr, wide), lambda m, r: (m, r, 0)),
                   pl.BlockSpec((1, tr, 2 * S5_HALF), lambda m, r: (m, r, 0))],
        out_shape=[jax.ShapeDtypeStruct((S5_MBLK, rows, wide), F32),
                   jax.ShapeDtypeStruct((S5_MBLK, rows, 2 * S5_HALF), F32)],
        compiler_params=_cparams(("parallel", "parallel")),
        name="s5_in",
    )(proj3, w_in, d_t)


def _s5_scan_kernel(b_ref, a_ref, o_ref, *, bsz, nk):
    nm = b_ref.shape[0]

    def body(k, carry):
        out = []
        for mi in range(nm):
            ar = a_ref[mi, :, :S5_HALF]
            ai = a_ref[mi, :, S5_HALF:]
            for b in range(bsz):
                sr, si = carry[2 * (mi * bsz + b)], carry[2 * (mi * bsz + b) + 1]
                row = b * nk + k
                o_ref[mi, pl.ds(row, 1), :] = jnp.concatenate([sr, si], axis=1)
                x = b_ref[mi, pl.ds(row, 1), :]
                out.append(ar * sr - ai * si + x[:, :S5_HALF])
                out.append(ar * si + ai * sr + x[:, S5_HALF:])
        return tuple(out)

    zero = jnp.zeros((1, S5_HALF), F32)
    lax.fori_loop(0, nk, body, tuple(zero for _ in range(2 * bsz * nm)))


def _s5_scan(bst, a_t, bsz):
    _, rows, wide = bst.shape
    nm = 2
    spec = pl.BlockSpec((nm, rows, wide), lambda m: (m, 0, 0))
    return pl.pallas_call(
        functools.partial(_s5_scan_kernel, bsz=bsz, nk=rows // bsz),
        grid=(S5_MBLK // nm,),
        in_specs=[spec, pl.BlockSpec((nm, 1, wide), lambda m: (m, 0, 0))],
        out_specs=spec,
        out_shape=jax.ShapeDtypeStruct(bst.shape, F32),
        compiler_params=_cparams(("parallel",)),
        name="s5_scan",
    )(bst, a_t)


def _s5_out_kernel(y_ref, s_ref, w_ref, o_ref):
    tr = y_ref.shape[1]
    y = jax.nn.gelu(y_ref[0] + jnp.dot(s_ref[0].astype(BF16), w_ref[0], preferred_element_type=F32))
    for s in range(S5_CHUNK):
        o_ref[0, pl.ds(s, tr, stride=S5_CHUNK), :] = y[:, s * LANE:(s + 1) * LANE]


def _s5_out(y_intra, s_prev, w_c, layer):
    _, rows, wide = y_intra.shape
    tr = min(rows, ROW_TILE)
    return pl.pallas_call(
        _s5_out_kernel,
        grid=(S5_MBLK, rows // tr),
        in_specs=[pl.BlockSpec((1, tr, wide), lambda m, r: (m, r, 0)),
                  pl.BlockSpec((1, tr, 2 * S5_HALF), lambda m, r: (m, r, 0)),
                  pl.BlockSpec((1, 2 * S5_HALF, wide), lambda m, r: (layer * S5_MBLK + m, 0, 0))],
        out_specs=pl.BlockSpec((1, tr * S5_CHUNK, LANE), lambda m, r: (m, r, 0)),
        out_shape=jax.ShapeDtypeStruct((S5_MBLK, rows * S5_CHUNK, LANE), F32),
        compiler_params=_cparams(("parallel", "parallel")),
        name="s5_out",
    )(y_intra, s_prev, w_c)


def _glu_kernel(y_ref, z_ref, w_ref, o_ref):
    nb = y_ref.shape[0]
    y = jnp.concatenate([y_ref[c] for c in range(nb)], axis=1)
    z = jnp.concatenate([z_ref[c] for c in range(nb)], axis=1)
    gate = jax.nn.sigmoid(jnp.dot(y.astype(BF16), w_ref[...], preferred_element_type=F32))
    o = (y * gate * (z * jax.nn.sigmoid(z))).astype(BF16)
    for c in range(nb):
        o_ref[c] = o[:, c * LANE:(c + 1) * LANE]


def _glu(y5, proj3, glu_w):
    nb, rows, _ = y5.shape
    tm = ROW_TILE
    spec = lambda blk: pl.BlockSpec((nb, tm, LANE), lambda i: (blk, i, 0))
    return pl.pallas_call(
        _glu_kernel,
        grid=(rows // tm,),
        in_specs=[spec(0), spec(CB_SZ // nb), pl.BlockSpec((S5_WIDTH, S5_WIDTH), lambda i: (0, 0))],
        out_specs=spec(0),
        out_shape=jax.ShapeDtypeStruct((nb, rows, LANE), BF16),
        compiler_params=_cparams(("parallel",)),
        name="s5_glu",
    )(y5, proj3, glu_w.astype(BF16))


def _outproj_kernel(a_ref, b_ref, c_ref, w_ref, x_ref, o_ref):
    parts = ([a_ref[h] for h in range(a_ref.shape[0])] + [b_ref[h] for h in range(b_ref.shape[0])]
             + [c_ref[h] for h in range(c_ref.shape[0])])
    mixed = jnp.concatenate(parts, axis=1)
    o_ref[...] = x_ref[...] + jnp.dot(mixed, w_ref[...], preferred_element_type=F32)


def _outproj(m_moba, m_nsa, m_s5, w_out, x2):
    rows, d = x2.shape
    tm = ROW_TILE
    lspec = lambda n: pl.BlockSpec((n, tm, LANE), lambda i: (0, i, 0))
    return pl.pallas_call(
        _outproj_kernel,
        grid=(rows // tm,),
        in_specs=[lspec(m_moba.shape[0]), lspec(m_nsa.shape[0]), lspec(m_s5.shape[0]),
                  pl.BlockSpec(w_out.shape, lambda i: (0, 0)),
                  pl.BlockSpec((tm, d), lambda i: (i, 0))],
        out_specs=pl.BlockSpec((tm, d), lambda i: (i, 0)),
        out_shape=jax.ShapeDtypeStruct((rows, d), F32),
        compiler_params=_cparams(("parallel",)),
        name="outproj",
    )(m_moba, m_nsa, m_s5, w_out.astype(BF16), x2)


def _w_in_source(blk):
    mw, nw, kvw, ng = MOBA_HEADS * HEAD_DIM, NSA_HEADS * HEAD_DIM, HEAD_DIM, 3 * NSA_HEADS
    o_kv = 4 * mw + nw
    o_ng = o_kv + 6 * kvw
    o_nz = o_ng + ng
    src = jnp.where(blk < CB_NZ, blk * LANE,
          jnp.where(blk < CB_NKS, o_nz + (blk - CB_NZ) * LANE,
          jnp.where(blk < CB_NKC, o_kv + 2 * kvw + (blk - CB_NKS) * LANE,
          jnp.where(blk < CB_NG, o_kv + (blk - CB_NKC) * LANE, o_ng))))
    real = jnp.where(blk < CB_NG, LANE, jnp.where(blk == CB_NG, ng, 0))
    return src, real


def _wperm_kernel(w_ref, o_ref):
    _, real = _w_in_source(pl.program_id(0))
    for layer in range(o_ref.shape[0]):
        x = w_ref[:, layer, :]
        x = jnp.where(_iota(x.shape, 0) < real, x, 0.0)
        o_ref[layer] = x.T.astype(BF16)


def _permute_w_in(w_in_all):
    wt = jnp.transpose(w_in_all, (2, 0, 1))
    _, nl, d = wt.shape
    return pl.pallas_call(
        _wperm_kernel,
        grid=(IN_BLOCKS,),
        in_specs=[pl.BlockSpec((pl.Element(LANE), pl.Element(nl), pl.Element(d)),
                               lambda c: (_w_in_source(c)[0], 0, 0))],
        out_specs=pl.BlockSpec((nl, d, LANE), lambda c: (0, 0, c)),
        out_shape=jax.ShapeDtypeStruct((nl, d, IN_BLOCKS * LANE), BF16),
        compiler_params=_cparams(("parallel",)),
        name="w_in_permute",
    )(wt)


def _layer(x2, bsz, seq, tabs, ctabs, overlap, w_in, s5_w, layer, norm_w, w_out, moba_q_norm, moba_k_norm,
           nsa_q_norm, nsa_kc_norm, nsa_ks_norm, nsa_kw_norm, nsa_pe_k, nsa_pe_v, nsa_cmp_k_w1, nsa_cmp_k_w2,
           nsa_cmp_v_w1, nsa_cmp_v_w2, s5_d, s5_glu_w):
    proj3 = _inproj(x2, norm_w, w_in, layer)
    mk_n, mv_b, nk, kmean, vs_t = _kprep(proj3, seq, moba_k_norm, nsa_ks_norm, nsa_kw_norm, tabs)
    m_moba = _moba(proj3, bsz, seq, moba_q_norm, tabs, kmean.transpose(1, 0, 2), mk_n, mv_b)
    k_cmp, v_cmp = _compress(proj3, bsz, seq, nsa_pe_k, nsa_pe_v, nsa_cmp_k_w1, nsa_cmp_k_w2,
                             nsa_cmp_v_w1, nsa_cmp_v_w2, nsa_kc_norm, ctabs)
    m_nsa = _nsa(proj3, bsz, seq, nsa_q_norm, tabs, k_cmp, v_cmp, overlap, nk, vs_t)
    w_s5_in, w_s5_out, a_t = s5_w
    d_t = jnp.tile(s5_d.astype(F32).reshape(S5_MBLK, 1, LANE), (1, 1, S5_CHUNK))
    y_intra, bst = _s5_in(proj3, w_s5_in, d_t, layer)
    y5 = _s5_out(y_intra, _s5_scan(bst, a_t[layer], bsz), w_s5_out, layer)
    m_s5 = _glu(y5, proj3, s5_glu_w)
    return _outproj(m_moba, m_nsa, m_s5, w_out, x2)


def kernel(x, norm_w, w_in, w_out, moba_q_norm, moba_k_norm, nsa_q_norm, nsa_kc_norm, nsa_ks_norm, nsa_kw_norm, nsa_pe_k, nsa_pe_v, nsa_cmp_k_w1, nsa_cmp_k_w2, nsa_cmp_v_w1, nsa_cmp_v_w2, s5_a_re, s5_a_im, s5_b_re, s5_b_im, s5_c_re, s5_c_im, s5_d, s5_log_dt, s5_glu_w):
    bsz, seq, d = x.shape
    tabs = _rope_tables(jnp.arange(seq, dtype=F32))
    ncmp = seq // NSA_CMP_STRIDE
    ctabs = _rope_tables(jnp.arange(ncmp, dtype=F32) * NSA_CMP_STRIDE + (NSA_CMP_LEN - 1))
    nsel = seq // NSA_SEL_BLOCK
    ci = np.arange(ncmp)[:, None] * NSA_CMP_STRIDE
    sj = np.arange(nsel)[None, :] * NSA_SEL_BLOCK
    overlap = np.zeros((ncmp, LANE), np.float32)
    overlap[:, :nsel] = (ci < sj + NSA_SEL_BLOCK) & (ci + NSA_CMP_LEN > sj)
    overlap = jnp.asarray(overlap)
    params = (norm_w, w_out, moba_q_norm, moba_k_norm, nsa_q_norm, nsa_kc_norm, nsa_ks_norm,
              nsa_kw_norm, nsa_pe_k, nsa_pe_v, nsa_cmp_k_w1, nsa_cmp_k_w2, nsa_cmp_v_w1, nsa_cmp_v_w2,
              s5_d, s5_glu_w)
    bpc_r, bpc_i, cpc, a_t = jax.vmap(_s5_factors)(s5_a_re, s5_a_im, s5_b_re, s5_b_im, s5_c_re, s5_c_im, s5_log_dt)
    merge = lambda a: a.reshape((a.shape[0] * a.shape[1],) + a.shape[2:])
    s5_w = tuple(_s5_wgen(merge(bpc_r), merge(bpc_i), merge(cpc))) + (a_t,)
    x2 = x.reshape(bsz * seq, d)
    w_perm = _permute_w_in(w_in)
    for layer in range(norm_w.shape[0]):
        x2 = _layer(x2, bsz, seq, tabs, ctabs, overlap, w_perm, s5_w, layer, *[p[layer] for p in params])
    return x2.reshape(bsz, seq, d)
```
